```python
import numpy as np
import jax, jax.numpy as jnp
from jax import lax

D_MODEL = 1024
BATCH = 32
SEQ = 256
DEPTH = 1
DEC_BATCH = 2
DEC_SEQ = 2048
PAST_LEN = 256

GRID_W = 64
GLA_HEADS = 4
GLA_DK = 64
GLA_DV = 128
GLA_LOWRANK = 16
GLA_GATE_NORM = 16.0
GLA_CHUNK = 64
ATT_HEADS = 8
ATT_KV_HEADS = 2
HEAD_DIM = 64
ROPE_AXIS_DIM = HEAD_DIM // 2
ROPE_THETA = 10000.0
Q_BLOCK = 128
D_FF = -(-8 * D_MODEL // (3 * 256)) * 256
N_MOD = 6
EPS = 1e-6
SPLIT_SIZES = (GLA_HEADS * GLA_DK, GLA_HEADS * GLA_DK, GLA_HEADS * GLA_DV, GLA_HEADS * GLA_DV,
               2 * GLA_LOWRANK, ATT_HEADS * HEAD_DIM, ATT_KV_HEADS * HEAD_DIM, ATT_KV_HEADS * HEAD_DIM)
D_IN_PROJ = sum(SPLIT_SIZES)
D_MIX = GLA_HEADS * GLA_DV + ATT_HEADS * HEAD_DIM

kernel_name = 'hybrid_gla_gqa_prefix_dit_step'

F32 = jnp.float32


def rms_norm(x, g):
    xf = x.astype(F32)
    y = xf * lax.rsqrt(jnp.mean(xf * xf, axis=-1, keepdims=True) + EPS)
    return (y * g.astype(F32)).astype(x.dtype)


def adaln_params(cond, ada_w, ada_b):
    mod = jax.nn.silu(cond) @ ada_w + ada_b
    return jnp.split(mod, N_MOD, axis=-1)


def modulate(h, shift, scale):
    return h * (1.0 + scale[:, None, :]) + shift[:, None, :]


def axial_rope_tables(T):
    rows = T // GRID_W
    row = jnp.repeat(jnp.arange(rows, dtype=F32), GRID_W)
    col = jnp.tile(jnp.arange(GRID_W, dtype=F32), rows)
    inv = ROPE_THETA ** (-jnp.arange(0, ROPE_AXIS_DIM, 2, dtype=F32) / ROPE_AXIS_DIM)
    ang = jnp.stack([row[:, None] * inv, col[:, None] * inv], axis=1)
    return jnp.cos(ang), jnp.sin(ang)


def apply_axial_rope(x, cos, sin):
    B, T, H, Dh = x.shape
    xr = x.reshape(B, T, H, 2, 2, ROPE_AXIS_DIM // 2).astype(F32)
    x1, x2 = xr[..., 0, :], xr[..., 1, :]
    c, s = cos[None, :, None], sin[None, :, None]
    out = jnp.stack([x1 * c - x2 * s, x1 * s + x2 * c], axis=-2)
    return out.reshape(B, T, H, Dh).astype(x.dtype)


def gla_chunked(q, k, v, g, s0):
    B, T, H, DK = q.shape
    DV = v.shape[-1]
    n = T // GLA_CHUNK
    q = q.astype(F32).reshape(B, n, GLA_CHUNK, H, DK) * (DK ** -0.5)
    k = k.astype(F32).reshape(B, n, GLA_CHUNK, H, DK)
    v = v.astype(F32).reshape(B, n, GLA_CHUNK, H, DV)
    b = jnp.cumsum(g.astype(F32).reshape(B, n, GLA_CHUNK, H, DK), axis=2)
    b_last = b[:, :, -1:]
    q_t = q * jnp.exp(b)
    k_t = k * jnp.exp(-b)
    k_e = k * jnp.exp(b_last - b)
    mask = jnp.tril(jnp.ones((GLA_CHUNK, GLA_CHUNK), dtype=bool))
    a = jnp.where(mask, jnp.einsum('bnihd,bnjhd->bnhij', q_t, k_t), 0.0)
    o_intra = jnp.einsum('bnhij,bnjhv->bnihv', a, v)
    u = jnp.einsum('bnjhd,bnjhv->bnhdv', k_e, v)
    decay = jnp.exp(b_last[:, :, 0])

    def step(s, inp):
        d, uu = inp
        return d[..., None] * s + uu, s

    s_fin, s_prev = lax.scan(step, s0.astype(F32), (jnp.moveaxis(decay, 1, 0), jnp.moveaxis(u, 1, 0)))
    s_prev = jnp.moveaxis(s_prev, 0, 1)
    o_inter = jnp.einsum('bnihd,bnhdv->bnihv', q_t, s_prev)
    return (o_intra + o_inter).reshape(B, T, H, DV), s_fin


def blocked_attention(q, k, v):
    B, T, H, Dh = q.shape
    Hkv = k.shape[2]
    G = H // Hkv
    nb = T // Q_BLOCK
    qb = jnp.moveaxis(q.reshape(B, nb, Q_BLOCK, Hkv, G, Dh), 1, 0) * (Dh ** -0.5)

    def one_block(qi):
        s = jnp.einsum('bqhgd,bkhd->bhgqk', qi.astype(F32), k.astype(F32))
        p = jax.nn.softmax(s, axis=-1)
        return jnp.einsum('bhgqk,bkhd->bqhgd', p, v.astype(F32))

    o = lax.map(one_block, qb)
    return jnp.moveaxis(o, 0, 1).reshape(B, T, H * Dh).astype(q.dtype)


def mixer(h, rope, ctx_kv, s_f0, s_b0, w_in, w_gk2, b_gk2, gla_g, q_g, k_g, w_out):
    B, T, _ = h.shape
    offs = np.cumsum(SPLIT_SIZES)[:-1].tolist()
    qg, kg, vg, og, lr, qa, ka, va = jnp.split(h @ w_in, offs, axis=-1)
    qg = qg.reshape(B, T, GLA_HEADS, GLA_DK)
    kg = kg.reshape(B, T, GLA_HEADS, GLA_DK)
    vg = vg.reshape(B, T, GLA_HEADS, GLA_DV)
    gk = jnp.einsum('btrl,rlk->btrk', lr.reshape(B, T, 2, GLA_LOWRANK), w_gk2) + b_gk2
    gk = (jax.nn.log_sigmoid(gk.astype(F32)) / GLA_GATE_NORM).reshape(B, T, 2, GLA_HEADS, GLA_DK)
    o_f, s_f = gla_chunked(qg, kg, vg, gk[:, :, 0], s_f0)
    o_b, s_b = gla_chunked(qg[:, ::-1], kg[:, ::-1], vg[:, ::-1], gk[:, ::-1, 1], s_b0)
    o_gla = rms_norm(o_f + o_b[:, ::-1], gla_g).astype(h.dtype)
    o_gla = (o_gla * jax.nn.silu(og.reshape(B, T, GLA_HEADS, GLA_DV))).reshape(B, T, GLA_HEADS * GLA_DV)
    qa = rms_norm(qa.reshape(B, T, ATT_HEADS, HEAD_DIM), q_g)
    ka = rms_norm(ka.reshape(B, T, ATT_KV_HEADS, HEAD_DIM), k_g)
    va = va.reshape(B, T, ATT_KV_HEADS, HEAD_DIM)
    if rope is None:
        keys, vals = ka, va
    else:
        qa = apply_axial_rope(qa, *rope)
        ka = apply_axial_rope(ka, *rope)
        ck, cv = ctx_kv
        keys = jnp.concatenate([ka, ck.astype(ka.dtype)], axis=1)
        vals = jnp.concatenate([va, cv.astype(va.dtype)], axis=1)
    o_att = blocked_attention(qa, keys, vals)
    out = jnp.concatenate([o_gla, o_att], axis=-1) @ w_out
    return out, ka, va, s_f, s_b


def swiglu(h, w1, w3, w2):
    return (jax.nn.silu(h @ w1) * (h @ w3)) @ w2


def block(x, cond, rope, ctx_kv, s_f0, s_b0, ada_w, ada_b, n1, n2, w_in, w_gk2, b_gk2,
          gla_g, q_g, k_g, w_out, w1, w3, w2):
    sh1, sc1, g1, sh2, sc2, g2 = adaln_params(cond, ada_w, ada_b)
    h = modulate(rms_norm(x, n1), sh1, sc1)
    out, k, v, s_f, s_b = mixer(h, rope, ctx_kv, s_f0, s_b0, w_in, w_gk2, b_gk2, gla_g, q_g, k_g, w_out)
    x = x + g1[:, None, :] * out
    h = modulate(rms_norm(x, n2), sh2, sc2)
    x = x + g2[:, None, :] * swiglu(h, w1, w3, w2)
    return x, k, v, s_f, s_b


def setup_inputs(seed: int = 0) -> dict:
    key = jax.random.key(seed)
    ks = jax.random.split(key, 24)
    nrm = jax.random.normal
    D = D_MODEL
    return {
        'x_prompt': nrm(ks[0], (BATCH, SEQ, D), F32),
        'x_sample': nrm(ks[1], (DEC_BATCH, DEC_SEQ, D), F32),
        'cache_k': nrm(ks[2], (DEC_BATCH, DEPTH, PAST_LEN, ATT_KV_HEADS, HEAD_DIM), F32),
        'cache_v': nrm(ks[3], (DEC_BATCH, DEPTH, PAST_LEN, ATT_KV_HEADS, HEAD_DIM), F32),
        'state_gla_fwd': nrm(ks[4], (DEC_BATCH, DEPTH, GLA_HEADS, GLA_DK, GLA_DV), F32),
        'state_gla_bwd': nrm(ks[5], (DEC_BATCH, DEPTH, GLA_HEADS, GLA_DK, GLA_DV), F32),
        'c': nrm(ks[6], (DEC_BATCH, D), F32),
        'c_ctx': nrm(ks[7], (D,), F32),
        'ada_w': nrm(ks[8], (DEPTH, D, N_MOD * D), F32) * D ** -0.5,
        'ada_b': nrm(ks[9], (DEPTH, N_MOD * D), F32) * 0.02,
        'norm1_g': 1.0 + 0.02 * nrm(ks[10], (DEPTH, D), F32),
        'norm2_g': 1.0 + 0.02 * nrm(ks[11], (DEPTH, D), F32),
        'w_in': nrm(ks[12], (DEPTH, D, D_IN_PROJ), F32) * D ** -0.5,
        'w_gk2': nrm(ks[13], (DEPTH, 2, GLA_LOWRANK, GLA_HEADS * GLA_DK), F32) * GLA_LOWRANK ** -0.5,
        'b_gk2': nrm(ks[14], (DEPTH, 2, GLA_HEADS * GLA_DK), F32) * 0.1,
        'gla_norm_g': 1.0 + 0.02 * nrm(ks[15], (DEPTH, GLA_DV), F32),
        'q_norm_g': 1.0 + 0.02 * nrm(ks[16], (DEPTH, HEAD_DIM), F32),
        'k_norm_g': 1.0 + 0.02 * nrm(ks[17], (DEPTH, HEAD_DIM), F32),
        'w_out': nrm(ks[18], (DEPTH, D_MIX, D), F32) * D_MIX ** -0.5,
        'w_ffn1': nrm(ks[19], (DEPTH, D, D_FF), F32) * D ** -0.5,
        'w_ffn3': nrm(ks[20], (DEPTH, D, D_FF), F32) * D ** -0.5,
        'w_ffn2': nrm(ks[21], (DEPTH, D_FF, D), F32) * D_FF ** -0.5,
        'final_g': 1.0 + 0.02 * nrm(ks[22], (D,), F32),
    }


def reference(x_prompt, x_sample, cache_k, cache_v, state_gla_fwd, state_gla_bwd, c, c_ctx,
              ada_w, ada_b, norm1_g, norm2_g, w_in, w_gk2, b_gk2, gla_norm_g, q_norm_g, k_norm_g,
              w_out, w_ffn1, w_ffn3, w_ffn2, final_g):
    B = x_prompt.shape[0]
    T = x_sample.shape[1]
    rope = axial_rope_tables(T)
    zero_state = jnp.zeros((B, GLA_HEADS, GLA_DK, GLA_DV), F32)
    cond_ctx = jnp.broadcast_to(c_ctx[None, :], (B, D_MODEL))
    xc, xl = x_prompt, x_sample
    ks_new, vs_new, sf_new, sb_new = [], [], [], []
    for l in range(DEPTH):
        lw = (ada_w[l], ada_b[l], norm1_g[l], norm2_g[l], w_in[l], w_gk2[l], b_gk2[l],
              gla_norm_g[l], q_norm_g[l], k_norm_g[l], w_out[l], w_ffn1[l], w_ffn3[l], w_ffn2[l])
        xc, k_c, v_c, s_f, s_b = block(xc, cond_ctx, None, None, zero_state, zero_state, *lw)
        ks_new.append(k_c)
        vs_new.append(v_c)
        sf_new.append(s_f)
        sb_new.append(s_b)
        xl, _, _, _, _ = block(xl, c, rope, (cache_k[:, l], cache_v[:, l]),
                               state_gla_fwd[:, l], state_gla_bwd[:, l], *lw)
    y_prompt = rms_norm(xc, final_g)
    y_sample = rms_norm(xl, final_g)
    new_cache_k = jnp.stack(ks_new, axis=1)
    new_cache_v = jnp.stack(vs_new, axis=1)
    new_state_gla_fwd = jnp.stack(sf_new, axis=1)
    new_state_gla_bwd = jnp.stack(sb_new, axis=1)
    return (y_prompt, y_sample, new_cache_k, new_cache_v, new_state_gla_fwd, new_state_gla_bwd)
```

```python
import functools

import numpy as np
import jax
import jax.numpy as jnp
from jax import lax
from jax.experimental import pallas as pl
from jax.experimental.pallas import tpu as pltpu

F32 = jnp.float32
BF16 = jnp.bfloat16

D_MODEL = 1024
GRID_W = 64
GLA_HEADS = 4
GLA_DK = 64
GLA_DV = 128
GLA_LOWRANK = 16
GLA_GATE_NORM = 16.0
GLA_CHUNK = 64
ATT_HEADS = 8
ATT_KV_HEADS = 2
HEAD_DIM = 64
ROPE_AXIS_DIM = HEAD_DIM // 2
ROPE_THETA = 10000.0
D_FF = -(-8 * D_MODEL // (3 * 256)) * 256
N_MOD = 6
EPS = 1e-6

D_QK = GLA_HEADS * GLA_DK
D_V = GLA_HEADS * GLA_DV
D_QA = ATT_HEADS * HEAD_DIM
D_KA = ATT_KV_HEADS * HEAD_DIM
GROUP = ATT_HEADS // ATT_KV_HEADS
O_Q, O_K, O_V, O_OG = 0, D_QK, 2 * D_QK, 2 * D_QK + D_V
O_QA = O_OG + D_V
O_KA = O_QA + D_QA
O_VA = O_KA + D_KA
O_LR = O_VA + D_KA
D_PROJ = O_LR + 2 * GLA_LOWRANK

COND_ROWS = 8
VMEM_LIMIT = 56 * 1024 * 1024


def _dot(a, b):
    return jnp.dot(a, b, preferred_element_type=F32)


def _dot_nt(a, b):
    return lax.dot_general(a, b, (((1,), (1,)), ((), ())), preferred_element_type=F32)


def _dot_tn(a, b):
    return lax.dot_general(a, b, (((0,), (0,)), ((), ())), preferred_element_type=F32)


def _rms(x, g):
    ms = jnp.mean(x * x, axis=-1, keepdims=True)
    return x * lax.rsqrt(ms + EPS) * g


def _silu(x):
    return x * jax.nn.sigmoid(x)


def _adaln_kernel(c_ref, w_ref, b_ref, o_ref):
    s = _silu(c_ref[...])
    o_ref[...] = _dot(s.astype(BF16), w_ref[...].astype(BF16)) + b_ref[...]


def _adaln(cond, ada_w, ada_b):
    n = ada_w.shape[1]
    tn = 1536
    return pl.pallas_call(
        _adaln_kernel,
        grid=(n // tn,),
        in_specs=[
            pl.BlockSpec((COND_ROWS, D_MODEL), lambda j: (0, 0)),
            pl.BlockSpec((D_MODEL, tn), lambda j: (0, j)),
            pl.BlockSpec((1, tn), lambda j: (0, j)),
        ],
        out_specs=pl.BlockSpec((COND_ROWS, tn), lambda j: (0, j)),
        out_shape=jax.ShapeDtypeStruct((COND_ROWS, n), F32),
        compiler_params=pltpu.CompilerParams(
            dimension_semantics=("arbitrary",), vmem_limit_bytes=VMEM_LIMIT),
        name="adaln",
    )(cond, ada_w, ada_b.reshape(1, n))


def _swap_halves(x, lane_lo):
    n = x.shape[-1]
    up = pltpu.roll(x, n - ROPE_AXIS_DIM // 2, axis=1)
    dn = pltpu.roll(x, ROPE_AXIS_DIM // 2, axis=1)
    return jnp.where(lane_lo, up, dn)


def _pre_kernel(*refs, rope, emit_cache):
    it = iter(refs)
    x_ref, mod_ref, n1_ref, win_ref, wgk_ref, bgk_ref = (next(it) for _ in range(6))
    qg_ref, kg_ref, bdq_ref, bdk_ref = (next(it) for _ in range(4))
    if rope:
        cos_ref, sin_ref = next(it), next(it)
    q_o, k_o, v_o, gate_o, g_o, qa_o, ka_o, va_o = (next(it) for _ in range(8))
    if emit_cache:
        kc_o, vc_o = next(it), next(it)

    mod = mod_ref[0]
    sh1 = mod[:, 0:D_MODEL]
    sc1 = mod[:, D_MODEL:2 * D_MODEL]
    h = _rms(x_ref[...], n1_ref[...]) * (1.0 + sc1) + sh1
    p = _dot(h.astype(BF16), win_ref[...])

    q_o[...] = p[:, O_Q:O_Q + D_QK].astype(BF16)
    k_o[...] = p[:, O_K:O_K + D_QK].astype(BF16)
    v_o[...] = p[:, O_V:O_V + D_V].astype(BF16)
    gate_o[...] = _silu(p[:, O_OG:O_OG + D_V]).astype(BF16)

    gk = _dot(p[:, O_LR:O_LR + 2 * GLA_LOWRANK].astype(BF16), wgk_ref[...]) + bgk_ref[...]
    g_o[...] = (jnp.minimum(gk, 0.0) - jnp.log(1.0 + jnp.exp(-jnp.abs(gk)))) * (1.0 / GLA_GATE_NORM)

    qa = p[:, O_QA:O_QA + D_QA]
    ka = p[:, O_KA:O_KA + D_KA]
    va = p[:, O_VA:O_VA + D_KA]
    qa = qa * lax.rsqrt(_dot((qa * qa).astype(BF16), bdq_ref[...]) + EPS) * qg_ref[...]
    ka = ka * lax.rsqrt(_dot((ka * ka).astype(BF16), bdk_ref[...]) + EPS) * kg_ref[...]
    if rope:
        lane_q = lax.broadcasted_iota(jnp.int32, qa.shape, 1)
        lane_k = lax.broadcasted_iota(jnp.int32, ka.shape, 1)
        lo_q = (lane_q % ROPE_AXIS_DIM) < ROPE_AXIS_DIM // 2
        lo_k = (lane_k % ROPE_AXIS_DIM) < ROPE_AXIS_DIM // 2
        cos = cos_ref[...]
        sin = sin_ref[...]
        qa = qa * cos + _swap_halves(qa, lo_q) * sin
        ka = ka * cos[:, :D_KA] + _swap_halves(ka, lo_k) * sin[:, :D_KA]
    if emit_cache:
        kc_o[...] = ka
        vc_o[...] = va
    qs = (qa * HEAD_DIM ** -0.5).astype(BF16)
    kb = ka.astype(BF16)
    vb = va.astype(BF16)
    for hd in range(ATT_HEADS):
        qa_o[hd] = qs[:, hd * HEAD_DIM:(hd + 1) * HEAD_DIM]
    for hd in range(ATT_KV_HEADS):
        ka_o[hd] = kb[:, hd * HEAD_DIM:(hd + 1) * HEAD_DIM]
        va_o[hd] = vb[:, hd * HEAD_DIM:(hd + 1) * HEAD_DIM]


def _pre_mixer(x, mod3, mod_row0, tokens_per_mod, seq_len, weights, rope_tabs, emit_cache, tm):
    ntok = x.shape[0]
    n1, w_in_r, w_gk, b_gk, qg, kg, bdq, bdk = weights
    rope = rope_tabs is not None
    tiles_per_mod = tokens_per_mod // tm
    tiles_per_seq = seq_len // tm
    whole = pl.BlockSpec(memory_space=pltpu.VMEM)
    row = lambda w: pl.BlockSpec((tm, w), lambda i: (i, 0))
    in_specs = [
        row(D_MODEL),
        pl.BlockSpec((1, 1, N_MOD * D_MODEL), lambda i: (mod_row0 + i // tiles_per_mod, 0, 0)),
        whole, whole, whole, whole, whole, whole, whole, whole,
    ]
    args = [x, mod3, n1, w_in_r, w_gk, b_gk, qg, kg, bdq, bdk]
    if rope:
        tab = pl.BlockSpec((tm, D_QA), lambda i: (i % tiles_per_seq, 0))
        in_specs += [tab, tab]
        args += list(rope_tabs)
    head = lambda nh: pl.BlockSpec((nh, tm, HEAD_DIM), lambda i: (0, i, 0))
    out_specs = [row(D_QK), row(D_QK), row(D_V), row(D_V), row(2 * D_QK),
                 head(ATT_HEADS), head(ATT_KV_HEADS), head(ATT_KV_HEADS)]
    out_shape = [
        jax.ShapeDtypeStruct((ntok, D_QK), BF16),
        jax.ShapeDtypeStruct((ntok, D_QK), BF16),
        jax.ShapeDtypeStruct((ntok, D_V), BF16),
        jax.ShapeDtypeStruct((ntok, D_V), BF16),
        jax.ShapeDtypeStruct((ntok, 2 * D_QK), F32),
        jax.ShapeDtypeStruct((ATT_HEADS, ntok, HEAD_DIM), BF16),
        jax.ShapeDtypeStruct((ATT_KV_HEADS, ntok, HEAD_DIM), BF16),
        jax.ShapeDtypeStruct((ATT_KV_HEADS, ntok, HEAD_DIM), BF16),
    ]
    if emit_cache:
        out_specs += [row(D_KA), row(D_KA)]
        out_shape += [jax.ShapeDtypeStruct((ntok, D_KA), F32)] * 2
    return pl.pallas_call(
        functools.partial(_pre_kernel, rope=rope, emit_cache=emit_cache),
        grid=(ntok // tm,),
        in_specs=in_specs,
        out_specs=out_specs,
        out_shape=out_shape,
        compiler_params=pltpu.CompilerParams(
            dimension_semantics=("arbitrary",), vmem_limit_bytes=VMEM_LIMIT),
        name="pre_mixer_rope" if rope else "pre_mixer",
    )(*args)


def _gla_chunk(c, state, fwd, q_ref, k_ref, v_ref, g_ref, tri, half_lo):
    C = GLA_CHUNK
    r0 = pl.multiple_of(c * C, C)
    rows = pl.ds(r0, C)
    qc = q_ref[0, rows, :].astype(F32) * GLA_DK ** -0.5
    kc = k_ref[0, rows, :].astype(F32)
    vc = v_ref[0, rows, :]
    gc = g_ref[0, rows, 0:D_QK] if fwd else g_ref[0, rows, D_QK:2 * D_QK]
    ltri = jnp.where(tri, 1.0, 0.0).astype(BF16)
    g_hi = gc.astype(BF16)
    g_lo = (gc - g_hi.astype(F32)).astype(BF16)
    b = _dot(ltri, g_hi) + _dot(ltri, g_lo)
    btot = b[C - 1:C, :] if fwd else b[0:1, :]
    qt = (qc * jnp.exp(b)).astype(BF16)
    kt = (kc * jnp.exp(-b)).astype(BF16)
    ke = (kc * jnp.exp(btot - b)).astype(BF16)
    ones = jnp.ones((C, GLA_DV), BF16)
    decay = jnp.exp(_dot_tn(g_hi, ones) + _dot_tn(g_lo, ones))
    state_b = state.astype(BF16)
    outs, incs = [], []
    for pair in range(GLA_HEADS // 2):
        lanes = slice(pair * 128, (pair + 1) * 128)
        qt_p, kt_p, ke_p = qt[:, lanes], kt[:, lanes], ke[:, lanes]
        s_p = state_b[pair * 128:(pair + 1) * 128, :]
        for hh in range(2):
            hd = 2 * pair + hh
            qt_m = jnp.where(half_lo if hh == 0 else ~half_lo, qt_p, jnp.zeros_like(qt_p))
            a = jnp.where(tri, _dot_nt(qt_m, kt_p), 0.0).astype(BF16)
            v_h = vc[:, hd * GLA_DV:(hd + 1) * GLA_DV]
            outs.append(_dot(a, v_h) + _dot(qt_m, s_p))
            u = _dot_tn(ke_p, v_h)
            incs.append(u[hh * GLA_DK:(hh + 1) * GLA_DK, :])
    new_state = decay * state + jnp.concatenate(incs, axis=0)
    return jnp.concatenate(outs, axis=1), new_state


def _gla_kernel(q_ref, k_ref, v_ref, g_ref, gate_ref, sf0_ref, sb0_ref, gn_ref,
                o_ref, sf_ref, sb_ref, of_scr):
    C = GLA_CHUNK
    n = q_ref.shape[1] // C
    row = lax.broadcasted_iota(jnp.int32, (C, C), 0)
    col = lax.broadcasted_iota(jnp.int32, (C, C), 1)
    half_lo = lax.broadcasted_iota(jnp.int32, (C, 128), 1) < GLA_DK
    refs = (q_ref, k_ref, v_ref, g_ref)

    def fwd_body(i, state):
        o, state = _gla_chunk(i, state, True, *refs, row >= col, half_lo)
        of_scr[pl.ds(pl.multiple_of(i * C, C), C), :] = o
        return state

    sf_ref[0] = lax.fori_loop(0, n, fwd_body, sf0_ref[0])

    def bwd_body(i, state):
        c = n - 1 - i
        o, state = _gla_chunk(c, state, False, *refs, row <= col, half_lo)
        rows = pl.ds(pl.multiple_of(c * C, C), C)
        o = o + of_scr[rows, :]
        gate = gate_ref[0, rows, :].astype(F32)
        gn = gn_ref[...]
        for hd in range(GLA_HEADS):
            sl = slice(hd * GLA_DV, (hd + 1) * GLA_DV)
            o_ref[0, rows, sl] = (_rms(o[:, sl], gn) * gate[:, sl]).astype(BF16)
        return state

    sb_ref[0] = lax.fori_loop(0, n, bwd_body, sb0_ref[0])


def _gla(q, k, v, g, gate, s_f0, s_b0, gla_g):
    b, t, _ = q.shape
    seq = lambda w: pl.BlockSpec((1, t, w), lambda i: (i, 0, 0))
    st = pl.BlockSpec((1, D_QK, GLA_DV), lambda i: (i, 0, 0))
    return pl.pallas_call(
        _gla_kernel,
        grid=(b,),
        in_specs=[seq(D_QK), seq(D_QK), seq(D_V), seq(2 * D_QK), seq(D_V), st, st,
                  pl.BlockSpec((1, GLA_DV), lambda i: (0, 0))],
        out_specs=[seq(D_V), st, st],
        out_shape=[jax.ShapeDtypeStruct((b, t, D_V), BF16),
                   jax.ShapeDtypeStruct((b, D_QK, GLA_DV), F32),
                   jax.ShapeDtypeStruct((b, D_QK, GLA_DV), F32)],
        scratch_shapes=[pltpu.VMEM((t, D_V), F32)],
        compiler_params=pltpu.CompilerParams(
            dimension_semantics=("arbitrary",), vmem_limit_bytes=VMEM_LIMIT),
        name="gla",
    )(q, k, v, g, gate, s_f0, s_b0, gla_g)


def _attn_kernel(q_ref, k_ref, v_ref, o_ref):
    tq = q_ref.shape[1]
    q = q_ref[...].reshape(GROUP * tq, HEAD_DIM)
    s = _dot_nt(q, k_ref[0])
    p = jnp.exp(s - jnp.max(s, axis=-1, keepdims=True))
    o = _dot(p.astype(BF16), v_ref[0]) / jnp.sum(p, axis=-1, keepdims=True)
    o = o.reshape(GROUP, tq, HEAD_DIM)
    o_ref[...] = jnp.concatenate([o[i] for i in range(GROUP)], axis=-1).astype(BF16)


def _attention(q, k, v, batch, tq):
    ntok = q.shape[1]
    t = ntok // batch
    tk = k.shape[1] // batch
    nq = t // tq
    kv = pl.BlockSpec((1, tk, HEAD_DIM), lambda b, j, i: (j, b, 0))
    return pl.pallas_call(
        _attn_kernel,
        grid=(batch, ATT_KV_HEADS, nq),
        in_specs=[pl.BlockSpec((GROUP, tq, HEAD_DIM), lambda b, j, i: (j, b * nq + i, 0)), kv, kv],
        out_specs=pl.BlockSpec((tq, GROUP * HEAD_DIM), lambda b, j, i: (b * nq + i, j)),
        out_shape=jax.ShapeDtypeStruct((ntok, D_QA), BF16),
        compiler_params=pltpu.CompilerParams(
            dimension_semantics=("arbitrary",) * 3, vmem_limit_bytes=VMEM_LIMIT),
        name="attention",
    )(q, k, v)


def _post_kernel(og_ref, oa_ref, x_ref, mod_ref, n2_ref, fg_ref, wo_ref, w1_ref, w3_ref, w2_ref,
                 y_ref):
    D = D_MODEL
    mod = mod_ref[0]
    g1, sh2, sc2, g2 = (mod[:, i * D:(i + 1) * D] for i in range(2, 6))
    out = _dot(og_ref[...], wo_ref[0:D_V, :]) + _dot(oa_ref[...], wo_ref[D_V:D_V + D_QA, :])
    x1 = x_ref[...] + g1 * out
    hb = (_rms(x1, n2_ref[...]) * (1.0 + sc2) + sh2).astype(BF16)
    act = (_silu(_dot(hb, w1_ref[...])) * _dot(hb, w3_ref[...])).astype(BF16)
    x2 = x1 + g2 * _dot(act, w2_ref[...])
    y_ref[...] = _rms(x2, fg_ref[...])


def _post_mixer(o_gla, o_att, x, mod3, mod_row0, tokens_per_mod, weights, tm):
    ntok = x.shape[0]
    tiles_per_mod = tokens_per_mod // tm
    whole = pl.BlockSpec(memory_space=pltpu.VMEM)
    row = lambda w: pl.BlockSpec((tm, w), lambda i: (i, 0))
    return pl.pallas_call(
        _post_kernel,
        grid=(ntok // tm,),
        in_specs=[row(D_V), row(D_QA), row(D_MODEL),
                  pl.BlockSpec((1, 1, N_MOD * D_MODEL), lambda i: (mod_row0 + i // tiles_per_mod, 0, 0)),
                  whole, whole, whole, whole, whole, whole],
        out_specs=row(D_MODEL),
        out_shape=jax.ShapeDtypeStruct((ntok, D_MODEL), F32),
        compiler_params=pltpu.CompilerParams(
            dimension_semantics=("arbitrary",), vmem_limit_bytes=VMEM_LIMIT),
        name="post_mixer",
    )(o_gla, o_att, x, mod3, *weights)


def _rope_tables(t):
    rows = t // GRID_W
    row = jnp.repeat(jnp.arange(rows, dtype=F32), GRID_W)
    col = jnp.tile(jnp.arange(GRID_W, dtype=F32), rows)
    inv = ROPE_THETA ** (-jnp.arange(0, ROPE_AXIS_DIM, 2, dtype=F32) / ROPE_AXIS_DIM)
    ang_r = row[:, None] * inv
    ang_c = col[:, None] * inv
    cos = jnp.concatenate([jnp.cos(ang_r)] * 2 + [jnp.cos(ang_c)] * 2, axis=-1)
    sin = jnp.concatenate([-jnp.sin(ang_r), jnp.sin(ang_r), -jnp.sin(ang_c), jnp.sin(ang_c)], axis=-1)
    return jnp.tile(cos, (1, ATT_HEADS)), jnp.tile(sin, (1, ATT_HEADS))


def _block_mean(width, group):
    return jnp.asarray(np.kron(np.eye(width // group), np.ones((group, group))) / group, BF16)


def kernel(x_prompt, x_sample, cache_k, cache_v, state_gla_fwd, state_gla_bwd, c, c_ctx,
           ada_w, ada_b, norm1_g, norm2_g, w_in, w_gk2, b_gk2, gla_norm_g, q_norm_g, k_norm_g,
           w_out, w_ffn1, w_ffn3, w_ffn2, final_g):
    B, T, D = x_prompt.shape
    BL, TL, _ = x_sample.shape
    TP = cache_k.shape[2]
    l = 0

    cond = jnp.zeros((COND_ROWS, D), F32).at[0].set(c_ctx).at[1:1 + BL].set(c)
    mod3 = _adaln(cond, ada_w[l], ada_b[l]).reshape(COND_ROWS, 1, N_MOD * D)

    offs = np.cumsum([0, D_QK, D_QK, D_V, D_V, 2 * GLA_LOWRANK, D_QA, D_KA, D_KA])
    seg = lambda i: w_in[l][:, offs[i]:offs[i + 1]]
    w_in_r = jnp.concatenate([seg(0), seg(1), seg(2), seg(3), seg(5), seg(6), seg(7), seg(4)],
                             axis=1).astype(BF16)
    zero = jnp.zeros((GLA_LOWRANK, D_QK), F32)
    w_gk = jnp.concatenate([jnp.concatenate([w_gk2[l, 0], zero], axis=1),
                            jnp.concatenate([zero, w_gk2[l, 1]], axis=1)], axis=0).astype(BF16)
    pre_w = (norm1_g[l].reshape(1, D), w_in_r, w_gk, b_gk2[l].reshape(1, 2 * D_QK),
             jnp.tile(q_norm_g[l], ATT_HEADS).reshape(1, D_QA),
             jnp.tile(k_norm_g[l], ATT_KV_HEADS).reshape(1, D_KA),
             _block_mean(D_QA, HEAD_DIM), _block_mean(D_KA, HEAD_DIM))
    post_w = (norm2_g[l].reshape(1, D), final_g.reshape(1, D), w_out[l].astype(BF16),
              w_ffn1[l].astype(BF16), w_ffn3[l].astype(BF16), w_ffn2[l].astype(BF16))
    gla_g = gla_norm_g[l].reshape(1, GLA_DV)
    tm = 512

    xc = x_prompt.reshape(B * T, D)
    q, k, v, gate, g, qa, ka, va, kc, vc = _pre_mixer(
        xc, mod3, 0, B * T, T, pre_w, None, True, tm)
    zeros = jnp.zeros((B, D_QK, GLA_DV), F32)
    r3 = lambda a, b, t: a.reshape(b, t, a.shape[-1])
    o_gla, s_f, s_b = _gla(r3(q, B, T), r3(k, B, T), r3(v, B, T), r3(g, B, T), r3(gate, B, T),
                           zeros, zeros, gla_g)
    o_att = _attention(qa, ka, va, B, T)
    y_prompt = _post_mixer(o_gla.reshape(B * T, D_V), o_att, xc, mod3, 0, B * T, post_w, tm)

    xl = x_sample.reshape(BL * TL, D)
    q, k, v, gate, g, qa, ka, va = _pre_mixer(
        xl, mod3, 1, TL, TL, pre_w, _rope_tables(TL), False, tm)
    o_gla, _, _ = _gla(r3(q, BL, TL), r3(k, BL, TL), r3(v, BL, TL), r3(g, BL, TL), r3(gate, BL, TL),
                       state_gla_fwd[:, l].reshape(BL, D_QK, GLA_DV),
                       state_gla_bwd[:, l].reshape(BL, D_QK, GLA_DV), gla_g)

    def with_cache(new, cache):
        new = new.reshape(ATT_KV_HEADS, BL, TL, HEAD_DIM)
        old = jnp.transpose(cache[:, l], (2, 0, 1, 3)).astype(BF16)
        return jnp.concatenate([new, old], axis=2).reshape(ATT_KV_HEADS, BL * (TL + TP), HEAD_DIM)

    o_att = _attention(qa, with_cache(ka, cache_k), with_cache(va, cache_v), BL, 128)
    y_sample = _post_mixer(o_gla.reshape(BL * TL, D_V), o_att, xl, mod3, 1, TL, post_w, tm)

    return (y_prompt.reshape(B, T, D), y_sample.reshape(BL, TL, D),
            kc.reshape(B, 1, T, ATT_KV_HEADS, HEAD_DIM), vc.reshape(B, 1, T, ATT_KV_HEADS, HEAD_DIM),
            s_f.reshape(B, 1, GLA_HEADS, GLA_DK, GLA_DV), s_b.reshape(B, 1, GLA_HEADS, GLA_DK, GLA_DV))
```

```python
import functools

import numpy as np
import jax
import jax.numpy as jnp
from jax import lax
from jax.experimental import pallas as pl
from jax.experimental.pallas import tpu as pltpu

F32 = jnp.float32
BF16 = jnp.bfloat16

D_MODEL = 1024
GRID_W = 64
GLA_HEADS = 4
GLA_DK = 64
GLA_DV = 128
GLA_LOWRANK = 16
GLA_GATE_NORM = 16.0
GLA_CHUNK = 64
ATT_HEADS = 8
ATT_KV_HEADS = 2
HEAD_DIM = 64
ROPE_AXIS_DIM = HEAD_DIM // 2
ROPE_THETA = 10000.0
D_FF = -(-8 * D_MODEL // (3 * 256)) * 256
N_MOD = 6
EPS = 1e-6

D_QK = GLA_HEADS * GLA_DK
D_V = GLA_HEADS * GLA_DV
D_QA = ATT_HEADS * HEAD_DIM
D_KA = ATT_KV_HEADS * HEAD_DIM
GROUP = ATT_HEADS // ATT_KV_HEADS
O_Q, O_K, O_V, O_OG = 0, D_QK, 2 * D_QK, 2 * D_QK + D_V
O_QA = O_OG + D_V
O_KA = O_QA + D_QA
O_VA = O_KA + D_KA
O_LR = O_VA + D_KA
D_PROJ = O_LR + 2 * GLA_LOWRANK

COND_ROWS = 8
VMEM_LIMIT = 56 * 1024 * 1024


def _dot(a, b):
    return jnp.dot(a, b, preferred_element_type=F32)


def _dot_nt(a, b):
    return lax.dot_general(a, b, (((1,), (1,)), ((), ())), preferred_element_type=F32)


def _dot_tn(a, b):
    return lax.dot_general(a, b, (((0,), (0,)), ((), ())), preferred_element_type=F32)


def _rms(x, g):
    ms = jnp.mean(x * x, axis=-1, keepdims=True)
    return x * lax.rsqrt(ms + EPS) * g


def _silu(x):
    return x * jax.nn.sigmoid(x)


def _adaln_kernel(c_ref, w_ref, b_ref, o_ref):
    s = _silu(c_ref[...])
    o_ref[...] = _dot(s.astype(BF16), w_ref[...].astype(BF16)) + b_ref[...]


def _adaln(cond, ada_w, ada_b):
    n = ada_w.shape[1]
    tn = 1536
    return pl.pallas_call(
        _adaln_kernel,
        grid=(n // tn,),
        in_specs=[
            pl.BlockSpec((COND_ROWS, D_MODEL), lambda j: (0, 0)),
            pl.BlockSpec((D_MODEL, tn), lambda j: (0, j)),
            pl.BlockSpec((1, tn), lambda j: (0, j)),
        ],
        out_specs=pl.BlockSpec((COND_ROWS, tn), lambda j: (0, j)),
        out_shape=jax.ShapeDtypeStruct((COND_ROWS, n), F32),
        compiler_params=pltpu.CompilerParams(
            dimension_semantics=("arbitrary",), vmem_limit_bytes=VMEM_LIMIT),
        name="adaln",
    )(cond, ada_w, ada_b.reshape(1, n))


def _swap_halves(x, lane_lo):
    n = x.shape[-1]
    up = pltpu.roll(x, n - ROPE_AXIS_DIM // 2, axis=1)
    dn = pltpu.roll(x, ROPE_AXIS_DIM // 2, axis=1)
    return jnp.where(lane_lo, up, dn)


def _pre_kernel(*refs, rope, emit_cache):
    it = iter(refs)
    x_ref, mod_ref, n1_ref, win_ref, wgk_ref, bgk_ref = (next(it) for _ in range(6))
    qg_ref, kg_ref, bdq_ref, bdk_ref = (next(it) for _ in range(4))
    if rope:
        cos_ref, sin_ref = next(it), next(it)
    q_o, k_o, v_o, gate_o, g_o, qa_o, ka_o, va_o = (next(it) for _ in range(8))
    if emit_cache:
        kc_o, vc_o = next(it), next(it)

    mod = mod_ref[0]
    sh1 = mod[:, 0:D_MODEL]
    sc1 = mod[:, D_MODEL:2 * D_MODEL]
    h = _rms(x_ref[...], n1_ref[...]) * (1.0 + sc1) + sh1
    p = _dot(h.astype(BF16), win_ref[...])

    q_o[...] = p[:, O_Q:O_Q + D_QK].astype(BF16)
    k_o[...] = p[:, O_K:O_K + D_QK].astype(BF16)
    v_o[...] = p[:, O_V:O_V + D_V].astype(BF16)
    gate_o[...] = _silu(p[:, O_OG:O_OG + D_V]).astype(BF16)

    gk = _dot(p[:, O_LR:O_LR + 2 * GLA_LOWRANK].astype(BF16), wgk_ref[...]) + bgk_ref[...]
    g_o[...] = (jnp.minimum(gk, 0.0) - jnp.log(1.0 + jnp.exp(-jnp.abs(gk)))) * (1.0 / GLA_GATE_NORM)

    qa = p[:, O_QA:O_QA + D_QA]
    ka = p[:, O_KA:O_KA + D_KA]
    va = p[:, O_VA:O_VA + D_KA]
    qa = qa * lax.rsqrt(_dot((qa * qa).astype(BF16), bdq_ref[...]) + EPS) * qg_ref[...]
    ka = ka * lax.rsqrt(_dot((ka * ka).astype(BF16), bdk_ref[...]) + EPS) * kg_ref[...]
    if rope:
        lane_q = lax.broadcasted_iota(jnp.int32, qa.shape, 1)
        lane_k = lax.broadcasted_iota(jnp.int32, ka.shape, 1)
        lo_q = (lane_q % ROPE_AXIS_DIM) < ROPE_AXIS_DIM // 2
        lo_k = (lane_k % ROPE_AXIS_DIM) < ROPE_AXIS_DIM // 2
        cos = cos_ref[...]
        sin = sin_ref[...]
        qa = qa * cos + _swap_halves(qa, lo_q) * sin
        ka = ka * cos[:, :D_KA] + _swap_halves(ka, lo_k) * sin[:, :D_KA]
    if emit_cache:
        kc_o[...] = ka
        vc_o[...] = va
    qa_o[...] = (qa * (HEAD_DIM ** -0.5 * LOG2E)).T.astype(BF16)
    ka_o[...] = ka.astype(BF16)
    va_o[...] = va.T.astype(BF16)


def _pre_mixer(x, mod3, mod_row0, tokens_per_mod, seq_len, weights, rope_tabs, emit_cache, tm):
    ntok = x.shape[0]
    n1, w_in_r, w_gk, b_gk, qg, kg, bdq, bdk = weights
    rope = rope_tabs is not None
    tiles_per_mod = tokens_per_mod // tm
    tiles_per_seq = seq_len // tm
    whole = pl.BlockSpec(memory_space=pltpu.VMEM)
    row = lambda w: pl.BlockSpec((tm, w), lambda i: (i, 0))
    in_specs = [
        row(D_MODEL),
        pl.BlockSpec((1, 1, N_MOD * D_MODEL), lambda i: (mod_row0 + i // tiles_per_mod, 0, 0)),
        whole, whole, whole, whole, whole, whole, whole, whole,
    ]
    args = [x, mod3, n1, w_in_r, w_gk, b_gk, qg, kg, bdq, bdk]
    if rope:
        tab = pl.BlockSpec((tm, D_QA), lambda i: (i % tiles_per_seq, 0))
        in_specs += [tab, tab]
        args += list(rope_tabs)
    col = lambda w: pl.BlockSpec((w, tm), lambda i: (0, i))
    out_specs = [row(D_QK), row(D_QK), row(D_V), row(D_V), row(2 * D_QK),
                 col(D_QA), row(D_KA), col(D_KA)]
    out_shape = [
        jax.ShapeDtypeStruct((ntok, D_QK), BF16),
        jax.ShapeDtypeStruct((ntok, D_QK), BF16),
        jax.ShapeDtypeStruct((ntok, D_V), BF16),
        jax.ShapeDtypeStruct((ntok, D_V), BF16),
        jax.ShapeDtypeStruct((ntok, 2 * D_QK), F32),
        jax.ShapeDtypeStruct((D_QA, ntok), BF16),
        jax.ShapeDtypeStruct((ntok, D_KA), BF16),
        jax.ShapeDtypeStruct((D_KA, ntok), BF16),
    ]
    if emit_cache:
        out_specs += [row(D_KA), row(D_KA)]
        out_shape += [jax.ShapeDtypeStruct((ntok, D_KA), F32)] * 2
    return pl.pallas_call(
        functools.partial(_pre_kernel, rope=rope, emit_cache=emit_cache),
        grid=(ntok // tm,),
        in_specs=in_specs,
        out_specs=out_specs,
        out_shape=out_shape,
        compiler_params=pltpu.CompilerParams(
            dimension_semantics=("arbitrary",), vmem_limit_bytes=VMEM_LIMIT),
        name="pre_mixer_rope" if rope else "pre_mixer",
    )(*args)


def _gla_chunk(c, state, fwd, q_ref, k_ref, v_ref, g_ref, tri, half_lo):
    C = GLA_CHUNK
    r0 = pl.multiple_of(c * C, C)
    rows = pl.ds(r0, C)
    qc = q_ref[0, rows, :].astype(F32) * GLA_DK ** -0.5
    kc = k_ref[0, rows, :].astype(F32)
    vc = v_ref[0, rows, :]
    gc = g_ref[0, rows, 0:D_QK] if fwd else g_ref[0, rows, D_QK:2 * D_QK]
    ltri = jnp.where(tri, 1.0, 0.0).astype(BF16)
    g_hi = gc.astype(BF16)
    g_lo = (gc - g_hi.astype(F32)).astype(BF16)
    b = _dot(ltri, g_hi) + _dot(ltri, g_lo)
    btot = b[C - 1:C, :] if fwd else b[0:1, :]
    qt = (qc * jnp.exp(b)).astype(BF16)
    kt = (kc * jnp.exp(-b)).astype(BF16)
    ke = (kc * jnp.exp(btot - b)).astype(BF16)
    ones = jnp.ones((C, GLA_DV), BF16)
    decay = jnp.exp(_dot_tn(g_hi, ones) + _dot_tn(g_lo, ones))
    state_b = state.astype(BF16)
    outs, incs = [], []
    for pair in range(GLA_HEADS // 2):
        lanes = slice(pair * 128, (pair + 1) * 128)
        qt_p, kt_p, ke_p = qt[:, lanes], kt[:, lanes], ke[:, lanes]
        s_p = state_b[pair * 128:(pair + 1) * 128, :]
        for hh in range(2):
            hd = 2 * pair + hh
            qt_m = jnp.where(half_lo if hh == 0 else ~half_lo, qt_p, jnp.zeros_like(qt_p))
            a = jnp.where(tri, _dot_nt(qt_m, kt_p), 0.0).astype(BF16)
            v_h = vc[:, hd * GLA_DV:(hd + 1) * GLA_DV]
            outs.append(_dot(a, v_h) + _dot(qt_m, s_p))
            u = _dot_tn(ke_p, v_h)
            incs.append(u[hh * GLA_DK:(hh + 1) * GLA_DK, :])
    new_state = decay * state + jnp.concatenate(incs, axis=0)
    return jnp.concatenate(outs, axis=1), new_state


def _gla_kernel(q_ref, k_ref, v_ref, g_ref, gate_ref, sf0_ref, sb0_ref, gn_ref,
                o_ref, sf_ref, sb_ref, of_scr):
    C = GLA_CHUNK
    n = q_ref.shape[1] // C
    row = lax.broadcasted_iota(jnp.int32, (C, C), 0)
    col = lax.broadcasted_iota(jnp.int32, (C, C), 1)
    half_lo = lax.broadcasted_iota(jnp.int32, (C, 128), 1) < GLA_DK
    refs = (q_ref, k_ref, v_ref, g_ref)

    def fwd_body(i, state):
        o, state = _gla_chunk(i, state, True, *refs, row >= col, half_lo)
        of_scr[pl.ds(pl.multiple_of(i * C, C), C), :] = o
        return state

    sf_ref[0] = lax.fori_loop(0, n, fwd_body, sf0_ref[0])

    def bwd_body(i, state):
        c = n - 1 - i
        o, state = _gla_chunk(c, state, False, *refs, row <= col, half_lo)
        rows = pl.ds(pl.multiple_of(c * C, C), C)
        o = o + of_scr[rows, :]
        gate = gate_ref[0, rows, :].astype(F32)
        gn = gn_ref[...]
        for hd in range(GLA_HEADS):
            sl = slice(hd * GLA_DV, (hd + 1) * GLA_DV)
            o_ref[0, rows, sl] = (_rms(o[:, sl], gn) * gate[:, sl]).astype(BF16)
        return state

    sb_ref[0] = lax.fori_loop(0, n, bwd_body, sb0_ref[0])


def _gla(q, k, v, g, gate, s_f0, s_b0, gla_g):
    b, t, _ = q.shape
    seq = lambda w: pl.BlockSpec((1, t, w), lambda i: (i, 0, 0))
    st = pl.BlockSpec((1, D_QK, GLA_DV), lambda i: (i, 0, 0))
    return pl.pallas_call(
        _gla_kernel,
        grid=(b,),
        in_specs=[seq(D_QK), seq(D_QK), seq(D_V), seq(2 * D_QK), seq(D_V), st, st,
                  pl.BlockSpec((1, GLA_DV), lambda i: (0, 0))],
        out_specs=[seq(D_V), st, st],
        out_shape=[jax.ShapeDtypeStruct((b, t, D_V), BF16),
                   jax.ShapeDtypeStruct((b, D_QK, GLA_DV), F32),
                   jax.ShapeDtypeStruct((b, D_QK, GLA_DV), F32)],
        scratch_shapes=[pltpu.VMEM((t, D_V), F32)],
        compiler_params=pltpu.CompilerParams(
            dimension_semantics=("arbitrary",), vmem_limit_bytes=VMEM_LIMIT),
        name="gla",
    )(q, k, v, g, gate, s_f0, s_b0, gla_g)


ATT_KEY_BLOCK = 128
ATT_Q_TILE = 256
LOG2E = 1.4426950408889634


def _attn_kernel(qt_ref, k_ref, vt_ref, o_ref):
    tk = k_ref.shape[0]
    tq = qt_ref.shape[1]
    kv_head = pl.program_id(1)
    own = lax.broadcasted_iota(jnp.int32, (D_KA, 1), 0) // HEAD_DIM == kv_head
    cols = []
    for g in range(GROUP):
        qt = qt_ref[g * HEAD_DIM:(g + 1) * HEAD_DIM, :]
        qt2 = jnp.concatenate([qt] * ATT_KV_HEADS, axis=0)
        cols.append(jnp.where(own, qt2, jnp.zeros_like(qt2)))
    qt_all = jnp.concatenate(cols, axis=1)
    s = _dot(k_ref[...], qt_all)
    p = jnp.exp2(s - jnp.max(s, axis=0, keepdims=True)).astype(BF16)
    vt1 = jnp.concatenate([vt_ref[...], jnp.ones((16, tk), BF16)], axis=0)
    acc = _dot(vt1, p)
    o = acc[:HEAD_DIM] / acc[HEAD_DIM:HEAD_DIM + 1]
    for g in range(GROUP):
        o_ref[:, g * HEAD_DIM:(g + 1) * HEAD_DIM] = o[:, g * tq:(g + 1) * tq].T.astype(BF16)


def _attention(qt, k, vt, batch):
    ntok = qt.shape[1]
    tk = k.shape[0] // batch
    tq = ATT_Q_TILE
    nq = ntok // batch // tq
    return pl.pallas_call(
        _attn_kernel,
        grid=(batch, ATT_KV_HEADS, nq),
        in_specs=[pl.BlockSpec((GROUP * HEAD_DIM, tq), lambda b, j, i: (j, b * nq + i)),
                  pl.BlockSpec((tk, D_KA), lambda b, j, i: (b, 0)),
                  pl.BlockSpec((HEAD_DIM, tk), lambda b, j, i: (j, b))],
        out_specs=pl.BlockSpec((tq, GROUP * HEAD_DIM), lambda b, j, i: (b * nq + i, j)),
        out_shape=jax.ShapeDtypeStruct((ntok, D_QA), BF16),
        compiler_params=pltpu.CompilerParams(
            dimension_semantics=("arbitrary",) * 3, vmem_limit_bytes=VMEM_LIMIT),
        name="attention",
    )(qt, k, vt)


def _post_kernel(og_ref, oa_ref, x_ref, mod_ref, n2_ref, fg_ref, wo_ref, w1_ref, w3_ref, w2_ref,
                 y_ref):
    D = D_MODEL
    mod = mod_ref[0]
    g1, sh2, sc2, g2 = (mod[:, i * D:(i + 1) * D] for i in range(2, 6))
    out = _dot(og_ref[...], wo_ref[0:D_V, :]) + _dot(oa_ref[...], wo_ref[D_V:D_V + D_QA, :])
    x1 = x_ref[...] + g1 * out
    hb = (_rms(x1, n2_ref[...]) * (1.0 + sc2) + sh2).astype(BF16)
    act = (_silu(_dot(hb, w1_ref[...])) * _dot(hb, w3_ref[...])).astype(BF16)
    x2 = x1 + g2 * _dot(act, w2_ref[...])
    y_ref[...] = _rms(x2, fg_ref[...])


def _post_mixer(o_gla, o_att, x, mod3, mod_row0, tokens_per_mod, weights, tm):
    ntok = x.shape[0]
    tiles_per_mod = tokens_per_mod // tm
    whole = pl.BlockSpec(memory_space=pltpu.VMEM)
    row = lambda w: pl.BlockSpec((tm, w), lambda i: (i, 0))
    return pl.pallas_call(
        _post_kernel,
        grid=(ntok // tm,),
        in_specs=[row(D_V), row(D_QA), row(D_MODEL),
                  pl.BlockSpec((1, 1, N_MOD * D_MODEL), lambda i: (mod_row0 + i // tiles_per_mod, 0, 0)),
                  whole, whole, whole, whole, whole, whole],
        out_specs=row(D_MODEL),
        out_shape=jax.ShapeDtypeStruct((ntok, D_MODEL), F32),
        compiler_params=pltpu.CompilerParams(
            dimension_semantics=("arbitrary",), vmem_limit_bytes=VMEM_LIMIT),
        name="post_mixer",
    )(o_gla, o_att, x, mod3, *weights)


def _rope_tables(t):
    rows = t // GRID_W
    row = jnp.repeat(jnp.arange(rows, dtype=F32), GRID_W)
    col = jnp.tile(jnp.arange(GRID_W, dtype=F32), rows)
    inv = ROPE_THETA ** (-jnp.arange(0, ROPE_AXIS_DIM, 2, dtype=F32) / ROPE_AXIS_DIM)
    ang_r = row[:, None] * inv
    ang_c = col[:, None] * inv
    cos = jnp.concatenate([jnp.cos(ang_r)] * 2 + [jnp.cos(ang_c)] * 2, axis=-1)
    sin = jnp.concatenate([-jnp.sin(ang_r), jnp.sin(ang_r), -jnp.sin(ang_c), jnp.sin(ang_c)], axis=-1)
    return jnp.tile(cos, (1, ATT_HEADS)), jnp.tile(sin, (1, ATT_HEADS))


def _block_mean(width, group):
    return jnp.asarray(np.kron(np.eye(width // group), np.ones((group, group))) / group, BF16)


def kernel(x_prompt, x_sample, cache_k, cache_v, state_gla_fwd, state_gla_bwd, c, c_ctx,
           ada_w, ada_b, norm1_g, norm2_g, w_in, w_gk2, b_gk2, gla_norm_g, q_norm_g, k_norm_g,
           w_out, w_ffn1, w_ffn3, w_ffn2, final_g):
    B, T, D = x_prompt.shape
    BL, TL, _ = x_sample.shape
    TP = cache_k.shape[2]
    l = 0

    cond = jnp.zeros((COND_ROWS, D), F32).at[0].set(c_ctx).at[1:1 + BL].set(c)
    mod3 = _adaln(cond, ada_w[l], ada_b[l]).reshape(COND_ROWS, 1, N_MOD * D)

    offs = np.cumsum([0, D_QK, D_QK, D_V, D_V, 2 * GLA_LOWRANK, D_QA, D_KA, D_KA])
    seg = lambda i: w_in[l][:, offs[i]:offs[i + 1]]
    w_in_r = jnp.concatenate([seg(0), seg(1), seg(2), seg(3), seg(5), seg(6), seg(7), seg(4)],
                             axis=1).astype(BF16)
    zero = jnp.zeros((GLA_LOWRANK, D_QK), F32)
    w_gk = jnp.concatenate([jnp.concatenate([w_gk2[l, 0], zero], axis=1),
                            jnp.concatenate([zero, w_gk2[l, 1]], axis=1)], axis=0).astype(BF16)
    pre_w = (norm1_g[l].reshape(1, D), w_in_r, w_gk, b_gk2[l].reshape(1, 2 * D_QK),
             jnp.tile(q_norm_g[l], ATT_HEADS).reshape(1, D_QA),
             jnp.tile(k_norm_g[l], ATT_KV_HEADS).reshape(1, D_KA),
             _block_mean(D_QA, HEAD_DIM), _block_mean(D_KA, HEAD_DIM))
    post_w = (norm2_g[l].reshape(1, D), final_g.reshape(1, D), w_out[l].astype(BF16),
              w_ffn1[l].astype(BF16), w_ffn3[l].astype(BF16), w_ffn2[l].astype(BF16))
    gla_g = gla_norm_g[l].reshape(1, GLA_DV)
    tm = 512

    xc = x_prompt.reshape(B * T, D)
    q, k, v, gate, g, qa, ka, va, kc, vc = _pre_mixer(
        xc, mod3, 0, B * T, T, pre_w, None, True, tm)
    zeros = jnp.zeros((B, D_QK, GLA_DV), F32)
    r3 = lambda a, b, t: a.reshape(b, t, a.shape[-1])
    o_gla, s_f, s_b = _gla(r3(q, B, T), r3(k, B, T), r3(v, B, T), r3(g, B, T), r3(gate, B, T),
                           zeros, zeros, gla_g)
    o_att = _attention(qa, ka, va, B)
    y_prompt = _post_mixer(o_gla.reshape(B * T, D_V), o_att, xc, mod3, 0, B * T, post_w, tm)

    xl = x_sample.reshape(BL * TL, D)
    q, k, v, gate, g, qa, ka, va = _pre_mixer(
        xl, mod3, 1, TL, TL, pre_w, _rope_tables(TL), False, tm)
    o_gla, _, _ = _gla(r3(q, BL, TL), r3(k, BL, TL), r3(v, BL, TL), r3(g, BL, TL), r3(gate, BL, TL),
                       state_gla_fwd[:, l].reshape(BL, D_QK, GLA_DV),
                       state_gla_bwd[:, l].reshape(BL, D_QK, GLA_DV), gla_g)

    k_old = cache_k[:, l].reshape(BL, TP, D_KA).astype(BF16)
    k_all = jnp.concatenate([ka.reshape(BL, TL, D_KA), k_old], axis=1).reshape(BL * (TL + TP), D_KA)
    vt_new = va.reshape(D_KA, BL, TL)
    vt_old = jnp.transpose(cache_v[:, l], (2, 3, 0, 1)).reshape(D_KA, BL, TP).astype(BF16)
    vt_all = jnp.concatenate([vt_new, vt_old], axis=2).reshape(D_KA, BL * (TL + TP))
    o_att = _attention(qa, k_all, vt_all, BL)
    y_sample = _post_mixer(o_gla.reshape(BL * TL, D_V), o_att, xl, mod3, 1, TL, post_w, tm)

    return (y_prompt.reshape(B, T, D), y_sample.reshape(BL, TL, D),
            kc.reshape(B, 1, T, ATT_KV_HEADS, HEAD_DIM), vc.reshape(B, 1, T, ATT_KV_HEADS, HEAD_DIM),
            s_f.reshape(B, 1, GLA_HEADS, GLA_DK, GLA_DV), s_b.reshape(B, 1, GLA_HEADS, GLA_DK, GLA_DV))
```

```python
import functools

import numpy as np
import jax
import jax.numpy as jnp
from jax import lax
from jax.experimental import pallas as pl
from jax.experimental.pallas import tpu as pltpu

F32 = jnp.float32
BF16 = jnp.bfloat16

D_MODEL = 1024
GRID_W = 64
GLA_HEADS = 4
GLA_DK = 64
GLA_DV = 128
GLA_LOWRANK = 16
GLA_GATE_NORM = 16.0
GLA_CHUNK = 64
ATT_HEADS = 8
ATT_KV_HEADS = 2
HEAD_DIM = 64
ROPE_AXIS_DIM = HEAD_DIM // 2
ROPE_THETA = 10000.0
D_FF = -(-8 * D_MODEL // (3 * 256)) * 256
N_MOD = 6
EPS = 1e-6

D_QK = GLA_HEADS * GLA_DK
D_V = GLA_HEADS * GLA_DV
D_QA = ATT_HEADS * HEAD_DIM
D_KA = ATT_KV_HEADS * HEAD_DIM
GROUP = ATT_HEADS // ATT_KV_HEADS
O_Q, O_K, O_V, O_OG = 0, D_QK, 2 * D_QK, 2 * D_QK + D_V
O_QA = O_OG + D_V
O_KA = O_QA + D_QA
O_VA = O_KA + D_KA
O_LR = O_VA + D_KA
D_PROJ = O_LR + 2 * GLA_LOWRANK

COND_ROWS = 8
VMEM_LIMIT = 56 * 1024 * 1024


def _dot(a, b):
    return jnp.dot(a, b, preferred_element_type=F32)


def _dot_nt(a, b):
    return lax.dot_general(a, b, (((1,), (1,)), ((), ())), preferred_element_type=F32)


def _dot_tn(a, b):
    return lax.dot_general(a, b, (((0,), (0,)), ((), ())), preferred_element_type=F32)


def _rms(x, g):
    ms = jnp.mean(x * x, axis=-1, keepdims=True)
    return x * lax.rsqrt(ms + EPS) * g


def _silu(x):
    return x * jax.nn.sigmoid(x)


def _adaln_kernel(c_ref, w_ref, b_ref, o_ref):
    s = _silu(c_ref[...])
    o_ref[...] = _dot(s.astype(BF16), w_ref[...].astype(BF16)) + b_ref[...]


def _adaln(cond, ada_w, ada_b):
    n = ada_w.shape[1]
    tn = 1536
    return pl.pallas_call(
        _adaln_kernel,
        grid=(n // tn,),
        in_specs=[
            pl.BlockSpec((COND_ROWS, D_MODEL), lambda j: (0, 0)),
            pl.BlockSpec((D_MODEL, tn), lambda j: (0, j)),
            pl.BlockSpec((1, tn), lambda j: (0, j)),
        ],
        out_specs=pl.BlockSpec((COND_ROWS, tn), lambda j: (0, j)),
        out_shape=jax.ShapeDtypeStruct((COND_ROWS, n), F32),
        compiler_params=pltpu.CompilerParams(
            dimension_semantics=("arbitrary",), vmem_limit_bytes=VMEM_LIMIT),
        name="adaln",
    )(cond, ada_w, ada_b.reshape(1, n))


def _swap_halves(x, lane_lo):
    n = x.shape[-1]
    up = pltpu.roll(x, n - ROPE_AXIS_DIM // 2, axis=1)
    dn = pltpu.roll(x, ROPE_AXIS_DIM // 2, axis=1)
    return jnp.where(lane_lo, up, dn)


def _pre_kernel(*refs, rope, emit_cache):
    it = iter(refs)
    x_ref, mod_ref, n1_ref, win_ref, wgk_ref, bgk_ref = (next(it) for _ in range(6))
    qg_ref, kg_ref, bdq_ref, bdk_ref = (next(it) for _ in range(4))
    if rope:
        cos_ref, sin_ref = next(it), next(it)
    q_o, k_o, v_o, gate_o, g_o, qa_o, ka_o, va_o = (next(it) for _ in range(8))
    if emit_cache:
        kc_o, vc_o = next(it), next(it)

    mod = mod_ref[0]
    sh1 = mod[:, 0:D_MODEL]
    sc1 = mod[:, D_MODEL:2 * D_MODEL]
    h = _rms(x_ref[...], n1_ref[...]) * (1.0 + sc1) + sh1
    p = _dot(h.astype(BF16), win_ref[...])

    q_o[...] = p[:, O_Q:O_Q + D_QK].astype(BF16)
    k_o[...] = p[:, O_K:O_K + D_QK].astype(BF16)
    v_o[...] = p[:, O_V:O_V + D_V].astype(BF16)
    gate_o[...] = _silu(p[:, O_OG:O_OG + D_V]).astype(BF16)

    gk = _dot(p[:, O_LR:O_LR + 2 * GLA_LOWRANK].astype(BF16), wgk_ref[...]) + bgk_ref[...]
    g_o[...] = (jnp.minimum(gk, 0.0) - jnp.log(1.0 + jnp.exp(-jnp.abs(gk)))) * (1.0 / GLA_GATE_NORM)

    qa = p[:, O_QA:O_QA + D_QA]
    ka = p[:, O_KA:O_KA + D_KA]
    va = p[:, O_VA:O_VA + D_KA]
    qa = qa * lax.rsqrt(_dot((qa * qa).astype(BF16), bdq_ref[...]) + EPS) * qg_ref[...]
    ka = ka * lax.rsqrt(_dot((ka * ka).astype(BF16), bdk_ref[...]) + EPS) * kg_ref[...]
    if rope:
        lane_q = lax.broadcasted_iota(jnp.int32, qa.shape, 1)
        lane_k = lax.broadcasted_iota(jnp.int32, ka.shape, 1)
        lo_q = (lane_q % ROPE_AXIS_DIM) < ROPE_AXIS_DIM // 2
        lo_k = (lane_k % ROPE_AXIS_DIM) < ROPE_AXIS_DIM // 2
        cos = cos_ref[...]
        sin = sin_ref[...]
        qa = qa * cos + _swap_halves(qa, lo_q) * sin
        ka = ka * cos[:, :D_KA] + _swap_halves(ka, lo_k) * sin[:, :D_KA]
    if emit_cache:
        kc_o[...] = ka
        vc_o[...] = va
    qa_o[...] = (qa * (HEAD_DIM ** -0.5 * LOG2E)).T.astype(BF16)
    ka_o[...] = ka.astype(BF16)
    va_o[...] = va.T.astype(BF16)


def _pre_mixer(x, mod3, mod_row0, tokens_per_mod, seq_len, weights, rope_tabs, emit_cache, tm):
    ntok = x.shape[0]
    n1, w_in_r, w_gk, b_gk, qg, kg, bdq, bdk = weights
    rope = rope_tabs is not None
    tiles_per_mod = tokens_per_mod // tm
    tiles_per_seq = seq_len // tm
    whole = pl.BlockSpec(memory_space=pltpu.VMEM)
    row = lambda w: pl.BlockSpec((tm, w), lambda i: (i, 0))
    in_specs = [
        row(D_MODEL),
        pl.BlockSpec((1, 1, N_MOD * D_MODEL), lambda i: (mod_row0 + i // tiles_per_mod, 0, 0)),
        whole, whole, whole, whole, whole, whole, whole, whole,
    ]
    args = [x, mod3, n1, w_in_r, w_gk, b_gk, qg, kg, bdq, bdk]
    if rope:
        tab = pl.BlockSpec((tm, D_QA), lambda i: (i % tiles_per_seq, 0))
        in_specs += [tab, tab]
        args += list(rope_tabs)
    col = lambda w: pl.BlockSpec((w, tm), lambda i: (0, i))
    out_specs = [row(D_QK), row(D_QK), row(D_V), row(D_V), row(2 * D_QK),
                 col(D_QA), row(D_KA), col(D_KA)]
    out_shape = [
        jax.ShapeDtypeStruct((ntok, D_QK), BF16),
        jax.ShapeDtypeStruct((ntok, D_QK), BF16),
        jax.ShapeDtypeStruct((ntok, D_V), BF16),
        jax.ShapeDtypeStruct((ntok, D_V), BF16),
        jax.ShapeDtypeStruct((ntok, 2 * D_QK), F32),
        jax.ShapeDtypeStruct((D_QA, ntok), BF16),
        jax.ShapeDtypeStruct((ntok, D_KA), BF16),
        jax.ShapeDtypeStruct((D_KA, ntok), BF16),
    ]
    if emit_cache:
        out_specs += [row(D_KA), row(D_KA)]
        out_shape += [jax.ShapeDtypeStruct((ntok, D_KA), F32)] * 2
    return pl.pallas_call(
        functools.partial(_pre_kernel, rope=rope, emit_cache=emit_cache),
        grid=(ntok // tm,),
        in_specs=in_specs,
        out_specs=out_specs,
        out_shape=out_shape,
        compiler_params=pltpu.CompilerParams(
            dimension_semantics=("arbitrary",), vmem_limit_bytes=VMEM_LIMIT),
        name="pre_mixer_rope" if rope else "pre_mixer",
    )(*args)


def _gla_chunks(units, q_ref, k_ref, v_ref, g_ref, half_lo):
    C = GLA_CHUNK
    n_pair = GLA_HEADS // 2
    rows = [pl.ds(pl.multiple_of(c * C, C), C) for (_, c, _, _, _, _) in units]

    g_his, bs = [], []
    for (bi, _, fwd, tri, _, _), r in zip(units, rows):
        gc = g_ref[bi, r, 0:D_QK] if fwd else g_ref[bi, r, D_QK:2 * D_QK]
        ltri = jnp.where(tri, 1.0, 0.0).astype(BF16)
        g_hi = gc.astype(BF16)
        g_lo = (gc - g_hi.astype(F32)).astype(BF16)
        g_his.append(g_hi)
        bs.append(_dot(ltri, g_hi) + _dot(ltri, g_lo))
    ones = jnp.ones((C, GLA_DV), BF16)
    decays = [jnp.exp(_dot_tn(g_hi, ones)) for g_hi in g_his]

    qts, kts, kes = [], [], []
    for (bi, _, fwd, _, _, _), r, b in zip(units, rows, bs):
        qc = q_ref[bi, r, :].astype(F32) * GLA_DK ** -0.5
        kc = k_ref[bi, r, :].astype(F32)
        btot = b[C - 1:C, :] if fwd else b[0:1, :]
        qts.append((qc * jnp.exp(b)).astype(BF16))
        kts.append((kc * jnp.exp(-b)).astype(BF16))
        kes.append((kc * jnp.exp(btot - b)).astype(BF16))

    scores, incs, qms = [], [], []
    for (bi, _, _, tri, _, _), r, qt, kt, ke in zip(units, rows, qts, kts, kes):
        for pair in range(n_pair):
            lanes = slice(pair * 128, (pair + 1) * 128)
            incs.append(_dot_tn(ke[:, lanes], v_ref[bi, r, pair * 2 * GLA_DV:(pair + 1) * 2 * GLA_DV]))
            for hh in range(2):
                qm = jnp.where(half_lo if hh == 0 else ~half_lo, qt[:, lanes], jnp.zeros((C, 128), BF16))
                qms.append(qm)
                scores.append(jnp.where(tri, _dot_nt(qm, kt[:, lanes]), 0.0).astype(BF16))

    for ui, ((bi, _, _, _, s_ref, o_ref), r) in enumerate(zip(units, rows)):
        state = s_ref[bi]
        state_b = state.astype(BF16)
        for hd in range(GLA_HEADS):
            pair = hd // 2
            lhs = jnp.concatenate([qms[ui * GLA_HEADS + hd], scores[ui * GLA_HEADS + hd]], axis=1)
            rhs = jnp.concatenate([state_b[pair * 128:(pair + 1) * 128, :],
                                   v_ref[bi, r, hd * GLA_DV:(hd + 1) * GLA_DV]], axis=0)
            o_ref[bi, r, hd * GLA_DV:(hd + 1) * GLA_DV] = _dot(lhs, rhs)
        inc = jnp.concatenate(
            [incs[ui * n_pair + hd // 2][(hd % 2) * GLA_DK:(hd % 2 + 1) * GLA_DK,
                                         (hd % 2) * GLA_DV:(hd % 2 + 1) * GLA_DV]
             for hd in range(GLA_HEADS)], axis=0)
        s_ref[bi] = decays[ui] * state + inc


def _gla_kernel(q_ref, k_ref, v_ref, g_ref, gate_ref, sf0_ref, sb0_ref, gn_ref,
                o_ref, sf_ref, sb_ref, of_scr, ob_scr):
    C = GLA_CHUNK
    nb, t, _ = q_ref.shape
    n = t // C
    row = lax.broadcasted_iota(jnp.int32, (C, C), 0)
    col = lax.broadcasted_iota(jnp.int32, (C, C), 1)
    half_lo = lax.broadcasted_iota(jnp.int32, (C, 128), 1) < GLA_DK
    refs = (q_ref, k_ref, v_ref, g_ref)
    sf_ref[...] = sf0_ref[...]
    sb_ref[...] = sb0_ref[...]

    def body(i, carry):
        units = []
        for bi in range(nb):
            units.append((bi, i, True, row >= col, sf_ref, of_scr))
            units.append((bi, n - 1 - i, False, row <= col, sb_ref, ob_scr))
        _gla_chunks(units, *refs, half_lo)
        return carry

    lax.fori_loop(0, n, body, 0)

    rt = GLA_OUT_ROWS
    gn = gn_ref[...]

    def finish(i, carry):
        rows = pl.ds(pl.multiple_of(i * rt, rt), rt)
        for bi in range(nb):
            o = of_scr[bi, rows, :] + ob_scr[bi, rows, :]
            gate = gate_ref[bi, rows, :].astype(F32)
            for hd in range(GLA_HEADS):
                sl = slice(hd * GLA_DV, (hd + 1) * GLA_DV)
                o_ref[bi, rows, sl] = (_rms(o[:, sl], gn) * gate[:, sl]).astype(BF16)
        return carry

    lax.fori_loop(0, t // rt, finish, 0)


GLA_OUT_ROWS = 64


def _gla(q, k, v, g, gate, s_f0, s_b0, gla_g, nb):
    b, t, _ = q.shape
    seq = lambda w: pl.BlockSpec((nb, t, w), lambda i: (i, 0, 0))
    st = pl.BlockSpec((nb, D_QK, GLA_DV), lambda i: (i, 0, 0))
    return pl.pallas_call(
        _gla_kernel,
        grid=(b // nb,),
        in_specs=[seq(D_QK), seq(D_QK), seq(D_V), seq(2 * D_QK), seq(D_V), st, st,
                  pl.BlockSpec((1, GLA_DV), lambda i: (0, 0))],
        out_specs=[seq(D_V), st, st],
        out_shape=[jax.ShapeDtypeStruct((b, t, D_V), BF16),
                   jax.ShapeDtypeStruct((b, D_QK, GLA_DV), F32),
                   jax.ShapeDtypeStruct((b, D_QK, GLA_DV), F32)],
        scratch_shapes=[pltpu.VMEM((nb, t, D_V), F32), pltpu.VMEM((nb, t, D_V), F32)],
        compiler_params=pltpu.CompilerParams(
            dimension_semantics=("arbitrary",), vmem_limit_bytes=VMEM_LIMIT),
        name="gla",
    )(q, k, v, g, gate, s_f0, s_b0, gla_g)


ATT_KEY_BLOCK = 128
ATT_Q_TILE = 256
LOG2E = 1.4426950408889634


def _attn_kernel(qt_ref, k_ref, vt_ref, o_ref):
    tk = k_ref.shape[0]
    tq = qt_ref.shape[1]
    kv_head = pl.program_id(1)
    own = lax.broadcasted_iota(jnp.int32, (D_KA, 1), 0) // HEAD_DIM == kv_head
    cols = []
    for g in range(GROUP):
        qt = qt_ref[g * HEAD_DIM:(g + 1) * HEAD_DIM, :]
        qt2 = jnp.concatenate([qt] * ATT_KV_HEADS, axis=0)
        cols.append(jnp.where(own, qt2, jnp.zeros_like(qt2)))
    qt_all = jnp.concatenate(cols, axis=1)
    s = _dot(k_ref[...], qt_all)
    p = jnp.exp2(s - jnp.max(s, axis=0, keepdims=True)).astype(BF16)
    vt1 = jnp.concatenate([vt_ref[...], jnp.ones((16, tk), BF16)], axis=0)
    acc = _dot(vt1, p)
    o = acc[:HEAD_DIM] / acc[HEAD_DIM:HEAD_DIM + 1]
    for g in range(GROUP):
        o_ref[:, g * HEAD_DIM:(g + 1) * HEAD_DIM] = o[:, g * tq:(g + 1) * tq].T.astype(BF16)


def _attention(qt, k, vt, batch):
    ntok = qt.shape[1]
    tk = k.shape[0] // batch
    tq = ATT_Q_TILE
    nq = ntok // batch // tq
    return pl.pallas_call(
        _attn_kernel,
        grid=(batch, ATT_KV_HEADS, nq),
        in_specs=[pl.BlockSpec((GROUP * HEAD_DIM, tq), lambda b, j, i: (j, b * nq + i)),
                  pl.BlockSpec((tk, D_KA), lambda b, j, i: (b, 0)),
                  pl.BlockSpec((HEAD_DIM, tk), lambda b, j, i: (j, b))],
        out_specs=pl.BlockSpec((tq, GROUP * HEAD_DIM), lambda b, j, i: (b * nq + i, j)),
        out_shape=jax.ShapeDtypeStruct((ntok, D_QA), BF16),
        compiler_params=pltpu.CompilerParams(
            dimension_semantics=("arbitrary",) * 3, vmem_limit_bytes=VMEM_LIMIT),
        name="attention",
    )(qt, k, vt)


def _post_kernel(og_ref, oa_ref, x_ref, mod_ref, n2_ref, fg_ref, wo_ref, w1_ref, w3_ref, w2_ref,
                 y_ref):
    D = D_MODEL
    mod = mod_ref[0]
    g1, sh2, sc2, g2 = (mod[:, i * D:(i + 1) * D] for i in range(2, 6))
    out = _dot(og_ref[...], wo_ref[0:D_V, :]) + _dot(oa_ref[...], wo_ref[D_V:D_V + D_QA, :])
    x1 = x_ref[...] + g1 * out
    hb = (_rms(x1, n2_ref[...]) * (1.0 + sc2) + sh2).astype(BF16)
    act = (_silu(_dot(hb, w1_ref[...])) * _dot(hb, w3_ref[...])).astype(BF16)
    x2 = x1 + g2 * _dot(act, w2_ref[...])
    y_ref[...] = _rms(x2, fg_ref[...])


def _post_mixer(o_gla, o_att, x, mod3, mod_row0, tokens_per_mod, weights, tm):
    ntok = x.shape[0]
    tiles_per_mod = tokens_per_mod // tm
    whole = pl.BlockSpec(memory_space=pltpu.VMEM)
    row = lambda w: pl.BlockSpec((tm, w), lambda i: (i, 0))
    return pl.pallas_call(
        _post_kernel,
        grid=(ntok // tm,),
        in_specs=[row(D_V), row(D_QA), row(D_MODEL),
                  pl.BlockSpec((1, 1, N_MOD * D_MODEL), lambda i: (mod_row0 + i // tiles_per_mod, 0, 0)),
                  whole, whole, whole, whole, whole, whole],
        out_specs=row(D_MODEL),
        out_shape=jax.ShapeDtypeStruct((ntok, D_MODEL), F32),
        compiler_params=pltpu.CompilerParams(
            dimension_semantics=("arbitrary",), vmem_limit_bytes=VMEM_LIMIT),
        name="post_mixer",
    )(o_gla, o_att, x, mod3, *weights)


def _rope_tables(t):
    rows = t // GRID_W
    row = jnp.repeat(jnp.arange(rows, dtype=F32), GRID_W)
    col = jnp.tile(jnp.arange(GRID_W, dtype=F32), rows)
    inv = ROPE_THETA ** (-jnp.arange(0, ROPE_AXIS_DIM, 2, dtype=F32) / ROPE_AXIS_DIM)
    ang_r = row[:, None] * inv
    ang_c = col[:, None] * inv
    cos = jnp.concatenate([jnp.cos(ang_r)] * 2 + [jnp.cos(ang_c)] * 2, axis=-1)
    sin = jnp.concatenate([-jnp.sin(ang_r), jnp.sin(ang_r), -jnp.sin(ang_c), jnp.sin(ang_c)], axis=-1)
    return jnp.tile(cos, (1, ATT_HEADS)), jnp.tile(sin, (1, ATT_HEADS))


def _block_mean(width, group):
    return jnp.asarray(np.kron(np.eye(width // group), np.ones((group, group))) / group, BF16)


def kernel(x_prompt, x_sample, cache_k, cache_v, state_gla_fwd, state_gla_bwd, c, c_ctx,
           ada_w, ada_b, norm1_g, norm2_g, w_in, w_gk2, b_gk2, gla_norm_g, q_norm_g, k_norm_g,
           w_out, w_ffn1, w_ffn3, w_ffn2, final_g):
    B, T, D = x_prompt.shape
    BL, TL, _ = x_sample.shape
    TP = cache_k.shape[2]
    l = 0

    cond = jnp.zeros((COND_ROWS, D), F32).at[0].set(c_ctx).at[1:1 + BL].set(c)
    mod3 = _adaln(cond, ada_w[l], ada_b[l]).reshape(COND_ROWS, 1, N_MOD * D)

    offs = np.cumsum([0, D_QK, D_QK, D_V, D_V, 2 * GLA_LOWRANK, D_QA, D_KA, D_KA])
    seg = lambda i: w_in[l][:, offs[i]:offs[i + 1]]
    w_in_r = jnp.concatenate([seg(0), seg(1), seg(2), seg(3), seg(5), seg(6), seg(7), seg(4)],
                             axis=1).astype(BF16)
    zero = jnp.zeros((GLA_LOWRANK, D_QK), F32)
    w_gk = jnp.concatenate([jnp.concatenate([w_gk2[l, 0], zero], axis=1),
                            jnp.concatenate([zero, w_gk2[l, 1]], axis=1)], axis=0).astype(BF16)
    pre_w = (norm1_g[l].reshape(1, D), w_in_r, w_gk, b_gk2[l].reshape(1, 2 * D_QK),
             jnp.tile(q_norm_g[l], ATT_HEADS).reshape(1, D_QA),
             jnp.tile(k_norm_g[l], ATT_KV_HEADS).reshape(1, D_KA),
             _block_mean(D_QA, HEAD_DIM), _block_mean(D_KA, HEAD_DIM))
    post_w = (norm2_g[l].reshape(1, D), final_g.reshape(1, D), w_out[l].astype(BF16),
              w_ffn1[l].astype(BF16), w_ffn3[l].astype(BF16), w_ffn2[l].astype(BF16))
    gla_g = gla_norm_g[l].reshape(1, GLA_DV)
    tm = 512

    xc = x_prompt.reshape(B * T, D)
    q, k, v, gate, g, qa, ka, va, kc, vc = _pre_mixer(
        xc, mod3, 0, B * T, T, pre_w, None, True, tm)
    zeros = jnp.zeros((B, D_QK, GLA_DV), F32)
    r3 = lambda a, b, t: a.reshape(b, t, a.shape[-1])
    o_gla, s_f, s_b = _gla(r3(q, B, T), r3(k, B, T), r3(v, B, T), r3(g, B, T), r3(gate, B, T),
                           zeros, zeros, gla_g, 4)
    o_att = _attention(qa, ka, va, B)
    y_prompt = _post_mixer(o_gla.reshape(B * T, D_V), o_att, xc, mod3, 0, B * T, post_w, tm)

    xl = x_sample.reshape(BL * TL, D)
    q, k, v, gate, g, qa, ka, va = _pre_mixer(
        xl, mod3, 1, TL, TL, pre_w, _rope_tables(TL), False, tm)
    o_gla, _, _ = _gla(r3(q, BL, TL), r3(k, BL, TL), r3(v, BL, TL), r3(g, BL, TL), r3(gate, BL, TL),
                       state_gla_fwd[:, l].reshape(BL, D_QK, GLA_DV),
                       state_gla_bwd[:, l].reshape(BL, D_QK, GLA_DV), gla_g, BL)

    k_old = cache_k[:, l].reshape(BL, TP, D_KA).astype(BF16)
    k_all = jnp.concatenate([ka.reshape(BL, TL, D_KA), k_old], axis=1).reshape(BL * (TL + TP), D_KA)
    vt_new = va.reshape(D_KA, BL, TL)
    vt_old = jnp.transpose(cache_v[:, l], (2, 3, 0, 1)).reshape(D_KA, BL, TP).astype(BF16)
    vt_all = jnp.concatenate([vt_new, vt_old], axis=2).reshape(D_KA, BL * (TL + TP))
    o_att = _attention(qa, k_all, vt_all, BL)
    y_sample = _post_mixer(o_gla.reshape(BL * TL, D_V), o_att, xl, mod3, 1, TL, post_w, tm)

    return (y_prompt.reshape(B, T, D), y_sample.reshape(BL, TL, D),
            kc.reshape(B, 1, T, ATT_KV_HEADS, HEAD_DIM), vc.reshape(B, 1, T, ATT_KV_HEADS, HEAD_DIM),
            s_f.reshape(B, 1, GLA_HEADS, GLA_DK, GLA_DV), s_b.reshape(B, 1, GLA_HEADS, GLA_DK, GLA_DV))
```

```python
import functools

import numpy as np
import jax
import jax.numpy as jnp
from jax import lax
from jax.experimental import pallas as pl
from jax.experimental.pallas import tpu as pltpu

F32 = jnp.float32
BF16 = jnp.bfloat16

D_MODEL = 1024
GRID_W = 64
GLA_HEADS = 4
GLA_DK = 64
GLA_DV = 128
GLA_LOWRANK = 16
GLA_GATE_NORM = 16.0
GLA_CHUNK = 64
ATT_HEADS = 8
ATT_KV_HEADS = 2
HEAD_DIM = 64
ROPE_AXIS_DIM = HEAD_DIM // 2
ROPE_THETA = 10000.0
D_FF = -(-8 * D_MODEL // (3 * 256)) * 256
N_MOD = 6
EPS = 1e-6

D_QK = GLA_HEADS * GLA_DK
D_V = GLA_HEADS * GLA_DV
D_QA = ATT_HEADS * HEAD_DIM
D_KA = ATT_KV_HEADS * HEAD_DIM
GROUP = ATT_HEADS // ATT_KV_HEADS
O_Q, O_K, O_V, O_OG = 0, D_QK, 2 * D_QK, 2 * D_QK + D_V
O_QA = O_OG + D_V
O_KA = O_QA + D_QA
O_VA = O_KA + D_KA
O_LR = O_VA + D_KA
D_PROJ = O_LR + 2 * GLA_LOWRANK

COND_ROWS = 8
VMEM_LIMIT = 56 * 1024 * 1024


def _dot(a, b):
    return jnp.dot(a, b, preferred_element_type=F32)


def _dot_nt(a, b):
    return lax.dot_general(a, b, (((1,), (1,)), ((), ())), preferred_element_type=F32)


def _dot_tn(a, b):
    return lax.dot_general(a, b, (((0,), (0,)), ((), ())), preferred_element_type=F32)


def _rms(x, g):
    ms = jnp.mean(x * x, axis=-1, keepdims=True)
    return x * lax.rsqrt(ms + EPS) * g


def _silu(x):
    return x * jax.nn.sigmoid(x)


def _adaln_kernel(c_ref, w_ref, b_ref, o_ref):
    s = _silu(c_ref[...])
    o_ref[...] = _dot(s.astype(BF16), w_ref[...].astype(BF16)) + b_ref[...]


def _adaln(cond, ada_w, ada_b):
    n = ada_w.shape[1]
    tn = 1536
    return pl.pallas_call(
        _adaln_kernel,
        grid=(n // tn,),
        in_specs=[
            pl.BlockSpec((COND_ROWS, D_MODEL), lambda j: (0, 0)),
            pl.BlockSpec((D_MODEL, tn), lambda j: (0, j)),
            pl.BlockSpec((1, tn), lambda j: (0, j)),
        ],
        out_specs=pl.BlockSpec((COND_ROWS, tn), lambda j: (0, j)),
        out_shape=jax.ShapeDtypeStruct((COND_ROWS, n), F32),
        compiler_params=pltpu.CompilerParams(
            dimension_semantics=("arbitrary",), vmem_limit_bytes=VMEM_LIMIT),
        name="adaln",
    )(cond, ada_w, ada_b.reshape(1, n))


def _swap_halves(x, lane_lo):
    n = x.shape[-1]
    up = pltpu.roll(x, n - ROPE_AXIS_DIM // 2, axis=1)
    dn = pltpu.roll(x, ROPE_AXIS_DIM // 2, axis=1)
    return jnp.where(lane_lo, up, dn)


def _pre_kernel(*refs, rope, emit_cache):
    it = iter(refs)
    x_ref, mod_ref, n1_ref, win_ref, wgk_ref, bgk_ref = (next(it) for _ in range(6))
    qg_ref, kg_ref, bdq_ref, bdk_ref = (next(it) for _ in range(4))
    if rope:
        cos_ref, sin_ref = next(it), next(it)
    q_o, k_o, v_o, gate_o, g_o, qa_o, ka_o, va_o = (next(it) for _ in range(8))
    if emit_cache:
        kc_o, vc_o = next(it), next(it)

    mod = mod_ref[0]
    sh1 = mod[:, 0:D_MODEL]
    sc1 = mod[:, D_MODEL:2 * D_MODEL]
    h = _rms(x_ref[...], n1_ref[...]) * (1.0 + sc1) + sh1
    p = _dot(h.astype(BF16), win_ref[...])

    q_o[...] = p[:, O_Q:O_Q + D_QK].astype(BF16)
    k_o[...] = p[:, O_K:O_K + D_QK].astype(BF16)
    v_o[...] = p[:, O_V:O_V + D_V].astype(BF16)
    gate_o[...] = _silu(p[:, O_OG:O_OG + D_V]).astype(BF16)

    gk = _dot(p[:, O_LR:O_LR + 2 * GLA_LOWRANK].astype(BF16), wgk_ref[...]) + bgk_ref[...]
    g_o[...] = (jnp.minimum(gk, 0.0) - jnp.log(1.0 + jnp.exp(-jnp.abs(gk)))) * (1.0 / GLA_GATE_NORM)

    qa = p[:, O_QA:O_QA + D_QA]
    ka = p[:, O_KA:O_KA + D_KA]
    va = p[:, O_VA:O_VA + D_KA]
    qa = qa * lax.rsqrt(_dot((qa * qa).astype(BF16), bdq_ref[...]) + EPS) * qg_ref[...]
    ka = ka * lax.rsqrt(_dot((ka * ka).astype(BF16), bdk_ref[...]) + EPS) * kg_ref[...]
    if rope:
        lane_q = lax.broadcasted_iota(jnp.int32, qa.shape, 1)
        lane_k = lax.broadcasted_iota(jnp.int32, ka.shape, 1)
        lo_q = (lane_q % ROPE_AXIS_DIM) < ROPE_AXIS_DIM // 2
        lo_k = (lane_k % ROPE_AXIS_DIM) < ROPE_AXIS_DIM // 2
        cos_k = cos_ref[...]
        sin_k = sin_ref[...]
        cos_q = jnp.concatenate([cos_k] * GROUP, axis=1)
        sin_q = jnp.concatenate([sin_k] * GROUP, axis=1)
        qa = qa * cos_q + _swap_halves(qa, lo_q) * sin_q
        ka = ka * cos_k + _swap_halves(ka, lo_k) * sin_k
    vat = va.T
    if emit_cache:
        kat = ka.T
        t = kc_o.shape[2]
        for sq in range(kc_o.shape[0]):
            kc_o[sq] = kat[:, sq * t:(sq + 1) * t]
            vc_o[sq] = vat[:, sq * t:(sq + 1) * t]
    qa_o[...] = (qa * (HEAD_DIM ** -0.5 * LOG2E)).T.astype(BF16)
    ka_o[...] = ka.astype(BF16)
    va_o[...] = vat.astype(BF16)


def _pre_mixer(x, mod3, mod_row0, tokens_per_mod, seq_len, weights, rope_tabs, emit_cache, tm):
    ntok = x.shape[0]
    n1, w_in_r, w_gk, b_gk, qg, kg, bdq, bdk = weights
    rope = rope_tabs is not None
    tiles_per_mod = tokens_per_mod // tm
    tiles_per_seq = seq_len // tm
    whole = pl.BlockSpec(memory_space=pltpu.VMEM)
    row = lambda w: pl.BlockSpec((tm, w), lambda i: (i, 0))
    in_specs = [
        row(D_MODEL),
        pl.BlockSpec((1, 1, N_MOD * D_MODEL), lambda i: (mod_row0 + i // tiles_per_mod, 0, 0)),
        whole, whole, whole, whole, whole, whole, whole, whole,
    ]
    args = [x, mod3, n1, w_in_r, w_gk, b_gk, qg, kg, bdq, bdk]
    if rope:
        tab = pl.BlockSpec((tm, D_KA), lambda i: (i % tiles_per_seq, 0))
        in_specs += [tab, tab]
        args += list(rope_tabs)
    col = lambda w: pl.BlockSpec((w, tm), lambda i: (0, i))
    out_specs = [row(D_QK), row(D_QK), row(D_V), row(D_V), row(2 * D_QK),
                 col(D_QA), row(D_KA), col(D_KA)]
    out_shape = [
        jax.ShapeDtypeStruct((ntok, D_QK), BF16),
        jax.ShapeDtypeStruct((ntok, D_QK), BF16),
        jax.ShapeDtypeStruct((ntok, D_V), BF16),
        jax.ShapeDtypeStruct((ntok, D_V), BF16),
        jax.ShapeDtypeStruct((ntok, 2 * D_QK), F32),
        jax.ShapeDtypeStruct((D_QA, ntok), BF16),
        jax.ShapeDtypeStruct((ntok, D_KA), BF16),
        jax.ShapeDtypeStruct((D_KA, ntok), BF16),
    ]
    if emit_cache:
        seqs = tm // seq_len
        out_specs += [pl.BlockSpec((seqs, D_KA, seq_len), lambda i: (i, 0, 0))] * 2
        out_shape += [jax.ShapeDtypeStruct((ntok // seq_len, D_KA, seq_len), F32)] * 2
    return pl.pallas_call(
        functools.partial(_pre_kernel, rope=rope, emit_cache=emit_cache),
        grid=(ntok // tm,),
        in_specs=in_specs,
        out_specs=out_specs,
        out_shape=out_shape,
        compiler_params=pltpu.CompilerParams(
            dimension_semantics=("arbitrary",), vmem_limit_bytes=VMEM_LIMIT),
        name="pre_mixer_rope" if rope else "pre_mixer",
    )(*args)


def _gla_chunks(units, q_ref, k_ref, v_ref, g_ref, half_lo):
    C = GLA_CHUNK
    n_pair = GLA_HEADS // 2
    rows = [pl.ds(pl.multiple_of(c * C, C), C) for (_, c, _, _, _, _) in units]

    g_his, bs = [], []
    for (bi, _, fwd, tri, _, _), r in zip(units, rows):
        gc = g_ref[bi, r, 0:D_QK] if fwd else g_ref[bi, r, D_QK:2 * D_QK]
        ltri = jnp.where(tri, 1.0, 0.0).astype(BF16)
        g_hi = gc.astype(BF16)
        g_lo = (gc - g_hi.astype(F32)).astype(BF16)
        g_his.append(g_hi)
        bs.append(_dot(ltri, g_hi) + _dot(ltri, g_lo))
    ones = jnp.ones((C, GLA_DV), BF16)
    decays = [jnp.exp(_dot_tn(g_hi, ones)) for g_hi in g_his]

    qts, kts, kes = [], [], []
    for (bi, _, fwd, _, _, _), r, b in zip(units, rows, bs):
        qc = q_ref[bi, r, :].astype(F32) * GLA_DK ** -0.5
        kc = k_ref[bi, r, :].astype(F32)
        btot = b[C - 1:C, :] if fwd else b[0:1, :]
        qts.append((qc * jnp.exp(b)).astype(BF16))
        kts.append((kc * jnp.exp(-b)).astype(BF16))
        kes.append((kc * jnp.exp(btot - b)).astype(BF16))

    scores, incs, qms = [], [], []
    for (bi, _, _, tri, _, _), r, qt, kt, ke in zip(units, rows, qts, kts, kes):
        for pair in range(n_pair):
            lanes = slice(pair * 128, (pair + 1) * 128)
            incs.append(_dot_tn(ke[:, lanes], v_ref[bi, r, pair * 2 * GLA_DV:(pair + 1) * 2 * GLA_DV]))
            for hh in range(2):
                qm = jnp.where(half_lo if hh == 0 else ~half_lo, qt[:, lanes], jnp.zeros((C, 128), BF16))
                qms.append(qm)
                scores.append(jnp.where(tri, _dot_nt(qm, kt[:, lanes]), 0.0).astype(BF16))

    for ui, ((bi, _, _, _, s_ref, o_ref), r) in enumerate(zip(units, rows)):
        state = s_ref[bi]
        state_b = state.astype(BF16)
        for hd in range(GLA_HEADS):
            pair = hd // 2
            lhs = jnp.concatenate([qms[ui * GLA_HEADS + hd], scores[ui * GLA_HEADS + hd]], axis=1)
            rhs = jnp.concatenate([state_b[pair * 128:(pair + 1) * 128, :],
                                   v_ref[bi, r, hd * GLA_DV:(hd + 1) * GLA_DV]], axis=0)
            o_ref[bi, r, hd * GLA_DV:(hd + 1) * GLA_DV] = _dot(lhs, rhs)
        inc = jnp.concatenate(
            [incs[ui * n_pair + hd // 2][(hd % 2) * GLA_DK:(hd % 2 + 1) * GLA_DK,
                                         (hd % 2) * GLA_DV:(hd % 2 + 1) * GLA_DV]
             for hd in range(GLA_HEADS)], axis=0)
        s_ref[bi] = decays[ui] * state + inc


def _gla_kernel(*args, zero_init):
    q_ref, k_ref, v_ref, g_ref, gate_ref, gn_ref = args[:6]
    o_ref, sf_ref, sb_ref, of_scr, ob_scr = args[-5:]
    C = GLA_CHUNK
    nb, t, _ = q_ref.shape
    n = t // C
    row = lax.broadcasted_iota(jnp.int32, (C, C), 0)
    col = lax.broadcasted_iota(jnp.int32, (C, C), 1)
    half_lo = lax.broadcasted_iota(jnp.int32, (C, 128), 1) < GLA_DK
    refs = (q_ref, k_ref, v_ref, g_ref)
    if zero_init:
        sf_ref[...] = jnp.zeros(sf_ref.shape, F32)
        sb_ref[...] = jnp.zeros(sb_ref.shape, F32)
    else:
        sf_ref[...] = args[6][...]
        sb_ref[...] = args[7][...]

    def body(i, carry):
        units = []
        for bi in range(nb):
            units.append((bi, i, True, row >= col, sf_ref, of_scr))
            units.append((bi, n - 1 - i, False, row <= col, sb_ref, ob_scr))
        _gla_chunks(units, *refs, half_lo)
        return carry

    lax.fori_loop(0, n, body, 0)

    rt = GLA_OUT_ROWS
    gn = gn_ref[...]

    def finish(i, carry):
        rows = pl.ds(pl.multiple_of(i * rt, rt), rt)
        for bi in range(nb):
            o = of_scr[bi, rows, :] + ob_scr[bi, rows, :]
            gate = gate_ref[bi, rows, :].astype(F32)
            for hd in range(GLA_HEADS):
                sl = slice(hd * GLA_DV, (hd + 1) * GLA_DV)
                o_ref[bi, rows, sl] = (_rms(o[:, sl], gn) * gate[:, sl]).astype(BF16)
        return carry

    lax.fori_loop(0, t // rt, finish, 0)


GLA_OUT_ROWS = 64


def _gla(q, k, v, g, gate, gla_g, init_states, nb):
    b, t, _ = q.shape
    seq = lambda w: pl.BlockSpec((nb, t, w), lambda i: (i, 0, 0))
    st = pl.BlockSpec((nb, D_QK, GLA_DV), lambda i: (i, 0, 0))
    zero_init = init_states is None
    return pl.pallas_call(
        functools.partial(_gla_kernel, zero_init=zero_init),
        grid=(b // nb,),
        in_specs=[seq(D_QK), seq(D_QK), seq(D_V), seq(2 * D_QK), seq(D_V),
                  pl.BlockSpec((1, GLA_DV), lambda i: (0, 0))] + ([] if zero_init else [st, st]),
        out_specs=[seq(D_V), st, st],
        out_shape=[jax.ShapeDtypeStruct((b, t, D_V), BF16),
                   jax.ShapeDtypeStruct((b, D_QK, GLA_DV), F32),
                   jax.ShapeDtypeStruct((b, D_QK, GLA_DV), F32)],
        scratch_shapes=[pltpu.VMEM((nb, t, D_V), F32), pltpu.VMEM((nb, t, D_V), F32)],
        compiler_params=pltpu.CompilerParams(
            dimension_semantics=("arbitrary",), vmem_limit_bytes=VMEM_LIMIT),
        name="gla",
    )(q, k, v, g, gate, gla_g, *(() if zero_init else init_states))


ATT_KEY_BLOCK = 128
ATT_Q_TILE = 256
LOG2E = 1.4426950408889634


def _attn_kernel(qt_ref, k_ref, vt_ref, o_ref):
    tk = k_ref.shape[0]
    tq = qt_ref.shape[1]
    kv_head = pl.program_id(1)
    own = lax.broadcasted_iota(jnp.int32, (D_KA, 1), 0) // HEAD_DIM == kv_head
    cols = []
    for g in range(GROUP):
        qt = qt_ref[g * HEAD_DIM:(g + 1) * HEAD_DIM, :]
        qt2 = jnp.concatenate([qt] * ATT_KV_HEADS, axis=0)
        cols.append(jnp.where(own, qt2, jnp.zeros_like(qt2)))
    qt_all = jnp.concatenate(cols, axis=1)
    s = _dot(k_ref[...], qt_all)
    p = jnp.exp2(s - jnp.max(s, axis=0, keepdims=True)).astype(BF16)
    vt1 = jnp.concatenate([vt_ref[...], jnp.ones((16, tk), BF16)], axis=0)
    acc = _dot(vt1, p)
    o = acc[:HEAD_DIM] / acc[HEAD_DIM:HEAD_DIM + 1]
    for g in range(GROUP):
        o_ref[:, g * HEAD_DIM:(g + 1) * HEAD_DIM] = o[:, g * tq:(g + 1) * tq].T.astype(BF16)


def _attention(qt, k, vt, batch):
    ntok = qt.shape[1]
    tk = k.shape[0] // batch
    tq = ATT_Q_TILE
    nq = ntok // batch // tq
    return pl.pallas_call(
        _attn_kernel,
        grid=(batch, ATT_KV_HEADS, nq),
        in_specs=[pl.BlockSpec((GROUP * HEAD_DIM, tq), lambda b, j, i: (j, b * nq + i)),
                  pl.BlockSpec((tk, D_KA), lambda b, j, i: (b, 0)),
                  pl.BlockSpec((HEAD_DIM, tk), lambda b, j, i: (j, b))],
        out_specs=pl.BlockSpec((tq, GROUP * HEAD_DIM), lambda b, j, i: (b * nq + i, j)),
        out_shape=jax.ShapeDtypeStruct((ntok, D_QA), BF16),
        compiler_params=pltpu.CompilerParams(
            dimension_semantics=("arbitrary",) * 3, vmem_limit_bytes=VMEM_LIMIT),
        name="attention",
    )(qt, k, vt)


def _post_kernel(og_ref, oa_ref, x_ref, mod_ref, n2_ref, fg_ref, wo_ref, w1_ref, w3_ref, w2_ref,
                 y_ref):
    D = D_MODEL
    mod = mod_ref[0]
    g1, sh2, sc2, g2 = (mod[:, i * D:(i + 1) * D] for i in range(2, 6))
    out = _dot(og_ref[...], wo_ref[0:D_V, :]) + _dot(oa_ref[...], wo_ref[D_V:D_V + D_QA, :])
    x1 = x_ref[...] + g1 * out
    hb = (_rms(x1, n2_ref[...]) * (1.0 + sc2) + sh2).astype(BF16)
    act = (_silu(_dot(hb, w1_ref[...])) * _dot(hb, w3_ref[...])).astype(BF16)
    x2 = x1 + g2 * _dot(act, w2_ref[...])
    y_ref[...] = _rms(x2, fg_ref[...])


def _post_mixer(o_gla, o_att, x, mod3, mod_row0, tokens_per_mod, weights, tm):
    ntok = x.shape[0]
    tiles_per_mod = tokens_per_mod // tm
    whole = pl.BlockSpec(memory_space=pltpu.VMEM)
    row = lambda w: pl.BlockSpec((tm, w), lambda i: (i, 0))
    return pl.pallas_call(
        _post_kernel,
        grid=(ntok // tm,),
        in_specs=[row(D_V), row(D_QA), row(D_MODEL),
                  pl.BlockSpec((1, 1, N_MOD * D_MODEL), lambda i: (mod_row0 + i // tiles_per_mod, 0, 0)),
                  whole, whole, whole, whole, whole, whole],
        out_specs=row(D_MODEL),
        out_shape=jax.ShapeDtypeStruct((ntok, D_MODEL), F32),
        compiler_params=pltpu.CompilerParams(
            dimension_semantics=("arbitrary",), vmem_limit_bytes=VMEM_LIMIT),
        name="post_mixer",
    )(o_gla, o_att, x, mod3, *weights)


def _rope_tables(t):
    rows = t // GRID_W
    row = jnp.repeat(jnp.arange(rows, dtype=F32), GRID_W)
    col = jnp.tile(jnp.arange(GRID_W, dtype=F32), rows)
    inv = ROPE_THETA ** (-jnp.arange(0, ROPE_AXIS_DIM, 2, dtype=F32) / ROPE_AXIS_DIM)
    ang_r = row[:, None] * inv
    ang_c = col[:, None] * inv
    cos = jnp.concatenate([jnp.cos(ang_r)] * 2 + [jnp.cos(ang_c)] * 2, axis=-1)
    sin = jnp.concatenate([-jnp.sin(ang_r), jnp.sin(ang_r), -jnp.sin(ang_c), jnp.sin(ang_c)], axis=-1)
    return jnp.tile(cos, (1, ATT_KV_HEADS)), jnp.tile(sin, (1, ATT_KV_HEADS))


def _block_mean(width, group):
    return jnp.asarray(np.kron(np.eye(width // group), np.ones((group, group))) / group, BF16)


def kernel(x_prompt, x_sample, cache_k, cache_v, state_gla_fwd, state_gla_bwd, c, c_ctx,
           ada_w, ada_b, norm1_g, norm2_g, w_in, w_gk2, b_gk2, gla_norm_g, q_norm_g, k_norm_g,
           w_out, w_ffn1, w_ffn3, w_ffn2, final_g):
    B, T, D = x_prompt.shape
    BL, TL, _ = x_sample.shape
    TP = cache_k.shape[2]
    l = 0

    cond = jnp.zeros((COND_ROWS, D), F32).at[0].set(c_ctx).at[1:1 + BL].set(c)
    mod3 = _adaln(cond, ada_w[l], ada_b[l]).reshape(COND_ROWS, 1, N_MOD * D)

    offs = np.cumsum([0, D_QK, D_QK, D_V, D_V, 2 * GLA_LOWRANK, D_QA, D_KA, D_KA])
    seg = lambda i: w_in[l][:, offs[i]:offs[i + 1]]
    w_in_r = jnp.concatenate([seg(0), seg(1), seg(2), seg(3), seg(5), seg(6), seg(7), seg(4)],
                             axis=1).astype(BF16)
    zero = jnp.zeros((GLA_LOWRANK, D_QK), F32)
    w_gk = jnp.concatenate([jnp.concatenate([w_gk2[l, 0], zero], axis=1),
                            jnp.concatenate([zero, w_gk2[l, 1]], axis=1)], axis=0).astype(BF16)
    pre_w = (norm1_g[l].reshape(1, D), w_in_r, w_gk, b_gk2[l].reshape(1, 2 * D_QK),
             jnp.tile(q_norm_g[l], ATT_HEADS).reshape(1, D_QA),
             jnp.tile(k_norm_g[l], ATT_KV_HEADS).reshape(1, D_KA),
             _block_mean(D_QA, HEAD_DIM), _block_mean(D_KA, HEAD_DIM))
    post_w = (norm2_g[l].reshape(1, D), final_g.reshape(1, D), w_out[l].astype(BF16),
              w_ffn1[l].astype(BF16), w_ffn3[l].astype(BF16), w_ffn2[l].astype(BF16))
    gla_g = gla_norm_g[l].reshape(1, GLA_DV)
    tm = 512

    xc = x_prompt.reshape(B * T, D)
    q, k, v, gate, g, qa, ka, va, kc, vc = _pre_mixer(
        xc, mod3, 0, B * T, T, pre_w, None, True, tm)
    r3 = lambda a, b, t: a.reshape(b, t, a.shape[-1])
    o_gla, s_f, s_b = _gla(r3(q, B, T), r3(k, B, T), r3(v, B, T), r3(g, B, T), r3(gate, B, T),
                           gla_g, None, 4)
    o_att = _attention(qa, ka, va, B)
    y_prompt = _post_mixer(o_gla.reshape(B * T, D_V), o_att, xc, mod3, 0, B * T, post_w, tm)

    xl = x_sample.reshape(BL * TL, D)
    q, k, v, gate, g, qa, ka, va = _pre_mixer(
        xl, mod3, 1, TL, TL, pre_w, _rope_tables(TL), False, tm)
    o_gla, _, _ = _gla(r3(q, BL, TL), r3(k, BL, TL), r3(v, BL, TL), r3(g, BL, TL), r3(gate, BL, TL),
                       gla_g, (state_gla_fwd[:, l].reshape(BL, D_QK, GLA_DV),
                               state_gla_bwd[:, l].reshape(BL, D_QK, GLA_DV)), BL)

    k_old = cache_k[:, l].reshape(BL, TP, D_KA).astype(BF16)
    k_all = jnp.concatenate([ka.reshape(BL, TL, D_KA), k_old], axis=1).reshape(BL * (TL + TP), D_KA)
    vt_new = va.reshape(D_KA, BL, TL)
    vt_old = jnp.transpose(cache_v[:, l], (2, 3, 0, 1)).reshape(D_KA, BL, TP).astype(BF16)
    vt_all = jnp.concatenate([vt_new, vt_old], axis=2).reshape(D_KA, BL * (TL + TP))
    o_att = _attention(qa, k_all, vt_all, BL)
    y_sample = _post_mixer(o_gla.reshape(BL * TL, D_V), o_att, xl, mod3, 1, TL, post_w, tm)

    def cache_out(ct):
        return jnp.transpose(ct.reshape(B, 1, ATT_KV_HEADS, HEAD_DIM, T), (0, 1, 4, 2, 3))

    return (y_prompt.reshape(B, T, D), y_sample.reshape(BL, TL, D), cache_out(kc), cache_out(vc),
            s_f.reshape(B, 1, GLA_HEADS, GLA_DK, GLA_DV), s_b.reshape(B, 1, GLA_HEADS, GLA_DK, GLA_DV))
```

```python
import functools

import numpy as np
import jax
import jax.numpy as jnp
from jax import lax
from jax.experimental import pallas as pl
from jax.experimental.pallas import tpu as pltpu

F32 = jnp.float32
BF16 = jnp.bfloat16

D_MODEL = 1024
GRID_W = 64
GLA_HEADS = 4
GLA_DK = 64
GLA_DV = 128
GLA_LOWRANK = 16
GLA_GATE_NORM = 16.0
GLA_CHUNK = 64
ATT_HEADS = 8
ATT_KV_HEADS = 2
HEAD_DIM = 64
ROPE_AXIS_DIM = HEAD_DIM // 2
ROPE_THETA = 10000.0
D_FF = -(-8 * D_MODEL // (3 * 256)) * 256
N_MOD = 6
EPS = 1e-6

D_QK = GLA_HEADS * GLA_DK
D_V = GLA_HEADS * GLA_DV
D_QA = ATT_HEADS * HEAD_DIM
D_KA = ATT_KV_HEADS * HEAD_DIM
GROUP = ATT_HEADS // ATT_KV_HEADS
O_Q, O_K, O_V, O_OG = 0, D_QK, 2 * D_QK, 2 * D_QK + D_V
O_QA = O_OG + D_V
O_KA = O_QA + D_QA
O_VA = O_KA + D_KA
O_LR = O_VA + D_KA
D_PROJ = O_LR + 2 * GLA_LOWRANK

COND_ROWS = 8
VMEM_LIMIT = 56 * 1024 * 1024


def _dot(a, b):
    return jnp.dot(a, b, preferred_element_type=F32)


def _dot_nt(a, b):
    return lax.dot_general(a, b, (((1,), (1,)), ((), ())), preferred_element_type=F32)


def _dot_tn(a, b):
    return lax.dot_general(a, b, (((0,), (0,)), ((), ())), preferred_element_type=F32)


def _rms(x, g):
    ms = jnp.mean(x * x, axis=-1, keepdims=True)
    return x * lax.rsqrt(ms + EPS) * g


def _silu(x):
    return x * jax.nn.sigmoid(x)


def _adaln_kernel(c_ref, w_ref, b_ref, o_ref):
    s = _silu(c_ref[...])
    o_ref[...] = _dot(s.astype(BF16), w_ref[...].astype(BF16)) + b_ref[...]


def _adaln(cond, ada_w, ada_b):
    n = ada_w.shape[1]
    tn = 1536
    return pl.pallas_call(
        _adaln_kernel,
        grid=(n // tn,),
        in_specs=[
            pl.BlockSpec((COND_ROWS, D_MODEL), lambda j: (0, 0)),
            pl.BlockSpec((D_MODEL, tn), lambda j: (0, j)),
            pl.BlockSpec((1, tn), lambda j: (0, j)),
        ],
        out_specs=pl.BlockSpec((COND_ROWS, tn), lambda j: (0, j)),
        out_shape=jax.ShapeDtypeStruct((COND_ROWS, n), F32),
        compiler_params=pltpu.CompilerParams(
            dimension_semantics=("arbitrary",), vmem_limit_bytes=VMEM_LIMIT),
        name="adaln",
    )(cond, ada_w, ada_b.reshape(1, n))


def _swap_halves(x, lane_lo):
    n = x.shape[-1]
    up = pltpu.roll(x, n - ROPE_AXIS_DIM // 2, axis=1)
    dn = pltpu.roll(x, ROPE_AXIS_DIM // 2, axis=1)
    return jnp.where(lane_lo, up, dn)


def _pre_kernel(*refs, rope, emit_cache):
    it = iter(refs)
    x_ref, mod_ref, n1_ref, win_ref, wgk_ref, bgk_ref = (next(it) for _ in range(6))
    qg_ref, kg_ref, bdq_ref, bdk_ref = (next(it) for _ in range(4))
    if rope:
        cos_ref, sin_ref = next(it), next(it)
    q_o, k_o, v_o, gate_o, g_o, qa_o, ka_o, va_o = (next(it) for _ in range(8))
    if emit_cache:
        kc_o, vc_o = next(it), next(it)

    mod = mod_ref[0]
    sh1 = mod[:, 0:D_MODEL]
    sc1 = mod[:, D_MODEL:2 * D_MODEL]
    h = _rms(x_ref[...], n1_ref[...]) * (1.0 + sc1) + sh1
    p = _dot(h.astype(BF16), win_ref[...])

    q_o[...] = p[:, O_Q:O_Q + D_QK].astype(BF16)
    k_o[...] = p[:, O_K:O_K + D_QK].astype(BF16)
    v_o[...] = p[:, O_V:O_V + D_V].astype(BF16)
    gate_o[...] = _silu(p[:, O_OG:O_OG + D_V]).astype(BF16)

    gk = _dot(p[:, O_LR:O_LR + 2 * GLA_LOWRANK].astype(BF16), wgk_ref[...]) + bgk_ref[...]
    g_o[...] = (jnp.minimum(gk, 0.0) - jnp.log(1.0 + jnp.exp(-jnp.abs(gk)))) * (1.0 / GLA_GATE_NORM)

    qa = p[:, O_QA:O_QA + D_QA]
    ka = p[:, O_KA:O_KA + D_KA]
    va = p[:, O_VA:O_VA + D_KA]
    qa = qa * lax.rsqrt(_dot((qa * qa).astype(BF16), bdq_ref[...]) + EPS) * qg_ref[...]
    ka = ka * lax.rsqrt(_dot((ka * ka).astype(BF16), bdk_ref[...]) + EPS) * kg_ref[...]
    if rope:
        lane_q = lax.broadcasted_iota(jnp.int32, qa.shape, 1)
        lane_k = lax.broadcasted_iota(jnp.int32, ka.shape, 1)
        lo_q = (lane_q % ROPE_AXIS_DIM) < ROPE_AXIS_DIM // 2
        lo_k = (lane_k % ROPE_AXIS_DIM) < ROPE_AXIS_DIM // 2
        cos_k = cos_ref[...]
        sin_k = sin_ref[...]
        cos_q = jnp.concatenate([cos_k] * GROUP, axis=1)
        sin_q = jnp.concatenate([sin_k] * GROUP, axis=1)
        qa = qa * cos_q + _swap_halves(qa, lo_q) * sin_q
        ka = ka * cos_k + _swap_halves(ka, lo_k) * sin_k
    vat = va.T
    if emit_cache:
        kat = ka.T
        t = kc_o.shape[2]
        for sq in range(kc_o.shape[0]):
            kc_o[sq] = kat[:, sq * t:(sq + 1) * t]
            vc_o[sq] = vat[:, sq * t:(sq + 1) * t]
    qa_o[...] = (qa * (HEAD_DIM ** -0.5 * LOG2E)).T.astype(BF16)
    ka_o[...] = ka.astype(BF16)
    va_o[...] = vat.astype(BF16)


def _pre_mixer(x, mod3, mod_row0, tokens_per_mod, seq_len, weights, rope_tabs, emit_cache, tm):
    ntok = x.shape[0]
    n1, w_in_r, w_gk, b_gk, qg, kg, bdq, bdk = weights
    rope = rope_tabs is not None
    tiles_per_mod = tokens_per_mod // tm
    tiles_per_seq = seq_len // tm
    whole = pl.BlockSpec(memory_space=pltpu.VMEM)
    row = lambda w: pl.BlockSpec((tm, w), lambda i: (i, 0))
    in_specs = [
        row(D_MODEL),
        pl.BlockSpec((1, 1, N_MOD * D_MODEL), lambda i: (mod_row0 + i // tiles_per_mod, 0, 0)),
        whole, whole, whole, whole, whole, whole, whole, whole,
    ]
    args = [x, mod3, n1, w_in_r, w_gk, b_gk, qg, kg, bdq, bdk]
    if rope:
        tab = pl.BlockSpec((tm, D_KA), lambda i: (i % tiles_per_seq, 0))
        in_specs += [tab, tab]
        args += list(rope_tabs)
    col = lambda w: pl.BlockSpec((w, tm), lambda i: (0, i))
    out_specs = [row(D_QK), row(D_QK), row(D_V), row(D_V), row(2 * D_QK),
                 col(D_QA), row(D_KA), col(D_KA)]
    out_shape = [
        jax.ShapeDtypeStruct((ntok, D_QK), BF16),
        jax.ShapeDtypeStruct((ntok, D_QK), BF16),
        jax.ShapeDtypeStruct((ntok, D_V), BF16),
        jax.ShapeDtypeStruct((ntok, D_V), BF16),
        jax.ShapeDtypeStruct((ntok, 2 * D_QK), F32),
        jax.ShapeDtypeStruct((D_QA, ntok), BF16),
        jax.ShapeDtypeStruct((ntok, D_KA), BF16),
        jax.ShapeDtypeStruct((D_KA, ntok), BF16),
    ]
    if emit_cache:
        seqs = tm // seq_len
        out_specs += [pl.BlockSpec((seqs, D_KA, seq_len), lambda i: (i, 0, 0))] * 2
        out_shape += [jax.ShapeDtypeStruct((ntok // seq_len, D_KA, seq_len), F32)] * 2
    return pl.pallas_call(
        functools.partial(_pre_kernel, rope=rope, emit_cache=emit_cache),
        grid=(ntok // tm,),
        in_specs=in_specs,
        out_specs=out_specs,
        out_shape=out_shape,
        compiler_params=pltpu.CompilerParams(
            dimension_semantics=("arbitrary",), vmem_limit_bytes=VMEM_LIMIT),
        name="pre_mixer_rope" if rope else "pre_mixer",
    )(*args)


def _gla_chunks(units, q_ref, k_ref, v_ref, g_ref, half_lo):
    C = GLA_CHUNK
    n_pair = GLA_HEADS // 2
    rows = [pl.ds(pl.multiple_of(c * C, C), C) for (_, c, _, _, _, _) in units]

    g_his, bs = [], []
    for (bi, _, fwd, tri, _, _), r in zip(units, rows):
        gc = g_ref[bi, r, 0:D_QK] if fwd else g_ref[bi, r, D_QK:2 * D_QK]
        ltri = jnp.where(tri, 1.0, 0.0).astype(BF16)
        g_hi = gc.astype(BF16)
        g_lo = (gc - g_hi.astype(F32)).astype(BF16)
        g_his.append(g_hi)
        bs.append(_dot(ltri, g_hi) + _dot(ltri, g_lo))
    ones = jnp.ones((C, GLA_DV), BF16)
    decays = [jnp.exp(_dot_tn(g_hi, ones)) for g_hi in g_his]

    qts, kts, kes = [], [], []
    for (bi, _, fwd, _, _, _), r, b in zip(units, rows, bs):
        qc = q_ref[bi, r, :].astype(F32) * GLA_DK ** -0.5
        kc = k_ref[bi, r, :].astype(F32)
        btot = b[C - 1:C, :] if fwd else b[0:1, :]
        qts.append((qc * jnp.exp(b)).astype(BF16))
        kts.append((kc * jnp.exp(-b)).astype(BF16))
        kes.append((kc * jnp.exp(btot - b)).astype(BF16))

    scores, incs, qms = [], [], []
    for (bi, _, _, tri, _, _), r, qt, kt, ke in zip(units, rows, qts, kts, kes):
        for pair in range(n_pair):
            lanes = slice(pair * 128, (pair + 1) * 128)
            incs.append(_dot_tn(ke[:, lanes], v_ref[bi, r, pair * 2 * GLA_DV:(pair + 1) * 2 * GLA_DV]))
            for hh in range(2):
                qm = jnp.where(half_lo if hh == 0 else ~half_lo, qt[:, lanes], jnp.zeros((C, 128), BF16))
                qms.append(qm)
                scores.append(jnp.where(tri, _dot_nt(qm, kt[:, lanes]), 0.0).astype(BF16))

    for ui, ((bi, _, _, _, s_ref, o_ref), r) in enumerate(zip(units, rows)):
        state = s_ref[bi]
        state_b = state.astype(BF16)
        for hd in range(GLA_HEADS):
            pair = hd // 2
            lhs = jnp.concatenate([qms[ui * GLA_HEADS + hd], scores[ui * GLA_HEADS + hd]], axis=1)
            rhs = jnp.concatenate([state_b[pair * 128:(pair + 1) * 128, :],
                                   v_ref[bi, r, hd * GLA_DV:(hd + 1) * GLA_DV]], axis=0)
            o_ref[bi, r, hd * GLA_DV:(hd + 1) * GLA_DV] = _dot(lhs, rhs)
        inc = jnp.concatenate(
            [incs[ui * n_pair + hd // 2][(hd % 2) * GLA_DK:(hd % 2 + 1) * GLA_DK,
                                         (hd % 2) * GLA_DV:(hd % 2 + 1) * GLA_DV]
             for hd in range(GLA_HEADS)], axis=0)
        s_ref[bi] = decays[ui] * state + inc


def _gla_kernel(*args, zero_init):
    q_ref, k_ref, v_ref, g_ref, gate_ref, gn_ref = args[:6]
    o_ref, sf_ref, sb_ref, of_scr, ob_scr = args[-5:]
    C = GLA_CHUNK
    nb, t, _ = q_ref.shape
    n = t // C
    row = lax.broadcasted_iota(jnp.int32, (C, C), 0)
    col = lax.broadcasted_iota(jnp.int32, (C, C), 1)
    half_lo = lax.broadcasted_iota(jnp.int32, (C, 128), 1) < GLA_DK
    refs = (q_ref, k_ref, v_ref, g_ref)
    if zero_init:
        sf_ref[...] = jnp.zeros(sf_ref.shape, F32)
        sb_ref[...] = jnp.zeros(sb_ref.shape, F32)
    else:
        sf_ref[...] = args[6][...]
        sb_ref[...] = args[7][...]

    def body(i, carry):
        units = []
        for bi in range(nb):
            units.append((bi, i, True, row >= col, sf_ref, of_scr))
            units.append((bi, n - 1 - i, False, row <= col, sb_ref, ob_scr))
        _gla_chunks(units, *refs, half_lo)
        return carry

    lax.fori_loop(0, n, body, 0)

    rt = GLA_OUT_ROWS
    gn = gn_ref[...]

    def finish(i, carry):
        rows = pl.ds(pl.multiple_of(i * rt, rt), rt)
        for bi in range(nb):
            o = of_scr[bi, rows, :] + ob_scr[bi, rows, :]
            gate = gate_ref[bi, rows, :].astype(F32)
            for hd in range(GLA_HEADS):
                sl = slice(hd * GLA_DV, (hd + 1) * GLA_DV)
                o_ref[bi, rows, sl] = (_rms(o[:, sl], gn) * gate[:, sl]).astype(BF16)
        return carry

    lax.fori_loop(0, t // rt, finish, 0)


GLA_OUT_ROWS = 64


def _gla(q, k, v, g, gate, gla_g, init_states, nb):
    b, t, _ = q.shape
    seq = lambda w: pl.BlockSpec((nb, t, w), lambda i: (i, 0, 0))
    st = pl.BlockSpec((nb, D_QK, GLA_DV), lambda i: (i, 0, 0))
    zero_init = init_states is None
    return pl.pallas_call(
        functools.partial(_gla_kernel, zero_init=zero_init),
        grid=(b // nb,),
        in_specs=[seq(D_QK), seq(D_QK), seq(D_V), seq(2 * D_QK), seq(D_V),
                  pl.BlockSpec((1, GLA_DV), lambda i: (0, 0))] + ([] if zero_init else [st, st]),
        out_specs=[seq(D_V), st, st],
        out_shape=[jax.ShapeDtypeStruct((b, t, D_V), BF16),
                   jax.ShapeDtypeStruct((b, D_QK, GLA_DV), F32),
                   jax.ShapeDtypeStruct((b, D_QK, GLA_DV), F32)],
        scratch_shapes=[pltpu.VMEM((nb, t, D_V), F32), pltpu.VMEM((nb, t, D_V), F32)],
        compiler_params=pltpu.CompilerParams(
            dimension_semantics=("arbitrary",), vmem_limit_bytes=VMEM_LIMIT),
        name="gla",
    )(q, k, v, g, gate, gla_g, *(() if zero_init else init_states))


ATT_KEY_BLOCK = 256
ATT_Q_TILE = 256
LOG2E = 1.4426950408889634


def _attn_kernel(qt_ref, qtn_ref, k_ref, kn_ref, vt_ref, o_ref, s0, s1, m0, m1):
    tk = k_ref.shape[0]
    tq = qt_ref.shape[1]
    kb = min(ATT_KEY_BLOCK, tk)
    nkb = tk // kb
    rows_q = GROUP * HEAD_DIM
    zero = jnp.zeros((HEAD_DIM, tq), BF16)
    ones = jnp.ones((16, kb), BF16)

    def q_ext(q_rows, kv):
        cols = []
        for g in range(GROUP):
            qt = q_rows[g * HEAD_DIM:(g + 1) * HEAD_DIM, :]
            cols.append(jnp.concatenate([qt, zero] if kv == 0 else [zero, qt], axis=0))
        return jnp.concatenate(cols, axis=1)

    def score_piece(keys_ref, q_all, s_scr, i, m):
        s = _dot(keys_ref[i * kb:(i + 1) * kb, :], q_all)
        s_scr[i * kb:(i + 1) * kb, :] = s
        bm = jnp.max(s.reshape(kb // 8, 8, GROUP * tq), axis=0)
        return bm if m is None else jnp.maximum(m, bm)

    def value_piece(s_scr, m, kv, i, acc):
        p = jnp.exp2(s_scr[i * kb:(i + 1) * kb, :] - m).astype(BF16)
        vt1 = jnp.concatenate([vt_ref[kv * HEAD_DIM:(kv + 1) * HEAD_DIM, i * kb:(i + 1) * kb], ones],
                              axis=0)
        d = _dot(vt1, p)
        return d if acc is None else acc + d

    def col_max(m8):
        return jnp.max(m8, axis=0, keepdims=True)

    def emit(acc, kv):
        o = acc[:HEAD_DIM] / acc[HEAD_DIM:HEAD_DIM + 1]
        for g in range(GROUP):
            hd = kv * GROUP + g
            o_ref[:, hd * HEAD_DIM:(hd + 1) * HEAD_DIM] = o[:, g * tq:(g + 1) * tq].T.astype(BF16)

    @pl.when(pl.program_id(0) == 0)
    def _():
        q_all = q_ext(qt_ref[0:rows_q, :], 0)
        m = None
        for i in range(nkb):
            m = score_piece(k_ref, q_all, s0, i, m)
        m0[...] = col_max(m)

    def pair(a_keys, a_q, a_s, a_m, b_s, b_m, b_kv):
        mb = b_m[...]
        ma = acc = None
        for i in range(nkb):
            ma = score_piece(a_keys, a_q, a_s, i, ma)
            acc = value_piece(b_s, mb, b_kv, i, acc)
        a_m[...] = col_max(ma)
        emit(acc, b_kv)

    pair(k_ref, q_ext(qt_ref[rows_q:2 * rows_q, :], 1), s1, m1, s0, m0, 0)
    pair(kn_ref, q_ext(qtn_ref[...], 0), s0, m0, s1, m1, 1)


def _attention(qt, k, vt, batch):
    ntok = qt.shape[1]
    tk = k.shape[0] // batch
    tq = ATT_Q_TILE
    nq = ntok // batch // tq
    steps = batch * nq
    nxt = lambda t: jnp.minimum(t + 1, steps - 1)
    return pl.pallas_call(
        _attn_kernel,
        grid=(steps,),
        in_specs=[pl.BlockSpec((D_QA, tq), lambda t: (0, t)),
                  pl.BlockSpec((GROUP * HEAD_DIM, tq), lambda t: (0, nxt(t))),
                  pl.BlockSpec((tk, D_KA), lambda t: (t // nq, 0)),
                  pl.BlockSpec((tk, D_KA), lambda t: (nxt(t) // nq, 0)),
                  pl.BlockSpec((D_KA, tk), lambda t: (0, t // nq))],
        out_specs=pl.BlockSpec((tq, D_QA), lambda t: (t, 0)),
        out_shape=jax.ShapeDtypeStruct((ntok, D_QA), BF16),
        scratch_shapes=[pltpu.VMEM((tk, GROUP * tq), F32), pltpu.VMEM((tk, GROUP * tq), F32),
                        pltpu.VMEM((1, GROUP * tq), F32), pltpu.VMEM((1, GROUP * tq), F32)],
        compiler_params=pltpu.CompilerParams(
            dimension_semantics=("arbitrary",), vmem_limit_bytes=VMEM_LIMIT),
        name="attention",
    )(qt, qt, k, k, vt)


def _post_kernel(og_ref, oa_ref, x_ref, mod_ref, n2_ref, fg_ref, wo_ref, w1_ref, w3_ref, w2_ref,
                 y_ref):
    D = D_MODEL
    mod = mod_ref[0]
    g1, sh2, sc2, g2 = (mod[:, i * D:(i + 1) * D] for i in range(2, 6))
    out = _dot(og_ref[...], wo_ref[0:D_V, :]) + _dot(oa_ref[...], wo_ref[D_V:D_V + D_QA, :])
    x1 = x_ref[...] + g1 * out
    hb = (_rms(x1, n2_ref[...]) * (1.0 + sc2) + sh2).astype(BF16)
    act = (_silu(_dot(hb, w1_ref[...])) * _dot(hb, w3_ref[...])).astype(BF16)
    x2 = x1 + g2 * _dot(act, w2_ref[...])
    y_ref[...] = _rms(x2, fg_ref[...])


def _post_mixer(o_gla, o_att, x, mod3, mod_row0, tokens_per_mod, weights, tm):
    ntok = x.shape[0]
    tiles_per_mod = tokens_per_mod // tm
    whole = pl.BlockSpec(memory_space=pltpu.VMEM)
    row = lambda w: pl.BlockSpec((tm, w), lambda i: (i, 0))
    return pl.pallas_call(
        _post_kernel,
        grid=(ntok // tm,),
        in_specs=[row(D_V), row(D_QA), row(D_MODEL),
                  pl.BlockSpec((1, 1, N_MOD * D_MODEL), lambda i: (mod_row0 + i // tiles_per_mod, 0, 0)),
                  whole, whole, whole, whole, whole, whole],
        out_specs=row(D_MODEL),
        out_shape=jax.ShapeDtypeStruct((ntok, D_MODEL), F32),
        compiler_params=pltpu.CompilerParams(
            dimension_semantics=("arbitrary",), vmem_limit_bytes=VMEM_LIMIT),
        name="post_mixer",
    )(o_gla, o_att, x, mod3, *weights)


def _rope_tables(t):
    rows = t // GRID_W
    row = jnp.repeat(jnp.arange(rows, dtype=F32), GRID_W)
    col = jnp.tile(jnp.arange(GRID_W, dtype=F32), rows)
    inv = ROPE_THETA ** (-jnp.arange(0, ROPE_AXIS_DIM, 2, dtype=F32) / ROPE_AXIS_DIM)
    ang_r = row[:, None] * inv
    ang_c = col[:, None] * inv
    cos = jnp.concatenate([jnp.cos(ang_r)] * 2 + [jnp.cos(ang_c)] * 2, axis=-1)
    sin = jnp.concatenate([-jnp.sin(ang_r), jnp.sin(ang_r), -jnp.sin(ang_c), jnp.sin(ang_c)], axis=-1)
    return jnp.tile(cos, (1, ATT_KV_HEADS)), jnp.tile(sin, (1, ATT_KV_HEADS))


def _block_mean(width, group):
    return jnp.asarray(np.kron(np.eye(width // group), np.ones((group, group))) / group, BF16)


def kernel(x_prompt, x_sample, cache_k, cache_v, state_gla_fwd, state_gla_bwd, c, c_ctx,
           ada_w, ada_b, norm1_g, norm2_g, w_in, w_gk2, b_gk2, gla_norm_g, q_norm_g, k_norm_g,
           w_out, w_ffn1, w_ffn3, w_ffn2, final_g):
    B, T, D = x_prompt.shape
    BL, TL, _ = x_sample.shape
    TP = cache_k.shape[2]
    l = 0

    cond = jnp.zeros((COND_ROWS, D), F32).at[0].set(c_ctx).at[1:1 + BL].set(c)
    mod3 = _adaln(cond, ada_w[l], ada_b[l]).reshape(COND_ROWS, 1, N_MOD * D)

    offs = np.cumsum([0, D_QK, D_QK, D_V, D_V, 2 * GLA_LOWRANK, D_QA, D_KA, D_KA])
    seg = lambda i: w_in[l][:, offs[i]:offs[i + 1]]
    w_in_r = jnp.concatenate([seg(0), seg(1), seg(2), seg(3), seg(5), seg(6), seg(7), seg(4)],
                             axis=1).astype(BF16)
    zero = jnp.zeros((GLA_LOWRANK, D_QK), F32)
    w_gk = jnp.concatenate([jnp.concatenate([w_gk2[l, 0], zero], axis=1),
                            jnp.concatenate([zero, w_gk2[l, 1]], axis=1)], axis=0).astype(BF16)
    pre_w = (norm1_g[l].reshape(1, D), w_in_r, w_gk, b_gk2[l].reshape(1, 2 * D_QK),
             jnp.tile(q_norm_g[l], ATT_HEADS).reshape(1, D_QA),
             jnp.tile(k_norm_g[l], ATT_KV_HEADS).reshape(1, D_KA),
             _block_mean(D_QA, HEAD_DIM), _block_mean(D_KA, HEAD_DIM))
    post_w = (norm2_g[l].reshape(1, D), final_g.reshape(1, D), w_out[l].astype(BF16),
              w_ffn1[l].astype(BF16), w_ffn3[l].astype(BF16), w_ffn2[l].astype(BF16))
    gla_g = gla_norm_g[l].reshape(1, GLA_DV)
    tm = 512

    xc = x_prompt.reshape(B * T, D)
    q, k, v, gate, g, qa, ka, va, kc, vc = _pre_mixer(
        xc, mod3, 0, B * T, T, pre_w, None, True, tm)
    r3 = lambda a, b, t: a.reshape(b, t, a.shape[-1])
    o_gla, s_f, s_b = _gla(r3(q, B, T), r3(k, B, T), r3(v, B, T), r3(g, B, T), r3(gate, B, T),
                           gla_g, None, 4)
    o_att = _attention(qa, ka, va, B)
    y_prompt = _post_mixer(o_gla.reshape(B * T, D_V), o_att, xc, mod3, 0, B * T, post_w, tm)

    xl = x_sample.reshape(BL * TL, D)
    q, k, v, gate, g, qa, ka, va = _pre_mixer(
        xl, mod3, 1, TL, TL, pre_w, _rope_tables(TL), False, tm)
    o_gla, _, _ = _gla(r3(q, BL, TL), r3(k, BL, TL), r3(v, BL, TL), r3(g, BL, TL), r3(gate, BL, TL),
                       gla_g, (state_gla_fwd[:, l].reshape(BL, D_QK, GLA_DV),
                               state_gla_bwd[:, l].reshape(BL, D_QK, GLA_DV)), BL)

    k_old = cache_k[:, l].reshape(BL, TP, D_KA).astype(BF16)
    k_all = jnp.concatenate([ka.reshape(BL, TL, D_KA), k_old], axis=1).reshape(BL * (TL + TP), D_KA)
    vt_new = va.reshape(D_KA, BL, TL)
    vt_old = jnp.transpose(cache_v[:, l], (2, 3, 0, 1)).reshape(D_KA, BL, TP).astype(BF16)
    vt_all = jnp.concatenate([vt_new, vt_old], axis=2).reshape(D_KA, BL * (TL + TP))
    o_att = _attention(qa, k_all, vt_all, BL)
    y_sample = _post_mixer(o_gla.reshape(BL * TL, D_V), o_att, xl, mod3, 1, TL, post_w, tm)

    def cache_out(ct):
        return jnp.transpose(ct.reshape(B, 1, ATT_KV_HEADS, HEAD_DIM, T), (0, 1, 4, 2, 3))

    return (y_prompt.reshape(B, T, D), y_sample.reshape(BL, TL, D), cache_out(kc), cache_out(vc),
            s_f.reshape(B, 1, GLA_HEADS, GLA_DK, GLA_DV), s_b.reshape(B, 1, GLA_HEADS, GLA_DK, GLA_DV))
```

```python
import functools

import numpy as np
import jax
import jax.numpy as jnp
from jax import lax
from jax.experimental import pallas as pl
from jax.experimental.pallas import tpu as pltpu

F32 = jnp.float32
BF16 = jnp.bfloat16

D_MODEL = 1024
GRID_W = 64
GLA_HEADS = 4
GLA_DK = 64
GLA_DV = 128
GLA_LOWRANK = 16
GLA_GATE_NORM = 16.0
GLA_CHUNK = 64
ATT_HEADS = 8
ATT_KV_HEADS = 2
HEAD_DIM = 64
ROPE_AXIS_DIM = HEAD_DIM // 2
ROPE_THETA = 10000.0
D_FF = -(-8 * D_MODEL // (3 * 256)) * 256
N_MOD = 6
EPS = 1e-6

D_QK = GLA_HEADS * GLA_DK
D_V = GLA_HEADS * GLA_DV
D_QA = ATT_HEADS * HEAD_DIM
D_KA = ATT_KV_HEADS * HEAD_DIM
GROUP = ATT_HEADS // ATT_KV_HEADS
O_Q, O_K, O_V, O_OG = 0, D_QK, 2 * D_QK, 2 * D_QK + D_V
O_QA = O_OG + D_V
O_KA = O_QA + D_QA
O_VA = O_KA + D_KA
O_LR = O_VA + D_KA
D_PROJ = O_LR + 2 * GLA_LOWRANK

COND_ROWS = 8
VMEM_LIMIT = 56 * 1024 * 1024


def _dot(a, b):
    return jnp.dot(a, b, preferred_element_type=F32)


def _dot_nt(a, b):
    return lax.dot_general(a, b, (((1,), (1,)), ((), ())), preferred_element_type=F32)


def _dot_tn(a, b):
    return lax.dot_general(a, b, (((0,), (0,)), ((), ())), preferred_element_type=F32)


def _rms(x, g):
    ms = jnp.mean(x * x, axis=-1, keepdims=True)
    return x * lax.rsqrt(ms + EPS) * g


def _silu(x):
    return x * jax.nn.sigmoid(x)


def _adaln_kernel(c_ref, w_ref, b_ref, o_ref):
    s = _silu(c_ref[...])
    o_ref[...] = _dot(s.astype(BF16), w_ref[...].astype(BF16)) + b_ref[...]


def _adaln(cond, ada_w, ada_b):
    n = ada_w.shape[1]
    tn = 1536
    return pl.pallas_call(
        _adaln_kernel,
        grid=(n // tn,),
        in_specs=[
            pl.BlockSpec((COND_ROWS, D_MODEL), lambda j: (0, 0)),
            pl.BlockSpec((D_MODEL, tn), lambda j: (0, j)),
            pl.BlockSpec((1, tn), lambda j: (0, j)),
        ],
        out_specs=pl.BlockSpec((COND_ROWS, tn), lambda j: (0, j)),
        out_shape=jax.ShapeDtypeStruct((COND_ROWS, n), F32),
        compiler_params=pltpu.CompilerParams(
            dimension_semantics=("arbitrary",), vmem_limit_bytes=VMEM_LIMIT),
        name="adaln",
    )(cond, ada_w, ada_b.reshape(1, n))


def _swap_halves(x, lane_lo):
    n = x.shape[-1]
    up = pltpu.roll(x, n - ROPE_AXIS_DIM // 2, axis=1)
    dn = pltpu.roll(x, ROPE_AXIS_DIM // 2, axis=1)
    return jnp.where(lane_lo, up, dn)


def _pre_kernel(*refs, rope, emit_cache):
    it = iter(refs)
    x_ref, mod_ref, n1_ref, win_ref, wgk_ref, bgk_ref = (next(it) for _ in range(6))
    qg_ref, kg_ref, bdq_ref, bdk_ref = (next(it) for _ in range(4))
    if rope:
        cos_ref, sin_ref = next(it), next(it)
    q_o, k_o, v_o, gate_o, g_o, qa_o, ka_o, va_o = (next(it) for _ in range(8))
    if emit_cache:
        kc_o, vc_o = next(it), next(it)

    mod = mod_ref[0]
    sh1 = mod[:, 0:D_MODEL]
    sc1 = mod[:, D_MODEL:2 * D_MODEL]
    h = _rms(x_ref[...], n1_ref[...]) * (1.0 + sc1) + sh1
    p = _dot(h.astype(BF16), win_ref[...])

    q_o[...] = p[:, O_Q:O_Q + D_QK].astype(BF16)
    k_o[...] = p[:, O_K:O_K + D_QK].astype(BF16)
    v_o[...] = p[:, O_V:O_V + D_V].astype(BF16)
    gate_o[...] = _silu(p[:, O_OG:O_OG + D_V]).astype(BF16)

    gk = _dot(p[:, O_LR:O_LR + 2 * GLA_LOWRANK].astype(BF16), wgk_ref[...]) + bgk_ref[...]
    g_o[...] = (jnp.minimum(gk, 0.0) - jnp.log(1.0 + jnp.exp(-jnp.abs(gk)))) * (1.0 / GLA_GATE_NORM)

    qa = p[:, O_QA:O_QA + D_QA]
    ka = p[:, O_KA:O_KA + D_KA]
    va = p[:, O_VA:O_VA + D_KA]
    qa = qa * lax.rsqrt(_dot((qa * qa).astype(BF16), bdq_ref[...]) + EPS) * qg_ref[...]
    ka = ka * lax.rsqrt(_dot((ka * ka).astype(BF16), bdk_ref[...]) + EPS) * kg_ref[...]
    if rope:
        lane_q = lax.broadcasted_iota(jnp.int32, qa.shape, 1)
        lane_k = lax.broadcasted_iota(jnp.int32, ka.shape, 1)
        lo_q = (lane_q % ROPE_AXIS_DIM) < ROPE_AXIS_DIM // 2
        lo_k = (lane_k % ROPE_AXIS_DIM) < ROPE_AXIS_DIM // 2
        cos_k = cos_ref[...]
        sin_k = sin_ref[...]
        cos_q = jnp.concatenate([cos_k] * GROUP, axis=1)
        sin_q = jnp.concatenate([sin_k] * GROUP, axis=1)
        qa = qa * cos_q + _swap_halves(qa, lo_q) * sin_q
        ka = ka * cos_k + _swap_halves(ka, lo_k) * sin_k
    vat = va.T
    if emit_cache:
        kat = ka.T
        t = kc_o.shape[2]
        for sq in range(kc_o.shape[0]):
            kc_o[sq] = kat[:, sq * t:(sq + 1) * t]
            vc_o[sq] = vat[:, sq * t:(sq + 1) * t]
    qa_o[...] = (qa * (HEAD_DIM ** -0.5 * LOG2E)).T.astype(BF16)
    ka_o[...] = ka.astype(BF16)
    va_o[...] = vat.astype(BF16)


def _pre_mixer(x, mod3, mod_row0, tokens_per_mod, seq_len, weights, rope_tabs, emit_cache, tm):
    ntok = x.shape[0]
    n1, w_in_r, w_gk, b_gk, qg, kg, bdq, bdk = weights
    rope = rope_tabs is not None
    tiles_per_mod = tokens_per_mod // tm
    tiles_per_seq = seq_len // tm
    whole = pl.BlockSpec(memory_space=pltpu.VMEM)
    row = lambda w: pl.BlockSpec((tm, w), lambda i: (i, 0))
    in_specs = [
        row(D_MODEL),
        pl.BlockSpec((1, 1, N_MOD * D_MODEL), lambda i: (mod_row0 + i // tiles_per_mod, 0, 0)),
        whole, whole, whole, whole, whole, whole, whole, whole,
    ]
    args = [x, mod3, n1, w_in_r, w_gk, b_gk, qg, kg, bdq, bdk]
    if rope:
        tab = pl.BlockSpec((tm, D_KA), lambda i: (i % tiles_per_seq, 0))
        in_specs += [tab, tab]
        args += list(rope_tabs)
    col = lambda w: pl.BlockSpec((w, tm), lambda i: (0, i))
    out_specs = [row(D_QK), row(D_QK), row(D_V), row(D_V), row(2 * D_QK),
                 col(D_QA), row(D_KA), col(D_KA)]
    out_shape = [
        jax.ShapeDtypeStruct((ntok, D_QK), BF16),
        jax.ShapeDtypeStruct((ntok, D_QK), BF16),
        jax.ShapeDtypeStruct((ntok, D_V), BF16),
        jax.ShapeDtypeStruct((ntok, D_V), BF16),
        jax.ShapeDtypeStruct((ntok, 2 * D_QK), F32),
        jax.ShapeDtypeStruct((D_QA, ntok), BF16),
        jax.ShapeDtypeStruct((ntok, D_KA), BF16),
        jax.ShapeDtypeStruct((D_KA, ntok), BF16),
    ]
    if emit_cache:
        seqs = tm // seq_len
        out_specs += [pl.BlockSpec((seqs, D_KA, seq_len), lambda i: (i, 0, 0))] * 2
        out_shape += [jax.ShapeDtypeStruct((ntok // seq_len, D_KA, seq_len), F32)] * 2
    return pl.pallas_call(
        functools.partial(_pre_kernel, rope=rope, emit_cache=emit_cache),
        grid=(ntok // tm,),
        in_specs=in_specs,
        out_specs=out_specs,
        out_shape=out_shape,
        compiler_params=pltpu.CompilerParams(
            dimension_semantics=("arbitrary",), vmem_limit_bytes=VMEM_LIMIT),
        name="pre_mixer_rope" if rope else "pre_mixer",
    )(*args)


def _gla_chunks(units, q_ref, k_ref, v_ref, g_ref, half_lo):
    C = GLA_CHUNK
    n_pair = GLA_HEADS // 2
    rows = [pl.ds(pl.multiple_of(c * C, C), C) for (_, c, _, _, _, _) in units]

    g_his, bs = [], []
    for (bi, _, fwd, tri, _, _), r in zip(units, rows):
        gc = g_ref[bi, r, 0:D_QK] if fwd else g_ref[bi, r, D_QK:2 * D_QK]
        ltri = jnp.where(tri, 1.0, 0.0).astype(BF16)
        g_hi = gc.astype(BF16)
        g_lo = (gc - g_hi.astype(F32)).astype(BF16)
        g_his.append(g_hi)
        bs.append(_dot(ltri, g_hi) + _dot(ltri, g_lo))
    ones = jnp.ones((C, GLA_DV), BF16)
    decays = [jnp.exp(_dot_tn(g_hi, ones)) for g_hi in g_his]

    qts, kts, kes = [], [], []
    for (bi, _, fwd, _, _, _), r, b in zip(units, rows, bs):
        qc = q_ref[bi, r, :].astype(F32) * GLA_DK ** -0.5
        kc = k_ref[bi, r, :].astype(F32)
        btot = b[C - 1:C, :] if fwd else b[0:1, :]
        qts.append((qc * jnp.exp(b)).astype(BF16))
        kts.append((kc * jnp.exp(-b)).astype(BF16))
        kes.append((kc * jnp.exp(btot - b)).astype(BF16))

    scores, incs, qms = [], [], []
    for (bi, _, _, tri, _, _), r, qt, kt, ke in zip(units, rows, qts, kts, kes):
        for pair in range(n_pair):
            lanes = slice(pair * 128, (pair + 1) * 128)
            incs.append(_dot_tn(ke[:, lanes], v_ref[bi, r, pair * 2 * GLA_DV:(pair + 1) * 2 * GLA_DV]))
            for hh in range(2):
                qm = jnp.where(half_lo if hh == 0 else ~half_lo, qt[:, lanes], jnp.zeros((C, 128), BF16))
                qms.append(qm)
                scores.append(jnp.where(tri, _dot_nt(qm, kt[:, lanes]), 0.0).astype(BF16))

    for ui, ((bi, _, _, _, s_ref, o_ref), r) in enumerate(zip(units, rows)):
        state = s_ref[bi]
        state_b = state.astype(BF16)
        for hd in range(GLA_HEADS):
            pair = hd // 2
            lhs = jnp.concatenate([qms[ui * GLA_HEADS + hd], scores[ui * GLA_HEADS + hd]], axis=1)
            rhs = jnp.concatenate([state_b[pair * 128:(pair + 1) * 128, :],
                                   v_ref[bi, r, hd * GLA_DV:(hd + 1) * GLA_DV]], axis=0)
            o_ref[bi, r, hd * GLA_DV:(hd + 1) * GLA_DV] = _dot(lhs, rhs)
        inc = jnp.concatenate(
            [incs[ui * n_pair + hd // 2][(hd % 2) * GLA_DK:(hd % 2 + 1) * GLA_DK,
                                         (hd % 2) * GLA_DV:(hd % 2 + 1) * GLA_DV]
             for hd in range(GLA_HEADS)], axis=0)
        s_ref[bi] = decays[ui] * state + inc


def _gla_kernel(*args, zero_init):
    q_ref, k_ref, v_ref, g_ref, gate_ref, gn_ref = args[:6]
    o_ref, sf_ref, sb_ref, of_scr, ob_scr = args[-5:]
    C = GLA_CHUNK
    nb, t, _ = q_ref.shape
    n = t // C
    row = lax.broadcasted_iota(jnp.int32, (C, C), 0)
    col = lax.broadcasted_iota(jnp.int32, (C, C), 1)
    half_lo = lax.broadcasted_iota(jnp.int32, (C, 128), 1) < GLA_DK
    refs = (q_ref, k_ref, v_ref, g_ref)
    if zero_init:
        sf_ref[...] = jnp.zeros(sf_ref.shape, F32)
        sb_ref[...] = jnp.zeros(sb_ref.shape, F32)
    else:
        sf_ref[...] = args[6][...]
        sb_ref[...] = args[7][...]

    def body(i, carry):
        units = []
        for bi in range(nb):
            units.append((bi, i, True, row >= col, sf_ref, of_scr))
            units.append((bi, n - 1 - i, False, row <= col, sb_ref, ob_scr))
        _gla_chunks(units, *refs, half_lo)
        return carry

    lax.fori_loop(0, n, body, 0)

    rt = GLA_OUT_ROWS
    gn = gn_ref[...]

    def finish(i, carry):
        rows = pl.ds(pl.multiple_of(i * rt, rt), rt)
        for bi in range(nb):
            o = of_scr[bi, rows, :] + ob_scr[bi, rows, :]
            gate = gate_ref[bi, rows, :].astype(F32)
            for hd in range(GLA_HEADS):
                sl = slice(hd * GLA_DV, (hd + 1) * GLA_DV)
                o_ref[bi, rows, sl] = (_rms(o[:, sl], gn) * gate[:, sl]).astype(BF16)
        return carry

    lax.fori_loop(0, t // rt, finish, 0)


GLA_OUT_ROWS = 64


def _gla(q, k, v, g, gate, gla_g, init_states, nb):
    b, t, _ = q.shape
    seq = lambda w: pl.BlockSpec((nb, t, w), lambda i: (i, 0, 0))
    st = pl.BlockSpec((nb, D_QK, GLA_DV), lambda i: (i, 0, 0))
    zero_init = init_states is None
    return pl.pallas_call(
        functools.partial(_gla_kernel, zero_init=zero_init),
        grid=(b // nb,),
        in_specs=[seq(D_QK), seq(D_QK), seq(D_V), seq(2 * D_QK), seq(D_V),
                  pl.BlockSpec((1, GLA_DV), lambda i: (0, 0))] + ([] if zero_init else [st, st]),
        out_specs=[seq(D_V), st, st],
        out_shape=[jax.ShapeDtypeStruct((b, t, D_V), BF16),
                   jax.ShapeDtypeStruct((b, D_QK, GLA_DV), F32),
                   jax.ShapeDtypeStruct((b, D_QK, GLA_DV), F32)],
        scratch_shapes=[pltpu.VMEM((nb, t, D_V), F32), pltpu.VMEM((nb, t, D_V), F32)],
        compiler_params=pltpu.CompilerParams(
            dimension_semantics=("arbitrary",), vmem_limit_bytes=VMEM_LIMIT),
        name="gla",
    )(q, k, v, g, gate, gla_g, *(() if zero_init else init_states))


ATT_KEY_BLOCK = 256
ATT_Q_TILE = 256
LOG2E = 1.4426950408889634


ATT_TILES_PER_STEP = 4


def _attn_kernel(qt_ref, k_ref, vt_ref, o_ref, s0, s1, m0, m1, *, keys_per_tile):
    tq = ATT_Q_TILE
    tiles = qt_ref.shape[1] // tq
    tk = k_ref.shape[0] // tiles if keys_per_tile else k_ref.shape[0]
    kb = min(ATT_KEY_BLOCK, tk)
    nkb = tk // kb
    zero = jnp.zeros((HEAD_DIM, tq), BF16)
    ones = jnp.ones((16, kb), BF16)
    items = [(qi, kv) for qi in range(tiles) for kv in range(ATT_KV_HEADS)]
    bufs = (s0, s1)
    mbufs = (m0, m1)

    def q_ext(qi, kv):
        cols = []
        for g in range(GROUP):
            hd = kv * GROUP + g
            qt = qt_ref[hd * HEAD_DIM:(hd + 1) * HEAD_DIM, qi * tq:(qi + 1) * tq]
            cols.append(jnp.concatenate([qt, zero] if kv == 0 else [zero, qt], axis=0))
        return jnp.concatenate(cols, axis=1)

    def key0(qi):
        return qi * tk if keys_per_tile else 0

    def score_piece(n, q_all, i, m):
        r0 = key0(items[n][0]) + i * kb
        s = _dot(k_ref[r0:r0 + kb, :], q_all)
        bufs[n % 2][i * kb:(i + 1) * kb, :] = s
        for r in range(kb // 8):
            tile = s[r * 8:(r + 1) * 8, :]
            m = tile if m is None else jnp.maximum(m, tile)
        return m

    def value_piece(n, m, i, acc):
        qi, kv = items[n]
        c0 = key0(qi) + i * kb
        p = jnp.exp2(bufs[n % 2][i * kb:(i + 1) * kb, :] - m).astype(BF16)
        vt1 = jnp.concatenate([vt_ref[kv * HEAD_DIM:(kv + 1) * HEAD_DIM, c0:c0 + kb], ones], axis=0)
        d = _dot(vt1, p)
        return d if acc is None else acc + d

    def emit(n, acc):
        qi, kv = items[n]
        o = acc[:HEAD_DIM] / acc[HEAD_DIM:HEAD_DIM + 1]
        for g in range(0, GROUP, 2):
            two = jnp.concatenate([o[:, g * tq:(g + 1) * tq], o[:, (g + 1) * tq:(g + 2) * tq]], axis=0)
            hd = kv * GROUP + g
            o_ref[qi * tq:(qi + 1) * tq, hd * HEAD_DIM:(hd + 2) * HEAD_DIM] = two.T.astype(BF16)

    def pair(n):
        q_all = q_ext(*items[n]) if n < len(items) else None
        m_prev = mbufs[(n - 1) % 2][...] if n > 0 else None
        m_new = acc = None
        for i in range(nkb):
            if n < len(items):
                m_new = score_piece(n, q_all, i, m_new)
            if n > 0:
                acc = value_piece(n - 1, m_prev, i, acc)
        if n > 0:
            emit(n - 1, acc)
        if n < len(items):
            mbufs[n % 2][...] = jnp.max(m_new, axis=0, keepdims=True)

    for n in range(len(items) + 1):
        if nkb > 1:
            pl.when(pl.program_id(0) > -(n + 1))(functools.partial(pair, n))
        else:
            pair(n)


def _attention(qt, k, vt, batch):
    ntok = qt.shape[1]
    tk = k.shape[0] // batch
    tq = ATT_Q_TILE
    tiles = ATT_TILES_PER_STEP
    nq = ntok // batch // tq
    keys_per_tile = nq < tiles
    if keys_per_tile:
        seqs = tiles // nq
        k_spec = pl.BlockSpec((seqs * tk, D_KA), lambda t: (t, 0))
        vt_spec = pl.BlockSpec((D_KA, seqs * tk), lambda t: (0, t))
    else:
        spq = nq // tiles
        k_spec = pl.BlockSpec((tk, D_KA), lambda t: (t // spq, 0))
        vt_spec = pl.BlockSpec((D_KA, tk), lambda t: (0, t // spq))
    return pl.pallas_call(
        functools.partial(_attn_kernel, keys_per_tile=keys_per_tile),
        grid=(ntok // (tiles * tq),),
        in_specs=[pl.BlockSpec((D_QA, tiles * tq), lambda t: (0, t)), k_spec, vt_spec],
        out_specs=pl.BlockSpec((tiles * tq, D_QA), lambda t: (t, 0)),
        out_shape=jax.ShapeDtypeStruct((ntok, D_QA), BF16),
        scratch_shapes=[pltpu.VMEM((tk, GROUP * tq), F32), pltpu.VMEM((tk, GROUP * tq), F32),
                        pltpu.VMEM((1, GROUP * tq), F32), pltpu.VMEM((1, GROUP * tq), F32)],
        compiler_params=pltpu.CompilerParams(
            dimension_semantics=("arbitrary",), vmem_limit_bytes=VMEM_LIMIT),
        name="attention",
    )(qt, k, vt)


def _post_kernel(og_ref, oa_ref, x_ref, mod_ref, n2_ref, fg_ref, wo_ref, w1_ref, w3_ref, w2_ref,
                 y_ref):
    D = D_MODEL
    mod = mod_ref[0]
    g1, sh2, sc2, g2 = (mod[:, i * D:(i + 1) * D] for i in range(2, 6))
    out = _dot(og_ref[...], wo_ref[0:D_V, :]) + _dot(oa_ref[...], wo_ref[D_V:D_V + D_QA, :])
    x1 = x_ref[...] + g1 * out
    hb = (_rms(x1, n2_ref[...]) * (1.0 + sc2) + sh2).astype(BF16)
    act = (_silu(_dot(hb, w1_ref[...])) * _dot(hb, w3_ref[...])).astype(BF16)
    x2 = x1 + g2 * _dot(act, w2_ref[...])
    y_ref[...] = _rms(x2, fg_ref[...])


def _post_mixer(o_gla, o_att, x, mod3, mod_row0, tokens_per_mod, weights, tm):
    ntok = x.shape[0]
    tiles_per_mod = tokens_per_mod // tm
    whole = pl.BlockSpec(memory_space=pltpu.VMEM)
    row = lambda w: pl.BlockSpec((tm, w), lambda i: (i, 0))
    return pl.pallas_call(
        _post_kernel,
        grid=(ntok // tm,),
        in_specs=[row(D_V), row(D_QA), row(D_MODEL),
                  pl.BlockSpec((1, 1, N_MOD * D_MODEL), lambda i: (mod_row0 + i // tiles_per_mod, 0, 0)),
                  whole, whole, whole, whole, whole, whole],
        out_specs=row(D_MODEL),
        out_shape=jax.ShapeDtypeStruct((ntok, D_MODEL), F32),
        compiler_params=pltpu.CompilerParams(
            dimension_semantics=("arbitrary",), vmem_limit_bytes=VMEM_LIMIT),
        name="post_mixer",
    )(o_gla, o_att, x, mod3, *weights)


def _rope_tables(t):
    rows = t // GRID_W
    row = jnp.repeat(jnp.arange(rows, dtype=F32), GRID_W)
    col = jnp.tile(jnp.arange(GRID_W, dtype=F32), rows)
    inv = ROPE_THETA ** (-jnp.arange(0, ROPE_AXIS_DIM, 2, dtype=F32) / ROPE_AXIS_DIM)
    ang_r = row[:, None] * inv
    ang_c = col[:, None] * inv
    cos = jnp.concatenate([jnp.cos(ang_r)] * 2 + [jnp.cos(ang_c)] * 2, axis=-1)
    sin = jnp.concatenate([-jnp.sin(ang_r), jnp.sin(ang_r), -jnp.sin(ang_c), jnp.sin(ang_c)], axis=-1)
    return jnp.tile(cos, (1, ATT_KV_HEADS)), jnp.tile(sin, (1, ATT_KV_HEADS))


def _block_mean(width, group):
    return jnp.asarray(np.kron(np.eye(width // group), np.ones((group, group))) / group, BF16)


def kernel(x_prompt, x_sample, cache_k, cache_v, state_gla_fwd, state_gla_bwd, c, c_ctx,
           ada_w, ada_b, norm1_g, norm2_g, w_in, w_gk2, b_gk2, gla_norm_g, q_norm_g, k_norm_g,
           w_out, w_ffn1, w_ffn3, w_ffn2, final_g):
    B, T, D = x_prompt.shape
    BL, TL, _ = x_sample.shape
    TP = cache_k.shape[2]
    l = 0

    cond = jnp.zeros((COND_ROWS, D), F32).at[0].set(c_ctx).at[1:1 + BL].set(c)
    mod3 = _adaln(cond, ada_w[l], ada_b[l]).reshape(COND_ROWS, 1, N_MOD * D)

    offs = np.cumsum([0, D_QK, D_QK, D_V, D_V, 2 * GLA_LOWRANK, D_QA, D_KA, D_KA])
    seg = lambda i: w_in[l][:, offs[i]:offs[i + 1]]
    w_in_r = jnp.concatenate([seg(0), seg(1), seg(2), seg(3), seg(5), seg(6), seg(7), seg(4)],
                             axis=1).astype(BF16)
    zero = jnp.zeros((GLA_LOWRANK, D_QK), F32)
    w_gk = jnp.concatenate([jnp.concatenate([w_gk2[l, 0], zero], axis=1),
                            jnp.concatenate([zero, w_gk2[l, 1]], axis=1)], axis=0).astype(BF16)
    pre_w = (norm1_g[l].reshape(1, D), w_in_r, w_gk, b_gk2[l].reshape(1, 2 * D_QK),
             jnp.tile(q_norm_g[l], ATT_HEADS).reshape(1, D_QA),
             jnp.tile(k_norm_g[l], ATT_KV_HEADS).reshape(1, D_KA),
             _block_mean(D_QA, HEAD_DIM), _block_mean(D_KA, HEAD_DIM))
    post_w = (norm2_g[l].reshape(1, D), final_g.reshape(1, D), w_out[l].astype(BF16),
              w_ffn1[l].astype(BF16), w_ffn3[l].astype(BF16), w_ffn2[l].astype(BF16))
    gla_g = gla_norm_g[l].reshape(1, GLA_DV)
    tm = 512

    xc = x_prompt.reshape(B * T, D)
    q, k, v, gate, g, qa, ka, va, kc, vc = _pre_mixer(
        xc, mod3, 0, B * T, T, pre_w, None, True, tm)
    r3 = lambda a, b, t: a.reshape(b, t, a.shape[-1])
    o_gla, s_f, s_b = _gla(r3(q, B, T), r3(k, B, T), r3(v, B, T), r3(g, B, T), r3(gate, B, T),
                           gla_g, None, 4)
    o_att = _attention(qa, ka, va, B)
    y_prompt = _post_mixer(o_gla.reshape(B * T, D_V), o_att, xc, mod3, 0, B * T, post_w, tm)

    xl = x_sample.reshape(BL * TL, D)
    q, k, v, gate, g, qa, ka, va = _pre_mixer(
        xl, mod3, 1, TL, TL, pre_w, _rope_tables(TL), False, tm)
    o_gla, _, _ = _gla(r3(q, BL, TL), r3(k, BL, TL), r3(v, BL, TL), r3(g, BL, TL), r3(gate, BL, TL),
                       gla_g, (state_gla_fwd[:, l].reshape(BL, D_QK, GLA_DV),
                               state_gla_bwd[:, l].reshape(BL, D_QK, GLA_DV)), BL)

    k_old = cache_k[:, l].reshape(BL, TP, D_KA).astype(BF16)
    k_all = jnp.concatenate([ka.reshape(BL, TL, D_KA), k_old], axis=1).reshape(BL * (TL + TP), D_KA)
    vt_new = va.reshape(D_KA, BL, TL)
    vt_old = jnp.transpose(cache_v[:, l], (2, 3, 0, 1)).reshape(D_KA, BL, TP).astype(BF16)
    vt_all = jnp.concatenate([vt_new, vt_old], axis=2).reshape(D_KA, BL * (TL + TP))
    o_att = _attention(qa, k_all, vt_all, BL)
    y_sample = _post_mixer(o_gla.reshape(BL * TL, D_V), o_att, xl, mod3, 1, TL, post_w, tm)

    def cache_out(ct):
        return jnp.transpose(ct.reshape(B, 1, ATT_KV_HEADS, HEAD_DIM, T), (0, 1, 4, 2, 3))

    return (y_prompt.reshape(B, T, D), y_sample.reshape(BL, TL, D), cache_out(kc), cache_out(vc),
            s_f.reshape(B, 1, GLA_HEADS, GLA_DK, GLA_DV), s_b.reshape(B, 1, GLA_HEADS, GLA_DK, GLA_DV))
```

```python
import functools

import numpy as np
import jax
import jax.numpy as jnp
from jax import lax
from jax.experimental import pallas as pl
from jax.experimental.pallas import tpu as pltpu

F32 = jnp.float32
BF16 = jnp.bfloat16

D_MODEL = 1024
GRID_W = 64
GLA_HEADS = 4
GLA_DK = 64
GLA_DV = 128
GLA_LOWRANK = 16
GLA_GATE_NORM = 16.0
GLA_CHUNK = 64
ATT_HEADS = 8
ATT_KV_HEADS = 2
HEAD_DIM = 64
ROPE_AXIS_DIM = HEAD_DIM // 2
ROPE_THETA = 10000.0
D_FF = -(-8 * D_MODEL // (3 * 256)) * 256
N_MOD = 6
EPS = 1e-6

D_QK = GLA_HEADS * GLA_DK
D_V = GLA_HEADS * GLA_DV
D_QA = ATT_HEADS * HEAD_DIM
D_KA = ATT_KV_HEADS * HEAD_DIM
GROUP = ATT_HEADS // ATT_KV_HEADS
O_Q, O_K, O_V, O_OG = 0, D_QK, 2 * D_QK, 2 * D_QK + D_V
O_QA = O_OG + D_V
O_KA = O_QA + D_QA
O_VA = O_KA + D_KA
O_LR = O_VA + D_KA
D_PROJ = O_LR + 2 * GLA_LOWRANK

COND_ROWS = 8
VMEM_LIMIT = 56 * 1024 * 1024


def _dot(a, b):
    return jnp.dot(a, b, preferred_element_type=F32)


def _dot_nt(a, b):
    return lax.dot_general(a, b, (((1,), (1,)), ((), ())), preferred_element_type=F32)


def _dot_tn(a, b):
    return lax.dot_general(a, b, (((0,), (0,)), ((), ())), preferred_element_type=F32)


def _rms(x, g):
    ms = jnp.mean(x * x, axis=-1, keepdims=True)
    return x * lax.rsqrt(ms + EPS) * g


def _silu(x):
    return x * jax.nn.sigmoid(x)


def _adaln_kernel(c_ref, w_ref, b_ref, o_ref):
    s = _silu(c_ref[...])
    o_ref[:, 0, :] = _dot(s.astype(BF16), w_ref[...].astype(BF16)) + b_ref[...]


def _adaln(cond, ada_w, ada_b):
    n = ada_w.shape[1]
    tn = 1536
    return pl.pallas_call(
        _adaln_kernel,
        grid=(n // tn,),
        in_specs=[
            pl.BlockSpec((COND_ROWS, D_MODEL), lambda j: (0, 0)),
            pl.BlockSpec((D_MODEL, tn), lambda j: (0, j)),
            pl.BlockSpec((1, tn), lambda j: (0, j)),
        ],
        out_specs=pl.BlockSpec((COND_ROWS, 1, tn), lambda j: (0, 0, j)),
        out_shape=jax.ShapeDtypeStruct((COND_ROWS, 1, n), F32),
        compiler_params=pltpu.CompilerParams(
            dimension_semantics=("arbitrary",), vmem_limit_bytes=VMEM_LIMIT),
        name="adaln",
    )(cond, ada_w, ada_b.reshape(1, n))


def _swap_halves(x, lane_lo):
    n = x.shape[-1]
    up = pltpu.roll(x, n - ROPE_AXIS_DIM // 2, axis=1)
    dn = pltpu.roll(x, ROPE_AXIS_DIM // 2, axis=1)
    return jnp.where(lane_lo, up, dn)


def _pre_kernel(*refs, rope, emit_cache):
    it = iter(refs)
    x_ref, mod_ref, n1_ref, win_ref, wgk_ref, bgk_ref = (next(it) for _ in range(6))
    qg_ref, kg_ref, bdq_ref, bdk_ref = (next(it) for _ in range(4))
    if rope:
        cos_ref, sin_ref = next(it), next(it)
    q_o, k_o, v_o, gate_o, g_o, qa_o, ka_o, va_o = (next(it) for _ in range(8))
    if emit_cache:
        kc_o, vc_o = next(it), next(it)

    mod = mod_ref[0]
    sh1 = mod[:, 0:D_MODEL]
    sc1 = mod[:, D_MODEL:2 * D_MODEL]
    h = _rms(x_ref[...], n1_ref[...]) * (1.0 + sc1) + sh1
    p = _dot(h.astype(BF16), win_ref[...])

    q_o[...] = p[:, O_Q:O_Q + D_QK].astype(BF16)
    k_o[...] = p[:, O_K:O_K + D_QK].astype(BF16)
    v_o[...] = p[:, O_V:O_V + D_V].astype(BF16)
    gate_o[...] = _silu(p[:, O_OG:O_OG + D_V]).astype(BF16)

    gk = _dot(p[:, O_LR:O_LR + 2 * GLA_LOWRANK].astype(BF16), wgk_ref[...]) + bgk_ref[...]
    g_o[...] = (jnp.minimum(gk, 0.0) - jnp.log(1.0 + jnp.exp(-jnp.abs(gk)))) * (1.0 / GLA_GATE_NORM)

    qa = p[:, O_QA:O_QA + D_QA]
    ka = p[:, O_KA:O_KA + D_KA]
    va = p[:, O_VA:O_VA + D_KA]
    qa = qa * lax.rsqrt(_dot((qa * qa).astype(BF16), bdq_ref[...]) + EPS) * qg_ref[...]
    ka = ka * lax.rsqrt(_dot((ka * ka).astype(BF16), bdk_ref[...]) + EPS) * kg_ref[...]
    if rope:
        lane_q = lax.broadcasted_iota(jnp.int32, qa.shape, 1)
        lane_k = lax.broadcasted_iota(jnp.int32, ka.shape, 1)
        lo_q = (lane_q % ROPE_AXIS_DIM) < ROPE_AXIS_DIM // 2
        lo_k = (lane_k % ROPE_AXIS_DIM) < ROPE_AXIS_DIM // 2
        cos_k = cos_ref[...]
        sin_k = sin_ref[...]
        cos_q = jnp.concatenate([cos_k] * GROUP, axis=1)
        sin_q = jnp.concatenate([sin_k] * GROUP, axis=1)
        qa = qa * cos_q + _swap_halves(qa, lo_q) * sin_q
        ka = ka * cos_k + _swap_halves(ka, lo_k) * sin_k
    vat = va.T
    if emit_cache:
        kat = ka.T
        t = kc_o.shape[2]
        for sq in range(kc_o.shape[0]):
            kc_o[sq] = kat[:, sq * t:(sq + 1) * t]
            vc_o[sq] = vat[:, sq * t:(sq + 1) * t]
    qa_o[...] = (qa * (HEAD_DIM ** -0.5 * LOG2E)).T.astype(BF16)
    ka_o[...] = ka.astype(BF16)
    va_o[...] = vat.astype(BF16)


def _pre_mixer(x, mod3, mod_row0, tokens_per_mod, seq_len, weights, rope_tabs, emit_cache, tm):
    ntok = x.shape[0]
    n1, w_in_r, w_gk, b_gk, qg, kg, bdq, bdk = weights
    rope = rope_tabs is not None
    tiles_per_mod = tokens_per_mod // tm
    tiles_per_seq = seq_len // tm
    whole = pl.BlockSpec(memory_space=pltpu.VMEM)
    row = lambda w: pl.BlockSpec((tm, w), lambda i: (i, 0))
    in_specs = [
        row(D_MODEL),
        pl.BlockSpec((1, 1, N_MOD * D_MODEL), lambda i: (mod_row0 + i // tiles_per_mod, 0, 0)),
        whole, whole, whole, whole, whole, whole, whole, whole,
    ]
    args = [x, mod3, n1, w_in_r, w_gk, b_gk, qg, kg, bdq, bdk]
    if rope:
        tab = pl.BlockSpec((tm, D_KA), lambda i: (i % tiles_per_seq, 0))
        in_specs += [tab, tab]
        args += list(rope_tabs)
    col = lambda w: pl.BlockSpec((w, tm), lambda i: (0, i))
    out_specs = [row(D_QK), row(D_QK), row(D_V), row(D_V), row(2 * D_QK),
                 col(D_QA), row(D_KA), col(D_KA)]
    out_shape = [
        jax.ShapeDtypeStruct((ntok, D_QK), BF16),
        jax.ShapeDtypeStruct((ntok, D_QK), BF16),
        jax.ShapeDtypeStruct((ntok, D_V), BF16),
        jax.ShapeDtypeStruct((ntok, D_V), BF16),
        jax.ShapeDtypeStruct((ntok, 2 * D_QK), F32),
        jax.ShapeDtypeStruct((D_QA, ntok), BF16),
        jax.ShapeDtypeStruct((ntok, D_KA), BF16),
        jax.ShapeDtypeStruct((D_KA, ntok), BF16),
    ]
    if emit_cache:
        seqs = tm // seq_len
        out_specs += [pl.BlockSpec((seqs, D_KA, seq_len), lambda i: (i, 0, 0))] * 2
        out_shape += [jax.ShapeDtypeStruct((ntok // seq_len, D_KA, seq_len), F32)] * 2
    return pl.pallas_call(
        functools.partial(_pre_kernel, rope=rope, emit_cache=emit_cache),
        grid=(ntok // tm,),
        in_specs=in_specs,
        out_specs=out_specs,
        out_shape=out_shape,
        compiler_params=pltpu.CompilerParams(
            dimension_semantics=("arbitrary",), vmem_limit_bytes=VMEM_LIMIT),
        name="pre_mixer_rope" if rope else "pre_mixer",
    )(*args)


def _gla_chunks(units, q_ref, k_ref, v_ref, g_ref, half_lo):
    C = GLA_CHUNK
    n_pair = GLA_HEADS // 2
    rows = [pl.ds(pl.multiple_of(c * C, C), C) for (_, c, _, _, _, _) in units]

    g_his, bs = [], []
    for (bi, _, fwd, tri, _, _), r in zip(units, rows):
        gc = g_ref[bi, r, 0:D_QK] if fwd else g_ref[bi, r, D_QK:2 * D_QK]
        ltri = jnp.where(tri, 1.0, 0.0).astype(BF16)
        g_hi = gc.astype(BF16)
        g_lo = (gc - g_hi.astype(F32)).astype(BF16)
        g_his.append(g_hi)
        bs.append(_dot(ltri, g_hi) + _dot(ltri, g_lo))
    ones = jnp.ones((C, GLA_DV), BF16)
    decays = [jnp.exp(_dot_tn(g_hi, ones)) for g_hi in g_his]

    qts, kts, kes = [], [], []
    for (bi, _, fwd, _, _, _), r, b in zip(units, rows, bs):
        qc = q_ref[bi, r, :].astype(F32) * GLA_DK ** -0.5
        kc = k_ref[bi, r, :].astype(F32)
        btot = b[C - 1:C, :] if fwd else b[0:1, :]
        qts.append((qc * jnp.exp(b)).astype(BF16))
        kts.append((kc * jnp.exp(-b)).astype(BF16))
        kes.append((kc * jnp.exp(btot - b)).astype(BF16))

    scores, incs, qms = [], [], []
    for (bi, _, _, tri, _, _), r, qt, kt, ke in zip(units, rows, qts, kts, kes):
        for pair in range(n_pair):
            lanes = slice(pair * 128, (pair + 1) * 128)
            incs.append(_dot_tn(ke[:, lanes], v_ref[bi, r, pair * 2 * GLA_DV:(pair + 1) * 2 * GLA_DV]))
            for hh in range(2):
                qm = jnp.where(half_lo if hh == 0 else ~half_lo, qt[:, lanes], jnp.zeros((C, 128), BF16))
                qms.append(qm)
                scores.append(jnp.where(tri, _dot_nt(qm, kt[:, lanes]), 0.0).astype(BF16))

    for ui, ((bi, _, _, _, s_ref, o_ref), r) in enumerate(zip(units, rows)):
        state = s_ref[bi]
        state_b = state.astype(BF16)
        for hd in range(GLA_HEADS):
            pair = hd // 2
            lhs = jnp.concatenate([qms[ui * GLA_HEADS + hd], scores[ui * GLA_HEADS + hd]], axis=1)
            rhs = jnp.concatenate([state_b[pair * 128:(pair + 1) * 128, :],
                                   v_ref[bi, r, hd * GLA_DV:(hd + 1) * GLA_DV]], axis=0)
            o_ref[bi, r, hd * GLA_DV:(hd + 1) * GLA_DV] = _dot(lhs, rhs)
        inc = jnp.concatenate(
            [incs[ui * n_pair + hd // 2][(hd % 2) * GLA_DK:(hd % 2 + 1) * GLA_DK,
                                         (hd % 2) * GLA_DV:(hd % 2 + 1) * GLA_DV]
             for hd in range(GLA_HEADS)], axis=0)
        s_ref[bi] = decays[ui] * state + inc


def _gla_kernel(*args, zero_init):
    q_ref, k_ref, v_ref, g_ref, gate_ref, gn_ref = args[:6]
    o_ref, sf_ref, sb_ref, of_scr, ob_scr = args[-5:]
    C = GLA_CHUNK
    nb, t, _ = q_ref.shape
    n = t // C
    row = lax.broadcasted_iota(jnp.int32, (C, C), 0)
    col = lax.broadcasted_iota(jnp.int32, (C, C), 1)
    half_lo = lax.broadcasted_iota(jnp.int32, (C, 128), 1) < GLA_DK
    refs = (q_ref, k_ref, v_ref, g_ref)
    if zero_init:
        sf_ref[...] = jnp.zeros(sf_ref.shape, F32)
        sb_ref[...] = jnp.zeros(sb_ref.shape, F32)
    else:
        sf_ref[...] = args[6][...]
        sb_ref[...] = args[7][...]

    def body(i, carry):
        units = []
        for bi in range(nb):
            units.append((bi, i, True, row >= col, sf_ref, of_scr))
            units.append((bi, n - 1 - i, False, row <= col, sb_ref, ob_scr))
        _gla_chunks(units, *refs, half_lo)
        return carry

    lax.fori_loop(0, n, body, 0)

    rt = GLA_OUT_ROWS
    gn = gn_ref[...]

    def finish(i, carry):
        rows = pl.ds(pl.multiple_of(i * rt, rt), rt)
        for bi in range(nb):
            o = of_scr[bi, rows, :] + ob_scr[bi, rows, :]
            gate = gate_ref[bi, rows, :].astype(F32)
            for hd in range(GLA_HEADS):
                sl = slice(hd * GLA_DV, (hd + 1) * GLA_DV)
                o_ref[bi, rows, sl] = (_rms(o[:, sl], gn) * gate[:, sl]).astype(BF16)
        return carry

    lax.fori_loop(0, t // rt, finish, 0)


GLA_OUT_ROWS = 64


def _gla(q, k, v, g, gate, gla_g, init_states, nb):
    b, t, _ = q.shape
    seq = lambda w: pl.BlockSpec((nb, t, w), lambda i: (i, 0, 0))
    st = pl.BlockSpec((nb, D_QK, GLA_DV), lambda i: (i, 0, 0))
    zero_init = init_states is None
    return pl.pallas_call(
        functools.partial(_gla_kernel, zero_init=zero_init),
        grid=(b // nb,),
        in_specs=[seq(D_QK), seq(D_QK), seq(D_V), seq(2 * D_QK), seq(D_V),
                  pl.BlockSpec((1, GLA_DV), lambda i: (0, 0))] + ([] if zero_init else [st, st]),
        out_specs=[seq(D_V), st, st],
        out_shape=[jax.ShapeDtypeStruct((b, t, D_V), BF16),
                   jax.ShapeDtypeStruct((b, D_QK, GLA_DV), F32),
                   jax.ShapeDtypeStruct((b, D_QK, GLA_DV), F32)],
        scratch_shapes=[pltpu.VMEM((nb, t, D_V), F32), pltpu.VMEM((nb, t, D_V), F32)],
        compiler_params=pltpu.CompilerParams(
            dimension_semantics=("arbitrary",), vmem_limit_bytes=VMEM_LIMIT),
        name="gla",
    )(q, k, v, g, gate, gla_g, *(() if zero_init else init_states))


ATT_KEY_BLOCK = 256
ATT_Q_TILE = 256
LOG2E = 1.4426950408889634


ATT_TILES_PER_STEP = 4


def _attn_kernel(*args, keys_per_tile, cached):
    qt_ref, k_ref, vt_ref = args[:3]
    kc_ref, vtc_ref = args[3:5] if cached else (None, None)
    o_ref, s0, s1, m0, m1 = args[-5:]
    tq = ATT_Q_TILE
    tiles = qt_ref.shape[1] // tq
    t_new = k_ref.shape[0] // tiles if keys_per_tile else k_ref.shape[0]
    tk = t_new + (kc_ref.shape[0] if cached else 0)
    kb = min(ATT_KEY_BLOCK, tk)
    nkb = tk // kb
    zero = jnp.zeros((HEAD_DIM, tq), BF16)
    ones = jnp.ones((16, kb), BF16)
    items = [(qi, kv) for qi in range(tiles) for kv in range(ATT_KV_HEADS)]
    bufs = (s0, s1)
    mbufs = (m0, m1)

    def q_ext(qi, kv):
        cols = []
        for g in range(GROUP):
            hd = kv * GROUP + g
            qt = qt_ref[hd * HEAD_DIM:(hd + 1) * HEAD_DIM, qi * tq:(qi + 1) * tq]
            cols.append(jnp.concatenate([qt, zero] if kv == 0 else [zero, qt], axis=0))
        return jnp.concatenate(cols, axis=1)

    def key0(qi):
        return qi * tk if keys_per_tile else 0

    def score_piece(n, q_all, i, m):
        r0 = key0(items[n][0]) + i * kb
        keys = k_ref[r0:r0 + kb, :] if i * kb < t_new else kc_ref[r0 - t_new:r0 - t_new + kb, :]
        s = _dot(keys, q_all)
        bufs[n % 2][i * kb:(i + 1) * kb, :] = s
        for r in range(kb // 8):
            tile = s[r * 8:(r + 1) * 8, :]
            m = tile if m is None else jnp.maximum(m, tile)
        return m

    def value_piece(n, m, i, acc):
        qi, kv = items[n]
        c0 = key0(qi) + i * kb
        p = jnp.exp2(bufs[n % 2][i * kb:(i + 1) * kb, :] - m).astype(BF16)
        hd_rows = slice(kv * HEAD_DIM, (kv + 1) * HEAD_DIM)
        vt = vt_ref[hd_rows, c0:c0 + kb] if i * kb < t_new else vtc_ref[hd_rows, c0 - t_new:c0 - t_new + kb]
        vt1 = jnp.concatenate([vt, ones], axis=0)
        d = _dot(vt1, p)
        return d if acc is None else acc + d

    def emit(n, acc):
        qi, kv = items[n]
        o = acc[:HEAD_DIM] / acc[HEAD_DIM:HEAD_DIM + 1]
        for g in range(0, GROUP, 2):
            two = jnp.concatenate([o[:, g * tq:(g + 1) * tq], o[:, (g + 1) * tq:(g + 2) * tq]], axis=0)
            hd = kv * GROUP + g
            o_ref[qi * tq:(qi + 1) * tq, hd * HEAD_DIM:(hd + 2) * HEAD_DIM] = two.T.astype(BF16)

    def pair(n):
        q_all = q_ext(*items[n]) if n < len(items) else None
        m_prev = mbufs[(n - 1) % 2][...] if n > 0 else None
        m_new = acc = None
        for i in range(nkb):
            if n < len(items):
                m_new = score_piece(n, q_all, i, m_new)
            if n > 0:
                acc = value_piece(n - 1, m_prev, i, acc)
        if n > 0:
            emit(n - 1, acc)
        if n < len(items):
            mbufs[n % 2][...] = jnp.max(m_new, axis=0, keepdims=True)

    for n in range(len(items) + 1):
        if nkb > 1:
            pl.when(pl.program_id(0) > -(n + 1))(functools.partial(pair, n))
        else:
            pair(n)


def _attention(qt, k, vt, batch, cache=None):
    ntok = qt.shape[1]
    t_new = k.shape[0] // batch
    tq = ATT_Q_TILE
    tiles = ATT_TILES_PER_STEP
    nq = ntok // batch // tq
    keys_per_tile = nq < tiles
    extra_specs, extra, tk = [], (), t_new
    if keys_per_tile:
        assert cache is None
        seqs = tiles // nq
        k_spec = pl.BlockSpec((seqs * t_new, D_KA), lambda t: (t, 0))
        vt_spec = pl.BlockSpec((D_KA, seqs * t_new), lambda t: (0, t))
    else:
        spq = nq // tiles
        k_spec = pl.BlockSpec((t_new, D_KA), lambda t: (t // spq, 0))
        vt_spec = pl.BlockSpec((D_KA, t_new), lambda t: (0, t // spq))
        if cache is not None:
            t_old = cache[0].shape[0] // batch
            assert t_new % ATT_KEY_BLOCK == 0 and t_old % ATT_KEY_BLOCK == 0
            extra_specs = [pl.BlockSpec((t_old, D_KA), lambda t: (t // spq, 0)),
                           pl.BlockSpec((D_KA, t_old), lambda t: (0, t // spq))]
            extra, tk = cache, t_new + t_old
    return pl.pallas_call(
        functools.partial(_attn_kernel, keys_per_tile=keys_per_tile, cached=cache is not None),
        grid=(ntok // (tiles * tq),),
        in_specs=[pl.BlockSpec((D_QA, tiles * tq), lambda t: (0, t)), k_spec, vt_spec] + extra_specs,
        out_specs=pl.BlockSpec((tiles * tq, D_QA), lambda t: (t, 0)),
        out_shape=jax.ShapeDtypeStruct((ntok, D_QA), BF16),
        scratch_shapes=[pltpu.VMEM((tk, GROUP * tq), F32), pltpu.VMEM((tk, GROUP * tq), F32),
                        pltpu.VMEM((1, GROUP * tq), F32), pltpu.VMEM((1, GROUP * tq), F32)],
        compiler_params=pltpu.CompilerParams(
            dimension_semantics=("arbitrary",), vmem_limit_bytes=VMEM_LIMIT),
        name="attention",
    )(qt, k, vt, *extra)


def _post_kernel(og_ref, oa_ref, x_ref, mod_ref, n2_ref, fg_ref, wo_ref, w1_ref, w3_ref, w2_ref,
                 y_ref):
    D = D_MODEL
    mod = mod_ref[0]
    g1, sh2, sc2, g2 = (mod[:, i * D:(i + 1) * D] for i in range(2, 6))
    out = _dot(og_ref[...], wo_ref[0:D_V, :]) + _dot(oa_ref[...], wo_ref[D_V:D_V + D_QA, :])
    x1 = x_ref[...] + g1 * out
    hb = (_rms(x1, n2_ref[...]) * (1.0 + sc2) + sh2).astype(BF16)
    act = (_silu(_dot(hb, w1_ref[...])) * _dot(hb, w3_ref[...])).astype(BF16)
    x2 = x1 + g2 * _dot(act, w2_ref[...])
    y_ref[...] = _rms(x2, fg_ref[...])


def _post_mixer(o_gla, o_att, x, mod3, mod_row0, tokens_per_mod, weights, tm):
    ntok = x.shape[0]
    tiles_per_mod = tokens_per_mod // tm
    whole = pl.BlockSpec(memory_space=pltpu.VMEM)
    row = lambda w: pl.BlockSpec((tm, w), lambda i: (i, 0))
    return pl.pallas_call(
        _post_kernel,
        grid=(ntok // tm,),
        in_specs=[row(D_V), row(D_QA), row(D_MODEL),
                  pl.BlockSpec((1, 1, N_MOD * D_MODEL), lambda i: (mod_row0 + i // tiles_per_mod, 0, 0)),
                  whole, whole, whole, whole, whole, whole],
        out_specs=row(D_MODEL),
        out_shape=jax.ShapeDtypeStruct((ntok, D_MODEL), F32),
        compiler_params=pltpu.CompilerParams(
            dimension_semantics=("arbitrary",), vmem_limit_bytes=VMEM_LIMIT),
        name="post_mixer",
    )(o_gla, o_att, x, mod3, *weights)


def _rope_tables(t):
    rows = t // GRID_W
    row = jnp.repeat(jnp.arange(rows, dtype=F32), GRID_W)
    col = jnp.tile(jnp.arange(GRID_W, dtype=F32), rows)
    inv = ROPE_THETA ** (-jnp.arange(0, ROPE_AXIS_DIM, 2, dtype=F32) / ROPE_AXIS_DIM)
    ang_r = row[:, None] * inv
    ang_c = col[:, None] * inv
    cos = jnp.concatenate([jnp.cos(ang_r)] * 2 + [jnp.cos(ang_c)] * 2, axis=-1)
    sin = jnp.concatenate([-jnp.sin(ang_r), jnp.sin(ang_r), -jnp.sin(ang_c), jnp.sin(ang_c)], axis=-1)
    return jnp.tile(cos, (1, ATT_KV_HEADS)), jnp.tile(sin, (1, ATT_KV_HEADS))


def _block_mean(width, group):
    return jnp.asarray(np.kron(np.eye(width // group), np.ones((group, group))) / group, BF16)


def kernel(x_prompt, x_sample, cache_k, cache_v, state_gla_fwd, state_gla_bwd, c, c_ctx,
           ada_w, ada_b, norm1_g, norm2_g, w_in, w_gk2, b_gk2, gla_norm_g, q_norm_g, k_norm_g,
           w_out, w_ffn1, w_ffn3, w_ffn2, final_g):
    B, T, D = x_prompt.shape
    BL, TL, _ = x_sample.shape
    TP = cache_k.shape[2]
    l = 0

    cond = jnp.zeros((COND_ROWS, D), F32).at[0].set(c_ctx).at[1:1 + BL].set(c)
    mod3 = _adaln(cond, ada_w[l], ada_b[l])

    offs = np.cumsum([0, D_QK, D_QK, D_V, D_V, 2 * GLA_LOWRANK, D_QA, D_KA, D_KA])
    seg = lambda i: w_in[l][:, offs[i]:offs[i + 1]]
    w_in_r = jnp.concatenate([seg(0), seg(1), seg(2), seg(3), seg(5), seg(6), seg(7), seg(4)],
                             axis=1).astype(BF16)
    zero = jnp.zeros((GLA_LOWRANK, D_QK), F32)
    w_gk = jnp.concatenate([jnp.concatenate([w_gk2[l, 0], zero], axis=1),
                            jnp.concatenate([zero, w_gk2[l, 1]], axis=1)], axis=0).astype(BF16)
    pre_w = (norm1_g[l].reshape(1, D), w_in_r, w_gk, b_gk2[l].reshape(1, 2 * D_QK),
             jnp.tile(q_norm_g[l], ATT_HEADS).reshape(1, D_QA),
             jnp.tile(k_norm_g[l], ATT_KV_HEADS).reshape(1, D_KA),
             _block_mean(D_QA, HEAD_DIM), _block_mean(D_KA, HEAD_DIM))
    post_w = (norm2_g[l].reshape(1, D), final_g.reshape(1, D), w_out[l].astype(BF16),
              w_ffn1[l].astype(BF16), w_ffn3[l].astype(BF16), w_ffn2[l].astype(BF16))
    gla_g = gla_norm_g[l].reshape(1, GLA_DV)
    tm = 512

    xc = x_prompt.reshape(B * T, D)
    q, k, v, gate, g, qa, ka, va, kc, vc = _pre_mixer(
        xc, mod3, 0, B * T, T, pre_w, None, True, tm)
    r3 = lambda a, b, t: a.reshape(b, t, a.shape[-1])
    o_gla, s_f, s_b = _gla(r3(q, B, T), r3(k, B, T), r3(v, B, T), r3(g, B, T), r3(gate, B, T),
                           gla_g, None, 4)
    o_att = _attention(qa, ka, va, B)
    y_prompt = _post_mixer(o_gla.reshape(B * T, D_V), o_att, xc, mod3, 0, B * T, post_w, tm)

    xl = x_sample.reshape(BL * TL, D)
    q, k, v, gate, g, qa, ka, va = _pre_mixer(
        xl, mod3, 1, TL, TL, pre_w, _rope_tables(TL), False, tm)
    o_gla, _, _ = _gla(r3(q, BL, TL), r3(k, BL, TL), r3(v, BL, TL), r3(g, BL, TL), r3(gate, BL, TL),
                       gla_g, (state_gla_fwd[:, l].reshape(BL, D_QK, GLA_DV),
                               state_gla_bwd[:, l].reshape(BL, D_QK, GLA_DV)), BL)

    k_old = cache_k[:, l].reshape(BL * TP, D_KA).astype(BF16)
    vt_old = jnp.transpose(cache_v[:, l], (2, 3, 0, 1)).reshape(D_KA, BL * TP).astype(BF16)
    o_att = _attention(qa, ka, va, BL, (k_old, vt_old))
    y_sample = _post_mixer(o_gla.reshape(BL * TL, D_V), o_att, xl, mod3, 1, TL, post_w, tm)

    def cache_out(ct):
        return jnp.transpose(ct.reshape(B, 1, ATT_KV_HEADS, HEAD_DIM, T), (0, 1, 4, 2, 3))

    return (y_prompt.reshape(B, T, D), y_sample.reshape(BL, TL, D), cache_out(kc), cache_out(vc),
            s_f.reshape(B, 1, GLA_HEADS, GLA_DK, GLA_DV), s_b.reshape(B, 1, GLA_HEADS, GLA_DK, GLA_DV))
```

```python
import functools

import numpy as np
import jax
import jax.numpy as jnp
from jax import lax
from jax.experimental import pallas as pl
from jax.experimental.pallas import tpu as pltpu

F32 = jnp.float32
BF16 = jnp.bfloat16

D_MODEL = 1024
GRID_W = 64
GLA_HEADS = 4
GLA_DK = 64
GLA_DV = 128
GLA_LOWRANK = 16
GLA_GATE_NORM = 16.0
GLA_CHUNK = 64
ATT_HEADS = 8
ATT_KV_HEADS = 2
HEAD_DIM = 64
ROPE_AXIS_DIM = HEAD_DIM // 2
ROPE_THETA = 10000.0
D_FF = -(-8 * D_MODEL // (3 * 256)) * 256
N_MOD = 6
EPS = 1e-6

D_QK = GLA_HEADS * GLA_DK
D_V = GLA_HEADS * GLA_DV
D_QA = ATT_HEADS * HEAD_DIM
D_KA = ATT_KV_HEADS * HEAD_DIM
GROUP = ATT_HEADS // ATT_KV_HEADS
O_Q, O_K, O_V, O_OG = 0, D_QK, 2 * D_QK, 2 * D_QK + D_V
O_QA = O_OG + D_V
O_KA = O_QA + D_QA
O_VA = O_KA + D_KA
O_LR = O_VA + D_KA
D_PROJ = O_LR + 2 * GLA_LOWRANK

COND_ROWS = 8
VMEM_LIMIT = 56 * 1024 * 1024


def _dot(a, b):
    return jnp.dot(a, b, preferred_element_type=F32)


def _dot_nt(a, b):
    return lax.dot_general(a, b, (((1,), (1,)), ((), ())), preferred_element_type=F32)


def _dot_tn(a, b):
    return lax.dot_general(a, b, (((0,), (0,)), ((), ())), preferred_element_type=F32)


def _rms(x, g):
    ms = jnp.mean(x * x, axis=-1, keepdims=True)
    return x * lax.rsqrt(ms + EPS) * g


def _silu(x):
    return x * jax.nn.sigmoid(x)


def _adaln_kernel(c_ref, w_ref, b_ref, o_ref):
    s = _silu(c_ref[...])
    o_ref[:, 0, :] = _dot(s.astype(BF16), w_ref[...].astype(BF16)) + b_ref[...]


def _adaln(cond, ada_w, ada_b):
    n = ada_w.shape[1]
    tn = 1536
    return pl.pallas_call(
        _adaln_kernel,
        grid=(n // tn,),
        in_specs=[
            pl.BlockSpec((COND_ROWS, D_MODEL), lambda j: (0, 0)),
            pl.BlockSpec((D_MODEL, tn), lambda j: (0, j)),
            pl.BlockSpec((1, tn), lambda j: (0, j)),
        ],
        out_specs=pl.BlockSpec((COND_ROWS, 1, tn), lambda j: (0, 0, j)),
        out_shape=jax.ShapeDtypeStruct((COND_ROWS, 1, n), F32),
        compiler_params=pltpu.CompilerParams(
            dimension_semantics=("arbitrary",), vmem_limit_bytes=VMEM_LIMIT),
        name="adaln",
    )(cond, ada_w, ada_b.reshape(1, n))


def _swap_halves(x, lane_lo):
    n = x.shape[-1]
    up = pltpu.roll(x, n - ROPE_AXIS_DIM // 2, axis=1)
    dn = pltpu.roll(x, ROPE_AXIS_DIM // 2, axis=1)
    return jnp.where(lane_lo, up, dn)


PRE_SUB_ROWS = 256


def _pre_kernel(*refs, rope, emit_cache):
    it = iter(refs)
    x_ref, mod_ref, n1_ref, win_ref, wgk_ref, bgk_ref = (next(it) for _ in range(6))
    qg_ref, kg_ref, bdq_ref, bdk_ref = (next(it) for _ in range(4))
    if rope:
        cos_ref, sin_ref = next(it), next(it)
    q_o, k_o, v_o, gate_o, g_o, qa_o, ka_o, va_o = (next(it) for _ in range(8))
    if emit_cache:
        kc_o, vc_o = next(it), next(it)

    sub = PRE_SUB_ROWS
    n_sub = x_ref.shape[0] // sub
    mod = mod_ref[0]
    sh1 = mod[:, 0:D_MODEL]
    sc1 = mod[:, D_MODEL:2 * D_MODEL]

    def project(j):
        rows = slice(j * sub, (j + 1) * sub)
        h = _rms(x_ref[rows, :], n1_ref[...]) * (1.0 + sc1) + sh1
        return _dot(h.astype(BF16), win_ref[...])

    def finish(j, p_ref):
        rows = slice(j * sub, (j + 1) * sub)
        q_o[rows, :] = p_ref[:, O_Q:O_Q + D_QK].astype(BF16)
        k_o[rows, :] = p_ref[:, O_K:O_K + D_QK].astype(BF16)
        v_o[rows, :] = p_ref[:, O_V:O_V + D_V].astype(BF16)
        gate_o[rows, :] = _silu(p_ref[:, O_OG:O_OG + D_V]).astype(BF16)

        gk = _dot(p_ref[:, O_LR:O_LR + 2 * GLA_LOWRANK].astype(BF16), wgk_ref[...]) + bgk_ref[...]
        g_o[rows, :] = ((jnp.minimum(gk, 0.0) - jnp.log(1.0 + jnp.exp(-jnp.abs(gk))))
                        * (1.0 / GLA_GATE_NORM))

        qa = p_ref[:, O_QA:O_QA + D_QA]
        ka = p_ref[:, O_KA:O_KA + D_KA]
        va = p_ref[:, O_VA:O_VA + D_KA]
        qa = qa * lax.rsqrt(_dot((qa * qa).astype(BF16), bdq_ref[...]) + EPS) * qg_ref[...]
        ka = ka * lax.rsqrt(_dot((ka * ka).astype(BF16), bdk_ref[...]) + EPS) * kg_ref[...]
        if rope:
            lane_q = lax.broadcasted_iota(jnp.int32, qa.shape, 1)
            lane_k = lax.broadcasted_iota(jnp.int32, ka.shape, 1)
            lo_q = (lane_q % ROPE_AXIS_DIM) < ROPE_AXIS_DIM // 2
            lo_k = (lane_k % ROPE_AXIS_DIM) < ROPE_AXIS_DIM // 2
            cos_k = cos_ref[rows, :]
            sin_k = sin_ref[rows, :]
            cos_q = jnp.concatenate([cos_k] * GROUP, axis=1)
            sin_q = jnp.concatenate([sin_k] * GROUP, axis=1)
            qa = qa * cos_q + _swap_halves(qa, lo_q) * sin_q
            ka = ka * cos_k + _swap_halves(ka, lo_k) * sin_k
        vat = va.T
        if emit_cache:
            kat = ka.T
            t = kc_o.shape[2]
            for sq in range(sub // t):
                kc_o[j * (sub // t) + sq] = kat[:, sq * t:(sq + 1) * t]
                vc_o[j * (sub // t) + sq] = vat[:, sq * t:(sq + 1) * t]
        qa_o[:, rows] = (qa * (HEAD_DIM ** -0.5 * LOG2E)).T.astype(BF16)
        ka_o[rows, :] = ka.astype(BF16)
        va_o[:, rows] = vat.astype(BF16)

    p_prev = None
    for n in range(n_sub + 1):
        p_new = project(n) if n < n_sub else None
        if n > 0:
            finish(n - 1, p_prev)
        p_prev = p_new


def _pre_mixer(x, mod3, mod_row0, tokens_per_mod, seq_len, weights, rope_tabs, emit_cache, tm):
    ntok = x.shape[0]
    n1, w_in_r, w_gk, b_gk, qg, kg, bdq, bdk = weights
    rope = rope_tabs is not None
    tiles_per_mod = tokens_per_mod // tm
    tiles_per_seq = seq_len // tm
    whole = pl.BlockSpec(memory_space=pltpu.VMEM)
    row = lambda w: pl.BlockSpec((tm, w), lambda i: (i, 0))
    in_specs = [
        row(D_MODEL),
        pl.BlockSpec((1, 1, N_MOD * D_MODEL), lambda i: (mod_row0 + i // tiles_per_mod, 0, 0)),
        whole, whole, whole, whole, whole, whole, whole, whole,
    ]
    args = [x, mod3, n1, w_in_r, w_gk, b_gk, qg, kg, bdq, bdk]
    if rope:
        tab = pl.BlockSpec((tm, D_KA), lambda i: (i % tiles_per_seq, 0))
        in_specs += [tab, tab]
        args += list(rope_tabs)
    col = lambda w: pl.BlockSpec((w, tm), lambda i: (0, i))
    out_specs = [row(D_QK), row(D_QK), row(D_V), row(D_V), row(2 * D_QK),
                 col(D_QA), row(D_KA), col(D_KA)]
    out_shape = [
        jax.ShapeDtypeStruct((ntok, D_QK), BF16),
        jax.ShapeDtypeStruct((ntok, D_QK), BF16),
        jax.ShapeDtypeStruct((ntok, D_V), BF16),
        jax.ShapeDtypeStruct((ntok, D_V), BF16),
        jax.ShapeDtypeStruct((ntok, 2 * D_QK), F32),
        jax.ShapeDtypeStruct((D_QA, ntok), BF16),
        jax.ShapeDtypeStruct((ntok, D_KA), BF16),
        jax.ShapeDtypeStruct((D_KA, ntok), BF16),
    ]
    if emit_cache:
        seqs = tm // seq_len
        out_specs += [pl.BlockSpec((seqs, D_KA, seq_len), lambda i: (i, 0, 0))] * 2
        out_shape += [jax.ShapeDtypeStruct((ntok // seq_len, D_KA, seq_len), F32)] * 2
    return pl.pallas_call(
        functools.partial(_pre_kernel, rope=rope, emit_cache=emit_cache),
        grid=(ntok // tm,),
        in_specs=in_specs,
        out_specs=out_specs,
        out_shape=out_shape,
        compiler_params=pltpu.CompilerParams(
            dimension_semantics=("arbitrary",), vmem_limit_bytes=VMEM_LIMIT),
        name="pre_mixer_rope" if rope else "pre_mixer",
    )(*args)


def _gla_chunks(units, q_ref, k_ref, v_ref, g_ref, half_lo):
    C = GLA_CHUNK
    n_pair = GLA_HEADS // 2
    rows = [pl.ds(pl.multiple_of(c * C, C), C) for (_, c, _, _, _, _) in units]

    g_his, bs = [], []
    for (bi, _, fwd, tri, _, _), r in zip(units, rows):
        gc = g_ref[bi, r, 0:D_QK] if fwd else g_ref[bi, r, D_QK:2 * D_QK]
        ltri = jnp.where(tri, 1.0, 0.0).astype(BF16)
        g_hi = gc.astype(BF16)
        g_lo = (gc - g_hi.astype(F32)).astype(BF16)
        g_his.append(g_hi)
        bs.append(_dot(ltri, g_hi) + _dot(ltri, g_lo))
    ones = jnp.ones((C, GLA_DV), BF16)
    decays = [jnp.exp(_dot_tn(g_hi, ones)) for g_hi in g_his]

    qts, kts, kes = [], [], []
    for (bi, _, fwd, _, _, _), r, b in zip(units, rows, bs):
        qc = q_ref[bi, r, :].astype(F32) * GLA_DK ** -0.5
        kc = k_ref[bi, r, :].astype(F32)
        btot = b[C - 1:C, :] if fwd else b[0:1, :]
        qts.append((qc * jnp.exp(b)).astype(BF16))
        kts.append((kc * jnp.exp(-b)).astype(BF16))
        kes.append((kc * jnp.exp(btot - b)).astype(BF16))

    scores, incs, qms = [], [], []
    for (bi, _, _, tri, _, _), r, qt, kt, ke in zip(units, rows, qts, kts, kes):
        for pair in range(n_pair):
            lanes = slice(pair * 128, (pair + 1) * 128)
            incs.append(_dot_tn(ke[:, lanes], v_ref[bi, r, pair * 2 * GLA_DV:(pair + 1) * 2 * GLA_DV]))
            for hh in range(2):
                qm = jnp.where(half_lo if hh == 0 else ~half_lo, qt[:, lanes], jnp.zeros((C, 128), BF16))
                qms.append(qm)
                scores.append(jnp.where(tri, _dot_nt(qm, kt[:, lanes]), 0.0).astype(BF16))

    for ui, ((bi, _, _, _, s_ref, o_ref), r) in enumerate(zip(units, rows)):
        state = s_ref[bi]
        state_b = state.astype(BF16)
        for hd in range(GLA_HEADS):
            pair = hd // 2
            lhs = jnp.concatenate([qms[ui * GLA_HEADS + hd], scores[ui * GLA_HEADS + hd]], axis=1)
            rhs = jnp.concatenate([state_b[pair * 128:(pair + 1) * 128, :],
                                   v_ref[bi, r, hd * GLA_DV:(hd + 1) * GLA_DV]], axis=0)
            o_ref[bi, r, hd * GLA_DV:(hd + 1) * GLA_DV] = _dot(lhs, rhs)
        inc = jnp.concatenate(
            [incs[ui * n_pair + hd // 2][(hd % 2) * GLA_DK:(hd % 2 + 1) * GLA_DK,
                                         (hd % 2) * GLA_DV:(hd % 2 + 1) * GLA_DV]
             for hd in range(GLA_HEADS)], axis=0)
        s_ref[bi] = decays[ui] * state + inc


def _gla_kernel(*args, zero_init):
    q_ref, k_ref, v_ref, g_ref, gate_ref, gn_ref = args[:6]
    o_ref, sf_ref, sb_ref, of_scr, ob_scr = args[-5:]
    C = GLA_CHUNK
    nb, t, _ = q_ref.shape
    n = t // C
    row = lax.broadcasted_iota(jnp.int32, (C, C), 0)
    col = lax.broadcasted_iota(jnp.int32, (C, C), 1)
    half_lo = lax.broadcasted_iota(jnp.int32, (C, 128), 1) < GLA_DK
    refs = (q_ref, k_ref, v_ref, g_ref)
    if zero_init:
        sf_ref[...] = jnp.zeros(sf_ref.shape, F32)
        sb_ref[...] = jnp.zeros(sb_ref.shape, F32)
    else:
        sf_ref[...] = args[6][...]
        sb_ref[...] = args[7][...]

    def body(i, carry):
        units = []
        for bi in range(nb):
            units.append((bi, i, True, row >= col, sf_ref, of_scr))
            units.append((bi, n - 1 - i, False, row <= col, sb_ref, ob_scr))
        _gla_chunks(units, *refs, half_lo)
        return carry

    lax.fori_loop(0, n, body, 0)

    rt = GLA_OUT_ROWS
    gn = gn_ref[...]

    def finish(i, carry):
        rows = pl.ds(pl.multiple_of(i * rt, rt), rt)
        for bi in range(nb):
            o = of_scr[bi, rows, :] + ob_scr[bi, rows, :]
            gate = gate_ref[bi, rows, :].astype(F32)
            for hd in range(GLA_HEADS):
                sl = slice(hd * GLA_DV, (hd + 1) * GLA_DV)
                o_ref[bi, rows, sl] = (_rms(o[:, sl], gn) * gate[:, sl]).astype(BF16)
        return carry

    lax.fori_loop(0, t // rt, finish, 0)


GLA_OUT_ROWS = 64


def _gla(q, k, v, g, gate, gla_g, init_states, nb):
    b, t, _ = q.shape
    seq = lambda w: pl.BlockSpec((nb, t, w), lambda i: (i, 0, 0))
    st = pl.BlockSpec((nb, D_QK, GLA_DV), lambda i: (i, 0, 0))
    zero_init = init_states is None
    return pl.pallas_call(
        functools.partial(_gla_kernel, zero_init=zero_init),
        grid=(b // nb,),
        in_specs=[seq(D_QK), seq(D_QK), seq(D_V), seq(2 * D_QK), seq(D_V),
                  pl.BlockSpec((1, GLA_DV), lambda i: (0, 0))] + ([] if zero_init else [st, st]),
        out_specs=[seq(D_V), st, st],
        out_shape=[jax.ShapeDtypeStruct((b, t, D_V), BF16),
                   jax.ShapeDtypeStruct((b, D_QK, GLA_DV), F32),
                   jax.ShapeDtypeStruct((b, D_QK, GLA_DV), F32)],
        scratch_shapes=[pltpu.VMEM((nb, t, D_V), F32), pltpu.VMEM((nb, t, D_V), F32)],
        compiler_params=pltpu.CompilerParams(
            dimension_semantics=("arbitrary",), vmem_limit_bytes=VMEM_LIMIT),
        name="gla",
    )(q, k, v, g, gate, gla_g, *(() if zero_init else init_states))


ATT_KEY_BLOCK = 256
ATT_Q_TILE = 256
LOG2E = 1.4426950408889634


ATT_TILES_PER_STEP = 4


def _attn_kernel(*args, keys_per_tile, cached):
    qt_ref, k_ref, vt_ref = args[:3]
    kc_ref, vtc_ref = args[3:5] if cached else (None, None)
    o_ref, s0, s1, m0, m1 = args[-5:]
    tq = ATT_Q_TILE
    tiles = qt_ref.shape[1] // tq
    t_new = k_ref.shape[0] // tiles if keys_per_tile else k_ref.shape[0]
    tk = t_new + (kc_ref.shape[0] if cached else 0)
    kb = min(ATT_KEY_BLOCK, tk)
    nkb = tk // kb
    zero = jnp.zeros((HEAD_DIM, tq), BF16)
    ones = jnp.ones((16, kb), BF16)
    items = [(qi, kv) for qi in range(tiles) for kv in range(ATT_KV_HEADS)]
    bufs = (s0, s1)
    mbufs = (m0, m1)

    def q_ext(qi, kv):
        cols = []
        for g in range(GROUP):
            hd = kv * GROUP + g
            qt = qt_ref[hd * HEAD_DIM:(hd + 1) * HEAD_DIM, qi * tq:(qi + 1) * tq]
            cols.append(jnp.concatenate([qt, zero] if kv == 0 else [zero, qt], axis=0))
        return jnp.concatenate(cols, axis=1)

    def key0(qi):
        return qi * tk if keys_per_tile else 0

    def score_piece(n, q_all, i, m):
        r0 = key0(items[n][0]) + i * kb
        keys = k_ref[r0:r0 + kb, :] if i * kb < t_new else kc_ref[r0 - t_new:r0 - t_new + kb, :]
        s = _dot(keys, q_all)
        bufs[n % 2][i * kb:(i + 1) * kb, :] = s
        for r in range(kb // 8):
            tile = s[r * 8:(r + 1) * 8, :]
            m = tile if m is None else jnp.maximum(m, tile)
        return m

    def value_piece(n, m, i, acc):
        qi, kv = items[n]
        c0 = key0(qi) + i * kb
        p = jnp.exp2(bufs[n % 2][i * kb:(i + 1) * kb, :] - m).astype(BF16)
        hd_rows = slice(kv * HEAD_DIM, (kv + 1) * HEAD_DIM)
        vt = vt_ref[hd_rows, c0:c0 + kb] if i * kb < t_new else vtc_ref[hd_rows, c0 - t_new:c0 - t_new + kb]
        vt1 = jnp.concatenate([vt, ones], axis=0)
        d = _dot(vt1, p)
        return d if acc is None else acc + d

    def emit(n, acc):
        qi, kv = items[n]
        o = acc[:HEAD_DIM] / acc[HEAD_DIM:HEAD_DIM + 1]
        for g in range(0, GROUP, 2):
            two = jnp.concatenate([o[:, g * tq:(g + 1) * tq], o[:, (g + 1) * tq:(g + 2) * tq]], axis=0)
            hd = kv * GROUP + g
            o_ref[qi * tq:(qi + 1) * tq, hd * HEAD_DIM:(hd + 2) * HEAD_DIM] = two.T.astype(BF16)

    def pair(n):
        q_all = q_ext(*items[n]) if n < len(items) else None
        m_prev = mbufs[(n - 1) % 2][...] if n > 0 else None
        m_new = acc = None
        for i in range(nkb):
            if n < len(items):
                m_new = score_piece(n, q_all, i, m_new)
            if n > 0:
                acc = value_piece(n - 1, m_prev, i, acc)
        if n > 0:
            emit(n - 1, acc)
        if n < len(items):
            mbufs[n % 2][...] = jnp.max(m_new, axis=0, keepdims=True)

    for n in range(len(items) + 1):
        if nkb > 1:
            pl.when(pl.program_id(0) > -(n + 1))(functools.partial(pair, n))
        else:
            pair(n)


def _attention(qt, k, vt, batch, cache=None):
    ntok = qt.shape[1]
    t_new = k.shape[0] // batch
    tq = ATT_Q_TILE
    tiles = ATT_TILES_PER_STEP
    nq = ntok // batch // tq
    keys_per_tile = nq < tiles
    extra_specs, extra, tk = [], (), t_new
    if keys_per_tile:
        assert cache is None
        seqs = tiles // nq
        k_spec = pl.BlockSpec((seqs * t_new, D_KA), lambda t: (t, 0))
        vt_spec = pl.BlockSpec((D_KA, seqs * t_new), lambda t: (0, t))
    else:
        spq = nq // tiles
        k_spec = pl.BlockSpec((t_new, D_KA), lambda t: (t // spq, 0))
        vt_spec = pl.BlockSpec((D_KA, t_new), lambda t: (0, t // spq))
        if cache is not None:
            t_old = cache[0].shape[0] // batch
            assert t_new % ATT_KEY_BLOCK == 0 and t_old % ATT_KEY_BLOCK == 0
            extra_specs = [pl.BlockSpec((t_old, D_KA), lambda t: (t // spq, 0)),
                           pl.BlockSpec((D_KA, t_old), lambda t: (0, t // spq))]
            extra, tk = cache, t_new + t_old
    return pl.pallas_call(
        functools.partial(_attn_kernel, keys_per_tile=keys_per_tile, cached=cache is not None),
        grid=(ntok // (tiles * tq),),
        in_specs=[pl.BlockSpec((D_QA, tiles * tq), lambda t: (0, t)), k_spec, vt_spec] + extra_specs,
        out_specs=pl.BlockSpec((tiles * tq, D_QA), lambda t: (t, 0)),
        out_shape=jax.ShapeDtypeStruct((ntok, D_QA), BF16),
        scratch_shapes=[pltpu.VMEM((tk, GROUP * tq), F32), pltpu.VMEM((tk, GROUP * tq), F32),
                        pltpu.VMEM((1, GROUP * tq), F32), pltpu.VMEM((1, GROUP * tq), F32)],
        compiler_params=pltpu.CompilerParams(
            dimension_semantics=("arbitrary",), vmem_limit_bytes=VMEM_LIMIT),
        name="attention",
    )(qt, k, vt, *extra)


def _post_kernel(og_ref, oa_ref, x_ref, mod_ref, n2_ref, fg_ref, wo_ref, w1_ref, w3_ref, w2_ref,
                 y_ref):
    D = D_MODEL
    mod = mod_ref[0]
    g1, sh2, sc2, g2 = (mod[:, i * D:(i + 1) * D] for i in range(2, 6))
    out = _dot(og_ref[...], wo_ref[0:D_V, :]) + _dot(oa_ref[...], wo_ref[D_V:D_V + D_QA, :])
    x1 = x_ref[...] + g1 * out
    hb = (_rms(x1, n2_ref[...]) * (1.0 + sc2) + sh2).astype(BF16)
    act = (_silu(_dot(hb, w1_ref[...])) * _dot(hb, w3_ref[...])).astype(BF16)
    x2 = x1 + g2 * _dot(act, w2_ref[...])
    y_ref[...] = _rms(x2, fg_ref[...])


def _post_mixer(o_gla, o_att, x, mod3, mod_row0, tokens_per_mod, weights, tm):
    ntok = x.shape[0]
    tiles_per_mod = tokens_per_mod // tm
    whole = pl.BlockSpec(memory_space=pltpu.VMEM)
    row = lambda w: pl.BlockSpec((tm, w), lambda i: (i, 0))
    return pl.pallas_call(
        _post_kernel,
        grid=(ntok // tm,),
        in_specs=[row(D_V), row(D_QA), row(D_MODEL),
                  pl.BlockSpec((1, 1, N_MOD * D_MODEL), lambda i: (mod_row0 + i // tiles_per_mod, 0, 0)),
                  whole, whole, whole, whole, whole, whole],
        out_specs=row(D_MODEL),
        out_shape=jax.ShapeDtypeStruct((ntok, D_MODEL), F32),
        compiler_params=pltpu.CompilerParams(
            dimension_semantics=("arbitrary",), vmem_limit_bytes=VMEM_LIMIT),
        name="post_mixer",
    )(o_gla, o_att, x, mod3, *weights)


def _rope_tables(t):
    rows = t // GRID_W
    row = jnp.repeat(jnp.arange(rows, dtype=F32), GRID_W)
    col = jnp.tile(jnp.arange(GRID_W, dtype=F32), rows)
    inv = ROPE_THETA ** (-jnp.arange(0, ROPE_AXIS_DIM, 2, dtype=F32) / ROPE_AXIS_DIM)
    ang_r = row[:, None] * inv
    ang_c = col[:, None] * inv
    cos = jnp.concatenate([jnp.cos(ang_r)] * 2 + [jnp.cos(ang_c)] * 2, axis=-1)
    sin = jnp.concatenate([-jnp.sin(ang_r), jnp.sin(ang_r), -jnp.sin(ang_c), jnp.sin(ang_c)], axis=-1)
    return jnp.tile(cos, (1, ATT_KV_HEADS)), jnp.tile(sin, (1, ATT_KV_HEADS))


def _block_mean(width, group):
    return jnp.asarray(np.kron(np.eye(width // group), np.ones((group, group))) / group, BF16)


def kernel(x_prompt, x_sample, cache_k, cache_v, state_gla_fwd, state_gla_bwd, c, c_ctx,
           ada_w, ada_b, norm1_g, norm2_g, w_in, w_gk2, b_gk2, gla_norm_g, q_norm_g, k_norm_g,
           w_out, w_ffn1, w_ffn3, w_ffn2, final_g):
    B, T, D = x_prompt.shape
    BL, TL, _ = x_sample.shape
    TP = cache_k.shape[2]
    l = 0

    cond = jnp.zeros((COND_ROWS, D), F32).at[0].set(c_ctx).at[1:1 + BL].set(c)
    mod3 = _adaln(cond, ada_w[l], ada_b[l])

    offs = np.cumsum([0, D_QK, D_QK, D_V, D_V, 2 * GLA_LOWRANK, D_QA, D_KA, D_KA])
    seg = lambda i: w_in[l][:, offs[i]:offs[i + 1]]
    w_in_r = jnp.concatenate([seg(0), seg(1), seg(2), seg(3), seg(5), seg(6), seg(7), seg(4)],
                             axis=1).astype(BF16)
    zero = jnp.zeros((GLA_LOWRANK, D_QK), F32)
    w_gk = jnp.concatenate([jnp.concatenate([w_gk2[l, 0], zero], axis=1),
                            jnp.concatenate([zero, w_gk2[l, 1]], axis=1)], axis=0).astype(BF16)
    pre_w = (norm1_g[l].reshape(1, D), w_in_r, w_gk, b_gk2[l].reshape(1, 2 * D_QK),
             jnp.tile(q_norm_g[l], ATT_HEADS).reshape(1, D_QA),
             jnp.tile(k_norm_g[l], ATT_KV_HEADS).reshape(1, D_KA),
             _block_mean(D_QA, HEAD_DIM), _block_mean(D_KA, HEAD_DIM))
    post_w = (norm2_g[l].reshape(1, D), final_g.reshape(1, D), w_out[l].astype(BF16),
              w_ffn1[l].astype(BF16), w_ffn3[l].astype(BF16), w_ffn2[l].astype(BF16))
    gla_g = gla_norm_g[l].reshape(1, GLA_DV)
    tm_pre = 1024
    tm_post = 512

    xc = x_prompt.reshape(B * T, D)
    q, k, v, gate, g, qa, ka, va, kc, vc = _pre_mixer(
        xc, mod3, 0, B * T, T, pre_w, None, True, tm_pre)
    r3 = lambda a, b, t: a.reshape(b, t, a.shape[-1])
    o_gla, s_f, s_b = _gla(r3(q, B, T), r3(k, B, T), r3(v, B, T), r3(g, B, T), r3(gate, B, T),
                           gla_g, None, 4)
    o_att = _attention(qa, ka, va, B)
    y_prompt = _post_mixer(o_gla.reshape(B * T, D_V), o_att, xc, mod3, 0, B * T, post_w, tm_post)

    xl = x_sample.reshape(BL * TL, D)
    q, k, v, gate, g, qa, ka, va = _pre_mixer(
        xl, mod3, 1, TL, TL, pre_w, _rope_tables(TL), False, tm_pre)
    o_gla, _, _ = _gla(r3(q, BL, TL), r3(k, BL, TL), r3(v, BL, TL), r3(g, BL, TL), r3(gate, BL, TL),
                       gla_g, (state_gla_fwd[:, l].reshape(BL, D_QK, GLA_DV),
                               state_gla_bwd[:, l].reshape(BL, D_QK, GLA_DV)), BL)

    k_old = cache_k[:, l].reshape(BL * TP, D_KA).astype(BF16)
    vt_old = jnp.transpose(cache_v[:, l], (2, 3, 0, 1)).reshape(D_KA, BL * TP).astype(BF16)
    o_att = _attention(qa, ka, va, BL, (k_old, vt_old))
    y_sample = _post_mixer(o_gla.reshape(BL * TL, D_V), o_att, xl, mod3, 1, TL, post_w, tm_post)

    def cache_out(ct):
        return jnp.transpose(ct.reshape(B, 1, ATT_KV_HEADS, HEAD_DIM, T), (0, 1, 4, 2, 3))

    return (y_prompt.reshape(B, T, D), y_sample.reshape(BL, TL, D), cache_out(kc), cache_out(vc),
            s_f.reshape(B, 1, GLA_HEADS, GLA_DK, GLA_DV), s_b.reshape(B, 1, GLA_HEADS, GLA_DK, GLA_DV))
```

```python
import functools

import numpy as np
import jax
import jax.numpy as jnp
from jax import lax
from jax.experimental import pallas as pl
from jax.experimental.pallas import tpu as pltpu

F32 = jnp.float32
BF16 = jnp.bfloat16

D_MODEL = 1024
GRID_W = 64
GLA_HEADS = 4
GLA_DK = 64
GLA_DV = 128
GLA_LOWRANK = 16
GLA_GATE_NORM = 16.0
GLA_CHUNK = 64
ATT_HEADS = 8
ATT_KV_HEADS = 2
HEAD_DIM = 64
ROPE_AXIS_DIM = HEAD_DIM // 2
ROPE_THETA = 10000.0
D_FF = -(-8 * D_MODEL // (3 * 256)) * 256
N_MOD = 6
EPS = 1e-6

D_QK = GLA_HEADS * GLA_DK
D_V = GLA_HEADS * GLA_DV
D_QA = ATT_HEADS * HEAD_DIM
D_KA = ATT_KV_HEADS * HEAD_DIM
GROUP = ATT_HEADS // ATT_KV_HEADS
O_Q, O_K, O_V, O_OG = 0, D_QK, 2 * D_QK, 2 * D_QK + D_V
O_QA = O_OG + D_V
O_KA = O_QA + D_QA
O_VA = O_KA + D_KA
O_LR = O_VA + D_KA
D_PROJ = O_LR + 2 * GLA_LOWRANK

COND_ROWS = 8
VMEM_LIMIT = 56 * 1024 * 1024


def _dot(a, b):
    return jnp.dot(a, b, preferred_element_type=F32)


def _dot_nt(a, b):
    return lax.dot_general(a, b, (((1,), (1,)), ((), ())), preferred_element_type=F32)


def _dot_tn(a, b):
    return lax.dot_general(a, b, (((0,), (0,)), ((), ())), preferred_element_type=F32)


def _rms(x, g):
    ms = jnp.mean(x * x, axis=-1, keepdims=True)
    return x * lax.rsqrt(ms + EPS) * g


def _silu(x):
    return x * jax.nn.sigmoid(x)


ADALN_K_BLOCK = 256


def _adaln_kernel(c_ref, w_ref, b_ref, o_ref, acc):
    k = pl.program_id(0)

    @pl.when(k == 0)
    def _():
        acc[...] = jnp.broadcast_to(b_ref[...], acc.shape)

    acc[...] += _dot(_silu(c_ref[...]).astype(BF16), w_ref[...].astype(BF16))

    @pl.when(k == pl.num_programs(0) - 1)
    def _():
        o_ref[:, 0, :] = acc[...]


def _adaln(cond, ada_w, ada_b):
    d, n = ada_w.shape
    tk = ADALN_K_BLOCK
    return pl.pallas_call(
        _adaln_kernel,
        grid=(d // tk,),
        in_specs=[
            pl.BlockSpec((COND_ROWS, tk), lambda k: (0, k)),
            pl.BlockSpec((tk, n), lambda k: (k, 0)),
            pl.BlockSpec((1, n), lambda k: (0, 0)),
        ],
        out_specs=pl.BlockSpec((COND_ROWS, 1, n), lambda k: (0, 0, 0)),
        out_shape=jax.ShapeDtypeStruct((COND_ROWS, 1, n), F32),
        scratch_shapes=[pltpu.VMEM((COND_ROWS, n), F32)],
        compiler_params=pltpu.CompilerParams(
            dimension_semantics=("arbitrary",), vmem_limit_bytes=VMEM_LIMIT),
        name="adaln",
    )(cond, ada_w, ada_b.reshape(1, n))


def _swap_halves(x, lane_lo):
    n = x.shape[-1]
    up = pltpu.roll(x, n - ROPE_AXIS_DIM // 2, axis=1)
    dn = pltpu.roll(x, ROPE_AXIS_DIM // 2, axis=1)
    return jnp.where(lane_lo, up, dn)


PRE_SUB_ROWS = 256


def _pre_kernel(*refs, rope, emit_cache):
    it = iter(refs)
    x_ref, mod_ref, n1_ref, win_ref, wgk_ref, bgk_ref = (next(it) for _ in range(6))
    qg_ref, kg_ref, bdq_ref, bdk_ref = (next(it) for _ in range(4))
    if rope:
        cos_ref, sin_ref = next(it), next(it)
    q_o, k_o, v_o, gate_o, g_o, qa_o, ka_o, va_o = (next(it) for _ in range(8))
    if emit_cache:
        kc_o, vc_o = next(it), next(it)

    sub = PRE_SUB_ROWS
    n_sub = x_ref.shape[0] // sub
    mod = mod_ref[0]
    sh1 = mod[:, 0:D_MODEL]
    sc1 = mod[:, D_MODEL:2 * D_MODEL]

    def project(j):
        rows = slice(j * sub, (j + 1) * sub)
        h = _rms(x_ref[rows, :], n1_ref[...]) * (1.0 + sc1) + sh1
        return _dot(h.astype(BF16), win_ref[...])

    def finish(j, p_ref):
        rows = slice(j * sub, (j + 1) * sub)
        q_o[rows, :] = p_ref[:, O_Q:O_Q + D_QK].astype(BF16)
        k_o[rows, :] = p_ref[:, O_K:O_K + D_QK].astype(BF16)
        v_o[rows, :] = p_ref[:, O_V:O_V + D_V].astype(BF16)
        gate_o[rows, :] = _silu(p_ref[:, O_OG:O_OG + D_V]).astype(BF16)

        gk = _dot(p_ref[:, O_LR:O_LR + 2 * GLA_LOWRANK].astype(BF16), wgk_ref[...]) + bgk_ref[...]
        g_o[rows, :] = ((jnp.minimum(gk, 0.0) - jnp.log(1.0 + jnp.exp(-jnp.abs(gk))))
                        * (1.0 / GLA_GATE_NORM))

        qa = p_ref[:, O_QA:O_QA + D_QA]
        ka = p_ref[:, O_KA:O_KA + D_KA]
        va = p_ref[:, O_VA:O_VA + D_KA]
        qa = qa * lax.rsqrt(_dot((qa * qa).astype(BF16), bdq_ref[...]) + EPS) * qg_ref[...]
        ka = ka * lax.rsqrt(_dot((ka * ka).astype(BF16), bdk_ref[...]) + EPS) * kg_ref[...]
        if rope:
            lane_q = lax.broadcasted_iota(jnp.int32, qa.shape, 1)
            lane_k = lax.broadcasted_iota(jnp.int32, ka.shape, 1)
            lo_q = (lane_q % ROPE_AXIS_DIM) < ROPE_AXIS_DIM // 2
            lo_k = (lane_k % ROPE_AXIS_DIM) < ROPE_AXIS_DIM // 2
            cos_k = cos_ref[rows, :]
            sin_k = sin_ref[rows, :]
            cos_q = jnp.concatenate([cos_k] * GROUP, axis=1)
            sin_q = jnp.concatenate([sin_k] * GROUP, axis=1)
            qa = qa * cos_q + _swap_halves(qa, lo_q) * sin_q
            ka = ka * cos_k + _swap_halves(ka, lo_k) * sin_k
        vat = va.T
        if emit_cache:
            kat = ka.T
            t = kc_o.shape[2]
            for sq in range(sub // t):
                kc_o[j * (sub // t) + sq] = kat[:, sq * t:(sq + 1) * t]
                vc_o[j * (sub // t) + sq] = vat[:, sq * t:(sq + 1) * t]
        qa_o[:, rows] = (qa * (HEAD_DIM ** -0.5 * LOG2E)).T.astype(BF16)
        ka_o[rows, :] = ka.astype(BF16)
        va_o[:, rows] = vat.astype(BF16)

    p_prev = None
    for n in range(n_sub + 1):
        p_new = project(n) if n < n_sub else None
        if n > 0:
            finish(n - 1, p_prev)
        p_prev = p_new


def _pre_mixer(x, mod3, mod_row0, tokens_per_mod, seq_len, weights, rope_tabs, emit_cache, tm):
    ntok = x.shape[0]
    n1, w_in_r, w_gk, b_gk, qg, kg, bdq, bdk = weights
    rope = rope_tabs is not None
    tiles_per_mod = tokens_per_mod // tm
    tiles_per_seq = seq_len // tm
    whole = pl.BlockSpec(memory_space=pltpu.VMEM)
    row = lambda w: pl.BlockSpec((tm, w), lambda i: (i, 0))
    in_specs = [
        row(D_MODEL),
        pl.BlockSpec((1, 1, N_MOD * D_MODEL), lambda i: (mod_row0 + i // tiles_per_mod, 0, 0)),
        whole, whole, whole, whole, whole, whole, whole, whole,
    ]
    args = [x, mod3, n1, w_in_r, w_gk, b_gk, qg, kg, bdq, bdk]
    if rope:
        tab = pl.BlockSpec((tm, D_KA), lambda i: (i % tiles_per_seq, 0))
        in_specs += [tab, tab]
        args += list(rope_tabs)
    col = lambda w: pl.BlockSpec((w, tm), lambda i: (0, i))
    out_specs = [row(D_QK), row(D_QK), row(D_V), row(D_V), row(2 * D_QK),
                 col(D_QA), row(D_KA), col(D_KA)]
    out_shape = [
        jax.ShapeDtypeStruct((ntok, D_QK), BF16),
        jax.ShapeDtypeStruct((ntok, D_QK), BF16),
        jax.ShapeDtypeStruct((ntok, D_V), BF16),
        jax.ShapeDtypeStruct((ntok, D_V), BF16),
        jax.ShapeDtypeStruct((ntok, 2 * D_QK), F32),
        jax.ShapeDtypeStruct((D_QA, ntok), BF16),
        jax.ShapeDtypeStruct((ntok, D_KA), BF16),
        jax.ShapeDtypeStruct((D_KA, ntok), BF16),
    ]
    if emit_cache:
        seqs = tm // seq_len
        out_specs += [pl.BlockSpec((seqs, D_KA, seq_len), lambda i: (i, 0, 0))] * 2
        out_shape += [jax.ShapeDtypeStruct((ntok // seq_len, D_KA, seq_len), F32)] * 2
    return pl.pallas_call(
        functools.partial(_pre_kernel, rope=rope, emit_cache=emit_cache),
        grid=(ntok // tm,),
        in_specs=in_specs,
        out_specs=out_specs,
        out_shape=out_shape,
        compiler_params=pltpu.CompilerParams(
            dimension_semantics=("arbitrary",), vmem_limit_bytes=VMEM_LIMIT),
        name="pre_mixer_rope" if rope else "pre_mixer",
    )(*args)


def _gla_chunks(units, q_ref, k_ref, v_ref, g_ref, half_lo):
    C = GLA_CHUNK
    n_pair = GLA_HEADS // 2
    rows = [pl.ds(pl.multiple_of(c * C, C), C) for (_, c, _, _, _, _) in units]

    g_his, bs = [], []
    for (bi, _, fwd, tri, _, _), r in zip(units, rows):
        gc = g_ref[bi, r, 0:D_QK] if fwd else g_ref[bi, r, D_QK:2 * D_QK]
        ltri = jnp.where(tri, 1.0, 0.0).astype(BF16)
        g_hi = gc.astype(BF16)
        g_lo = (gc - g_hi.astype(F32)).astype(BF16)
        g_his.append(g_hi)
        bs.append(_dot(ltri, g_hi) + _dot(ltri, g_lo))
    ones = jnp.ones((C, GLA_DV), BF16)
    decays = [jnp.exp(_dot_tn(g_hi, ones)) for g_hi in g_his]

    qts, kts, kes = [], [], []
    for (bi, _, fwd, _, _, _), r, b in zip(units, rows, bs):
        qc = q_ref[bi, r, :].astype(F32) * GLA_DK ** -0.5
        kc = k_ref[bi, r, :].astype(F32)
        btot = b[C - 1:C, :] if fwd else b[0:1, :]
        qts.append((qc * jnp.exp(b)).astype(BF16))
        kts.append((kc * jnp.exp(-b)).astype(BF16))
        kes.append((kc * jnp.exp(btot - b)).astype(BF16))

    scores, incs, qms = [], [], []
    for (bi, _, _, tri, _, _), r, qt, kt, ke in zip(units, rows, qts, kts, kes):
        for pair in range(n_pair):
            lanes = slice(pair * 128, (pair + 1) * 128)
            incs.append(_dot_tn(ke[:, lanes], v_ref[bi, r, pair * 2 * GLA_DV:(pair + 1) * 2 * GLA_DV]))
            for hh in range(2):
                qm = jnp.where(half_lo if hh == 0 else ~half_lo, qt[:, lanes], jnp.zeros((C, 128), BF16))
                qms.append(qm)
                scores.append(jnp.where(tri, _dot_nt(qm, kt[:, lanes]), 0.0).astype(BF16))

    for ui, ((bi, _, _, _, s_ref, o_ref), r) in enumerate(zip(units, rows)):
        state = s_ref[bi]
        state_b = state.astype(BF16)
        for hd in range(GLA_HEADS):
            pair = hd // 2
            lhs = jnp.concatenate([qms[ui * GLA_HEADS + hd], scores[ui * GLA_HEADS + hd]], axis=1)
            rhs = jnp.concatenate([state_b[pair * 128:(pair + 1) * 128, :],
                                   v_ref[bi, r, hd * GLA_DV:(hd + 1) * GLA_DV]], axis=0)
            o_ref[bi, r, hd * GLA_DV:(hd + 1) * GLA_DV] = _dot(lhs, rhs)
        inc = jnp.concatenate(
            [incs[ui * n_pair + hd // 2][(hd % 2) * GLA_DK:(hd % 2 + 1) * GLA_DK,
                                         (hd % 2) * GLA_DV:(hd % 2 + 1) * GLA_DV]
             for hd in range(GLA_HEADS)], axis=0)
        s_ref[bi] = decays[ui] * state + inc


def _gla_kernel(*args, zero_init):
    q_ref, k_ref, v_ref, g_ref, gate_ref, gn_ref = args[:6]
    o_ref, sf_ref, sb_ref, of_scr, ob_scr = args[-5:]
    C = GLA_CHUNK
    nb, t, _ = q_ref.shape
    n = t // C
    row = lax.broadcasted_iota(jnp.int32, (C, C), 0)
    col = lax.broadcasted_iota(jnp.int32, (C, C), 1)
    half_lo = lax.broadcasted_iota(jnp.int32, (C, 128), 1) < GLA_DK
    refs = (q_ref, k_ref, v_ref, g_ref)
    if zero_init:
        sf_ref[...] = jnp.zeros(sf_ref.shape, F32)
        sb_ref[...] = jnp.zeros(sb_ref.shape, F32)
    else:
        sf_ref[...] = args[6][...]
        sb_ref[...] = args[7][...]

    def body(i, carry):
        units = []
        for bi in range(nb):
            units.append((bi, i, True, row >= col, sf_ref, of_scr))
            units.append((bi, n - 1 - i, False, row <= col, sb_ref, ob_scr))
        _gla_chunks(units, *refs, half_lo)
        return carry

    lax.fori_loop(0, n, body, 0)

    rt = GLA_OUT_ROWS
    gn = gn_ref[...]

    def finish(i, carry):
        rows = pl.ds(pl.multiple_of(i * rt, rt), rt)
        for bi in range(nb):
            o = of_scr[bi, rows, :] + ob_scr[bi, rows, :]
            gate = gate_ref[bi, rows, :].astype(F32)
            for hd in range(GLA_HEADS):
                sl = slice(hd * GLA_DV, (hd + 1) * GLA_DV)
                o_ref[bi, rows, sl] = (_rms(o[:, sl], gn) * gate[:, sl]).astype(BF16)
        return carry

    lax.fori_loop(0, t // rt, finish, 0)


GLA_OUT_ROWS = 64


def _gla(q, k, v, g, gate, gla_g, init_states, nb):
    b, t, _ = q.shape
    seq = lambda w: pl.BlockSpec((nb, t, w), lambda i: (i, 0, 0))
    st = pl.BlockSpec((nb, D_QK, GLA_DV), lambda i: (i, 0, 0))
    zero_init = init_states is None
    return pl.pallas_call(
        functools.partial(_gla_kernel, zero_init=zero_init),
        grid=(b // nb,),
        in_specs=[seq(D_QK), seq(D_QK), seq(D_V), seq(2 * D_QK), seq(D_V),
                  pl.BlockSpec((1, GLA_DV), lambda i: (0, 0))] + ([] if zero_init else [st, st]),
        out_specs=[seq(D_V), st, st],
        out_shape=[jax.ShapeDtypeStruct((b, t, D_V), BF16),
                   jax.ShapeDtypeStruct((b, D_QK, GLA_DV), F32),
                   jax.ShapeDtypeStruct((b, D_QK, GLA_DV), F32)],
        scratch_shapes=[pltpu.VMEM((nb, t, D_V), F32), pltpu.VMEM((nb, t, D_V), F32)],
        compiler_params=pltpu.CompilerParams(
            dimension_semantics=("arbitrary",), vmem_limit_bytes=VMEM_LIMIT),
        name="gla",
    )(q, k, v, g, gate, gla_g, *(() if zero_init else init_states))


ATT_KEY_BLOCK = 256
ATT_Q_TILE = 256
LOG2E = 1.4426950408889634


ATT_TILES_PER_STEP = 4


def _attn_kernel(*args, keys_per_tile, cached):
    qt_ref, k_ref, vt_ref = args[:3]
    kc_ref, vtc_ref = args[3:5] if cached else (None, None)
    o_ref, s0, s1, m0, m1 = args[-5:]
    tq = ATT_Q_TILE
    tiles = qt_ref.shape[1] // tq
    t_new = k_ref.shape[0] // tiles if keys_per_tile else k_ref.shape[0]
    tk = t_new + (kc_ref.shape[0] if cached else 0)
    kb = min(ATT_KEY_BLOCK, tk)
    nkb = tk // kb
    zero = jnp.zeros((HEAD_DIM, tq), BF16)
    ones = jnp.ones((16, kb), BF16)
    items = [(qi, kv) for qi in range(tiles) for kv in range(ATT_KV_HEADS)]
    bufs = (s0, s1)
    mbufs = (m0, m1)

    def q_ext(qi, kv):
        cols = []
        for g in range(GROUP):
            hd = kv * GROUP + g
            qt = qt_ref[hd * HEAD_DIM:(hd + 1) * HEAD_DIM, qi * tq:(qi + 1) * tq]
            cols.append(jnp.concatenate([qt, zero] if kv == 0 else [zero, qt], axis=0))
        return jnp.concatenate(cols, axis=1)

    def key0(qi):
        return qi * tk if keys_per_tile else 0

    def score_piece(n, q_all, i, m):
        r0 = key0(items[n][0]) + i * kb
        keys = k_ref[r0:r0 + kb, :] if i * kb < t_new else kc_ref[r0 - t_new:r0 - t_new + kb, :]
        s = _dot(keys, q_all)
        bufs[n % 2][i * kb:(i + 1) * kb, :] = s
        for r in range(kb // 8):
            tile = s[r * 8:(r + 1) * 8, :]
            m = tile if m is None else jnp.maximum(m, tile)
        return m

    def value_piece(n, m, i, acc):
        qi, kv = items[n]
        c0 = key0(qi) + i * kb
        p = jnp.exp2(bufs[n % 2][i * kb:(i + 1) * kb, :] - m).astype(BF16)
        hd_rows = slice(kv * HEAD_DIM, (kv + 1) * HEAD_DIM)
        vt = vt_ref[hd_rows, c0:c0 + kb] if i * kb < t_new else vtc_ref[hd_rows, c0 - t_new:c0 - t_new + kb]
        vt1 = jnp.concatenate([vt, ones], axis=0)
        d = _dot(vt1, p)
        return d if acc is None else acc + d

    def emit(n, acc):
        qi, kv = items[n]
        o = acc[:HEAD_DIM] / acc[HEAD_DIM:HEAD_DIM + 1]
        for g in range(0, GROUP, 2):
            two = jnp.concatenate([o[:, g * tq:(g + 1) * tq], o[:, (g + 1) * tq:(g + 2) * tq]], axis=0)
            hd = kv * GROUP + g
            o_ref[qi * tq:(qi + 1) * tq, hd * HEAD_DIM:(hd + 2) * HEAD_DIM] = two.T.astype(BF16)

    def pair(n):
        q_all = q_ext(*items[n]) if n < len(items) else None
        m_prev = mbufs[(n - 1) % 2][...] if n > 0 else None
        m_new = acc = None
        for i in range(nkb):
            if n < len(items):
                m_new = score_piece(n, q_all, i, m_new)
            if n > 0:
                acc = value_piece(n - 1, m_prev, i, acc)
        if n > 0:
            emit(n - 1, acc)
        if n < len(items):
            mbufs[n % 2][...] = jnp.max(m_new, axis=0, keepdims=True)

    for n in range(len(items) + 1):
        if nkb > 1:
            pl.when(pl.program_id(0) > -(n + 1))(functools.partial(pair, n))
        else:
            pair(n)


def _attention(qt, k, vt, batch, cache=None):
    ntok = qt.shape[1]
    t_new = k.shape[0] // batch
    tq = ATT_Q_TILE
    tiles = ATT_TILES_PER_STEP
    nq = ntok // batch // tq
    keys_per_tile = nq < tiles
    extra_specs, extra, tk = [], (), t_new
    if keys_per_tile:
        assert cache is None
        seqs = tiles // nq
        k_spec = pl.BlockSpec((seqs * t_new, D_KA), lambda t: (t, 0))
        vt_spec = pl.BlockSpec((D_KA, seqs * t_new), lambda t: (0, t))
    else:
        spq = nq // tiles
        k_spec = pl.BlockSpec((t_new, D_KA), lambda t: (t // spq, 0))
        vt_spec = pl.BlockSpec((D_KA, t_new), lambda t: (0, t // spq))
        if cache is not None:
            t_old = cache[0].shape[0] // batch
            assert t_new % ATT_KEY_BLOCK == 0 and t_old % ATT_KEY_BLOCK == 0
            extra_specs = [pl.BlockSpec((t_old, D_KA), lambda t: (t // spq, 0)),
                           pl.BlockSpec((D_KA, t_old), lambda t: (0, t // spq))]
            extra, tk = cache, t_new + t_old
    return pl.pallas_call(
        functools.partial(_attn_kernel, keys_per_tile=keys_per_tile, cached=cache is not None),
        grid=(ntok // (tiles * tq),),
        in_specs=[pl.BlockSpec((D_QA, tiles * tq), lambda t: (0, t)), k_spec, vt_spec] + extra_specs,
        out_specs=pl.BlockSpec((tiles * tq, D_QA), lambda t: (t, 0)),
        out_shape=jax.ShapeDtypeStruct((ntok, D_QA), BF16),
        scratch_shapes=[pltpu.VMEM((tk, GROUP * tq), F32), pltpu.VMEM((tk, GROUP * tq), F32),
                        pltpu.VMEM((1, GROUP * tq), F32), pltpu.VMEM((1, GROUP * tq), F32)],
        compiler_params=pltpu.CompilerParams(
            dimension_semantics=("arbitrary",), vmem_limit_bytes=VMEM_LIMIT),
        name="attention",
    )(qt, k, vt, *extra)


def _post_kernel(og_ref, oa_ref, x_ref, mod_ref, n2_ref, fg_ref, wo_ref, w1_ref, w3_ref, w2_ref,
                 y_ref):
    D = D_MODEL
    mod = mod_ref[0]
    g1, sh2, sc2, g2 = (mod[:, i * D:(i + 1) * D] for i in range(2, 6))
    out = _dot(og_ref[...], wo_ref[0:D_V, :]) + _dot(oa_ref[...], wo_ref[D_V:D_V + D_QA, :])
    x1 = x_ref[...] + g1 * out
    hb = (_rms(x1, n2_ref[...]) * (1.0 + sc2) + sh2).astype(BF16)
    act = (_silu(_dot(hb, w1_ref[...])) * _dot(hb, w3_ref[...])).astype(BF16)
    x2 = x1 + g2 * _dot(act, w2_ref[...])
    y_ref[...] = _rms(x2, fg_ref[...])


def _post_mixer(o_gla, o_att, x, mod3, mod_row0, tokens_per_mod, weights, tm):
    ntok = x.shape[0]
    tiles_per_mod = tokens_per_mod // tm
    whole = pl.BlockSpec(memory_space=pltpu.VMEM)
    row = lambda w: pl.BlockSpec((tm, w), lambda i: (i, 0))
    return pl.pallas_call(
        _post_kernel,
        grid=(ntok // tm,),
        in_specs=[row(D_V), row(D_QA), row(D_MODEL),
                  pl.BlockSpec((1, 1, N_MOD * D_MODEL), lambda i: (mod_row0 + i // tiles_per_mod, 0, 0)),
                  whole, whole, whole, whole, whole, whole],
        out_specs=row(D_MODEL),
        out_shape=jax.ShapeDtypeStruct((ntok, D_MODEL), F32),
        compiler_params=pltpu.CompilerParams(
            dimension_semantics=("arbitrary",), vmem_limit_bytes=VMEM_LIMIT),
        name="post_mixer",
    )(o_gla, o_att, x, mod3, *weights)


def _rope_tables(t):
    rows = t // GRID_W
    row = jnp.repeat(jnp.arange(rows, dtype=F32), GRID_W)
    col = jnp.tile(jnp.arange(GRID_W, dtype=F32), rows)
    inv = ROPE_THETA ** (-jnp.arange(0, ROPE_AXIS_DIM, 2, dtype=F32) / ROPE_AXIS_DIM)
    ang_r = row[:, None] * inv
    ang_c = col[:, None] * inv
    cos = jnp.concatenate([jnp.cos(ang_r)] * 2 + [jnp.cos(ang_c)] * 2, axis=-1)
    sin = jnp.concatenate([-jnp.sin(ang_r), jnp.sin(ang_r), -jnp.sin(ang_c), jnp.sin(ang_c)], axis=-1)
    return jnp.tile(cos, (1, ATT_KV_HEADS)), jnp.tile(sin, (1, ATT_KV_HEADS))


def _block_mean(width, group):
    return jnp.asarray(np.kron(np.eye(width // group), np.ones((group, group))) / group, BF16)


def kernel(x_prompt, x_sample, cache_k, cache_v, state_gla_fwd, state_gla_bwd, c, c_ctx,
           ada_w, ada_b, norm1_g, norm2_g, w_in, w_gk2, b_gk2, gla_norm_g, q_norm_g, k_norm_g,
           w_out, w_ffn1, w_ffn3, w_ffn2, final_g):
    B, T, D = x_prompt.shape
    BL, TL, _ = x_sample.shape
    TP = cache_k.shape[2]
    l = 0

    cond = jnp.zeros((COND_ROWS, D), F32).at[0].set(c_ctx).at[1:1 + BL].set(c)
    mod3 = _adaln(cond, ada_w[l], ada_b[l])

    offs = np.cumsum([0, D_QK, D_QK, D_V, D_V, 2 * GLA_LOWRANK, D_QA, D_KA, D_KA])
    seg = lambda i: w_in[l][:, offs[i]:offs[i + 1]]
    w_in_r = jnp.concatenate([seg(0), seg(1), seg(2), seg(3), seg(5), seg(6), seg(7), seg(4)],
                             axis=1).astype(BF16)
    zero = jnp.zeros((GLA_LOWRANK, D_QK), F32)
    w_gk = jnp.concatenate([jnp.concatenate([w_gk2[l, 0], zero], axis=1),
                            jnp.concatenate([zero, w_gk2[l, 1]], axis=1)], axis=0).astype(BF16)
    pre_w = (norm1_g[l].reshape(1, D), w_in_r, w_gk, b_gk2[l].reshape(1, 2 * D_QK),
             jnp.tile(q_norm_g[l], ATT_HEADS).reshape(1, D_QA),
             jnp.tile(k_norm_g[l], ATT_KV_HEADS).reshape(1, D_KA),
             _block_mean(D_QA, HEAD_DIM), _block_mean(D_KA, HEAD_DIM))
    post_w = (norm2_g[l].reshape(1, D), final_g.reshape(1, D), w_out[l].astype(BF16),
              w_ffn1[l].astype(BF16), w_ffn3[l].astype(BF16), w_ffn2[l].astype(BF16))
    gla_g = gla_norm_g[l].reshape(1, GLA_DV)
    tm_pre = 1024
    tm_post = 512

    xc = x_prompt.reshape(B * T, D)
    q, k, v, gate, g, qa, ka, va, kc, vc = _pre_mixer(
        xc, mod3, 0, B * T, T, pre_w, None, True, tm_pre)
    r3 = lambda a, b, t: a.reshape(b, t, a.shape[-1])
    o_gla, s_f, s_b = _gla(r3(q, B, T), r3(k, B, T), r3(v, B, T), r3(g, B, T), r3(gate, B, T),
                           gla_g, None, 8)
    o_att = _attention(qa, ka, va, B)
    y_prompt = _post_mixer(o_gla.reshape(B * T, D_V), o_att, xc, mod3, 0, B * T, post_w, tm_post)

    xl = x_sample.reshape(BL * TL, D)
    q, k, v, gate, g, qa, ka, va = _pre_mixer(
        xl, mod3, 1, TL, TL, pre_w, _rope_tables(TL), False, tm_pre)
    o_gla, _, _ = _gla(r3(q, BL, TL), r3(k, BL, TL), r3(v, BL, TL), r3(g, BL, TL), r3(gate, BL, TL),
                       gla_g, (state_gla_fwd[:, l].reshape(BL, D_QK, GLA_DV),
                               state_gla_bwd[:, l].reshape(BL, D_QK, GLA_DV)), BL)

    k_old = cache_k[:, l].reshape(BL * TP, D_KA).astype(BF16)
    vt_old = jnp.transpose(cache_v[:, l], (2, 3, 0, 1)).reshape(D_KA, BL * TP).astype(BF16)
    o_att = _attention(qa, ka, va, BL, (k_old, vt_old))
    y_sample = _post_mixer(o_gla.reshape(BL * TL, D_V), o_att, xl, mod3, 1, TL, post_w, tm_post)

    def cache_out(ct):
        return jnp.transpose(ct.reshape(B, 1, ATT_KV_HEADS, HEAD_DIM, T), (0, 1, 4, 2, 3))

    return (y_prompt.reshape(B, T, D), y_sample.reshape(BL, TL, D), cache_out(kc), cache_out(vc),
            s_f.reshape(B, 1, GLA_HEADS, GLA_DK, GLA_DV), s_b.reshape(B, 1, GLA_HEADS, GLA_DK, GLA_DV))
```

```python
import functools

import numpy as np
import jax
import jax.numpy as jnp
from jax import lax
from jax.experimental import pallas as pl
from jax.experimental.pallas import tpu as pltpu

F32 = jnp.float32
BF16 = jnp.bfloat16

D_MODEL = 1024
GRID_W = 64
GLA_HEADS = 4
GLA_DK = 64
GLA_DV = 128
GLA_LOWRANK = 16
GLA_GATE_NORM = 16.0
GLA_CHUNK = 64
ATT_HEADS = 8
ATT_KV_HEADS = 2
HEAD_DIM = 64
ROPE_AXIS_DIM = HEAD_DIM // 2
ROPE_THETA = 10000.0
D_FF = -(-8 * D_MODEL // (3 * 256)) * 256
N_MOD = 6
EPS = 1e-6

D_QK = GLA_HEADS * GLA_DK
D_V = GLA_HEADS * GLA_DV
D_QA = ATT_HEADS * HEAD_DIM
D_KA = ATT_KV_HEADS * HEAD_DIM
GROUP = ATT_HEADS // ATT_KV_HEADS
O_Q, O_K, O_V, O_OG = 0, D_QK, 2 * D_QK, 2 * D_QK + D_V
O_QA = O_OG + D_V
O_KA = O_QA + D_QA
O_VA = O_KA + D_KA
O_LR = O_VA + D_KA
D_PROJ = O_LR + 2 * GLA_LOWRANK

COND_ROWS = 8
VMEM_LIMIT = 56 * 1024 * 1024


def _dot(a, b):
    return jnp.dot(a, b, preferred_element_type=F32)


def _dot_nt(a, b):
    return lax.dot_general(a, b, (((1,), (1,)), ((), ())), preferred_element_type=F32)


def _dot_tn(a, b):
    return lax.dot_general(a, b, (((0,), (0,)), ((), ())), preferred_element_type=F32)


def _rms(x, g):
    ms = jnp.mean(x * x, axis=-1, keepdims=True)
    return x * lax.rsqrt(ms + EPS) * g


def _silu(x):
    return x * jax.nn.sigmoid(x)


ADALN_K_BLOCK = 256


def _adaln_kernel(c_ref, w_ref, b_ref, o_ref, acc):
    k = pl.program_id(0)

    @pl.when(k == 0)
    def _():
        acc[...] = jnp.broadcast_to(b_ref[...], acc.shape)

    acc[...] += _dot(_silu(c_ref[...]).astype(BF16), w_ref[...].astype(BF16))

    @pl.when(k == pl.num_programs(0) - 1)
    def _():
        o_ref[:, 0, :] = acc[...]


def _adaln(cond, ada_w, ada_b):
    d, n = ada_w.shape
    tk = ADALN_K_BLOCK
    return pl.pallas_call(
        _adaln_kernel,
        grid=(d // tk,),
        in_specs=[
            pl.BlockSpec((COND_ROWS, tk), lambda k: (0, k)),
            pl.BlockSpec((tk, n), lambda k: (k, 0)),
            pl.BlockSpec((1, n), lambda k: (0, 0)),
        ],
        out_specs=pl.BlockSpec((COND_ROWS, 1, n), lambda k: (0, 0, 0)),
        out_shape=jax.ShapeDtypeStruct((COND_ROWS, 1, n), F32),
        scratch_shapes=[pltpu.VMEM((COND_ROWS, n), F32)],
        compiler_params=pltpu.CompilerParams(
            dimension_semantics=("arbitrary",), vmem_limit_bytes=VMEM_LIMIT),
        name="adaln",
    )(cond, ada_w, ada_b.reshape(1, n))


def _swap_halves(x, lane_lo):
    n = x.shape[-1]
    up = pltpu.roll(x, n - ROPE_AXIS_DIM // 2, axis=1)
    dn = pltpu.roll(x, ROPE_AXIS_DIM // 2, axis=1)
    return jnp.where(lane_lo, up, dn)


PRE_SUB_ROWS = 256


def _pre_kernel(*refs, rope, emit_cache):
    it = iter(refs)
    x_ref, mod_ref, n1_ref, win_ref, wgk_ref, bgk_ref = (next(it) for _ in range(6))
    qg_ref, kg_ref, bdq_ref, bdk_ref = (next(it) for _ in range(4))
    if rope:
        cos_ref, sin_ref = next(it), next(it)
    q_o, k_o, v_o, gate_o, g_o, qa_o, ka_o, va_o = (next(it) for _ in range(8))
    if emit_cache:
        kc_o, vc_o = next(it), next(it)

    sub = PRE_SUB_ROWS
    n_sub = x_ref.shape[0] // sub
    mod = mod_ref[0]
    sh1 = mod[:, 0:D_MODEL]
    sc1 = mod[:, D_MODEL:2 * D_MODEL]

    def project(j):
        rows = slice(j * sub, (j + 1) * sub)
        h = _rms(x_ref[rows, :], n1_ref[...]) * (1.0 + sc1) + sh1
        return _dot(h.astype(BF16), win_ref[...])

    def finish(j, p_ref):
        rows = slice(j * sub, (j + 1) * sub)
        q_o[rows, :] = p_ref[:, O_Q:O_Q + D_QK].astype(BF16)
        k_o[rows, :] = p_ref[:, O_K:O_K + D_QK].astype(BF16)
        v_o[rows, :] = p_ref[:, O_V:O_V + D_V].astype(BF16)
        gate_o[rows, :] = _silu(p_ref[:, O_OG:O_OG + D_V]).astype(BF16)

        gk = _dot(p_ref[:, O_LR:O_LR + 2 * GLA_LOWRANK].astype(BF16), wgk_ref[...]) + bgk_ref[...]
        g_o[rows, :] = ((jnp.minimum(gk, 0.0) - jnp.log(1.0 + jnp.exp(-jnp.abs(gk))))
                        * (1.0 / GLA_GATE_NORM))

        qa = p_ref[:, O_QA:O_QA + D_QA]
        ka = p_ref[:, O_KA:O_KA + D_KA]
        va = p_ref[:, O_VA:O_VA + D_KA]
        qa = qa * lax.rsqrt(_dot((qa * qa).astype(BF16), bdq_ref[...]) + EPS) * qg_ref[...]
        ka = ka * lax.rsqrt(_dot((ka * ka).astype(BF16), bdk_ref[...]) + EPS) * kg_ref[...]
        if rope:
            lane_q = lax.broadcasted_iota(jnp.int32, qa.shape, 1)
            lane_k = lax.broadcasted_iota(jnp.int32, ka.shape, 1)
            lo_q = (lane_q % ROPE_AXIS_DIM) < ROPE_AXIS_DIM // 2
            lo_k = (lane_k % ROPE_AXIS_DIM) < ROPE_AXIS_DIM // 2
            cos_k = cos_ref[rows, :]
            sin_k = sin_ref[rows, :]
            cos_q = jnp.concatenate([cos_k] * GROUP, axis=1)
            sin_q = jnp.concatenate([sin_k] * GROUP, axis=1)
            qa = qa * cos_q + _swap_halves(qa, lo_q) * sin_q
            ka = ka * cos_k + _swap_halves(ka, lo_k) * sin_k
        vat = va.T
        if emit_cache:
            kat = ka.T
            t = kc_o.shape[2]
            for sq in range(sub // t):
                kc_o[j * (sub // t) + sq] = kat[:, sq * t:(sq + 1) * t]
                vc_o[j * (sub // t) + sq] = vat[:, sq * t:(sq + 1) * t]
        qa_o[:, rows] = (qa * (HEAD_DIM ** -0.5 * LOG2E)).T.astype(BF16)
        ka_o[rows, :] = ka.astype(BF16)
        va_o[:, rows] = vat.astype(BF16)

    p_prev = None
    for n in range(n_sub + 1):
        p_new = project(n) if n < n_sub else None
        if n > 0:
            finish(n - 1, p_prev)
        p_prev = p_new


def _pre_mixer(x, mod3, mod_row0, tokens_per_mod, seq_len, weights, rope_tabs, emit_cache, tm):
    ntok = x.shape[0]
    n1, w_in_r, w_gk, b_gk, qg, kg, bdq, bdk = weights
    rope = rope_tabs is not None
    tiles_per_mod = tokens_per_mod // tm
    tiles_per_seq = seq_len // tm
    whole = pl.BlockSpec(memory_space=pltpu.VMEM)
    row = lambda w: pl.BlockSpec((tm, w), lambda i: (i, 0))
    in_specs = [
        row(D_MODEL),
        pl.BlockSpec((1, 1, N_MOD * D_MODEL), lambda i: (mod_row0 + i // tiles_per_mod, 0, 0)),
        whole, whole, whole, whole, whole, whole, whole, whole,
    ]
    args = [x, mod3, n1, w_in_r, w_gk, b_gk, qg, kg, bdq, bdk]
    if rope:
        tab = pl.BlockSpec((tm, D_KA), lambda i: (i % tiles_per_seq, 0))
        in_specs += [tab, tab]
        args += list(rope_tabs)
    col = lambda w: pl.BlockSpec((w, tm), lambda i: (0, i))
    out_specs = [row(D_QK), row(D_QK), row(D_V), row(D_V), row(2 * D_QK),
                 col(D_QA), row(D_KA), col(D_KA)]
    out_shape = [
        jax.ShapeDtypeStruct((ntok, D_QK), BF16),
        jax.ShapeDtypeStruct((ntok, D_QK), BF16),
        jax.ShapeDtypeStruct((ntok, D_V), BF16),
        jax.ShapeDtypeStruct((ntok, D_V), BF16),
        jax.ShapeDtypeStruct((ntok, 2 * D_QK), F32),
        jax.ShapeDtypeStruct((D_QA, ntok), BF16),
        jax.ShapeDtypeStruct((ntok, D_KA), BF16),
        jax.ShapeDtypeStruct((D_KA, ntok), BF16),
    ]
    if emit_cache:
        seqs = tm // seq_len
        out_specs += [pl.BlockSpec((seqs, D_KA, seq_len), lambda i: (i, 0, 0))] * 2
        out_shape += [jax.ShapeDtypeStruct((ntok // seq_len, D_KA, seq_len), F32)] * 2
    return pl.pallas_call(
        functools.partial(_pre_kernel, rope=rope, emit_cache=emit_cache),
        grid=(ntok // tm,),
        in_specs=in_specs,
        out_specs=out_specs,
        out_shape=out_shape,
        compiler_params=pltpu.CompilerParams(
            dimension_semantics=("arbitrary",), vmem_limit_bytes=VMEM_LIMIT),
        name="pre_mixer_rope" if rope else "pre_mixer",
    )(*args)


def _gla_chunks(units, q_ref, k_ref, v_ref, g_ref, half_lo):
    C = GLA_CHUNK
    n_pair = GLA_HEADS // 2
    rows = [pl.ds(pl.multiple_of(c * C, C), C) for (_, c, _, _, _, _) in units]

    bs = []
    for (bi, _, fwd, tri, _, _), r in zip(units, rows):
        gc = g_ref[bi, r, 0:D_QK] if fwd else g_ref[bi, r, D_QK:2 * D_QK]
        ltri = jnp.where(tri, 1.0, 0.0).astype(BF16)
        g_hi = gc.astype(BF16)
        g_lo = (gc - g_hi.astype(F32)).astype(BF16)
        bs.append(_dot(ltri, g_hi) + _dot(ltri, g_lo))

    qts, kts, kes, decays = [], [], [], []
    for (bi, _, fwd, _, _, _), r, b in zip(units, rows, bs):
        qc = q_ref[bi, r, :].astype(F32) * GLA_DK ** -0.5
        kc = k_ref[bi, r, :].astype(F32)
        btot = b[C - 1:C, :] if fwd else b[0:1, :]
        qts.append((qc * jnp.exp(b)).astype(BF16))
        kts.append((kc * jnp.exp(-b)).astype(BF16))
        kes.append((kc * jnp.exp(btot - b)).astype(BF16))
        decays.append(jnp.transpose(jnp.broadcast_to(jnp.exp(btot), (8, D_QK)))[:, 0:1])

    scores, incs, qms = [], [], []
    for (bi, _, _, tri, _, _), r, qt, kt, ke in zip(units, rows, qts, kts, kes):
        for pair in range(n_pair):
            lanes = slice(pair * 128, (pair + 1) * 128)
            incs.append(_dot_tn(ke[:, lanes], v_ref[bi, r, pair * 2 * GLA_DV:(pair + 1) * 2 * GLA_DV]))
            for hh in range(2):
                qm = jnp.where(half_lo if hh == 0 else ~half_lo, qt[:, lanes], jnp.zeros((C, 128), BF16))
                qms.append(qm)
                scores.append(jnp.where(tri, _dot_nt(qm, kt[:, lanes]), 0.0).astype(BF16))

    for ui, ((bi, _, _, _, s_ref, o_ref), r) in enumerate(zip(units, rows)):
        state = s_ref[bi]
        state_b = state.astype(BF16)
        for hd in range(GLA_HEADS):
            pair = hd // 2
            lhs = jnp.concatenate([qms[ui * GLA_HEADS + hd], scores[ui * GLA_HEADS + hd]], axis=1)
            rhs = jnp.concatenate([state_b[pair * 128:(pair + 1) * 128, :],
                                   v_ref[bi, r, hd * GLA_DV:(hd + 1) * GLA_DV]], axis=0)
            o_ref[bi, r, hd * GLA_DV:(hd + 1) * GLA_DV] = _dot(lhs, rhs)
        inc = jnp.concatenate(
            [incs[ui * n_pair + hd // 2][(hd % 2) * GLA_DK:(hd % 2 + 1) * GLA_DK,
                                         (hd % 2) * GLA_DV:(hd % 2 + 1) * GLA_DV]
             for hd in range(GLA_HEADS)], axis=0)
        s_ref[bi] = decays[ui] * state + inc


def _gla_kernel(*args, zero_init):
    q_ref, k_ref, v_ref, g_ref, gate_ref, gn_ref = args[:6]
    o_ref, sf_ref, sb_ref, of_scr, ob_scr = args[-5:]
    C = GLA_CHUNK
    nb, t, _ = q_ref.shape
    n = t // C
    row = lax.broadcasted_iota(jnp.int32, (C, C), 0)
    col = lax.broadcasted_iota(jnp.int32, (C, C), 1)
    half_lo = lax.broadcasted_iota(jnp.int32, (C, 128), 1) < GLA_DK
    refs = (q_ref, k_ref, v_ref, g_ref)
    if zero_init:
        sf_ref[...] = jnp.zeros(sf_ref.shape, F32)
        sb_ref[...] = jnp.zeros(sb_ref.shape, F32)
    else:
        sf_ref[...] = args[6][...]
        sb_ref[...] = args[7][...]

    steps = min(GLA_STEPS_PER_ITER, n)

    def body(i, carry):
        units = []
        for u in range(steps):
            c = i * steps + u
            for bi in range(nb):
                units.append((bi, c, True, row >= col, sf_ref, of_scr))
                units.append((bi, n - 1 - c, False, row <= col, sb_ref, ob_scr))
        _gla_chunks(units, *refs, half_lo)
        return carry

    lax.fori_loop(0, n // steps, body, 0)

    rt = GLA_OUT_ROWS
    gn = gn_ref[...]

    def finish(i, carry):
        rows = pl.ds(pl.multiple_of(i * rt, rt), rt)
        for bi in range(nb):
            o = of_scr[bi, rows, :] + ob_scr[bi, rows, :]
            gate = gate_ref[bi, rows, :].astype(F32)
            for hd in range(GLA_HEADS):
                sl = slice(hd * GLA_DV, (hd + 1) * GLA_DV)
                o_ref[bi, rows, sl] = (_rms(o[:, sl], gn) * gate[:, sl]).astype(BF16)
        return carry

    lax.fori_loop(0, t // rt, finish, 0)


GLA_STEPS_PER_ITER = 4
GLA_OUT_ROWS = 64


def _gla(q, k, v, g, gate, gla_g, init_states, nb):
    b, t, _ = q.shape
    seq = lambda w: pl.BlockSpec((nb, t, w), lambda i: (i, 0, 0))
    st = pl.BlockSpec((nb, D_QK, GLA_DV), lambda i: (i, 0, 0))
    zero_init = init_states is None
    return pl.pallas_call(
        functools.partial(_gla_kernel, zero_init=zero_init),
        grid=(b // nb,),
        in_specs=[seq(D_QK), seq(D_QK), seq(D_V), seq(2 * D_QK), seq(D_V),
                  pl.BlockSpec((1, GLA_DV), lambda i: (0, 0))] + ([] if zero_init else [st, st]),
        out_specs=[seq(D_V), st, st],
        out_shape=[jax.ShapeDtypeStruct((b, t, D_V), BF16),
                   jax.ShapeDtypeStruct((b, D_QK, GLA_DV), F32),
                   jax.ShapeDtypeStruct((b, D_QK, GLA_DV), F32)],
        scratch_shapes=[pltpu.VMEM((nb, t, D_V), F32), pltpu.VMEM((nb, t, D_V), F32)],
        compiler_params=pltpu.CompilerParams(
            dimension_semantics=("arbitrary",), vmem_limit_bytes=VMEM_LIMIT),
        name="gla",
    )(q, k, v, g, gate, gla_g, *(() if zero_init else init_states))


ATT_KEY_BLOCK = 256
ATT_Q_TILE = 256
LOG2E = 1.4426950408889634


ATT_TILES_PER_STEP = 4


def _attn_kernel(*args, keys_per_tile, cached):
    qt_ref, k_ref, vt_ref = args[:3]
    kc_ref, vtc_ref = args[3:5] if cached else (None, None)
    o_ref, s0, s1, m0, m1 = args[-5:]
    tq = ATT_Q_TILE
    tiles = qt_ref.shape[1] // tq
    t_new = k_ref.shape[0] // tiles if keys_per_tile else k_ref.shape[0]
    tk = t_new + (kc_ref.shape[0] if cached else 0)
    kb = min(ATT_KEY_BLOCK, tk)
    nkb = tk // kb
    zero = jnp.zeros((HEAD_DIM, tq), BF16)
    ones = jnp.ones((16, kb), BF16)
    items = [(qi, kv) for qi in range(tiles) for kv in range(ATT_KV_HEADS)]
    bufs = (s0, s1)
    mbufs = (m0, m1)

    def q_ext(qi, kv):
        cols = []
        for g in range(GROUP):
            hd = kv * GROUP + g
            qt = qt_ref[hd * HEAD_DIM:(hd + 1) * HEAD_DIM, qi * tq:(qi + 1) * tq]
            cols.append(jnp.concatenate([qt, zero] if kv == 0 else [zero, qt], axis=0))
        return jnp.concatenate(cols, axis=1)

    def key0(qi):
        return qi * tk if keys_per_tile else 0

    def score_piece(n, q_all, i, m):
        r0 = key0(items[n][0]) + i * kb
        keys = k_ref[r0:r0 + kb, :] if i * kb < t_new else kc_ref[r0 - t_new:r0 - t_new + kb, :]
        s = _dot(keys, q_all)
        bufs[n % 2][i * kb:(i + 1) * kb, :] = s
        for r in range(kb // 8):
            tile = s[r * 8:(r + 1) * 8, :]
            m = tile if m is None else jnp.maximum(m, tile)
        return m

    def value_piece(n, m, i, acc):
        qi, kv = items[n]
        c0 = key0(qi) + i * kb
        p = jnp.exp2(bufs[n % 2][i * kb:(i + 1) * kb, :] - m).astype(BF16)
        hd_rows = slice(kv * HEAD_DIM, (kv + 1) * HEAD_DIM)
        vt = vt_ref[hd_rows, c0:c0 + kb] if i * kb < t_new else vtc_ref[hd_rows, c0 - t_new:c0 - t_new + kb]
        vt1 = jnp.concatenate([vt, ones], axis=0)
        d = _dot(vt1, p)
        return d if acc is None else acc + d

    def emit(n, acc):
        qi, kv = items[n]
        o = acc[:HEAD_DIM] / acc[HEAD_DIM:HEAD_DIM + 1]
        for g in range(0, GROUP, 2):
            two = jnp.concatenate([o[:, g * tq:(g + 1) * tq], o[:, (g + 1) * tq:(g + 2) * tq]], axis=0)
            hd = kv * GROUP + g
            o_ref[qi * tq:(qi + 1) * tq, hd * HEAD_DIM:(hd + 2) * HEAD_DIM] = two.T.astype(BF16)

    def pair(n):
        q_all = q_ext(*items[n]) if n < len(items) else None
        m_prev = mbufs[(n - 1) % 2][...] if n > 0 else None
        m_new = acc = None
        for i in range(nkb):
            if n < len(items):
                m_new = score_piece(n, q_all, i, m_new)
            if n > 0:
                acc = value_piece(n - 1, m_prev, i, acc)
        if n > 0:
            emit(n - 1, acc)
        if n < len(items):
            mbufs[n % 2][...] = jnp.max(m_new, axis=0, keepdims=True)

    for n in range(len(items) + 1):
        if nkb > 1:
            pl.when(pl.program_id(0) > -(n + 1))(functools.partial(pair, n))
        else:
            pair(n)


def _attention(qt, k, vt, batch, cache=None):
    ntok = qt.shape[1]
    t_new = k.shape[0] // batch
    tq = ATT_Q_TILE
    tiles = ATT_TILES_PER_STEP
    nq = ntok // batch // tq
    keys_per_tile = nq < tiles
    extra_specs, extra, tk = [], (), t_new
    if keys_per_tile:
        assert cache is None
        seqs = tiles // nq
        k_spec = pl.BlockSpec((seqs * t_new, D_KA), lambda t: (t, 0))
        vt_spec = pl.BlockSpec((D_KA, seqs * t_new), lambda t: (0, t))
    else:
        spq = nq // tiles
        k_spec = pl.BlockSpec((t_new, D_KA), lambda t: (t // spq, 0))
        vt_spec = pl.BlockSpec((D_KA, t_new), lambda t: (0, t // spq))
        if cache is not None:
            t_old = cache[0].shape[0] // batch
            assert t_new % ATT_KEY_BLOCK == 0 and t_old % ATT_KEY_BLOCK == 0
            extra_specs = [pl.BlockSpec((t_old, D_KA), lambda t: (t // spq, 0)),
                           pl.BlockSpec((D_KA, t_old), lambda t: (0, t // spq))]
            extra, tk = cache, t_new + t_old
    return pl.pallas_call(
        functools.partial(_attn_kernel, keys_per_tile=keys_per_tile, cached=cache is not None),
        grid=(ntok // (tiles * tq),),
        in_specs=[pl.BlockSpec((D_QA, tiles * tq), lambda t: (0, t)), k_spec, vt_spec] + extra_specs,
        out_specs=pl.BlockSpec((tiles * tq, D_QA), lambda t: (t, 0)),
        out_shape=jax.ShapeDtypeStruct((ntok, D_QA), BF16),
        scratch_shapes=[pltpu.VMEM((tk, GROUP * tq), F32), pltpu.VMEM((tk, GROUP * tq), F32),
                        pltpu.VMEM((1, GROUP * tq), F32), pltpu.VMEM((1, GROUP * tq), F32)],
        compiler_params=pltpu.CompilerParams(
            dimension_semantics=("arbitrary",), vmem_limit_bytes=VMEM_LIMIT),
        name="attention",
    )(qt, k, vt, *extra)


def _post_kernel(og_ref, oa_ref, x_ref, mod_ref, n2_ref, fg_ref, wo_ref, w1_ref, w3_ref, w2_ref,
                 y_ref):
    D = D_MODEL
    mod = mod_ref[0]
    g1, sh2, sc2, g2 = (mod[:, i * D:(i + 1) * D] for i in range(2, 6))
    out = _dot(og_ref[...], wo_ref[0:D_V, :]) + _dot(oa_ref[...], wo_ref[D_V:D_V + D_QA, :])
    x1 = x_ref[...] + g1 * out
    hb = (_rms(x1, n2_ref[...]) * (1.0 + sc2) + sh2).astype(BF16)
    act = (_silu(_dot(hb, w1_ref[...])) * _dot(hb, w3_ref[...])).astype(BF16)
    x2 = x1 + g2 * _dot(act, w2_ref[...])
    y_ref[...] = _rms(x2, fg_ref[...])


def _post_mixer(o_gla, o_att, x, mod3, mod_row0, tokens_per_mod, weights, tm):
    ntok = x.shape[0]
    tiles_per_mod = tokens_per_mod // tm
    whole = pl.BlockSpec(memory_space=pltpu.VMEM)
    row = lambda w: pl.BlockSpec((tm, w), lambda i: (i, 0))
    return pl.pallas_call(
        _post_kernel,
        grid=(ntok // tm,),
        in_specs=[row(D_V), row(D_QA), row(D_MODEL),
                  pl.BlockSpec((1, 1, N_MOD * D_MODEL), lambda i: (mod_row0 + i // tiles_per_mod, 0, 0)),
                  whole, whole, whole, whole, whole, whole],
        out_specs=row(D_MODEL),
        out_shape=jax.ShapeDtypeStruct((ntok, D_MODEL), F32),
        compiler_params=pltpu.CompilerParams(
            dimension_semantics=("arbitrary",), vmem_limit_bytes=VMEM_LIMIT),
        name="post_mixer",
    )(o_gla, o_att, x, mod3, *weights)


def _rope_tables(t):
    rows = t // GRID_W
    row = jnp.repeat(jnp.arange(rows, dtype=F32), GRID_W)
    col = jnp.tile(jnp.arange(GRID_W, dtype=F32), rows)
    inv = ROPE_THETA ** (-jnp.arange(0, ROPE_AXIS_DIM, 2, dtype=F32) / ROPE_AXIS_DIM)
    ang_r = row[:, None] * inv
    ang_c = col[:, None] * inv
    cos = jnp.concatenate([jnp.cos(ang_r)] * 2 + [jnp.cos(ang_c)] * 2, axis=-1)
    sin = jnp.concatenate([-jnp.sin(ang_r), jnp.sin(ang_r), -jnp.sin(ang_c), jnp.sin(ang_c)], axis=-1)
    return jnp.tile(cos, (1, ATT_KV_HEADS)), jnp.tile(sin, (1, ATT_KV_HEADS))


def _block_mean(width, group):
    return jnp.asarray(np.kron(np.eye(width // group), np.ones((group, group))) / group, BF16)


def kernel(x_prompt, x_sample, cache_k, cache_v, state_gla_fwd, state_gla_bwd, c, c_ctx,
           ada_w, ada_b, norm1_g, norm2_g, w_in, w_gk2, b_gk2, gla_norm_g, q_norm_g, k_norm_g,
           w_out, w_ffn1, w_ffn3, w_ffn2, final_g):
    B, T, D = x_prompt.shape
    BL, TL, _ = x_sample.shape
    TP = cache_k.shape[2]
    l = 0

    cond = jnp.zeros((COND_ROWS, D), F32).at[0].set(c_ctx).at[1:1 + BL].set(c)
    mod3 = _adaln(cond, ada_w[l], ada_b[l])

    offs = np.cumsum([0, D_QK, D_QK, D_V, D_V, 2 * GLA_LOWRANK, D_QA, D_KA, D_KA])
    seg = lambda i: w_in[l][:, offs[i]:offs[i + 1]]
    w_in_r = jnp.concatenate([seg(0), seg(1), seg(2), seg(3), seg(5), seg(6), seg(7), seg(4)],
                             axis=1).astype(BF16)
    zero = jnp.zeros((GLA_LOWRANK, D_QK), F32)
    w_gk = jnp.concatenate([jnp.concatenate([w_gk2[l, 0], zero], axis=1),
                            jnp.concatenate([zero, w_gk2[l, 1]], axis=1)], axis=0).astype(BF16)
    pre_w = (norm1_g[l].reshape(1, D), w_in_r, w_gk, b_gk2[l].reshape(1, 2 * D_QK),
             jnp.tile(q_norm_g[l], ATT_HEADS).reshape(1, D_QA),
             jnp.tile(k_norm_g[l], ATT_KV_HEADS).reshape(1, D_KA),
             _block_mean(D_QA, HEAD_DIM), _block_mean(D_KA, HEAD_DIM))
    post_w = (norm2_g[l].reshape(1, D), final_g.reshape(1, D), w_out[l].astype(BF16),
              w_ffn1[l].astype(BF16), w_ffn3[l].astype(BF16), w_ffn2[l].astype(BF16))
    gla_g = gla_norm_g[l].reshape(1, GLA_DV)
    tm_pre = 1024
    tm_post = 512

    xc = x_prompt.reshape(B * T, D)
    q, k, v, gate, g, qa, ka, va, kc, vc = _pre_mixer(
        xc, mod3, 0, B * T, T, pre_w, None, True, tm_pre)
    r3 = lambda a, b, t: a.reshape(b, t, a.shape[-1])
    o_gla, s_f, s_b = _gla(r3(q, B, T), r3(k, B, T), r3(v, B, T), r3(g, B, T), r3(gate, B, T),
                           gla_g, None, 4)
    o_att = _attention(qa, ka, va, B)
    y_prompt = _post_mixer(o_gla.reshape(B * T, D_V), o_att, xc, mod3, 0, B * T, post_w, tm_post)

    xl = x_sample.reshape(BL * TL, D)
    q, k, v, gate, g, qa, ka, va = _pre_mixer(
        xl, mod3, 1, TL, TL, pre_w, _rope_tables(TL), False, tm_pre)
    o_gla, _, _ = _gla(r3(q, BL, TL), r3(k, BL, TL), r3(v, BL, TL), r3(g, BL, TL), r3(gate, BL, TL),
                       gla_g, (state_gla_fwd[:, l].reshape(BL, D_QK, GLA_DV),
                               state_gla_bwd[:, l].reshape(BL, D_QK, GLA_DV)), BL)

    k_old = cache_k[:, l].reshape(BL * TP, D_KA).astype(BF16)
    vt_old = jnp.transpose(cache_v[:, l], (2, 3, 0, 1)).reshape(D_KA, BL * TP).astype(BF16)
    o_att = _attention(qa, ka, va, BL, (k_old, vt_old))
    y_sample = _post_mixer(o_gla.reshape(BL * TL, D_V), o_att, xl, mod3, 1, TL, post_w, tm_post)

    def cache_out(ct):
        return jnp.transpose(ct.reshape(B, 1, ATT_KV_HEADS, HEAD_DIM, T), (0, 1, 4, 2, 3))

    return (y_prompt.reshape(B, T, D), y_sample.reshape(BL, TL, D), cache_out(kc), cache_out(vc),
            s_f.reshape(B, 1, GLA_HEADS, GLA_DK, GLA_DV), s_b.reshape(B, 1, GLA_HEADS, GLA_DK, GLA_DV))
```

```python
import functools

import numpy as np
import jax
import jax.numpy as jnp
from jax import lax
from jax.experimental import pallas as pl
from jax.experimental.pallas import tpu as pltpu

F32 = jnp.float32
BF16 = jnp.bfloat16

D_MODEL = 1024
GRID_W = 64
GLA_HEADS = 4
GLA_DK = 64
GLA_DV = 128
GLA_LOWRANK = 16
GLA_GATE_NORM = 16.0
GLA_CHUNK = 64
ATT_HEADS = 8
ATT_KV_HEADS = 2
HEAD_DIM = 64
ROPE_AXIS_DIM = HEAD_DIM // 2
ROPE_THETA = 10000.0
D_FF = -(-8 * D_MODEL // (3 * 256)) * 256
N_MOD = 6
EPS = 1e-6

D_QK = GLA_HEADS * GLA_DK
D_V = GLA_HEADS * GLA_DV
D_QA = ATT_HEADS * HEAD_DIM
D_KA = ATT_KV_HEADS * HEAD_DIM
GROUP = ATT_HEADS // ATT_KV_HEADS
O_Q, O_K, O_V, O_OG = 0, D_QK, 2 * D_QK, 2 * D_QK + D_V
O_QA = O_OG + D_V
O_KA = O_QA + D_QA
O_VA = O_KA + D_KA
O_LR = O_VA + D_KA
D_PROJ = O_LR + 2 * GLA_LOWRANK

COND_ROWS = 8
VMEM_LIMIT = 58 * 1024 * 1024


def _dot(a, b):
    return jnp.dot(a, b, preferred_element_type=F32)


def _dot_nt(a, b):
    return lax.dot_general(a, b, (((1,), (1,)), ((), ())), preferred_element_type=F32)


def _dot_tn(a, b):
    return lax.dot_general(a, b, (((0,), (0,)), ((), ())), preferred_element_type=F32)


def _rms(x, g):
    ms = jnp.mean(x * x, axis=-1, keepdims=True)
    return x * lax.rsqrt(ms + EPS) * g


def _silu(x):
    return x * jax.nn.sigmoid(x)


ADALN_K_BLOCK = 256


def _adaln_kernel(c_ref, w_ref, b_ref, o_ref, acc):
    k = pl.program_id(0)

    @pl.when(k == 0)
    def _():
        acc[...] = jnp.broadcast_to(b_ref[...], acc.shape)

    acc[...] += _dot(_silu(c_ref[...]).astype(BF16), w_ref[...].astype(BF16))

    @pl.when(k == pl.num_programs(0) - 1)
    def _():
        o_ref[:, 0, :] = acc[...]


def _adaln(cond, ada_w, ada_b):
    d, n = ada_w.shape
    tk = ADALN_K_BLOCK
    return pl.pallas_call(
        _adaln_kernel,
        grid=(d // tk,),
        in_specs=[
            pl.BlockSpec((COND_ROWS, tk), lambda k: (0, k)),
            pl.BlockSpec((tk, n), lambda k: (k, 0)),
            pl.BlockSpec((1, n), lambda k: (0, 0)),
        ],
        out_specs=pl.BlockSpec((COND_ROWS, 1, n), lambda k: (0, 0, 0)),
        out_shape=jax.ShapeDtypeStruct((COND_ROWS, 1, n), F32),
        scratch_shapes=[pltpu.VMEM((COND_ROWS, n), F32)],
        compiler_params=pltpu.CompilerParams(
            dimension_semantics=("arbitrary",), vmem_limit_bytes=VMEM_LIMIT),
        name="adaln",
    )(cond, ada_w, ada_b.reshape(1, n))


def _swap_halves(x, lane_lo):
    n = x.shape[-1]
    up = pltpu.roll(x, n - ROPE_AXIS_DIM // 2, axis=1)
    dn = pltpu.roll(x, ROPE_AXIS_DIM // 2, axis=1)
    return jnp.where(lane_lo, up, dn)


PRE_SUB_ROWS = 256


def _pre_kernel(*refs, rope, emit_cache):
    it = iter(refs)
    x_ref, mod_ref, n1_ref, win_ref, wgk_ref, bgk_ref = (next(it) for _ in range(6))
    qg_ref, kg_ref, bdq_ref, bdk_ref = (next(it) for _ in range(4))
    if rope:
        cos_ref, sin_ref = next(it), next(it)
    q_o, k_o, v_o, gate_o, g_o, qa_o, ka_o, va_o = (next(it) for _ in range(8))
    if emit_cache:
        kc_o, vc_o = next(it), next(it)

    sub = PRE_SUB_ROWS
    n_sub = x_ref.shape[0] // sub
    mod = mod_ref[0]
    sh1 = mod[:, 0:D_MODEL]
    sc1 = mod[:, D_MODEL:2 * D_MODEL]

    def project(j):
        rows = slice(j * sub, (j + 1) * sub)
        h = _rms(x_ref[rows, :], n1_ref[...]) * (1.0 + sc1) + sh1
        return _dot(h.astype(BF16), win_ref[...])

    def finish(j, p_ref):
        rows = slice(j * sub, (j + 1) * sub)
        q_o[rows, :] = p_ref[:, O_Q:O_Q + D_QK].astype(BF16)
        k_o[rows, :] = p_ref[:, O_K:O_K + D_QK].astype(BF16)
        v_o[rows, :] = p_ref[:, O_V:O_V + D_V].astype(BF16)
        gate_o[rows, :] = _silu(p_ref[:, O_OG:O_OG + D_V]).astype(BF16)

        gk = _dot(p_ref[:, O_LR:O_LR + 2 * GLA_LOWRANK].astype(BF16), wgk_ref[...]) + bgk_ref[...]
        g_o[rows, :] = ((jnp.minimum(gk, 0.0) - jnp.log(1.0 + jnp.exp(-jnp.abs(gk))))
                        * (1.0 / GLA_GATE_NORM))

        qa = p_ref[:, O_QA:O_QA + D_QA]
        ka = p_ref[:, O_KA:O_KA + D_KA]
        va = p_ref[:, O_VA:O_VA + D_KA]
        qa = qa * lax.rsqrt(_dot((qa * qa).astype(BF16), bdq_ref[...]) + EPS) * qg_ref[...]
        ka = ka * lax.rsqrt(_dot((ka * ka).astype(BF16), bdk_ref[...]) + EPS) * kg_ref[...]
        if rope:
            lane_q = lax.broadcasted_iota(jnp.int32, qa.shape, 1)
            lane_k = lax.broadcasted_iota(jnp.int32, ka.shape, 1)
            lo_q = (lane_q % ROPE_AXIS_DIM) < ROPE_AXIS_DIM // 2
            lo_k = (lane_k % ROPE_AXIS_DIM) < ROPE_AXIS_DIM // 2
            cos_k = cos_ref[rows, :]
            sin_k = sin_ref[rows, :]
            cos_q = jnp.concatenate([cos_k] * GROUP, axis=1)
            sin_q = jnp.concatenate([sin_k] * GROUP, axis=1)
            qa = qa * cos_q + _swap_halves(qa, lo_q) * sin_q
            ka = ka * cos_k + _swap_halves(ka, lo_k) * sin_k
        vat = va.T
        if emit_cache:
            kat = ka.T
            t = kc_o.shape[2]
            for sq in range(sub // t):
                kc_o[j * (sub // t) + sq] = kat[:, sq * t:(sq + 1) * t]
                vc_o[j * (sub // t) + sq] = vat[:, sq * t:(sq + 1) * t]
        qa_o[:, rows] = (qa * (HEAD_DIM ** -0.5 * LOG2E)).T.astype(BF16)
        ka_o[rows, :] = ka.astype(BF16)
        va_o[:, rows] = vat.astype(BF16)

    p_prev = None
    for n in range(n_sub + 1):
        p_new = project(n) if n < n_sub else None
        if n > 0:
            finish(n - 1, p_prev)
        p_prev = p_new


def _pre_mixer(x, mod3, mod_row0, tokens_per_mod, seq_len, weights, rope_tabs, emit_cache, tm):
    ntok = x.shape[0]
    n1, w_in_r, w_gk, b_gk, qg, kg, bdq, bdk = weights
    rope = rope_tabs is not None
    tiles_per_mod = tokens_per_mod // tm
    tiles_per_seq = seq_len // tm
    whole = pl.BlockSpec(memory_space=pltpu.VMEM)
    row = lambda w: pl.BlockSpec((tm, w), lambda i: (i, 0))
    in_specs = [
        row(D_MODEL),
        pl.BlockSpec((1, 1, N_MOD * D_MODEL), lambda i: (mod_row0 + i // tiles_per_mod, 0, 0)),
        whole, whole, whole, whole, whole, whole, whole, whole,
    ]
    args = [x, mod3, n1, w_in_r, w_gk, b_gk, qg, kg, bdq, bdk]
    if rope:
        tab = pl.BlockSpec((tm, D_KA), lambda i: (i % tiles_per_seq, 0))
        in_specs += [tab, tab]
        args += list(rope_tabs)
    col = lambda w: pl.BlockSpec((w, tm), lambda i: (0, i))
    out_specs = [row(D_QK), row(D_QK), row(D_V), row(D_V), row(2 * D_QK),
                 col(D_QA), row(D_KA), col(D_KA)]
    out_shape = [
        jax.ShapeDtypeStruct((ntok, D_QK), BF16),
        jax.ShapeDtypeStruct((ntok, D_QK), BF16),
        jax.ShapeDtypeStruct((ntok, D_V), BF16),
        jax.ShapeDtypeStruct((ntok, D_V), BF16),
        jax.ShapeDtypeStruct((ntok, 2 * D_QK), F32),
        jax.ShapeDtypeStruct((D_QA, ntok), BF16),
        jax.ShapeDtypeStruct((ntok, D_KA), BF16),
        jax.ShapeDtypeStruct((D_KA, ntok), BF16),
    ]
    if emit_cache:
        seqs = tm // seq_len
        out_specs += [pl.BlockSpec((seqs, D_KA, seq_len), lambda i: (i, 0, 0))] * 2
        out_shape += [jax.ShapeDtypeStruct((ntok // seq_len, D_KA, seq_len), F32)] * 2
    return pl.pallas_call(
        functools.partial(_pre_kernel, rope=rope, emit_cache=emit_cache),
        grid=(ntok // tm,),
        in_specs=in_specs,
        out_specs=out_specs,
        out_shape=out_shape,
        compiler_params=pltpu.CompilerParams(
            dimension_semantics=("arbitrary",), vmem_limit_bytes=VMEM_LIMIT),
        name="pre_mixer_rope" if rope else "pre_mixer",
    )(*args)


def _gla_chunks(units, q_ref, k_ref, v_ref, g_ref, half_lo):
    C = GLA_CHUNK
    n_pair = GLA_HEADS // 2
    rows = [pl.ds(pl.multiple_of(c * C, C), C) for (_, c, _, _, _, _) in units]

    bs = []
    for (bi, _, fwd, tri, _, _), r in zip(units, rows):
        gc = g_ref[bi, r, 0:D_QK] if fwd else g_ref[bi, r, D_QK:2 * D_QK]
        ltri = jnp.where(tri, 1.0, 0.0).astype(BF16)
        g_hi = gc.astype(BF16)
        g_lo = (gc - g_hi.astype(F32)).astype(BF16)
        bs.append(_dot(ltri, g_hi) + _dot(ltri, g_lo))

    qts, kts, kes, decays = [], [], [], []
    for (bi, _, fwd, _, _, _), r, b in zip(units, rows, bs):
        qc = q_ref[bi, r, :].astype(F32) * GLA_DK ** -0.5
        kc = k_ref[bi, r, :].astype(F32)
        btot = b[C - 1:C, :] if fwd else b[0:1, :]
        qts.append((qc * jnp.exp(b)).astype(BF16))
        kts.append((kc * jnp.exp(-b)).astype(BF16))
        kes.append((kc * jnp.exp(btot - b)).astype(BF16))
        decays.append(jnp.transpose(jnp.broadcast_to(jnp.exp(btot), (8, D_QK)))[:, 0:1])

    scores, incs, qms = [], [], []
    for (bi, _, _, tri, _, _), r, qt, kt, ke in zip(units, rows, qts, kts, kes):
        for pair in range(n_pair):
            lanes = slice(pair * 128, (pair + 1) * 128)
            incs.append(_dot_tn(ke[:, lanes], v_ref[bi, r, pair * 2 * GLA_DV:(pair + 1) * 2 * GLA_DV]))
            for hh in range(2):
                qm = jnp.where(half_lo if hh == 0 else ~half_lo, qt[:, lanes], jnp.zeros((C, 128), BF16))
                qms.append(qm)
                scores.append(jnp.where(tri, _dot_nt(qm, kt[:, lanes]), 0.0).astype(BF16))

    for ui, ((bi, _, _, _, s_ref, o_ref), r) in enumerate(zip(units, rows)):
        state = s_ref[bi]
        state_b = state.astype(BF16)
        for hd in range(GLA_HEADS):
            pair = hd // 2
            lhs = jnp.concatenate([qms[ui * GLA_HEADS + hd], scores[ui * GLA_HEADS + hd]], axis=1)
            rhs = jnp.concatenate([state_b[pair * 128:(pair + 1) * 128, :],
                                   v_ref[bi, r, hd * GLA_DV:(hd + 1) * GLA_DV]], axis=0)
            o_ref[bi, r, hd * GLA_DV:(hd + 1) * GLA_DV] = _dot(lhs, rhs)
        inc = jnp.concatenate(
            [incs[ui * n_pair + hd // 2][(hd % 2) * GLA_DK:(hd % 2 + 1) * GLA_DK,
                                         (hd % 2) * GLA_DV:(hd % 2 + 1) * GLA_DV]
             for hd in range(GLA_HEADS)], axis=0)
        s_ref[bi] = decays[ui] * state + inc


def _gla_kernel(*args, zero_init):
    q_ref, k_ref, v_ref, g_ref, gate_ref, gn_ref = args[:6]
    o_ref, sf_ref, sb_ref, of_scr, ob_scr = args[-5:]
    C = GLA_CHUNK
    nb, t, _ = q_ref.shape
    n = t // C
    row = lax.broadcasted_iota(jnp.int32, (C, C), 0)
    col = lax.broadcasted_iota(jnp.int32, (C, C), 1)
    half_lo = lax.broadcasted_iota(jnp.int32, (C, 128), 1) < GLA_DK
    refs = (q_ref, k_ref, v_ref, g_ref)
    if zero_init:
        sf_ref[...] = jnp.zeros(sf_ref.shape, F32)
        sb_ref[...] = jnp.zeros(sb_ref.shape, F32)
    else:
        sf_ref[...] = args[6][...]
        sb_ref[...] = args[7][...]

    steps = min(GLA_STEPS_PER_ITER, n)
    assert n % steps == 0

    def body(i, carry):
        units = []
        for u in range(steps):
            c = i * steps + u
            for bi in range(nb):
                units.append((bi, c, True, row >= col, sf_ref, of_scr))
                units.append((bi, n - 1 - c, False, row <= col, sb_ref, ob_scr))
        _gla_chunks(units, *refs, half_lo)
        return carry

    lax.fori_loop(0, n // steps, body, 0)

    rt = GLA_OUT_ROWS
    gn = gn_ref[...]

    def finish(i, carry):
        rows = pl.ds(pl.multiple_of(i * rt, rt), rt)
        for bi in range(nb):
            o = of_scr[bi, rows, :] + ob_scr[bi, rows, :]
            gate = gate_ref[bi, rows, :].astype(F32)
            for hd in range(GLA_HEADS):
                sl = slice(hd * GLA_DV, (hd + 1) * GLA_DV)
                o_ref[bi, rows, sl] = (_rms(o[:, sl], gn) * gate[:, sl]).astype(BF16)
        return carry

    lax.fori_loop(0, t // rt, finish, 0)


GLA_STEPS_PER_ITER = 4
GLA_OUT_ROWS = 64


def _gla(q, k, v, g, gate, gla_g, init_states, nb):
    b, t, _ = q.shape
    seq = lambda w: pl.BlockSpec((nb, t, w), lambda i: (i, 0, 0))
    st = pl.BlockSpec((nb, D_QK, GLA_DV), lambda i: (i, 0, 0))
    zero_init = init_states is None
    return pl.pallas_call(
        functools.partial(_gla_kernel, zero_init=zero_init),
        grid=(b // nb,),
        in_specs=[seq(D_QK), seq(D_QK), seq(D_V), seq(2 * D_QK), seq(D_V),
                  pl.BlockSpec((1, GLA_DV), lambda i: (0, 0))] + ([] if zero_init else [st, st]),
        out_specs=[seq(D_V), st, st],
        out_shape=[jax.ShapeDtypeStruct((b, t, D_V), BF16),
                   jax.ShapeDtypeStruct((b, D_QK, GLA_DV), F32),
                   jax.ShapeDtypeStruct((b, D_QK, GLA_DV), F32)],
        scratch_shapes=[pltpu.VMEM((nb, t, D_V), F32), pltpu.VMEM((nb, t, D_V), F32)],
        compiler_params=pltpu.CompilerParams(
            dimension_semantics=("arbitrary",), vmem_limit_bytes=VMEM_LIMIT),
        name="gla",
    )(q, k, v, g, gate, gla_g, *(() if zero_init else init_states))


ATT_KEY_BLOCK = 256
ATT_Q_TILE = 256
LOG2E = 1.4426950408889634


ATT_TILES_PER_STEP = 4


def _attn_kernel(*args, keys_per_tile, cached):
    qt_ref, k_ref, vt_ref = args[:3]
    kc_ref, vtc_ref = args[3:5] if cached else (None, None)
    o_ref, s0, s1, m0, m1 = args[-5:]
    tq = ATT_Q_TILE
    tiles = qt_ref.shape[1] // tq
    t_new = k_ref.shape[0] // tiles if keys_per_tile else k_ref.shape[0]
    tk = t_new + (kc_ref.shape[0] if cached else 0)
    kb = min(ATT_KEY_BLOCK, tk)
    nkb = tk // kb
    zero = jnp.zeros((HEAD_DIM, tq), BF16)
    ones = jnp.ones((16, kb), BF16)
    items = [(qi, kv) for qi in range(tiles) for kv in range(ATT_KV_HEADS)]
    bufs = (s0, s1)
    mbufs = (m0, m1)

    def q_ext(qi, kv):
        cols = []
        for g in range(GROUP):
            hd = kv * GROUP + g
            qt = qt_ref[hd * HEAD_DIM:(hd + 1) * HEAD_DIM, qi * tq:(qi + 1) * tq]
            cols.append(jnp.concatenate([qt, zero] if kv == 0 else [zero, qt], axis=0))
        return jnp.concatenate(cols, axis=1)

    def key0(qi):
        return qi * tk if keys_per_tile else 0

    def score_piece(n, q_all, i, m):
        r0 = key0(items[n][0]) + i * kb
        keys = k_ref[r0:r0 + kb, :] if i * kb < t_new else kc_ref[r0 - t_new:r0 - t_new + kb, :]
        s = _dot(keys, q_all)
        bufs[n % 2][i * kb:(i + 1) * kb, :] = s
        for r in range(kb // 8):
            tile = s[r * 8:(r + 1) * 8, :]
            m = tile if m is None else jnp.maximum(m, tile)
        return m

    def value_piece(n, m, i, acc):
        qi, kv = items[n]
        c0 = key0(qi) + i * kb
        p = jnp.exp2(bufs[n % 2][i * kb:(i + 1) * kb, :] - m).astype(BF16)
        hd_rows = slice(kv * HEAD_DIM, (kv + 1) * HEAD_DIM)
        vt = vt_ref[hd_rows, c0:c0 + kb] if i * kb < t_new else vtc_ref[hd_rows, c0 - t_new:c0 - t_new + kb]
        vt1 = jnp.concatenate([vt, ones], axis=0)
        d = _dot(vt1, p)
        return d if acc is None else acc + d

    def emit(n, acc):
        qi, kv = items[n]
        o = acc[:HEAD_DIM] / acc[HEAD_DIM:HEAD_DIM + 1]
        for g in range(0, GROUP, 2):
            two = jnp.concatenate([o[:, g * tq:(g + 1) * tq], o[:, (g + 1) * tq:(g + 2) * tq]], axis=0)
            hd = kv * GROUP + g
            o_ref[qi * tq:(qi + 1) * tq, hd * HEAD_DIM:(hd + 2) * HEAD_DIM] = two.T.astype(BF16)

    def pair(n):
        q_all = q_ext(*items[n]) if n < len(items) else None
        m_prev = mbufs[(n - 1) % 2][...] if n > 0 else None
        m_new = acc = None
        for i in range(nkb):
            if n < len(items):
                m_new = score_piece(n, q_all, i, m_new)
            if n > 0:
                acc = value_piece(n - 1, m_prev, i, acc)
        if n > 0:
            emit(n - 1, acc)
        if n < len(items):
            mbufs[n % 2][...] = jnp.max(m_new, axis=0, keepdims=True)

    for n in range(len(items) + 1):
        if nkb > 1:
            pl.when(pl.program_id(0) > -(n + 1))(functools.partial(pair, n))
        else:
            pair(n)


def _attention(qt, k, vt, batch, cache=None):
    ntok = qt.shape[1]
    t_new = k.shape[0] // batch
    tq = ATT_Q_TILE
    tiles = ATT_TILES_PER_STEP
    nq = ntok // batch // tq
    keys_per_tile = nq < tiles
    extra_specs, extra, tk = [], (), t_new
    if keys_per_tile:
        assert cache is None
        seqs = tiles // nq
        k_spec = pl.BlockSpec((seqs * t_new, D_KA), lambda t: (t, 0))
        vt_spec = pl.BlockSpec((D_KA, seqs * t_new), lambda t: (0, t))
    else:
        spq = nq // tiles
        k_spec = pl.BlockSpec((t_new, D_KA), lambda t: (t // spq, 0))
        vt_spec = pl.BlockSpec((D_KA, t_new), lambda t: (0, t // spq))
        if cache is not None:
            t_old = cache[0].shape[0] // batch
            assert t_new % ATT_KEY_BLOCK == 0 and t_old % ATT_KEY_BLOCK == 0
            extra_specs = [pl.BlockSpec((t_old, D_KA), lambda t: (t // spq, 0)),
                           pl.BlockSpec((D_KA, t_old), lambda t: (0, t // spq))]
            extra, tk = cache, t_new + t_old
    return pl.pallas_call(
        functools.partial(_attn_kernel, keys_per_tile=keys_per_tile, cached=cache is not None),
        grid=(ntok // (tiles * tq),),
        in_specs=[pl.BlockSpec((D_QA, tiles * tq), lambda t: (0, t)), k_spec, vt_spec] + extra_specs,
        out_specs=pl.BlockSpec((tiles * tq, D_QA), lambda t: (t, 0)),
        out_shape=jax.ShapeDtypeStruct((ntok, D_QA), BF16),
        scratch_shapes=[pltpu.VMEM((tk, GROUP * tq), F32), pltpu.VMEM((tk, GROUP * tq), F32),
                        pltpu.VMEM((1, GROUP * tq), F32), pltpu.VMEM((1, GROUP * tq), F32)],
        compiler_params=pltpu.CompilerParams(
            dimension_semantics=("arbitrary",), vmem_limit_bytes=VMEM_LIMIT),
        name="attention",
    )(qt, k, vt, *extra)


def _post_kernel(og_ref, oa_ref, x_ref, mod_ref, n2_ref, fg_ref, wo_ref, w1_ref, w3_ref, w2_ref,
                 y_ref):
    D = D_MODEL
    mod = mod_ref[0]
    g1, sh2, sc2, g2 = (mod[:, i * D:(i + 1) * D] for i in range(2, 6))
    out = _dot(og_ref[...], wo_ref[0:D_V, :]) + _dot(oa_ref[...], wo_ref[D_V:D_V + D_QA, :])
    x1 = x_ref[...] + g1 * out
    hb = (_rms(x1, n2_ref[...]) * (1.0 + sc2) + sh2).astype(BF16)
    act = (_silu(_dot(hb, w1_ref[...])) * _dot(hb, w3_ref[...])).astype(BF16)
    x2 = x1 + g2 * _dot(act, w2_ref[...])
    y_ref[...] = _rms(x2, fg_ref[...])


def _post_mixer(o_gla, o_att, x, mod3, mod_row0, tokens_per_mod, weights, tm):
    ntok = x.shape[0]
    tiles_per_mod = tokens_per_mod // tm
    whole = pl.BlockSpec(memory_space=pltpu.VMEM)
    row = lambda w: pl.BlockSpec((tm, w), lambda i: (i, 0))
    return pl.pallas_call(
        _post_kernel,
        grid=(ntok // tm,),
        in_specs=[row(D_V), row(D_QA), row(D_MODEL),
                  pl.BlockSpec((1, 1, N_MOD * D_MODEL), lambda i: (mod_row0 + i // tiles_per_mod, 0, 0)),
                  whole, whole, whole, whole, whole, whole],
        out_specs=row(D_MODEL),
        out_shape=jax.ShapeDtypeStruct((ntok, D_MODEL), F32),
        compiler_params=pltpu.CompilerParams(
            dimension_semantics=("arbitrary",), vmem_limit_bytes=VMEM_LIMIT),
        name="post_mixer",
    )(o_gla, o_att, x, mod3, *weights)


def _rope_tables(t):
    rows = t // GRID_W
    row = jnp.repeat(jnp.arange(rows, dtype=F32), GRID_W)
    col = jnp.tile(jnp.arange(GRID_W, dtype=F32), rows)
    inv = ROPE_THETA ** (-jnp.arange(0, ROPE_AXIS_DIM, 2, dtype=F32) / ROPE_AXIS_DIM)
    ang_r = row[:, None] * inv
    ang_c = col[:, None] * inv
    cos = jnp.concatenate([jnp.cos(ang_r)] * 2 + [jnp.cos(ang_c)] * 2, axis=-1)
    sin = jnp.concatenate([-jnp.sin(ang_r), jnp.sin(ang_r), -jnp.sin(ang_c), jnp.sin(ang_c)], axis=-1)
    return jnp.tile(cos, (1, ATT_KV_HEADS)), jnp.tile(sin, (1, ATT_KV_HEADS))


def _block_mean(width, group):
    return jnp.asarray(np.kron(np.eye(width // group), np.ones((group, group))) / group, BF16)


def kernel(x_prompt, x_sample, cache_k, cache_v, state_gla_fwd, state_gla_bwd, c, c_ctx,
           ada_w, ada_b, norm1_g, norm2_g, w_in, w_gk2, b_gk2, gla_norm_g, q_norm_g, k_norm_g,
           w_out, w_ffn1, w_ffn3, w_ffn2, final_g):
    B, T, D = x_prompt.shape
    BL, TL, _ = x_sample.shape
    TP = cache_k.shape[2]
    l = 0

    cond = jnp.zeros((COND_ROWS, D), F32).at[0].set(c_ctx).at[1:1 + BL].set(c)
    mod3 = _adaln(cond, ada_w[l], ada_b[l])

    offs = np.cumsum([0, D_QK, D_QK, D_V, D_V, 2 * GLA_LOWRANK, D_QA, D_KA, D_KA])
    seg = lambda i: w_in[l][:, offs[i]:offs[i + 1]]
    w_in_r = jnp.concatenate([seg(0), seg(1), seg(2), seg(3), seg(5), seg(6), seg(7), seg(4)],
                             axis=1).astype(BF16)
    zero = jnp.zeros((GLA_LOWRANK, D_QK), F32)
    w_gk = jnp.concatenate([jnp.concatenate([w_gk2[l, 0], zero], axis=1),
                            jnp.concatenate([zero, w_gk2[l, 1]], axis=1)], axis=0).astype(BF16)
    pre_w = (norm1_g[l].reshape(1, D), w_in_r, w_gk, b_gk2[l].reshape(1, 2 * D_QK),
             jnp.tile(q_norm_g[l], ATT_HEADS).reshape(1, D_QA),
             jnp.tile(k_norm_g[l], ATT_KV_HEADS).reshape(1, D_KA),
             _block_mean(D_QA, HEAD_DIM), _block_mean(D_KA, HEAD_DIM))
    post_w = (norm2_g[l].reshape(1, D), final_g.reshape(1, D), w_out[l].astype(BF16),
              w_ffn1[l].astype(BF16), w_ffn3[l].astype(BF16), w_ffn2[l].astype(BF16))
    gla_g = gla_norm_g[l].reshape(1, GLA_DV)
    tm_pre = 1024
    tm_post = 1024

    xc = x_prompt.reshape(B * T, D)
    q, k, v, gate, g, qa, ka, va, kc, vc = _pre_mixer(
        xc, mod3, 0, B * T, T, pre_w, None, True, tm_pre)
    r3 = lambda a, b, t: a.reshape(b, t, a.shape[-1])
    o_gla, s_f, s_b = _gla(r3(q, B, T), r3(k, B, T), r3(v, B, T), r3(g, B, T), r3(gate, B, T),
                           gla_g, None, 4)
    o_att = _attention(qa, ka, va, B)
    y_prompt = _post_mixer(o_gla.reshape(B * T, D_V), o_att, xc, mod3, 0, B * T, post_w, tm_post)

    xl = x_sample.reshape(BL * TL, D)
    q, k, v, gate, g, qa, ka, va = _pre_mixer(
        xl, mod3, 1, TL, TL, pre_w, _rope_tables(TL), False, tm_pre)
    o_gla, _, _ = _gla(r3(q, BL, TL), r3(k, BL, TL), r3(v, BL, TL), r3(g, BL, TL), r3(gate, BL, TL),
                       gla_g, (state_gla_fwd[:, l].reshape(BL, D_QK, GLA_DV),
                               state_gla_bwd[:, l].reshape(BL, D_QK, GLA_DV)), BL)

    k_old = cache_k[:, l].reshape(BL * TP, D_KA).astype(BF16)
    vt_old = jnp.transpose(cache_v[:, l], (2, 3, 0, 1)).reshape(D_KA, BL * TP).astype(BF16)
    o_att = _attention(qa, ka, va, BL, (k_old, vt_old))
    y_sample = _post_mixer(o_gla.reshape(BL * TL, D_V), o_att, xl, mod3, 1, TL, post_w, tm_post)

    def cache_out(ct):
        return jnp.transpose(ct.reshape(B, 1, ATT_KV_HEADS, HEAD_DIM, T), (0, 1, 4, 2, 3))

    return (y_prompt.reshape(B, T, D), y_sample.reshape(BL, TL, D), cache_out(kc), cache_out(vc),
            s_f.reshape(B, 1, GLA_HEADS, GLA_DK, GLA_DV), s_b.reshape(B, 1, GLA_HEADS, GLA_DK, GLA_DV))
```

```python
import functools

import numpy as np
import jax
import jax.numpy as jnp
from jax import lax
from jax.experimental import pallas as pl
from jax.experimental.pallas import tpu as pltpu

F32 = jnp.float32
BF16 = jnp.bfloat16

D_MODEL = 1024
GRID_W = 64
GLA_HEADS = 4
GLA_DK = 64
GLA_DV = 128
GLA_LOWRANK = 16
GLA_GATE_NORM = 16.0
GLA_CHUNK = 64
ATT_HEADS = 8
ATT_KV_HEADS = 2
HEAD_DIM = 64
ROPE_AXIS_DIM = HEAD_DIM // 2
ROPE_THETA = 10000.0
D_FF = -(-8 * D_MODEL // (3 * 256)) * 256
N_MOD = 6
EPS = 1e-6

D_QK = GLA_HEADS * GLA_DK
D_V = GLA_HEADS * GLA_DV
D_QA = ATT_HEADS * HEAD_DIM
D_KA = ATT_KV_HEADS * HEAD_DIM
GROUP = ATT_HEADS // ATT_KV_HEADS
O_Q, O_K, O_V, O_OG = 0, D_QK, 2 * D_QK, 2 * D_QK + D_V
O_QA = O_OG + D_V
O_KA = O_QA + D_QA
O_VA = O_KA + D_KA
O_LR = O_VA + D_KA
D_PROJ = O_LR + 2 * GLA_LOWRANK

COND_ROWS = 8
VMEM_LIMIT = 56 * 1024 * 1024


def _dot(a, b):
    return jnp.dot(a, b, preferred_element_type=F32)


def _dot_nt(a, b):
    return lax.dot_general(a, b, (((1,), (1,)), ((), ())), preferred_element_type=F32)


def _dot_tn(a, b):
    return lax.dot_general(a, b, (((0,), (0,)), ((), ())), preferred_element_type=F32)


def _rms(x, g):
    ms = jnp.mean(x * x, axis=-1, keepdims=True)
    return x * lax.rsqrt(ms + EPS) * g


def _silu(x):
    return x * jax.nn.sigmoid(x)


ADALN_K_BLOCK = 256


def _adaln_kernel(c_ref, w_ref, b_ref, o_ref, acc):
    k = pl.program_id(0)

    @pl.when(k == 0)
    def _():
        acc[...] = jnp.broadcast_to(b_ref[...], acc.shape)

    acc[...] += _dot(_silu(c_ref[...]).astype(BF16), w_ref[...].astype(BF16))

    @pl.when(k == pl.num_programs(0) - 1)
    def _():
        o_ref[:, 0, :] = acc[...]


def _adaln(cond, ada_w, ada_b):
    d, n = ada_w.shape
    tk = ADALN_K_BLOCK
    return pl.pallas_call(
        _adaln_kernel,
        grid=(d // tk,),
        in_specs=[
            pl.BlockSpec((COND_ROWS, tk), lambda k: (0, k)),
            pl.BlockSpec((tk, n), lambda k: (k, 0)),
            pl.BlockSpec((1, n), lambda k: (0, 0)),
        ],
        out_specs=pl.BlockSpec((COND_ROWS, 1, n), lambda k: (0, 0, 0)),
        out_shape=jax.ShapeDtypeStruct((COND_ROWS, 1, n), F32),
        scratch_shapes=[pltpu.VMEM((COND_ROWS, n), F32)],
        compiler_params=pltpu.CompilerParams(
            dimension_semantics=("arbitrary",), vmem_limit_bytes=VMEM_LIMIT),
        name="adaln",
    )(cond, ada_w, ada_b.reshape(1, n))


def _swap_halves(x, lane_lo):
    n = x.shape[-1]
    up = pltpu.roll(x, n - ROPE_AXIS_DIM // 2, axis=1)
    dn = pltpu.roll(x, ROPE_AXIS_DIM // 2, axis=1)
    return jnp.where(lane_lo, up, dn)


PRE_SUB_ROWS = 256


def _pre_kernel(*refs, rope, emit_cache):
    it = iter(refs)
    x_ref, mod_ref, n1_ref, win_ref, wgk_ref, bgk_ref = (next(it) for _ in range(6))
    qg_ref, kg_ref, bdq_ref, bdk_ref = (next(it) for _ in range(4))
    if rope:
        cos_ref, sin_ref = next(it), next(it)
    q_o, k_o, v_o, gate_o, g_o, qa_o, ka_o, va_o = (next(it) for _ in range(8))
    if emit_cache:
        kc_o, vc_o = next(it), next(it)

    sub = PRE_SUB_ROWS
    n_sub = x_ref.shape[0] // sub
    mod = mod_ref[0]
    sh1 = mod[:, 0:D_MODEL]
    sc1 = mod[:, D_MODEL:2 * D_MODEL]

    def project(j):
        rows = slice(j * sub, (j + 1) * sub)
        h = _rms(x_ref[rows, :], n1_ref[...]) * (1.0 + sc1) + sh1
        return _dot(h.astype(BF16), win_ref[...])

    def finish(j, p_ref):
        rows = slice(j * sub, (j + 1) * sub)
        q_o[rows, :] = p_ref[:, O_Q:O_Q + D_QK].astype(BF16)
        k_o[rows, :] = p_ref[:, O_K:O_K + D_QK].astype(BF16)
        v_o[rows, :] = p_ref[:, O_V:O_V + D_V].astype(BF16)
        gate_o[rows, :] = _silu(p_ref[:, O_OG:O_OG + D_V]).astype(BF16)

        gk = _dot(p_ref[:, O_LR:O_LR + 2 * GLA_LOWRANK].astype(BF16), wgk_ref[...]) + bgk_ref[...]
        g_o[rows, :] = ((jnp.minimum(gk, 0.0) - jnp.log(1.0 + jnp.exp(-jnp.abs(gk))))
                        * (1.0 / GLA_GATE_NORM))

        qa = p_ref[:, O_QA:O_QA + D_QA]
        ka = p_ref[:, O_KA:O_KA + D_KA]
        va = p_ref[:, O_VA:O_VA + D_KA]
        qa = qa * lax.rsqrt(_dot((qa * qa).astype(BF16), bdq_ref[...]) + EPS) * qg_ref[...]
        ka = ka * lax.rsqrt(_dot((ka * ka).astype(BF16), bdk_ref[...]) + EPS) * kg_ref[...]
        if rope:
            lane_q = lax.broadcasted_iota(jnp.int32, qa.shape, 1)
            lane_k = lax.broadcasted_iota(jnp.int32, ka.shape, 1)
            lo_q = (lane_q % ROPE_AXIS_DIM) < ROPE_AXIS_DIM // 2
            lo_k = (lane_k % ROPE_AXIS_DIM) < ROPE_AXIS_DIM // 2
            cos_k = cos_ref[rows, :]
            sin_k = sin_ref[rows, :]
            cos_q = jnp.concatenate([cos_k] * GROUP, axis=1)
            sin_q = jnp.concatenate([sin_k] * GROUP, axis=1)
            qa = qa * cos_q + _swap_halves(qa, lo_q) * sin_q
            ka = ka * cos_k + _swap_halves(ka, lo_k) * sin_k
        vat = va.T
        if emit_cache:
            kat = ka.T
            t = kc_o.shape[2]
            for sq in range(sub // t):
                kc_o[j * (sub // t) + sq] = kat[:, sq * t:(sq + 1) * t]
                vc_o[j * (sub // t) + sq] = vat[:, sq * t:(sq + 1) * t]
        qa_o[:, rows] = (qa * (HEAD_DIM ** -0.5 * LOG2E)).T.astype(BF16)
        ka_o[rows, :] = ka.astype(BF16)
        va_o[:, rows] = vat.astype(BF16)

    p_prev = None
    for n in range(n_sub + 1):
        p_new = project(n) if n < n_sub else None
        if n > 0:
            finish(n - 1, p_prev)
        p_prev = p_new


def _pre_mixer(x, mod3, mod_row0, tokens_per_mod, seq_len, weights, rope_tabs, emit_cache, tm):
    ntok = x.shape[0]
    n1, w_in_r, w_gk, b_gk, qg, kg, bdq, bdk = weights
    rope = rope_tabs is not None
    tiles_per_mod = tokens_per_mod // tm
    tiles_per_seq = seq_len // tm
    whole = pl.BlockSpec(memory_space=pltpu.VMEM)
    row = lambda w: pl.BlockSpec((tm, w), lambda i: (i, 0))
    in_specs = [
        row(D_MODEL),
        pl.BlockSpec((1, 1, N_MOD * D_MODEL), lambda i: (mod_row0 + i // tiles_per_mod, 0, 0)),
        whole, whole, whole, whole, whole, whole, whole, whole,
    ]
    args = [x, mod3, n1, w_in_r, w_gk, b_gk, qg, kg, bdq, bdk]
    if rope:
        tab = pl.BlockSpec((tm, D_KA), lambda i: (i % tiles_per_seq, 0))
        in_specs += [tab, tab]
        args += list(rope_tabs)
    col = lambda w: pl.BlockSpec((w, tm), lambda i: (0, i))
    out_specs = [row(D_QK), row(D_QK), row(D_V), row(D_V), row(2 * D_QK),
                 col(D_QA), row(D_KA), col(D_KA)]
    out_shape = [
        jax.ShapeDtypeStruct((ntok, D_QK), BF16),
        jax.ShapeDtypeStruct((ntok, D_QK), BF16),
        jax.ShapeDtypeStruct((ntok, D_V), BF16),
        jax.ShapeDtypeStruct((ntok, D_V), BF16),
        jax.ShapeDtypeStruct((ntok, 2 * D_QK), F32),
        jax.ShapeDtypeStruct((D_QA, ntok), BF16),
        jax.ShapeDtypeStruct((ntok, D_KA), BF16),
        jax.ShapeDtypeStruct((D_KA, ntok), BF16),
    ]
    if emit_cache:
        seqs = tm // seq_len
        out_specs += [pl.BlockSpec((seqs, D_KA, seq_len), lambda i: (i, 0, 0))] * 2
        out_shape += [jax.ShapeDtypeStruct((ntok // seq_len, D_KA, seq_len), F32)] * 2
    return pl.pallas_call(
        functools.partial(_pre_kernel, rope=rope, emit_cache=emit_cache),
        grid=(ntok // tm,),
        in_specs=in_specs,
        out_specs=out_specs,
        out_shape=out_shape,
        compiler_params=pltpu.CompilerParams(
            dimension_semantics=("arbitrary",), vmem_limit_bytes=VMEM_LIMIT),
        name="pre_mixer_rope" if rope else "pre_mixer",
    )(*args)


def _gla_chunks(units, q_ref, k_ref, v_ref, g_ref, half_lo):
    C = GLA_CHUNK
    n_pair = GLA_HEADS // 2
    rows = [pl.ds(pl.multiple_of(c * C, C), C) for (_, c, _, _, _, _) in units]

    bs = []
    for (bi, _, fwd, tri, _, _), r in zip(units, rows):
        gc = g_ref[bi, r, 0:D_QK] if fwd else g_ref[bi, r, D_QK:2 * D_QK]
        ltri = jnp.where(tri, 1.0, 0.0).astype(BF16)
        g_hi = gc.astype(BF16)
        g_lo = (gc - g_hi.astype(F32)).astype(BF16)
        bs.append(_dot(ltri, g_hi) + _dot(ltri, g_lo))

    qts, kts, kes, decays = [], [], [], []
    for (bi, _, fwd, _, _, _), r, b in zip(units, rows, bs):
        qc = q_ref[bi, r, :].astype(F32) * GLA_DK ** -0.5
        kc = k_ref[bi, r, :].astype(F32)
        btot = b[C - 1:C, :] if fwd else b[0:1, :]
        qts.append((qc * jnp.exp(b)).astype(BF16))
        kts.append((kc * jnp.exp(-b)).astype(BF16))
        kes.append((kc * jnp.exp(btot - b)).astype(BF16))
        decays.append(jnp.transpose(jnp.broadcast_to(jnp.exp(btot), (8, D_QK)))[:, 0:1])

    scores, incs, qms = [], [], []
    for (bi, _, _, tri, _, _), r, qt, kt, ke in zip(units, rows, qts, kts, kes):
        for pair in range(n_pair):
            lanes = slice(pair * 128, (pair + 1) * 128)
            incs.append(_dot_tn(ke[:, lanes], v_ref[bi, r, pair * 2 * GLA_DV:(pair + 1) * 2 * GLA_DV]))
            for hh in range(2):
                qm = jnp.where(half_lo if hh == 0 else ~half_lo, qt[:, lanes], jnp.zeros((C, 128), BF16))
                qms.append(qm)
                scores.append(jnp.where(tri, _dot_nt(qm, kt[:, lanes]), 0.0).astype(BF16))

    for ui, ((bi, _, _, _, s_ref, o_ref), r) in enumerate(zip(units, rows)):
        state = s_ref[bi]
        state_b = state.astype(BF16)
        for hd in range(GLA_HEADS):
            pair = hd // 2
            lhs = jnp.concatenate([qms[ui * GLA_HEADS + hd], scores[ui * GLA_HEADS + hd]], axis=1)
            rhs = jnp.concatenate([state_b[pair * 128:(pair + 1) * 128, :],
                                   v_ref[bi, r, hd * GLA_DV:(hd + 1) * GLA_DV]], axis=0)
            o_ref[bi, r, hd * GLA_DV:(hd + 1) * GLA_DV] = _dot(lhs, rhs)
        inc = jnp.concatenate(
            [incs[ui * n_pair + hd // 2][(hd % 2) * GLA_DK:(hd % 2 + 1) * GLA_DK,
                                         (hd % 2) * GLA_DV:(hd % 2 + 1) * GLA_DV]
             for hd in range(GLA_HEADS)], axis=0)
        s_ref[bi] = decays[ui] * state + inc


def _gla_kernel(*args, zero_init):
    q_ref, k_ref, v_ref, g_ref, gate_ref, gn_ref = args[:6]
    o_ref, sf_ref, sb_ref, of_scr, ob_scr = args[-5:]
    C = GLA_CHUNK
    nb, t, _ = q_ref.shape
    n = t // C
    row = lax.broadcasted_iota(jnp.int32, (C, C), 0)
    col = lax.broadcasted_iota(jnp.int32, (C, C), 1)
    half_lo = lax.broadcasted_iota(jnp.int32, (C, 128), 1) < GLA_DK
    refs = (q_ref, k_ref, v_ref, g_ref)
    if zero_init:
        sf_ref[...] = jnp.zeros(sf_ref.shape, F32)
        sb_ref[...] = jnp.zeros(sb_ref.shape, F32)
    else:
        sf_ref[...] = args[6][...]
        sb_ref[...] = args[7][...]

    steps = min(GLA_STEPS_PER_ITER, n)
    assert n % steps == 0

    def body(i, carry):
        units = []
        for u in range(steps):
            c = i * steps + u
            for bi in range(nb):
                units.append((bi, c, True, row >= col, sf_ref, of_scr))
                units.append((bi, n - 1 - c, False, row <= col, sb_ref, ob_scr))
        _gla_chunks(units, *refs, half_lo)
        return carry

    lax.fori_loop(0, n // steps, body, 0)

    rt = GLA_OUT_ROWS
    gn = gn_ref[...]

    def finish(i, carry):
        rows = pl.ds(pl.multiple_of(i * rt, rt), rt)
        for bi in range(nb):
            o = of_scr[bi, rows, :] + ob_scr[bi, rows, :]
            gate = gate_ref[bi, rows, :].astype(F32)
            for hd in range(GLA_HEADS):
                sl = slice(hd * GLA_DV, (hd + 1) * GLA_DV)
                o_ref[bi, rows, sl] = (_rms(o[:, sl], gn) * gate[:, sl]).astype(BF16)
        return carry

    lax.fori_loop(0, t // rt, finish, 0)


GLA_STEPS_PER_ITER = 4
GLA_OUT_ROWS = 64


def _gla(q, k, v, g, gate, gla_g, init_states, nb):
    b, t, _ = q.shape
    seq = lambda w: pl.BlockSpec((nb, t, w), lambda i: (i, 0, 0))
    st = pl.BlockSpec((nb, D_QK, GLA_DV), lambda i: (i, 0, 0))
    zero_init = init_states is None
    return pl.pallas_call(
        functools.partial(_gla_kernel, zero_init=zero_init),
        grid=(b // nb,),
        in_specs=[seq(D_QK), seq(D_QK), seq(D_V), seq(2 * D_QK), seq(D_V),
                  pl.BlockSpec((1, GLA_DV), lambda i: (0, 0))] + ([] if zero_init else [st, st]),
        out_specs=[seq(D_V), st, st],
        out_shape=[jax.ShapeDtypeStruct((b, t, D_V), BF16),
                   jax.ShapeDtypeStruct((b, D_QK, GLA_DV), F32),
                   jax.ShapeDtypeStruct((b, D_QK, GLA_DV), F32)],
        scratch_shapes=[pltpu.VMEM((nb, t, D_V), F32), pltpu.VMEM((nb, t, D_V), F32)],
        compiler_params=pltpu.CompilerParams(
            dimension_semantics=("arbitrary",), vmem_limit_bytes=VMEM_LIMIT),
        name="gla",
    )(q, k, v, g, gate, gla_g, *(() if zero_init else init_states))


ATT_KEY_BLOCK = 256
ATT_Q_TILE = 256
LOG2E = 1.4426950408889634


ATT_TILES_PER_STEP = 4


def _attn_kernel(*args, keys_per_tile, cached):
    qt_ref, k_ref, vt_ref = args[:3]
    kc_ref, vtc_ref = args[3:5] if cached else (None, None)
    o_ref, s0, s1, m0, m1, a0, a1 = args[-7:]
    tq = ATT_Q_TILE
    tiles = qt_ref.shape[1] // tq
    t_new = k_ref.shape[0] // tiles if keys_per_tile else k_ref.shape[0]
    tk = t_new + (kc_ref.shape[0] if cached else 0)
    kb = min(ATT_KEY_BLOCK, tk)
    nkb = tk // kb
    zero = jnp.zeros((HEAD_DIM, tq), BF16)
    ones = jnp.ones((16, kb), BF16)
    items = [(qi, kv) for qi in range(tiles) for kv in range(ATT_KV_HEADS)]
    bufs = (s0, s1)
    mbufs = (m0, m1)
    abufs = (a0, a1)
    separate = nkb > 1

    def q_ext(qi, kv):
        cols = []
        for g in range(GROUP):
            hd = kv * GROUP + g
            qt = qt_ref[hd * HEAD_DIM:(hd + 1) * HEAD_DIM, qi * tq:(qi + 1) * tq]
            cols.append(jnp.concatenate([qt, zero] if kv == 0 else [zero, qt], axis=0))
        return jnp.concatenate(cols, axis=1)

    def key0(qi):
        return qi * tk if keys_per_tile else 0

    def score_piece(n, q_all, i, m):
        r0 = key0(items[n][0]) + i * kb
        keys = k_ref[r0:r0 + kb, :] if i * kb < t_new else kc_ref[r0 - t_new:r0 - t_new + kb, :]
        s = _dot(keys, q_all)
        bufs[n % 2][i * kb:(i + 1) * kb, :] = s
        for r in range(kb // 8):
            tile = s[r * 8:(r + 1) * 8, :]
            m = tile if m is None else jnp.maximum(m, tile)
        return m

    def value_piece(n, m, i, acc):
        qi, kv = items[n]
        c0 = key0(qi) + i * kb
        p = jnp.exp2(bufs[n % 2][i * kb:(i + 1) * kb, :] - m).astype(BF16)
        hd_rows = slice(kv * HEAD_DIM, (kv + 1) * HEAD_DIM)
        vt = vt_ref[hd_rows, c0:c0 + kb] if i * kb < t_new else vtc_ref[hd_rows, c0 - t_new:c0 - t_new + kb]
        vt1 = jnp.concatenate([vt, ones], axis=0)
        d = _dot(vt1, p)
        return d if acc is None else acc + d

    def emit(n, acc):
        qi, kv = items[n]
        o = acc[:HEAD_DIM] / acc[HEAD_DIM:HEAD_DIM + 1]
        for g in range(0, GROUP, 2):
            two = jnp.concatenate([o[:, g * tq:(g + 1) * tq], o[:, (g + 1) * tq:(g + 2) * tq]], axis=0)
            hd = kv * GROUP + g
            o_ref[qi * tq:(qi + 1) * tq, hd * HEAD_DIM:(hd + 2) * HEAD_DIM] = two.T.astype(BF16)

    def pair(n):
        if separate and n >= 2:
            emit(n - 2, abufs[n % 2][...])
        if n > len(items):
            return
        q_all = q_ext(*items[n]) if n < len(items) else None
        m_prev = mbufs[(n - 1) % 2][...] if n > 0 else None
        m_new = acc = None
        for i in range(nkb):
            if n < len(items):
                m_new = score_piece(n, q_all, i, m_new)
            if n > 0:
                acc = value_piece(n - 1, m_prev, i, acc)
        if n > 0 and separate:
            abufs[(n - 1) % 2][...] = acc
        elif n > 0:
            emit(n - 1, acc)
        if n < len(items):
            mbufs[n % 2][...] = jnp.max(m_new, axis=0, keepdims=True)

    for n in range(len(items) + (2 if separate else 1)):
        if separate:
            pl.when(pl.program_id(0) > -(n + 1))(functools.partial(pair, n))
        else:
            pair(n)


def _attention(qt, k, vt, batch, cache=None):
    ntok = qt.shape[1]
    t_new = k.shape[0] // batch
    tq = ATT_Q_TILE
    tiles = ATT_TILES_PER_STEP
    nq = ntok // batch // tq
    keys_per_tile = nq < tiles
    extra_specs, extra, tk = [], (), t_new
    if keys_per_tile:
        assert cache is None
        seqs = tiles // nq
        k_spec = pl.BlockSpec((seqs * t_new, D_KA), lambda t: (t, 0))
        vt_spec = pl.BlockSpec((D_KA, seqs * t_new), lambda t: (0, t))
    else:
        spq = nq // tiles
        k_spec = pl.BlockSpec((t_new, D_KA), lambda t: (t // spq, 0))
        vt_spec = pl.BlockSpec((D_KA, t_new), lambda t: (0, t // spq))
        if cache is not None:
            t_old = cache[0].shape[0] // batch
            assert t_new % ATT_KEY_BLOCK == 0 and t_old % ATT_KEY_BLOCK == 0
            extra_specs = [pl.BlockSpec((t_old, D_KA), lambda t: (t // spq, 0)),
                           pl.BlockSpec((D_KA, t_old), lambda t: (0, t // spq))]
            extra, tk = cache, t_new + t_old
    return pl.pallas_call(
        functools.partial(_attn_kernel, keys_per_tile=keys_per_tile, cached=cache is not None),
        grid=(ntok // (tiles * tq),),
        in_specs=[pl.BlockSpec((D_QA, tiles * tq), lambda t: (0, t)), k_spec, vt_spec] + extra_specs,
        out_specs=pl.BlockSpec((tiles * tq, D_QA), lambda t: (t, 0)),
        out_shape=jax.ShapeDtypeStruct((ntok, D_QA), BF16),
        scratch_shapes=[pltpu.VMEM((tk, GROUP * tq), F32), pltpu.VMEM((tk, GROUP * tq), F32),
                        pltpu.VMEM((1, GROUP * tq), F32), pltpu.VMEM((1, GROUP * tq), F32),
                        pltpu.VMEM((HEAD_DIM + 16, GROUP * tq), F32),
                        pltpu.VMEM((HEAD_DIM + 16, GROUP * tq), F32)],
        compiler_params=pltpu.CompilerParams(
            dimension_semantics=("arbitrary",), vmem_limit_bytes=VMEM_LIMIT),
        name="attention",
    )(qt, k, vt, *extra)


def _post_kernel(og_ref, oa_ref, x_ref, mod_ref, n2_ref, fg_ref, wo_ref, w1_ref, w3_ref, w2_ref,
                 y_ref):
    D = D_MODEL
    mod = mod_ref[0]
    g1, sh2, sc2, g2 = (mod[:, i * D:(i + 1) * D] for i in range(2, 6))
    out = _dot(og_ref[...], wo_ref[0:D_V, :]) + _dot(oa_ref[...], wo_ref[D_V:D_V + D_QA, :])
    x1 = x_ref[...] + g1 * out
    hb = (_rms(x1, n2_ref[...]) * (1.0 + sc2) + sh2).astype(BF16)
    act = (_silu(_dot(hb, w1_ref[...])) * _dot(hb, w3_ref[...])).astype(BF16)
    x2 = x1 + g2 * _dot(act, w2_ref[...])
    y_ref[...] = _rms(x2, fg_ref[...])


def _post_mixer(o_gla, o_att, x, mod3, mod_row0, tokens_per_mod, weights, tm):
    ntok = x.shape[0]
    tiles_per_mod = tokens_per_mod // tm
    whole = pl.BlockSpec(memory_space=pltpu.VMEM)
    row = lambda w: pl.BlockSpec((tm, w), lambda i: (i, 0))
    return pl.pallas_call(
        _post_kernel,
        grid=(ntok // tm,),
        in_specs=[row(D_V), row(D_QA), row(D_MODEL),
                  pl.BlockSpec((1, 1, N_MOD * D_MODEL), lambda i: (mod_row0 + i // tiles_per_mod, 0, 0)),
                  whole, whole, whole, whole, whole, whole],
        out_specs=row(D_MODEL),
        out_shape=jax.ShapeDtypeStruct((ntok, D_MODEL), F32),
        compiler_params=pltpu.CompilerParams(
            dimension_semantics=("arbitrary",), vmem_limit_bytes=VMEM_LIMIT),
        name="post_mixer",
    )(o_gla, o_att, x, mod3, *weights)


def _rope_tables(t):
    rows = t // GRID_W
    row = jnp.repeat(jnp.arange(rows, dtype=F32), GRID_W)
    col = jnp.tile(jnp.arange(GRID_W, dtype=F32), rows)
    inv = ROPE_THETA ** (-jnp.arange(0, ROPE_AXIS_DIM, 2, dtype=F32) / ROPE_AXIS_DIM)
    ang_r = row[:, None] * inv
    ang_c = col[:, None] * inv
    cos = jnp.concatenate([jnp.cos(ang_r)] * 2 + [jnp.cos(ang_c)] * 2, axis=-1)
    sin = jnp.concatenate([-jnp.sin(ang_r), jnp.sin(ang_r), -jnp.sin(ang_c), jnp.sin(ang_c)], axis=-1)
    return jnp.tile(cos, (1, ATT_KV_HEADS)), jnp.tile(sin, (1, ATT_KV_HEADS))


def _block_mean(width, group):
    return jnp.asarray(np.kron(np.eye(width // group), np.ones((group, group))) / group, BF16)


def kernel(x_prompt, x_sample, cache_k, cache_v, state_gla_fwd, state_gla_bwd, c, c_ctx,
           ada_w, ada_b, norm1_g, norm2_g, w_in, w_gk2, b_gk2, gla_norm_g, q_norm_g, k_norm_g,
           w_out, w_ffn1, w_ffn3, w_ffn2, final_g):
    B, T, D = x_prompt.shape
    BL, TL, _ = x_sample.shape
    TP = cache_k.shape[2]
    l = 0

    cond = jnp.zeros((COND_ROWS, D), F32).at[0].set(c_ctx).at[1:1 + BL].set(c)
    mod3 = _adaln(cond, ada_w[l], ada_b[l])

    offs = np.cumsum([0, D_QK, D_QK, D_V, D_V, 2 * GLA_LOWRANK, D_QA, D_KA, D_KA])
    seg = lambda i: w_in[l][:, offs[i]:offs[i + 1]]
    w_in_r = jnp.concatenate([seg(0), seg(1), seg(2), seg(3), seg(5), seg(6), seg(7), seg(4)],
                             axis=1).astype(BF16)
    zero = jnp.zeros((GLA_LOWRANK, D_QK), F32)
    w_gk = jnp.concatenate([jnp.concatenate([w_gk2[l, 0], zero], axis=1),
                            jnp.concatenate([zero, w_gk2[l, 1]], axis=1)], axis=0).astype(BF16)
    pre_w = (norm1_g[l].reshape(1, D), w_in_r, w_gk, b_gk2[l].reshape(1, 2 * D_QK),
             jnp.tile(q_norm_g[l], ATT_HEADS).reshape(1, D_QA),
             jnp.tile(k_norm_g[l], ATT_KV_HEADS).reshape(1, D_KA),
             _block_mean(D_QA, HEAD_DIM), _block_mean(D_KA, HEAD_DIM))
    post_w = (norm2_g[l].reshape(1, D), final_g.reshape(1, D), w_out[l].astype(BF16),
              w_ffn1[l].astype(BF16), w_ffn3[l].astype(BF16), w_ffn2[l].astype(BF16))
    gla_g = gla_norm_g[l].reshape(1, GLA_DV)
    tm_pre = 1024
    tm_post = 512

    xc = x_prompt.reshape(B * T, D)
    q, k, v, gate, g, qa, ka, va, kc, vc = _pre_mixer(
        xc, mod3, 0, B * T, T, pre_w, None, True, tm_pre)
    r3 = lambda a, b, t: a.reshape(b, t, a.shape[-1])
    o_gla, s_f, s_b = _gla(r3(q, B, T), r3(k, B, T), r3(v, B, T), r3(g, B, T), r3(gate, B, T),
                           gla_g, None, 4)
    o_att = _attention(qa, ka, va, B)
    y_prompt = _post_mixer(o_gla.reshape(B * T, D_V), o_att, xc, mod3, 0, B * T, post_w, tm_post)

    xl = x_sample.reshape(BL * TL, D)
    q, k, v, gate, g, qa, ka, va = _pre_mixer(
        xl, mod3, 1, TL, TL, pre_w, _rope_tables(TL), False, tm_pre)
    o_gla, _, _ = _gla(r3(q, BL, TL), r3(k, BL, TL), r3(v, BL, TL), r3(g, BL, TL), r3(gate, BL, TL),
                       gla_g, (state_gla_fwd[:, l].reshape(BL, D_QK, GLA_DV),
                               state_gla_bwd[:, l].reshape(BL, D_QK, GLA_DV)), BL)

    k_old = cache_k[:, l].reshape(BL * TP, D_KA).astype(BF16)
    vt_old = jnp.transpose(cache_v[:, l], (2, 3, 0, 1)).reshape(D_KA, BL * TP).astype(BF16)
    o_att = _attention(qa, ka, va, BL, (k_old, vt_old))
    y_sample = _post_mixer(o_gla.reshape(BL * TL, D_V), o_att, xl, mod3, 1, TL, post_w, tm_post)

    def cache_out(ct):
        return jnp.transpose(ct.reshape(B, 1, ATT_KV_HEADS, HEAD_DIM, T), (0, 1, 4, 2, 3))

    return (y_prompt.reshape(B, T, D), y_sample.reshape(BL, TL, D), cache_out(kc), cache_out(vc),
            s_f.reshape(B, 1, GLA_HEADS, GLA_DK, GLA_DV), s_b.reshape(B, 1, GLA_HEADS, GLA_DK, GLA_DV))
```

```python
import functools

import numpy as np
import jax
import jax.numpy as jnp
from jax import lax
from jax.experimental import pallas as pl
from jax.experimental.pallas import tpu as pltpu

F32 = jnp.float32
BF16 = jnp.bfloat16

D_MODEL = 1024
GRID_W = 64
GLA_HEADS = 4
GLA_DK = 64
GLA_DV = 128
GLA_LOWRANK = 16
GLA_GATE_NORM = 16.0
GLA_CHUNK = 64
ATT_HEADS = 8
ATT_KV_HEADS = 2
HEAD_DIM = 64
ROPE_AXIS_DIM = HEAD_DIM // 2
ROPE_THETA = 10000.0
D_FF = -(-8 * D_MODEL // (3 * 256)) * 256
N_MOD = 6
EPS = 1e-6

D_QK = GLA_HEADS * GLA_DK
D_V = GLA_HEADS * GLA_DV
D_QA = ATT_HEADS * HEAD_DIM
D_KA = ATT_KV_HEADS * HEAD_DIM
GROUP = ATT_HEADS // ATT_KV_HEADS
O_Q, O_K, O_V, O_OG = 0, D_QK, 2 * D_QK, 2 * D_QK + D_V
O_QA = O_OG + D_V
O_KA = O_QA + D_QA
O_VA = O_KA + D_KA
O_LR = O_VA + D_KA
D_PROJ = O_LR + 2 * GLA_LOWRANK

COND_ROWS = 8
VMEM_LIMIT = 56 * 1024 * 1024


def _dot(a, b):
    return jnp.dot(a, b, preferred_element_type=F32)


def _dot_nt(a, b):
    return lax.dot_general(a, b, (((1,), (1,)), ((), ())), preferred_element_type=F32)


def _dot_tn(a, b):
    return lax.dot_general(a, b, (((0,), (0,)), ((), ())), preferred_element_type=F32)


def _rms(x, g):
    ms = jnp.mean(x * x, axis=-1, keepdims=True)
    return x * lax.rsqrt(ms + EPS) * g


def _silu(x):
    return x * jax.nn.sigmoid(x)


ADALN_K_BLOCK = 256


def _adaln_kernel(c_ref, w_ref, b_ref, o_ref, acc):
    k = pl.program_id(0)

    @pl.when(k == 0)
    def _():
        acc[...] = jnp.broadcast_to(b_ref[...], acc.shape)

    acc[...] += _dot(_silu(c_ref[...]).astype(BF16), w_ref[...].astype(BF16))

    @pl.when(k == pl.num_programs(0) - 1)
    def _():
        o_ref[:, 0, :] = acc[...]


def _adaln(cond, ada_w, ada_b):
    d, n = ada_w.shape
    tk = ADALN_K_BLOCK
    return pl.pallas_call(
        _adaln_kernel,
        grid=(d // tk,),
        in_specs=[
            pl.BlockSpec((COND_ROWS, tk), lambda k: (0, k)),
            pl.BlockSpec((tk, n), lambda k: (k, 0)),
            pl.BlockSpec((1, n), lambda k: (0, 0)),
        ],
        out_specs=pl.BlockSpec((COND_ROWS, 1, n), lambda k: (0, 0, 0)),
        out_shape=jax.ShapeDtypeStruct((COND_ROWS, 1, n), F32),
        scratch_shapes=[pltpu.VMEM((COND_ROWS, n), F32)],
        compiler_params=pltpu.CompilerParams(
            dimension_semantics=("arbitrary",), vmem_limit_bytes=VMEM_LIMIT),
        name="adaln",
    )(cond, ada_w, ada_b.reshape(1, n))


def _swap_halves(x, lane_lo):
    n = x.shape[-1]
    up = pltpu.roll(x, n - ROPE_AXIS_DIM // 2, axis=1)
    dn = pltpu.roll(x, ROPE_AXIS_DIM // 2, axis=1)
    return jnp.where(lane_lo, up, dn)


PRE_SUB_ROWS = 256


def _pre_kernel(*refs, rope, emit_cache):
    it = iter(refs)
    x_ref, mod_ref, n1_ref, win_ref, wgk_ref, bgk_ref = (next(it) for _ in range(6))
    qg_ref, kg_ref, bdq_ref, bdk_ref = (next(it) for _ in range(4))
    if rope:
        cos_ref, sin_ref = next(it), next(it)
    q_o, k_o, v_o, gate_o, g_o, qa_o, ka_o, va_o = (next(it) for _ in range(8))
    if emit_cache:
        kc_o, vc_o = next(it), next(it)

    sub = PRE_SUB_ROWS
    n_sub = x_ref.shape[0] // sub
    mod = mod_ref[0]
    sh1 = mod[:, 0:D_MODEL]
    sc1 = mod[:, D_MODEL:2 * D_MODEL]

    def project(j):
        rows = slice(j * sub, (j + 1) * sub)
        h = _rms(x_ref[rows, :], n1_ref[...]) * (1.0 + sc1) + sh1
        return _dot(h.astype(BF16), win_ref[...])

    def finish(j, p_ref):
        rows = slice(j * sub, (j + 1) * sub)
        q_o[rows, :] = p_ref[:, O_Q:O_Q + D_QK].astype(BF16)
        k_o[rows, :] = p_ref[:, O_K:O_K + D_QK].astype(BF16)
        v_o[rows, :] = p_ref[:, O_V:O_V + D_V].astype(BF16)
        gate_o[rows, :] = _silu(p_ref[:, O_OG:O_OG + D_V]).astype(BF16)

        gk = _dot(p_ref[:, O_LR:O_LR + 2 * GLA_LOWRANK].astype(BF16), wgk_ref[...]) + bgk_ref[...]
        g_o[rows, :] = ((jnp.minimum(gk, 0.0) - jnp.log(1.0 + jnp.exp(-jnp.abs(gk))))
                        * (1.0 / GLA_GATE_NORM))

        qa = p_ref[:, O_QA:O_QA + D_QA]
        ka = p_ref[:, O_KA:O_KA + D_KA]
        va = p_ref[:, O_VA:O_VA + D_KA]
        qa = qa * lax.rsqrt(_dot((qa * qa).astype(BF16), bdq_ref[...]) + EPS) * qg_ref[...]
        ka = ka * lax.rsqrt(_dot((ka * ka).astype(BF16), bdk_ref[...]) + EPS) * kg_ref[...]
        if rope:
            lane_q = lax.broadcasted_iota(jnp.int32, qa.shape, 1)
            lane_k = lax.broadcasted_iota(jnp.int32, ka.shape, 1)
            lo_q = (lane_q % ROPE_AXIS_DIM) < ROPE_AXIS_DIM // 2
            lo_k = (lane_k % ROPE_AXIS_DIM) < ROPE_AXIS_DIM // 2
            cos_k = cos_ref[rows, :]
            sin_k = sin_ref[rows, :]
            cos_q = jnp.concatenate([cos_k] * GROUP, axis=1)
            sin_q = jnp.concatenate([sin_k] * GROUP, axis=1)
            qa = qa * cos_q + _swap_halves(qa, lo_q) * sin_q
            ka = ka * cos_k + _swap_halves(ka, lo_k) * sin_k
        vat = va.T
        if emit_cache:
            kat = ka.T
            t = kc_o.shape[2]
            for sq in range(sub // t):
                kc_o[j * (sub // t) + sq] = kat[:, sq * t:(sq + 1) * t]
                vc_o[j * (sub // t) + sq] = vat[:, sq * t:(sq + 1) * t]
        qa_o[:, rows] = (qa * (HEAD_DIM ** -0.5 * LOG2E)).T.astype(BF16)
        ka_o[rows, :] = ka.astype(BF16)
        va_o[:, rows] = vat.astype(BF16)

    p_prev = None
    for n in range(n_sub + 1):
        p_new = project(n) if n < n_sub else None
        if n > 0:
            finish(n - 1, p_prev)
        p_prev = p_new


def _pre_mixer(x, mod3, mod_row0, tokens_per_mod, seq_len, weights, rope_tabs, emit_cache, tm):
    ntok = x.shape[0]
    n1, w_in_r, w_gk, b_gk, qg, kg, bdq, bdk = weights
    rope = rope_tabs is not None
    tiles_per_mod = tokens_per_mod // tm
    tiles_per_seq = seq_len // tm
    whole = pl.BlockSpec(memory_space=pltpu.VMEM)
    row = lambda w: pl.BlockSpec((tm, w), lambda i: (i, 0))
    in_specs = [
        row(D_MODEL),
        pl.BlockSpec((1, 1, N_MOD * D_MODEL), lambda i: (mod_row0 + i // tiles_per_mod, 0, 0)),
        whole, whole, whole, whole, whole, whole, whole, whole,
    ]
    args = [x, mod3, n1, w_in_r, w_gk, b_gk, qg, kg, bdq, bdk]
    if rope:
        tab = pl.BlockSpec((tm, D_KA), lambda i: (i % tiles_per_seq, 0))
        in_specs += [tab, tab]
        args += list(rope_tabs)
    col = lambda w: pl.BlockSpec((w, tm), lambda i: (0, i))
    out_specs = [row(D_QK), row(D_QK), row(D_V), row(D_V), row(2 * D_QK),
                 col(D_QA), row(D_KA), col(D_KA)]
    out_shape = [
        jax.ShapeDtypeStruct((ntok, D_QK), BF16),
        jax.ShapeDtypeStruct((ntok, D_QK), BF16),
        jax.ShapeDtypeStruct((ntok, D_V), BF16),
        jax.ShapeDtypeStruct((ntok, D_V), BF16),
        jax.ShapeDtypeStruct((ntok, 2 * D_QK), F32),
        jax.ShapeDtypeStruct((D_QA, ntok), BF16),
        jax.ShapeDtypeStruct((ntok, D_KA), BF16),
        jax.ShapeDtypeStruct((D_KA, ntok), BF16),
    ]
    if emit_cache:
        seqs = tm // seq_len
        out_specs += [pl.BlockSpec((seqs, D_KA, seq_len), lambda i: (i, 0, 0))] * 2
        out_shape += [jax.ShapeDtypeStruct((ntok // seq_len, D_KA, seq_len), F32)] * 2
    return pl.pallas_call(
        functools.partial(_pre_kernel, rope=rope, emit_cache=emit_cache),
        grid=(ntok // tm,),
        in_specs=in_specs,
        out_specs=out_specs,
        out_shape=out_shape,
        compiler_params=pltpu.CompilerParams(
            dimension_semantics=("arbitrary",), vmem_limit_bytes=VMEM_LIMIT),
        name="pre_mixer_rope" if rope else "pre_mixer",
    )(*args)


def _gla_chunks(units, q_ref, k_ref, v_ref, g_ref, half_lo):
    C = GLA_CHUNK
    n_pair = GLA_HEADS // 2
    rows = [pl.ds(pl.multiple_of(c * C, C), C) for (_, c, _, _, _, _) in units]

    bs = []
    for (bi, _, fwd, tri, _, _), r in zip(units, rows):
        gc = g_ref[bi, r, 0:D_QK] if fwd else g_ref[bi, r, D_QK:2 * D_QK]
        ltri = jnp.where(tri, 1.0, 0.0).astype(BF16)
        g_hi = gc.astype(BF16)
        g_lo = (gc - g_hi.astype(F32)).astype(BF16)
        bs.append(_dot(ltri, g_hi) + _dot(ltri, g_lo))

    qts, kts, kes, decays = [], [], [], []
    for (bi, _, fwd, _, _, _), r, b in zip(units, rows, bs):
        qc = q_ref[bi, r, :].astype(F32) * GLA_DK ** -0.5
        kc = k_ref[bi, r, :].astype(F32)
        btot = b[C - 1:C, :] if fwd else b[0:1, :]
        qts.append((qc * jnp.exp(b)).astype(BF16))
        kts.append((kc * jnp.exp(-b)).astype(BF16))
        kes.append((kc * jnp.exp(btot - b)).astype(BF16))
        decays.append(jnp.transpose(jnp.broadcast_to(jnp.exp(btot), (8, D_QK)))[:, 0:1])

    scores, incs, qms = [], [], []
    for (bi, _, _, tri, _, _), r, qt, kt, ke in zip(units, rows, qts, kts, kes):
        for pair in range(n_pair):
            lanes = slice(pair * 128, (pair + 1) * 128)
            incs.append(_dot_tn(ke[:, lanes], v_ref[bi, r, pair * 2 * GLA_DV:(pair + 1) * 2 * GLA_DV]))
            for hh in range(2):
                qm = jnp.where(half_lo if hh == 0 else ~half_lo, qt[:, lanes], jnp.zeros((C, 128), BF16))
                qms.append(qm)
                scores.append(jnp.where(tri, _dot_nt(qm, kt[:, lanes]), 0.0).astype(BF16))

    for ui, ((bi, _, _, _, s_ref, o_ref), r) in enumerate(zip(units, rows)):
        state = s_ref[bi]
        state_b = state.astype(BF16)
        for hd in range(GLA_HEADS):
            pair = hd // 2
            lhs = jnp.concatenate([qms[ui * GLA_HEADS + hd], scores[ui * GLA_HEADS + hd]], axis=1)
            rhs = jnp.concatenate([state_b[pair * 128:(pair + 1) * 128, :],
                                   v_ref[bi, r, hd * GLA_DV:(hd + 1) * GLA_DV]], axis=0)
            o_ref[bi, r, hd * GLA_DV:(hd + 1) * GLA_DV] = _dot(lhs, rhs)
        inc = jnp.concatenate(
            [incs[ui * n_pair + hd // 2][(hd % 2) * GLA_DK:(hd % 2 + 1) * GLA_DK,
                                         (hd % 2) * GLA_DV:(hd % 2 + 1) * GLA_DV]
             for hd in range(GLA_HEADS)], axis=0)
        s_ref[bi] = decays[ui] * state + inc


def _gla_kernel(*args, zero_init):
    q_ref, k_ref, v_ref, g_ref, gate_ref, gn_ref = args[:6]
    o_ref, sf_ref, sb_ref, of_scr, ob_scr = args[-5:]
    C = GLA_CHUNK
    nb, t, _ = q_ref.shape
    n = t // C
    row = lax.broadcasted_iota(jnp.int32, (C, C), 0)
    col = lax.broadcasted_iota(jnp.int32, (C, C), 1)
    half_lo = lax.broadcasted_iota(jnp.int32, (C, 128), 1) < GLA_DK
    refs = (q_ref, k_ref, v_ref, g_ref)
    if zero_init:
        sf_ref[...] = jnp.zeros(sf_ref.shape, F32)
        sb_ref[...] = jnp.zeros(sb_ref.shape, F32)
    else:
        sf_ref[...] = args[6][...]
        sb_ref[...] = args[7][...]

    steps = min(GLA_STEPS_PER_ITER, n)
    assert n % steps == 0

    def body(i, carry):
        units = []
        for u in range(steps):
            c = i * steps + u
            for bi in range(nb):
                units.append((bi, c, True, row >= col, sf_ref, of_scr))
                units.append((bi, n - 1 - c, False, row <= col, sb_ref, ob_scr))
        _gla_chunks(units, *refs, half_lo)
        return carry

    lax.fori_loop(0, n // steps, body, 0)

    rt = GLA_OUT_ROWS
    gn = gn_ref[...]

    def finish(i, carry):
        rows = pl.ds(pl.multiple_of(i * rt, rt), rt)
        for bi in range(nb):
            o = of_scr[bi, rows, :] + ob_scr[bi, rows, :]
            gate = gate_ref[bi, rows, :].astype(F32)
            for hd in range(GLA_HEADS):
                sl = slice(hd * GLA_DV, (hd + 1) * GLA_DV)
                o_ref[bi, rows, sl] = (_rms(o[:, sl], gn) * gate[:, sl]).astype(BF16)
        return carry

    lax.fori_loop(0, t // rt, finish, 0)


GLA_STEPS_PER_ITER = 4
GLA_OUT_ROWS = 64


def _gla(q, k, v, g, gate, gla_g, init_states, nb):
    b, t, _ = q.shape
    seq = lambda w: pl.BlockSpec((nb, t, w), lambda i: (i, 0, 0))
    st = pl.BlockSpec((nb, D_QK, GLA_DV), lambda i: (i, 0, 0))
    zero_init = init_states is None
    return pl.pallas_call(
        functools.partial(_gla_kernel, zero_init=zero_init),
        grid=(b // nb,),
        in_specs=[seq(D_QK), seq(D_QK), seq(D_V), seq(2 * D_QK), seq(D_V),
                  pl.BlockSpec((1, GLA_DV), lambda i: (0, 0))] + ([] if zero_init else [st, st]),
        out_specs=[seq(D_V), st, st],
        out_shape=[jax.ShapeDtypeStruct((b, t, D_V), BF16),
                   jax.ShapeDtypeStruct((b, D_QK, GLA_DV), F32),
                   jax.ShapeDtypeStruct((b, D_QK, GLA_DV), F32)],
        scratch_shapes=[pltpu.VMEM((nb, t, D_V), F32), pltpu.VMEM((nb, t, D_V), F32)],
        compiler_params=pltpu.CompilerParams(
            dimension_semantics=("arbitrary",), vmem_limit_bytes=VMEM_LIMIT),
        name="gla",
    )(q, k, v, g, gate, gla_g, *(() if zero_init else init_states))


ATT_KEY_BLOCK = 256
ATT_Q_TILE = 256
LOG2E = 1.4426950408889634


ATT_TILES_PER_STEP = 4


def _attn_kernel(*args, keys_per_tile, cached):
    qt_ref, k_ref, vt_ref = args[:3]
    kc_ref, vtc_ref = args[3:5] if cached else (None, None)
    o_ref, s0, s1, m0, m1, a0, a1 = args[-7:]
    tq = ATT_Q_TILE
    tiles = qt_ref.shape[1] // tq
    t_new = k_ref.shape[0] // tiles if keys_per_tile else k_ref.shape[0]
    tk = t_new + (kc_ref.shape[0] if cached else 0)
    kb = min(ATT_KEY_BLOCK, tk)
    nkb = tk // kb
    zero = jnp.zeros((HEAD_DIM, tq), BF16)
    ones = jnp.ones((16, kb), BF16)
    items = [(qi, kv) for qi in range(tiles) for kv in range(ATT_KV_HEADS)]
    bufs = (s0, s1)
    mbufs = (m0, m1)
    abufs = (a0, a1)
    separate = nkb > 1

    def q_ext(qi, kv):
        cols = []
        for g in range(GROUP):
            hd = kv * GROUP + g
            qt = qt_ref[hd * HEAD_DIM:(hd + 1) * HEAD_DIM, qi * tq:(qi + 1) * tq]
            cols.append(jnp.concatenate([qt, zero] if kv == 0 else [zero, qt], axis=0))
        return jnp.concatenate(cols, axis=1)

    def key0(qi):
        return qi * tk if keys_per_tile else 0

    def score_piece(n, q_all, i, m):
        r0 = key0(items[n][0]) + i * kb
        keys = k_ref[r0:r0 + kb, :] if i * kb < t_new else kc_ref[r0 - t_new:r0 - t_new + kb, :]
        s = _dot(keys, q_all)
        bufs[n % 2][i * kb:(i + 1) * kb, :] = s
        for r in range(kb // 8):
            tile = s[r * 8:(r + 1) * 8, :]
            m = tile if m is None else jnp.maximum(m, tile)
        return m

    def value_piece(n, m, i, acc):
        qi, kv = items[n]
        c0 = key0(qi) + i * kb
        p = jnp.exp2(bufs[n % 2][i * kb:(i + 1) * kb, :] - m).astype(BF16)
        hd_rows = slice(kv * HEAD_DIM, (kv + 1) * HEAD_DIM)
        vt = vt_ref[hd_rows, c0:c0 + kb] if i * kb < t_new else vtc_ref[hd_rows, c0 - t_new:c0 - t_new + kb]
        vt1 = jnp.concatenate([vt, ones], axis=0)
        d = _dot(vt1, p)
        return d if acc is None else acc + d

    def emit(n, acc):
        qi, kv = items[n]
        o = acc[:HEAD_DIM] / acc[HEAD_DIM:HEAD_DIM + 1]
        for g in range(0, GROUP, 2):
            two = jnp.concatenate([o[:, g * tq:(g + 1) * tq], o[:, (g + 1) * tq:(g + 2) * tq]], axis=0)
            hd = kv * GROUP + g
            o_ref[qi * tq:(qi + 1) * tq, hd * HEAD_DIM:(hd + 2) * HEAD_DIM] = two.T.astype(BF16)

    def pair(n):
        if separate and n >= 2:
            emit(n - 2, abufs[n % 2][...])
        if n > len(items):
            return
        q_all = q_ext(*items[n]) if n < len(items) else None
        m_prev = mbufs[(n - 1) % 2][...] if n > 0 else None
        m_new = acc = None
        for i in range(nkb):
            if n < len(items):
                m_new = score_piece(n, q_all, i, m_new)
            if n > 0:
                acc = value_piece(n - 1, m_prev, i, acc)
        if n > 0 and separate:
            abufs[(n - 1) % 2][...] = acc
        elif n > 0:
            emit(n - 1, acc)
        if n < len(items):
            mbufs[n % 2][...] = jnp.max(m_new, axis=0, keepdims=True)

    for n in range(len(items) + (2 if separate else 1)):
        if separate:
            pl.when(pl.program_id(0) > -(n + 1))(functools.partial(pair, n))
        else:
            pair(n)


def _attention(qt, k, vt, batch, cache=None):
    ntok = qt.shape[1]
    t_new = k.shape[0] // batch
    tq = ATT_Q_TILE
    tiles = ATT_TILES_PER_STEP
    nq = ntok // batch // tq
    keys_per_tile = nq < tiles
    extra_specs, extra, tk = [], (), t_new
    if keys_per_tile:
        assert cache is None
        seqs = tiles // nq
        k_spec = pl.BlockSpec((seqs * t_new, D_KA), lambda t: (t, 0))
        vt_spec = pl.BlockSpec((D_KA, seqs * t_new), lambda t: (0, t))
    else:
        spq = nq // tiles
        k_spec = pl.BlockSpec((t_new, D_KA), lambda t: (t // spq, 0))
        vt_spec = pl.BlockSpec((D_KA, t_new), lambda t: (0, t // spq))
        if cache is not None:
            t_old = cache[0].shape[0] // batch
            assert t_new % ATT_KEY_BLOCK == 0 and t_old % ATT_KEY_BLOCK == 0
            extra_specs = [pl.BlockSpec((t_old, D_KA), lambda t: (t // spq, 0)),
                           pl.BlockSpec((D_KA, t_old), lambda t: (0, t // spq))]
            extra, tk = cache, t_new + t_old
    return pl.pallas_call(
        functools.partial(_attn_kernel, keys_per_tile=keys_per_tile, cached=cache is not None),
        grid=(ntok // (tiles * tq),),
        in_specs=[pl.BlockSpec((D_QA, tiles * tq), lambda t: (0, t)), k_spec, vt_spec] + extra_specs,
        out_specs=pl.BlockSpec((tiles * tq, D_QA), lambda t: (t, 0)),
        out_shape=jax.ShapeDtypeStruct((ntok, D_QA), BF16),
        scratch_shapes=[pltpu.VMEM((tk, GROUP * tq), F32), pltpu.VMEM((tk, GROUP * tq), F32),
                        pltpu.VMEM((1, GROUP * tq), F32), pltpu.VMEM((1, GROUP * tq), F32),
                        pltpu.VMEM((HEAD_DIM + 16, GROUP * tq), F32),
                        pltpu.VMEM((HEAD_DIM + 16, GROUP * tq), F32)],
        compiler_params=pltpu.CompilerParams(
            dimension_semantics=("arbitrary",), vmem_limit_bytes=VMEM_LIMIT),
        name="attention",
    )(qt, k, vt, *extra)


def _post_kernel(og_ref, oa_ref, x_ref, mod_ref, n2_ref, fg_ref, wo_ref, w1_ref, w3_ref, w2_ref,
                 y_ref):
    D = D_MODEL
    mod = mod_ref[0]
    g1, sh2, sc2, g2 = (mod[:, i * D:(i + 1) * D] for i in range(2, 6))
    out = _dot(og_ref[...], wo_ref[0:D_V, :]) + _dot(oa_ref[...], wo_ref[D_V:D_V + D_QA, :])
    x1 = x_ref[...] + g1 * out
    hb = (_rms(x1, n2_ref[...]) * (1.0 + sc2) + sh2).astype(BF16)
    act = (_silu(_dot(hb, w1_ref[...])) * _dot(hb, w3_ref[...])).astype(BF16)
    x2 = x1 + g2 * _dot(act, w2_ref[...])
    y_ref[...] = _rms(x2, fg_ref[...])


POST_WEIGHT_STEPS = 16


def _post_cast_kernel(og_ref, oa_ref, x_ref, mod_ref, n2_ref, fg_ref, wo_f, w1_f, w3_f, w2_f,
                      y_ref, wo_o, w1_o, w3_o, w2_o, wo_s, w1_s, w3_s, w2_s):
    s = pl.program_id(0)

    @pl.when(s < POST_WEIGHT_STEPS)
    def _():
        for src, out, keep in ((wo_f, wo_o, wo_s), (w1_f, w1_o, w1_s), (w3_f, w3_o, w3_s),
                               (w2_f, w2_o, w2_s)):
            rows = src.shape[0]
            slab = src[...].astype(BF16)
            out[...] = slab
            keep[pl.ds(pl.multiple_of(s * rows, rows), rows), :] = slab

    @pl.when(s >= POST_WEIGHT_STEPS)
    def _():
        _post_kernel(og_ref, oa_ref, x_ref, mod_ref, n2_ref, fg_ref, wo_s, w1_s, w3_s, w2_s, y_ref)


def _post_mixer_cast(o_gla, o_att, x, mod3, mod_row, weights, tm):
    ntok = x.shape[0]
    n2, fg, wo, w1, w3, w2 = weights
    p = POST_WEIGHT_STEPS
    whole = pl.BlockSpec(memory_space=pltpu.VMEM)
    row = lambda w: pl.BlockSpec((tm, w), lambda i: (jnp.maximum(i - p, 0), 0))
    slab = lambda a: pl.BlockSpec((a.shape[0] // p, a.shape[1]), lambda i: (jnp.minimum(i, p - 1), 0))
    bf = lambda a: jax.ShapeDtypeStruct(a.shape, BF16)
    return pl.pallas_call(
        _post_cast_kernel,
        grid=(p + ntok // tm,),
        in_specs=[row(D_V), row(D_QA), row(D_MODEL),
                  pl.BlockSpec((1, 1, N_MOD * D_MODEL), lambda i: (mod_row, 0, 0)),
                  whole, whole, slab(wo), slab(w1), slab(w3), slab(w2)],
        out_specs=[row(D_MODEL), slab(wo), slab(w1), slab(w3), slab(w2)],
        out_shape=[jax.ShapeDtypeStruct((ntok, D_MODEL), F32), bf(wo), bf(w1), bf(w3), bf(w2)],
        scratch_shapes=[pltpu.VMEM(a.shape, BF16) for a in (wo, w1, w3, w2)],
        compiler_params=pltpu.CompilerParams(
            dimension_semantics=("arbitrary",), vmem_limit_bytes=VMEM_LIMIT),
        name="post_mixer_cast",
    )(o_gla, o_att, x, mod3, n2, fg, wo, w1, w3, w2)


def _post_mixer(o_gla, o_att, x, mod3, mod_row0, tokens_per_mod, weights, tm):
    ntok = x.shape[0]
    tiles_per_mod = tokens_per_mod // tm
    whole = pl.BlockSpec(memory_space=pltpu.VMEM)
    row = lambda w: pl.BlockSpec((tm, w), lambda i: (i, 0))
    return pl.pallas_call(
        _post_kernel,
        grid=(ntok // tm,),
        in_specs=[row(D_V), row(D_QA), row(D_MODEL),
                  pl.BlockSpec((1, 1, N_MOD * D_MODEL), lambda i: (mod_row0 + i // tiles_per_mod, 0, 0)),
                  whole, whole, whole, whole, whole, whole],
        out_specs=row(D_MODEL),
        out_shape=jax.ShapeDtypeStruct((ntok, D_MODEL), F32),
        compiler_params=pltpu.CompilerParams(
            dimension_semantics=("arbitrary",), vmem_limit_bytes=VMEM_LIMIT),
        name="post_mixer",
    )(o_gla, o_att, x, mod3, *weights)


def _rope_tables(t):
    rows = t // GRID_W
    row = jnp.repeat(jnp.arange(rows, dtype=F32), GRID_W)
    col = jnp.tile(jnp.arange(GRID_W, dtype=F32), rows)
    inv = ROPE_THETA ** (-jnp.arange(0, ROPE_AXIS_DIM, 2, dtype=F32) / ROPE_AXIS_DIM)
    ang_r = row[:, None] * inv
    ang_c = col[:, None] * inv
    cos = jnp.concatenate([jnp.cos(ang_r)] * 2 + [jnp.cos(ang_c)] * 2, axis=-1)
    sin = jnp.concatenate([-jnp.sin(ang_r), jnp.sin(ang_r), -jnp.sin(ang_c), jnp.sin(ang_c)], axis=-1)
    return jnp.tile(cos, (1, ATT_KV_HEADS)), jnp.tile(sin, (1, ATT_KV_HEADS))


def _block_mean(width, group):
    return jnp.asarray(np.kron(np.eye(width // group), np.ones((group, group))) / group, BF16)


def kernel(x_prompt, x_sample, cache_k, cache_v, state_gla_fwd, state_gla_bwd, c, c_ctx,
           ada_w, ada_b, norm1_g, norm2_g, w_in, w_gk2, b_gk2, gla_norm_g, q_norm_g, k_norm_g,
           w_out, w_ffn1, w_ffn3, w_ffn2, final_g):
    B, T, D = x_prompt.shape
    BL, TL, _ = x_sample.shape
    TP = cache_k.shape[2]
    l = 0

    cond = jnp.zeros((COND_ROWS, D), F32).at[0].set(c_ctx).at[1:1 + BL].set(c)
    mod3 = _adaln(cond, ada_w[l], ada_b[l])

    offs = np.cumsum([0, D_QK, D_QK, D_V, D_V, 2 * GLA_LOWRANK, D_QA, D_KA, D_KA])
    seg = lambda i: w_in[l][:, offs[i]:offs[i + 1]]
    w_in_r = jnp.concatenate([seg(0), seg(1), seg(2), seg(3), seg(5), seg(6), seg(7), seg(4)],
                             axis=1).astype(BF16)
    zero = jnp.zeros((GLA_LOWRANK, D_QK), F32)
    w_gk = jnp.concatenate([jnp.concatenate([w_gk2[l, 0], zero], axis=1),
                            jnp.concatenate([zero, w_gk2[l, 1]], axis=1)], axis=0).astype(BF16)
    pre_w = (norm1_g[l].reshape(1, D), w_in_r, w_gk, b_gk2[l].reshape(1, 2 * D_QK),
             jnp.tile(q_norm_g[l], ATT_HEADS).reshape(1, D_QA),
             jnp.tile(k_norm_g[l], ATT_KV_HEADS).reshape(1, D_KA),
             _block_mean(D_QA, HEAD_DIM), _block_mean(D_KA, HEAD_DIM))
    post_norms = (norm2_g[l].reshape(1, D), final_g.reshape(1, D))
    post_w_f32 = (w_out[l], w_ffn1[l], w_ffn3[l], w_ffn2[l])
    gla_g = gla_norm_g[l].reshape(1, GLA_DV)
    tm_pre = 1024
    tm_post = 512

    xc = x_prompt.reshape(B * T, D)
    q, k, v, gate, g, qa, ka, va, kc, vc = _pre_mixer(
        xc, mod3, 0, B * T, T, pre_w, None, True, tm_pre)
    r3 = lambda a, b, t: a.reshape(b, t, a.shape[-1])
    o_gla, s_f, s_b = _gla(r3(q, B, T), r3(k, B, T), r3(v, B, T), r3(g, B, T), r3(gate, B, T),
                           gla_g, None, 4)
    o_att = _attention(qa, ka, va, B)
    y_prompt, *post_w_bf16 = _post_mixer_cast(o_gla.reshape(B * T, D_V), o_att, xc, mod3, 0,
                                              post_norms + post_w_f32, tm_post)
    post_w = post_norms + tuple(post_w_bf16)

    xl = x_sample.reshape(BL * TL, D)
    q, k, v, gate, g, qa, ka, va = _pre_mixer(
        xl, mod3, 1, TL, TL, pre_w, _rope_tables(TL), False, tm_pre)
    o_gla, _, _ = _gla(r3(q, BL, TL), r3(k, BL, TL), r3(v, BL, TL), r3(g, BL, TL), r3(gate, BL, TL),
                       gla_g, (state_gla_fwd[:, l].reshape(BL, D_QK, GLA_DV),
                               state_gla_bwd[:, l].reshape(BL, D_QK, GLA_DV)), BL)

    k_old = cache_k[:, l].reshape(BL * TP, D_KA).astype(BF16)
    vt_old = jnp.transpose(cache_v[:, l], (2, 3, 0, 1)).reshape(D_KA, BL * TP).astype(BF16)
    o_att = _attention(qa, ka, va, BL, (k_old, vt_old))
    y_sample = _post_mixer(o_gla.reshape(BL * TL, D_V), o_att, xl, mod3, 1, TL, post_w, tm_post)

    def cache_out(ct):
        return jnp.transpose(ct.reshape(B, 1, ATT_KV_HEADS, HEAD_DIM, T), (0, 1, 4, 2, 3))

    return (y_prompt.reshape(B, T, D), y_sample.reshape(BL, TL, D), cache_out(kc), cache_out(vc),
            s_f.reshape(B, 1, GLA_HEADS, GLA_DK, GLA_DV), s_b.reshape(B, 1, GLA_HEADS, GLA_DK, GLA_DV))
```

```python
import functools

import numpy as np
import jax
import jax.numpy as jnp
from jax import lax
from jax.experimental import pallas as pl
from jax.experimental.pallas import tpu as pltpu

F32 = jnp.float32
BF16 = jnp.bfloat16

D_MODEL = 1024
GRID_W = 64
GLA_HEADS = 4
GLA_DK = 64
GLA_DV = 128
GLA_LOWRANK = 16
GLA_GATE_NORM = 16.0
GLA_CHUNK = 64
ATT_HEADS = 8
ATT_KV_HEADS = 2
HEAD_DIM = 64
ROPE_AXIS_DIM = HEAD_DIM // 2
ROPE_THETA = 10000.0
D_FF = -(-8 * D_MODEL // (3 * 256)) * 256
N_MOD = 6
EPS = 1e-6

D_QK = GLA_HEADS * GLA_DK
D_V = GLA_HEADS * GLA_DV
D_QA = ATT_HEADS * HEAD_DIM
D_KA = ATT_KV_HEADS * HEAD_DIM
GROUP = ATT_HEADS // ATT_KV_HEADS
O_Q, O_K, O_V, O_OG = 0, D_QK, 2 * D_QK, 2 * D_QK + D_V
O_QA = O_OG + D_V
O_KA = O_QA + D_QA
O_VA = O_KA + D_KA
O_LR = O_VA + D_KA
D_PROJ = O_LR + 2 * GLA_LOWRANK

COND_ROWS = 8
VMEM_LIMIT = 56 * 1024 * 1024


def _dot(a, b):
    return jnp.dot(a, b, preferred_element_type=F32)


def _dot_nt(a, b):
    return lax.dot_general(a, b, (((1,), (1,)), ((), ())), preferred_element_type=F32)


def _dot_tn(a, b):
    return lax.dot_general(a, b, (((0,), (0,)), ((), ())), preferred_element_type=F32)


def _rms(x, g):
    ms = jnp.mean(x * x, axis=-1, keepdims=True)
    return x * lax.rsqrt(ms + EPS) * g


def _silu(x):
    return x * jax.nn.sigmoid(x)


ADALN_K_BLOCK = 256


def _adaln_kernel(c_ref, w_ref, b_ref, o_ref, acc):
    k = pl.program_id(0)

    @pl.when(k == 0)
    def _():
        acc[...] = jnp.broadcast_to(b_ref[...], acc.shape)

    acc[...] += _dot(_silu(c_ref[...]).astype(BF16), w_ref[...].astype(BF16))

    @pl.when(k == pl.num_programs(0) - 1)
    def _():
        o_ref[:, 0, :] = acc[...]


def _adaln(cond, ada_w, ada_b):
    d, n = ada_w.shape
    tk = ADALN_K_BLOCK
    return pl.pallas_call(
        _adaln_kernel,
        grid=(d // tk,),
        in_specs=[
            pl.BlockSpec((COND_ROWS, tk), lambda k: (0, k)),
            pl.BlockSpec((tk, n), lambda k: (k, 0)),
            pl.BlockSpec((1, n), lambda k: (0, 0)),
        ],
        out_specs=pl.BlockSpec((COND_ROWS, 1, n), lambda k: (0, 0, 0)),
        out_shape=jax.ShapeDtypeStruct((COND_ROWS, 1, n), F32),
        scratch_shapes=[pltpu.VMEM((COND_ROWS, n), F32)],
        compiler_params=pltpu.CompilerParams(
            dimension_semantics=("arbitrary",), vmem_limit_bytes=VMEM_LIMIT),
        name="adaln",
    )(cond, ada_w, ada_b.reshape(1, n))


def _swap_halves(x, lane_lo):
    n = x.shape[-1]
    up = pltpu.roll(x, n - ROPE_AXIS_DIM // 2, axis=1)
    dn = pltpu.roll(x, ROPE_AXIS_DIM // 2, axis=1)
    return jnp.where(lane_lo, up, dn)


PRE_SUB_ROWS = 256


def _pre_kernel(*refs, rope, emit_cache, n_cast):
    it = iter(refs)
    x_ref, mod_ref, n1_ref, win_ref, wgk_ref, bgk_ref = (next(it) for _ in range(6))
    qg_ref, kg_ref, bdq_ref, bdk_ref = (next(it) for _ in range(4))
    if rope:
        cos_ref, sin_ref = next(it), next(it)
    cast_src = [next(it) for _ in range(n_cast)]
    q_o, k_o, v_o, gate_o, g_o, qa_o, ka_o, va_o = (next(it) for _ in range(8))
    if emit_cache:
        kc_o, vc_o = next(it), next(it)
    for src in cast_src:
        next(it)[...] = src[...].astype(BF16)

    sub = PRE_SUB_ROWS
    n_sub = x_ref.shape[0] // sub
    mod = mod_ref[0]
    sh1 = mod[:, 0:D_MODEL]
    sc1 = mod[:, D_MODEL:2 * D_MODEL]

    def project(j):
        rows = slice(j * sub, (j + 1) * sub)
        h = _rms(x_ref[rows, :], n1_ref[...]) * (1.0 + sc1) + sh1
        return _dot(h.astype(BF16), win_ref[...])

    def finish(j, p_ref):
        rows = slice(j * sub, (j + 1) * sub)
        q_o[rows, :] = p_ref[:, O_Q:O_Q + D_QK].astype(BF16)
        k_o[rows, :] = p_ref[:, O_K:O_K + D_QK].astype(BF16)
        v_o[rows, :] = p_ref[:, O_V:O_V + D_V].astype(BF16)
        gate_o[rows, :] = _silu(p_ref[:, O_OG:O_OG + D_V]).astype(BF16)

        gk = _dot(p_ref[:, O_LR:O_LR + 2 * GLA_LOWRANK].astype(BF16), wgk_ref[...]) + bgk_ref[...]
        g_o[rows, :] = ((jnp.minimum(gk, 0.0) - jnp.log(1.0 + jnp.exp(-jnp.abs(gk))))
                        * (1.0 / GLA_GATE_NORM))

        qa = p_ref[:, O_QA:O_QA + D_QA]
        ka = p_ref[:, O_KA:O_KA + D_KA]
        va = p_ref[:, O_VA:O_VA + D_KA]
        qa = qa * lax.rsqrt(_dot((qa * qa).astype(BF16), bdq_ref[...]) + EPS) * qg_ref[...]
        ka = ka * lax.rsqrt(_dot((ka * ka).astype(BF16), bdk_ref[...]) + EPS) * kg_ref[...]
        if rope:
            lane_q = lax.broadcasted_iota(jnp.int32, qa.shape, 1)
            lane_k = lax.broadcasted_iota(jnp.int32, ka.shape, 1)
            lo_q = (lane_q % ROPE_AXIS_DIM) < ROPE_AXIS_DIM // 2
            lo_k = (lane_k % ROPE_AXIS_DIM) < ROPE_AXIS_DIM // 2
            cos_k = cos_ref[rows, :]
            sin_k = sin_ref[rows, :]
            cos_q = jnp.concatenate([cos_k] * GROUP, axis=1)
            sin_q = jnp.concatenate([sin_k] * GROUP, axis=1)
            qa = qa * cos_q + _swap_halves(qa, lo_q) * sin_q
            ka = ka * cos_k + _swap_halves(ka, lo_k) * sin_k
        vat = va.T
        if emit_cache:
            kat = ka.T
            t = kc_o.shape[2]
            for sq in range(sub // t):
                kc_o[j * (sub // t) + sq] = kat[:, sq * t:(sq + 1) * t]
                vc_o[j * (sub // t) + sq] = vat[:, sq * t:(sq + 1) * t]
        qa_o[:, rows] = (qa * (HEAD_DIM ** -0.5 * LOG2E)).T.astype(BF16)
        ka_o[rows, :] = ka.astype(BF16)
        va_o[:, rows] = vat.astype(BF16)

    p_prev = None
    for n in range(n_sub + 1):
        p_new = project(n) if n < n_sub else None
        if n > 0:
            finish(n - 1, p_prev)
        p_prev = p_new


def _pre_mixer(x, mod3, mod_row0, tokens_per_mod, seq_len, weights, rope_tabs, emit_cache, tm,
               cast=()):
    ntok = x.shape[0]
    n1, w_in_r, w_gk, b_gk, qg, kg, bdq, bdk = weights
    rope = rope_tabs is not None
    tiles_per_mod = tokens_per_mod // tm
    tiles_per_seq = seq_len // tm
    whole = pl.BlockSpec(memory_space=pltpu.VMEM)
    row = lambda w: pl.BlockSpec((tm, w), lambda i: (i, 0))
    in_specs = [
        row(D_MODEL),
        pl.BlockSpec((1, 1, N_MOD * D_MODEL), lambda i: (mod_row0 + i // tiles_per_mod, 0, 0)),
        whole, whole, whole, whole, whole, whole, whole, whole,
    ]
    args = [x, mod3, n1, w_in_r, w_gk, b_gk, qg, kg, bdq, bdk]
    if rope:
        tab = pl.BlockSpec((tm, D_KA), lambda i: (i % tiles_per_seq, 0))
        in_specs += [tab, tab]
        args += list(rope_tabs)
    col = lambda w: pl.BlockSpec((w, tm), lambda i: (0, i))
    out_specs = [row(D_QK), row(D_QK), row(D_V), row(D_V), row(2 * D_QK),
                 col(D_QA), row(D_KA), col(D_KA)]
    out_shape = [
        jax.ShapeDtypeStruct((ntok, D_QK), BF16),
        jax.ShapeDtypeStruct((ntok, D_QK), BF16),
        jax.ShapeDtypeStruct((ntok, D_V), BF16),
        jax.ShapeDtypeStruct((ntok, D_V), BF16),
        jax.ShapeDtypeStruct((ntok, 2 * D_QK), F32),
        jax.ShapeDtypeStruct((D_QA, ntok), BF16),
        jax.ShapeDtypeStruct((ntok, D_KA), BF16),
        jax.ShapeDtypeStruct((D_KA, ntok), BF16),
    ]
    if emit_cache:
        seqs = tm // seq_len
        out_specs += [pl.BlockSpec((seqs, D_KA, seq_len), lambda i: (i, 0, 0))] * 2
        out_shape += [jax.ShapeDtypeStruct((ntok // seq_len, D_KA, seq_len), F32)] * 2
    steps = ntok // tm
    for a in cast:
        slab = pl.BlockSpec((a.shape[0] // steps, a.shape[1]), lambda i: (i, 0))
        in_specs.append(slab)
        args.append(a)
        out_specs.append(slab)
        out_shape.append(jax.ShapeDtypeStruct(a.shape, BF16))
    return pl.pallas_call(
        functools.partial(_pre_kernel, rope=rope, emit_cache=emit_cache, n_cast=len(cast)),
        grid=(ntok // tm,),
        in_specs=in_specs,
        out_specs=out_specs,
        out_shape=out_shape,
        compiler_params=pltpu.CompilerParams(
            dimension_semantics=("arbitrary",), vmem_limit_bytes=VMEM_LIMIT),
        name="pre_mixer_rope" if rope else "pre_mixer",
    )(*args)


def _gla_chunks(units, q_ref, k_ref, v_ref, g_ref, half_lo):
    C = GLA_CHUNK
    n_pair = GLA_HEADS // 2
    rows = [pl.ds(pl.multiple_of(c * C, C), C) for (_, c, _, _, _, _) in units]

    bs = []
    for (bi, _, fwd, tri, _, _), r in zip(units, rows):
        gc = g_ref[bi, r, 0:D_QK] if fwd else g_ref[bi, r, D_QK:2 * D_QK]
        ltri = jnp.where(tri, 1.0, 0.0).astype(BF16)
        g_hi = gc.astype(BF16)
        g_lo = (gc - g_hi.astype(F32)).astype(BF16)
        bs.append(_dot(ltri, g_hi) + _dot(ltri, g_lo))

    qts, kts, kes, decays = [], [], [], []
    for (bi, _, fwd, _, _, _), r, b in zip(units, rows, bs):
        qc = q_ref[bi, r, :].astype(F32) * GLA_DK ** -0.5
        kc = k_ref[bi, r, :].astype(F32)
        btot = b[C - 1:C, :] if fwd else b[0:1, :]
        qts.append((qc * jnp.exp(b)).astype(BF16))
        kts.append((kc * jnp.exp(-b)).astype(BF16))
        kes.append((kc * jnp.exp(btot - b)).astype(BF16))
        decays.append(jnp.transpose(jnp.broadcast_to(jnp.exp(btot), (8, D_QK)))[:, 0:1])

    scores, incs, qms = [], [], []
    for (bi, _, _, tri, _, _), r, qt, kt, ke in zip(units, rows, qts, kts, kes):
        for pair in range(n_pair):
            lanes = slice(pair * 128, (pair + 1) * 128)
            incs.append(_dot_tn(ke[:, lanes], v_ref[bi, r, pair * 2 * GLA_DV:(pair + 1) * 2 * GLA_DV]))
            for hh in range(2):
                qm = jnp.where(half_lo if hh == 0 else ~half_lo, qt[:, lanes], jnp.zeros((C, 128), BF16))
                qms.append(qm)
                scores.append(jnp.where(tri, _dot_nt(qm, kt[:, lanes]), 0.0).astype(BF16))

    for ui, ((bi, _, _, _, s_ref, o_ref), r) in enumerate(zip(units, rows)):
        state = s_ref[bi]
        state_b = state.astype(BF16)
        for hd in range(GLA_HEADS):
            pair = hd // 2
            lhs = jnp.concatenate([qms[ui * GLA_HEADS + hd], scores[ui * GLA_HEADS + hd]], axis=1)
            rhs = jnp.concatenate([state_b[pair * 128:(pair + 1) * 128, :],
                                   v_ref[bi, r, hd * GLA_DV:(hd + 1) * GLA_DV]], axis=0)
            o_ref[bi, r, hd * GLA_DV:(hd + 1) * GLA_DV] = _dot(lhs, rhs)
        inc = jnp.concatenate(
            [incs[ui * n_pair + hd // 2][(hd % 2) * GLA_DK:(hd % 2 + 1) * GLA_DK,
                                         (hd % 2) * GLA_DV:(hd % 2 + 1) * GLA_DV]
             for hd in range(GLA_HEADS)], axis=0)
        s_ref[bi] = decays[ui] * state + inc


def _gla_kernel(*args, zero_init):
    q_ref, k_ref, v_ref, g_ref, gate_ref, gn_ref = args[:6]
    o_ref, sf_ref, sb_ref, of_scr, ob_scr = args[-5:]
    C = GLA_CHUNK
    nb, t, _ = q_ref.shape
    n = t // C
    row = lax.broadcasted_iota(jnp.int32, (C, C), 0)
    col = lax.broadcasted_iota(jnp.int32, (C, C), 1)
    half_lo = lax.broadcasted_iota(jnp.int32, (C, 128), 1) < GLA_DK
    refs = (q_ref, k_ref, v_ref, g_ref)
    if zero_init:
        sf_ref[...] = jnp.zeros(sf_ref.shape, F32)
        sb_ref[...] = jnp.zeros(sb_ref.shape, F32)
    else:
        sf_ref[...] = args[6][...]
        sb_ref[...] = args[7][...]

    steps = min(GLA_STEPS_PER_ITER, n)
    assert n % steps == 0

    def body(i, carry):
        units = []
        for u in range(steps):
            c = i * steps + u
            for bi in range(nb):
                units.append((bi, c, True, row >= col, sf_ref, of_scr))
                units.append((bi, n - 1 - c, False, row <= col, sb_ref, ob_scr))
        _gla_chunks(units, *refs, half_lo)
        return carry

    lax.fori_loop(0, n // steps, body, 0)

    rt = GLA_OUT_ROWS
    gn = gn_ref[...]

    def finish(i, carry):
        rows = pl.ds(pl.multiple_of(i * rt, rt), rt)
        for bi in range(nb):
            o = of_scr[bi, rows, :] + ob_scr[bi, rows, :]
            gate = gate_ref[bi, rows, :].astype(F32)
            for hd in range(GLA_HEADS):
                sl = slice(hd * GLA_DV, (hd + 1) * GLA_DV)
                o_ref[bi, rows, sl] = (_rms(o[:, sl], gn) * gate[:, sl]).astype(BF16)
        return carry

    lax.fori_loop(0, t // rt, finish, 0)


GLA_STEPS_PER_ITER = 4
GLA_OUT_ROWS = 64


def _gla(q, k, v, g, gate, gla_g, init_states, nb):
    b, t, _ = q.shape
    seq = lambda w: pl.BlockSpec((nb, t, w), lambda i: (i, 0, 0))
    st = pl.BlockSpec((nb, D_QK, GLA_DV), lambda i: (i, 0, 0))
    zero_init = init_states is None
    return pl.pallas_call(
        functools.partial(_gla_kernel, zero_init=zero_init),
        grid=(b // nb,),
        in_specs=[seq(D_QK), seq(D_QK), seq(D_V), seq(2 * D_QK), seq(D_V),
                  pl.BlockSpec((1, GLA_DV), lambda i: (0, 0))] + ([] if zero_init else [st, st]),
        out_specs=[seq(D_V), st, st],
        out_shape=[jax.ShapeDtypeStruct((b, t, D_V), BF16),
                   jax.ShapeDtypeStruct((b, D_QK, GLA_DV), F32),
                   jax.ShapeDtypeStruct((b, D_QK, GLA_DV), F32)],
        scratch_shapes=[pltpu.VMEM((nb, t, D_V), F32), pltpu.VMEM((nb, t, D_V), F32)],
        compiler_params=pltpu.CompilerParams(
            dimension_semantics=("arbitrary",), vmem_limit_bytes=VMEM_LIMIT),
        name="gla",
    )(q, k, v, g, gate, gla_g, *(() if zero_init else init_states))


ATT_KEY_BLOCK = 256
ATT_Q_TILE = 256
LOG2E = 1.4426950408889634


ATT_TILES_PER_STEP = 4


def _attn_kernel(*args, keys_per_tile, cached):
    qt_ref, k_ref, vt_ref = args[:3]
    kc_ref, vtc_ref = args[3:5] if cached else (None, None)
    o_ref, s0, s1, m0, m1, a0, a1 = args[-7:]
    tq = ATT_Q_TILE
    tiles = qt_ref.shape[1] // tq
    t_new = k_ref.shape[0] // tiles if keys_per_tile else k_ref.shape[0]
    tk = t_new + (kc_ref.shape[0] if cached else 0)
    kb = min(ATT_KEY_BLOCK, tk)
    nkb = tk // kb
    zero = jnp.zeros((HEAD_DIM, tq), BF16)
    ones = jnp.ones((16, kb), BF16)
    items = [(qi, kv) for qi in range(tiles) for kv in range(ATT_KV_HEADS)]
    bufs = (s0, s1)
    mbufs = (m0, m1)
    abufs = (a0, a1)
    separate = nkb > 1

    def q_ext(qi, kv):
        cols = []
        for g in range(GROUP):
            hd = kv * GROUP + g
            qt = qt_ref[hd * HEAD_DIM:(hd + 1) * HEAD_DIM, qi * tq:(qi + 1) * tq]
            cols.append(jnp.concatenate([qt, zero] if kv == 0 else [zero, qt], axis=0))
        return jnp.concatenate(cols, axis=1)

    def key0(qi):
        return qi * tk if keys_per_tile else 0

    def score_piece(n, q_all, i, m):
        r0 = key0(items[n][0]) + i * kb
        keys = k_ref[r0:r0 + kb, :] if i * kb < t_new else kc_ref[r0 - t_new:r0 - t_new + kb, :]
        s = _dot(keys, q_all)
        bufs[n % 2][i * kb:(i + 1) * kb, :] = s
        for r in range(kb // 8):
            tile = s[r * 8:(r + 1) * 8, :]
            m = tile if m is None else jnp.maximum(m, tile)
        return m

    def value_piece(n, m, i, acc):
        qi, kv = items[n]
        c0 = key0(qi) + i * kb
        p = jnp.exp2(bufs[n % 2][i * kb:(i + 1) * kb, :] - m).astype(BF16)
        hd_rows = slice(kv * HEAD_DIM, (kv + 1) * HEAD_DIM)
        vt = vt_ref[hd_rows, c0:c0 + kb] if i * kb < t_new else vtc_ref[hd_rows, c0 - t_new:c0 - t_new + kb]
        vt1 = jnp.concatenate([vt, ones], axis=0)
        d = _dot(vt1, p)
        return d if acc is None else acc + d

    def emit(n, acc):
        qi, kv = items[n]
        o = acc[:HEAD_DIM] / acc[HEAD_DIM:HEAD_DIM + 1]
        for g in range(0, GROUP, 2):
            two = jnp.concatenate([o[:, g * tq:(g + 1) * tq], o[:, (g + 1) * tq:(g + 2) * tq]], axis=0)
            hd = kv * GROUP + g
            o_ref[qi * tq:(qi + 1) * tq, hd * HEAD_DIM:(hd + 2) * HEAD_DIM] = two.T.astype(BF16)

    def pair(n):
        if separate and n >= 2:
            emit(n - 2, abufs[n % 2][...])
        if n > len(items):
            return
        q_all = q_ext(*items[n]) if n < len(items) else None
        m_prev = mbufs[(n - 1) % 2][...] if n > 0 else None
        m_new = acc = None
        for i in range(nkb):
            if n < len(items):
                m_new = score_piece(n, q_all, i, m_new)
            if n > 0:
                acc = value_piece(n - 1, m_prev, i, acc)
        if n > 0 and separate:
            abufs[(n - 1) % 2][...] = acc
        elif n > 0:
            emit(n - 1, acc)
        if n < len(items):
            mbufs[n % 2][...] = jnp.max(m_new, axis=0, keepdims=True)

    for n in range(len(items) + (2 if separate else 1)):
        if separate:
            pl.when(pl.program_id(0) > -(n + 1))(functools.partial(pair, n))
        else:
            pair(n)


def _attention(qt, k, vt, batch, cache=None):
    ntok = qt.shape[1]
    t_new = k.shape[0] // batch
    tq = ATT_Q_TILE
    tiles = ATT_TILES_PER_STEP
    nq = ntok // batch // tq
    keys_per_tile = nq < tiles
    extra_specs, extra, tk = [], (), t_new
    if keys_per_tile:
        assert cache is None
        seqs = tiles // nq
        k_spec = pl.BlockSpec((seqs * t_new, D_KA), lambda t: (t, 0))
        vt_spec = pl.BlockSpec((D_KA, seqs * t_new), lambda t: (0, t))
    else:
        spq = nq // tiles
        k_spec = pl.BlockSpec((t_new, D_KA), lambda t: (t // spq, 0))
        vt_spec = pl.BlockSpec((D_KA, t_new), lambda t: (0, t // spq))
        if cache is not None:
            t_old = cache[0].shape[0] // batch
            assert t_new % ATT_KEY_BLOCK == 0 and t_old % ATT_KEY_BLOCK == 0
            extra_specs = [pl.BlockSpec((t_old, D_KA), lambda t: (t // spq, 0)),
                           pl.BlockSpec((D_KA, t_old), lambda t: (0, t // spq))]
            extra, tk = cache, t_new + t_old
    return pl.pallas_call(
        functools.partial(_attn_kernel, keys_per_tile=keys_per_tile, cached=cache is not None),
        grid=(ntok // (tiles * tq),),
        in_specs=[pl.BlockSpec((D_QA, tiles * tq), lambda t: (0, t)), k_spec, vt_spec] + extra_specs,
        out_specs=pl.BlockSpec((tiles * tq, D_QA), lambda t: (t, 0)),
        out_shape=jax.ShapeDtypeStruct((ntok, D_QA), BF16),
        scratch_shapes=[pltpu.VMEM((tk, GROUP * tq), F32), pltpu.VMEM((tk, GROUP * tq), F32),
                        pltpu.VMEM((1, GROUP * tq), F32), pltpu.VMEM((1, GROUP * tq), F32),
                        pltpu.VMEM((HEAD_DIM + 16, GROUP * tq), F32),
                        pltpu.VMEM((HEAD_DIM + 16, GROUP * tq), F32)],
        compiler_params=pltpu.CompilerParams(
            dimension_semantics=("arbitrary",), vmem_limit_bytes=VMEM_LIMIT),
        name="attention",
    )(qt, k, vt, *extra)


def _post_kernel(og_ref, oa_ref, x_ref, mod_ref, n2_ref, fg_ref, wo_ref, w1_ref, w3_ref, w2_ref,
                 y_ref):
    D = D_MODEL
    mod = mod_ref[0]
    g1, sh2, sc2, g2 = (mod[:, i * D:(i + 1) * D] for i in range(2, 6))
    out = _dot(og_ref[...], wo_ref[0:D_V, :]) + _dot(oa_ref[...], wo_ref[D_V:D_V + D_QA, :])
    x1 = x_ref[...] + g1 * out
    hb = (_rms(x1, n2_ref[...]) * (1.0 + sc2) + sh2).astype(BF16)
    act = (_silu(_dot(hb, w1_ref[...])) * _dot(hb, w3_ref[...])).astype(BF16)
    x2 = x1 + g2 * _dot(act, w2_ref[...])
    y_ref[...] = _rms(x2, fg_ref[...])


def _post_mixer(o_gla, o_att, x, mod3, mod_row0, tokens_per_mod, weights, tm):
    ntok = x.shape[0]
    tiles_per_mod = tokens_per_mod // tm
    whole = pl.BlockSpec(memory_space=pltpu.VMEM)
    row = lambda w: pl.BlockSpec((tm, w), lambda i: (i, 0))
    return pl.pallas_call(
        _post_kernel,
        grid=(ntok // tm,),
        in_specs=[row(D_V), row(D_QA), row(D_MODEL),
                  pl.BlockSpec((1, 1, N_MOD * D_MODEL), lambda i: (mod_row0 + i // tiles_per_mod, 0, 0)),
                  whole, whole, whole, whole, whole, whole],
        out_specs=row(D_MODEL),
        out_shape=jax.ShapeDtypeStruct((ntok, D_MODEL), F32),
        compiler_params=pltpu.CompilerParams(
            dimension_semantics=("arbitrary",), vmem_limit_bytes=VMEM_LIMIT),
        name="post_mixer",
    )(o_gla, o_att, x, mod3, *weights)


def _rope_tables(t):
    rows = t // GRID_W
    row = jnp.repeat(jnp.arange(rows, dtype=F32), GRID_W)
    col = jnp.tile(jnp.arange(GRID_W, dtype=F32), rows)
    inv = ROPE_THETA ** (-jnp.arange(0, ROPE_AXIS_DIM, 2, dtype=F32) / ROPE_AXIS_DIM)
    ang_r = row[:, None] * inv
    ang_c = col[:, None] * inv
    cos = jnp.concatenate([jnp.cos(ang_r)] * 2 + [jnp.cos(ang_c)] * 2, axis=-1)
    sin = jnp.concatenate([-jnp.sin(ang_r), jnp.sin(ang_r), -jnp.sin(ang_c), jnp.sin(ang_c)], axis=-1)
    return jnp.tile(cos, (1, ATT_KV_HEADS)), jnp.tile(sin, (1, ATT_KV_HEADS))


def _block_mean(width, group):
    return jnp.asarray(np.kron(np.eye(width // group), np.ones((group, group))) / group, BF16)


def kernel(x_prompt, x_sample, cache_k, cache_v, state_gla_fwd, state_gla_bwd, c, c_ctx,
           ada_w, ada_b, norm1_g, norm2_g, w_in, w_gk2, b_gk2, gla_norm_g, q_norm_g, k_norm_g,
           w_out, w_ffn1, w_ffn3, w_ffn2, final_g):
    B, T, D = x_prompt.shape
    BL, TL, _ = x_sample.shape
    TP = cache_k.shape[2]
    l = 0

    cond = jnp.zeros((COND_ROWS, D), F32).at[0].set(c_ctx).at[1:1 + BL].set(c)
    mod3 = _adaln(cond, ada_w[l], ada_b[l])

    offs = np.cumsum([0, D_QK, D_QK, D_V, D_V, 2 * GLA_LOWRANK, D_QA, D_KA, D_KA])
    seg = lambda i: w_in[l][:, offs[i]:offs[i + 1]]
    w_in_r = jnp.concatenate([seg(0), seg(1), seg(2), seg(3), seg(5), seg(6), seg(7), seg(4)],
                             axis=1).astype(BF16)
    zero = jnp.zeros((GLA_LOWRANK, D_QK), F32)
    w_gk = jnp.concatenate([jnp.concatenate([w_gk2[l, 0], zero], axis=1),
                            jnp.concatenate([zero, w_gk2[l, 1]], axis=1)], axis=0).astype(BF16)
    pre_w = (norm1_g[l].reshape(1, D), w_in_r, w_gk, b_gk2[l].reshape(1, 2 * D_QK),
             jnp.tile(q_norm_g[l], ATT_HEADS).reshape(1, D_QA),
             jnp.tile(k_norm_g[l], ATT_KV_HEADS).reshape(1, D_KA),
             _block_mean(D_QA, HEAD_DIM), _block_mean(D_KA, HEAD_DIM))
    post_norms = (norm2_g[l].reshape(1, D), final_g.reshape(1, D))
    post_w_f32 = (w_out[l], w_ffn1[l], w_ffn3[l], w_ffn2[l])
    gla_g = gla_norm_g[l].reshape(1, GLA_DV)
    tm_pre = 1024
    tm_post = 512

    xc = x_prompt.reshape(B * T, D)
    q, k, v, gate, g, qa, ka, va, kc, vc, *post_w_bf16 = _pre_mixer(
        xc, mod3, 0, B * T, T, pre_w, None, True, tm_pre, cast=post_w_f32)
    post_w = post_norms + tuple(post_w_bf16)
    r3 = lambda a, b, t: a.reshape(b, t, a.shape[-1])
    o_gla, s_f, s_b = _gla(r3(q, B, T), r3(k, B, T), r3(v, B, T), r3(g, B, T), r3(gate, B, T),
                           gla_g, None, 4)
    o_att = _attention(qa, ka, va, B)
    y_prompt = _post_mixer(o_gla.reshape(B * T, D_V), o_att, xc, mod3, 0, B * T, post_w, tm_post)

    xl = x_sample.reshape(BL * TL, D)
    q, k, v, gate, g, qa, ka, va = _pre_mixer(
        xl, mod3, 1, TL, TL, pre_w, _rope_tables(TL), False, tm_pre)
    o_gla, _, _ = _gla(r3(q, BL, TL), r3(k, BL, TL), r3(v, BL, TL), r3(g, BL, TL), r3(gate, BL, TL),
                       gla_g, (state_gla_fwd[:, l].reshape(BL, D_QK, GLA_DV),
                               state_gla_bwd[:, l].reshape(BL, D_QK, GLA_DV)), BL)

    k_old = cache_k[:, l].reshape(BL * TP, D_KA).astype(BF16)
    vt_old = jnp.transpose(cache_v[:, l], (2, 3, 0, 1)).reshape(D_KA, BL * TP).astype(BF16)
    o_att = _attention(qa, ka, va, BL, (k_old, vt_old))
    y_sample = _post_mixer(o_gla.reshape(BL * TL, D_V), o_att, xl, mod3, 1, TL, post_w, tm_post)

    def cache_out(ct):
        return jnp.transpose(ct.reshape(B, 1, ATT_KV_HEADS, HEAD_DIM, T), (0, 1, 4, 2, 3))

    return (y_prompt.reshape(B, T, D), y_sample.reshape(BL, TL, D), cache_out(kc), cache_out(vc),
            s_f.reshape(B, 1, GLA_HEADS, GLA_DK, GLA_DV), s_b.reshape(B, 1, GLA_HEADS, GLA_DK, GLA_DV))
```

```python
import functools

import numpy as np
import jax
import jax.numpy as jnp
from jax import lax
from jax.experimental import pallas as pl
from jax.experimental.pallas import tpu as pltpu

F32 = jnp.float32
BF16 = jnp.bfloat16

D_MODEL = 1024
GRID_W = 64
GLA_HEADS = 4
GLA_DK = 64
GLA_DV = 128
GLA_LOWRANK = 16
GLA_GATE_NORM = 16.0
GLA_CHUNK = 64
ATT_HEADS = 8
ATT_KV_HEADS = 2
HEAD_DIM = 64
ROPE_AXIS_DIM = HEAD_DIM // 2
ROPE_THETA = 10000.0
D_FF = -(-8 * D_MODEL // (3 * 256)) * 256
N_MOD = 6
EPS = 1e-6

D_QK = GLA_HEADS * GLA_DK
D_V = GLA_HEADS * GLA_DV
D_QA = ATT_HEADS * HEAD_DIM
D_KA = ATT_KV_HEADS * HEAD_DIM
GROUP = ATT_HEADS // ATT_KV_HEADS
O_Q, O_K, O_V, O_OG = 0, D_QK, 2 * D_QK, 2 * D_QK + D_V
O_QA = O_OG + D_V
O_KA = O_QA + D_QA
O_VA = O_KA + D_KA
O_LR = O_VA + D_KA
D_PROJ = O_LR + 2 * GLA_LOWRANK

COND_ROWS = 8
VMEM_LIMIT = 56 * 1024 * 1024


def _dot(a, b):
    return jnp.dot(a, b, preferred_element_type=F32)


def _dot_nt(a, b):
    return lax.dot_general(a, b, (((1,), (1,)), ((), ())), preferred_element_type=F32)


def _dot_tn(a, b):
    return lax.dot_general(a, b, (((0,), (0,)), ((), ())), preferred_element_type=F32)


def _rms(x, g):
    ms = jnp.mean(x * x, axis=-1, keepdims=True)
    return x * lax.rsqrt(ms + EPS) * g


def _silu(x):
    return x * jax.nn.sigmoid(x)


ADALN_K_BLOCK = 256


def _adaln_kernel(c_ref, w_ref, b_ref, o_ref, acc):
    k = pl.program_id(0)

    @pl.when(k == 0)
    def _():
        acc[...] = jnp.broadcast_to(b_ref[...], acc.shape)

    acc[...] += _dot(_silu(c_ref[...]).astype(BF16), w_ref[...].astype(BF16))

    @pl.when(k == pl.num_programs(0) - 1)
    def _():
        o_ref[:, 0, :] = acc[...]


def _adaln(cond, ada_w, ada_b):
    d, n = ada_w.shape
    tk = ADALN_K_BLOCK
    return pl.pallas_call(
        _adaln_kernel,
        grid=(d // tk,),
        in_specs=[
            pl.BlockSpec((COND_ROWS, tk), lambda k: (0, k)),
            pl.BlockSpec((tk, n), lambda k: (k, 0)),
            pl.BlockSpec((1, n), lambda k: (0, 0)),
        ],
        out_specs=pl.BlockSpec((COND_ROWS, 1, n), lambda k: (0, 0, 0)),
        out_shape=jax.ShapeDtypeStruct((COND_ROWS, 1, n), F32),
        scratch_shapes=[pltpu.VMEM((COND_ROWS, n), F32)],
        compiler_params=pltpu.CompilerParams(
            dimension_semantics=("arbitrary",), vmem_limit_bytes=VMEM_LIMIT),
        name="adaln",
    )(cond, ada_w, ada_b.reshape(1, n))


def _swap_halves(x, lane_lo):
    n = x.shape[-1]
    up = pltpu.roll(x, n - ROPE_AXIS_DIM // 2, axis=1)
    dn = pltpu.roll(x, ROPE_AXIS_DIM // 2, axis=1)
    return jnp.where(lane_lo, up, dn)


PRE_SUB_ROWS = 256


def _pre_kernel(*refs, rope, emit_cache, n_cast):
    it = iter(refs)
    x_ref, mod_ref, n1_ref, win_ref, wgk_ref, bgk_ref = (next(it) for _ in range(6))
    qg_ref, kg_ref, bdq_ref, bdk_ref = (next(it) for _ in range(4))
    if rope:
        cos_ref, sin_ref = next(it), next(it)
    cast_src = [next(it) for _ in range(n_cast)]
    q_o, k_o, v_o, gate_o, g_o, qa_o, ka_o, va_o = (next(it) for _ in range(8))
    if emit_cache:
        kc_o, vc_o = next(it), next(it)
    for src in cast_src:
        next(it)[...] = src[...].astype(BF16)

    sub = PRE_SUB_ROWS
    n_sub = x_ref.shape[0] // sub
    mod = mod_ref[0]
    sh1 = mod[:, 0:D_MODEL]
    sc1 = mod[:, D_MODEL:2 * D_MODEL]

    def project(j):
        rows = slice(j * sub, (j + 1) * sub)
        h = _rms(x_ref[rows, :], n1_ref[...]) * (1.0 + sc1) + sh1
        return _dot(h.astype(BF16), win_ref[...])

    def finish(j, p_ref):
        rows = slice(j * sub, (j + 1) * sub)
        q_o[rows, :] = p_ref[:, O_Q:O_Q + D_QK].astype(BF16)
        k_o[rows, :] = p_ref[:, O_K:O_K + D_QK].astype(BF16)
        v_o[rows, :] = p_ref[:, O_V:O_V + D_V].astype(BF16)
        gate_o[rows, :] = _silu(p_ref[:, O_OG:O_OG + D_V]).astype(BF16)

        gk = _dot(p_ref[:, O_LR:O_LR + 2 * GLA_LOWRANK].astype(BF16), wgk_ref[...]) + bgk_ref[...]
        g_o[rows, :] = ((jnp.minimum(gk, 0.0) - jnp.log(1.0 + jnp.exp(-jnp.abs(gk))))
                        * (1.0 / GLA_GATE_NORM))

        qa = p_ref[:, O_QA:O_QA + D_QA]
        ka = p_ref[:, O_KA:O_KA + D_KA]
        va = p_ref[:, O_VA:O_VA + D_KA]
        qa = qa * lax.rsqrt(_dot((qa * qa).astype(BF16), bdq_ref[...]) + EPS) * qg_ref[...]
        ka = ka * lax.rsqrt(_dot((ka * ka).astype(BF16), bdk_ref[...]) + EPS) * kg_ref[...]
        if rope:
            lane_q = lax.broadcasted_iota(jnp.int32, qa.shape, 1)
            lane_k = lax.broadcasted_iota(jnp.int32, ka.shape, 1)
            lo_q = (lane_q % ROPE_AXIS_DIM) < ROPE_AXIS_DIM // 2
            lo_k = (lane_k % ROPE_AXIS_DIM) < ROPE_AXIS_DIM // 2
            cos_k = cos_ref[rows, :]
            sin_k = sin_ref[rows, :]
            cos_q = jnp.concatenate([cos_k] * GROUP, axis=1)
            sin_q = jnp.concatenate([sin_k] * GROUP, axis=1)
            qa = qa * cos_q + _swap_halves(qa, lo_q) * sin_q
            ka = ka * cos_k + _swap_halves(ka, lo_k) * sin_k
        vat = va.T
        if emit_cache:
            kat = ka.T
            t = kc_o.shape[2]
            for sq in range(sub // t):
                kc_o[j * (sub // t) + sq] = kat[:, sq * t:(sq + 1) * t]
                vc_o[j * (sub // t) + sq] = vat[:, sq * t:(sq + 1) * t]
        qa_o[:, rows] = (qa * (HEAD_DIM ** -0.5 * LOG2E)).T.astype(BF16)
        ka_o[rows, :] = ka.astype(BF16)
        va_o[:, rows] = vat.astype(BF16)

    p_prev = None
    for n in range(n_sub + 1):
        p_new = project(n) if n < n_sub else None
        if n > 0:
            finish(n - 1, p_prev)
        p_prev = p_new


def _pre_mixer(x, mod3, mod_row0, tokens_per_mod, seq_len, weights, rope_tabs, emit_cache, tm,
               cast=()):
    ntok = x.shape[0]
    n1, w_in_r, w_gk, b_gk, qg, kg, bdq, bdk = weights
    rope = rope_tabs is not None
    tiles_per_mod = tokens_per_mod // tm
    tiles_per_seq = seq_len // tm
    whole = pl.BlockSpec(memory_space=pltpu.VMEM)
    row = lambda w: pl.BlockSpec((tm, w), lambda i: (i, 0))
    in_specs = [
        row(D_MODEL),
        pl.BlockSpec((1, 1, N_MOD * D_MODEL), lambda i: (mod_row0 + i // tiles_per_mod, 0, 0)),
        whole, whole, whole, whole, whole, whole, whole, whole,
    ]
    args = [x, mod3, n1, w_in_r, w_gk, b_gk, qg, kg, bdq, bdk]
    if rope:
        tab = pl.BlockSpec((tm, D_KA), lambda i: (i % tiles_per_seq, 0))
        in_specs += [tab, tab]
        args += list(rope_tabs)
    col = lambda w: pl.BlockSpec((w, tm), lambda i: (0, i))
    out_specs = [row(D_QK), row(D_QK), row(D_V), row(D_V), row(2 * D_QK),
                 col(D_QA), row(D_KA), col(D_KA)]
    out_shape = [
        jax.ShapeDtypeStruct((ntok, D_QK), BF16),
        jax.ShapeDtypeStruct((ntok, D_QK), BF16),
        jax.ShapeDtypeStruct((ntok, D_V), BF16),
        jax.ShapeDtypeStruct((ntok, D_V), BF16),
        jax.ShapeDtypeStruct((ntok, 2 * D_QK), F32),
        jax.ShapeDtypeStruct((D_QA, ntok), BF16),
        jax.ShapeDtypeStruct((ntok, D_KA), BF16),
        jax.ShapeDtypeStruct((D_KA, ntok), BF16),
    ]
    if emit_cache:
        seqs = tm // seq_len
        out_specs += [pl.BlockSpec((seqs, D_KA, seq_len), lambda i: (i, 0, 0))] * 2
        out_shape += [jax.ShapeDtypeStruct((ntok // seq_len, D_KA, seq_len), F32)] * 2
    steps = ntok // tm
    for a in cast:
        slab = pl.BlockSpec((a.shape[0] // steps, a.shape[1]), lambda i: (i, 0))
        in_specs.append(slab)
        args.append(a)
        out_specs.append(slab)
        out_shape.append(jax.ShapeDtypeStruct(a.shape, BF16))
    return pl.pallas_call(
        functools.partial(_pre_kernel, rope=rope, emit_cache=emit_cache, n_cast=len(cast)),
        grid=(ntok // tm,),
        in_specs=in_specs,
        out_specs=out_specs,
        out_shape=out_shape,
        compiler_params=pltpu.CompilerParams(
            dimension_semantics=("arbitrary",), vmem_limit_bytes=VMEM_LIMIT),
        name="pre_mixer_rope" if rope else "pre_mixer",
    )(*args)


def _gla_chunks(units, q_ref, k_ref, v_ref, g_ref, half_lo):
    C = GLA_CHUNK
    n_pair = GLA_HEADS // 2
    rows = [pl.ds(pl.multiple_of(c * C, C), C) for (_, c, _, _, _, _) in units]

    bs = []
    for (bi, _, fwd, tri, _, _), r in zip(units, rows):
        gc = g_ref[bi, r, 0:D_QK] if fwd else g_ref[bi, r, D_QK:2 * D_QK]
        ltri = jnp.where(tri, 1.0, 0.0).astype(BF16)
        g_hi = gc.astype(BF16)
        g_lo = (gc - g_hi.astype(F32)).astype(BF16)
        bs.append(_dot(ltri, g_hi) + _dot(ltri, g_lo))

    qts, kts, kes, decays = [], [], [], []
    for (bi, _, fwd, _, _, _), r, b in zip(units, rows, bs):
        qc = q_ref[bi, r, :].astype(F32) * GLA_DK ** -0.5
        kc = k_ref[bi, r, :].astype(F32)
        btot = b[C - 1:C, :] if fwd else b[0:1, :]
        qts.append((qc * jnp.exp(b)).astype(BF16))
        kts.append((kc * jnp.exp(-b)).astype(BF16))
        kes.append((kc * jnp.exp(btot - b)).astype(BF16))
        decays.append(jnp.transpose(jnp.broadcast_to(jnp.exp(btot), (8, D_QK)))[:, 0:1])

    scores, incs, qms = [], [], []
    for (bi, _, _, tri, _, _), r, qt, kt, ke in zip(units, rows, qts, kts, kes):
        for pair in range(n_pair):
            lanes = slice(pair * 128, (pair + 1) * 128)
            incs.append(_dot_tn(ke[:, lanes], v_ref[bi, r, pair * 2 * GLA_DV:(pair + 1) * 2 * GLA_DV]))
            for hh in range(2):
                qm = jnp.where(half_lo if hh == 0 else ~half_lo, qt[:, lanes], jnp.zeros((C, 128), BF16))
                qms.append(qm)
                scores.append(jnp.where(tri, _dot_nt(qm, kt[:, lanes]), 0.0).astype(BF16))

    for ui, ((bi, _, _, _, s_ref, o_ref), r) in enumerate(zip(units, rows)):
        state = s_ref[bi]
        state_b = state.astype(BF16)
        for hd in range(GLA_HEADS):
            pair = hd // 2
            lhs = jnp.concatenate([qms[ui * GLA_HEADS + hd], scores[ui * GLA_HEADS + hd]], axis=1)
            rhs = jnp.concatenate([state_b[pair * 128:(pair + 1) * 128, :],
                                   v_ref[bi, r, hd * GLA_DV:(hd + 1) * GLA_DV]], axis=0)
            o_ref[bi, r, hd * GLA_DV:(hd + 1) * GLA_DV] = _dot(lhs, rhs)
        inc = jnp.concatenate(
            [incs[ui * n_pair + hd // 2][(hd % 2) * GLA_DK:(hd % 2 + 1) * GLA_DK,
                                         (hd % 2) * GLA_DV:(hd % 2 + 1) * GLA_DV]
             for hd in range(GLA_HEADS)], axis=0)
        s_ref[bi] = decays[ui] * state + inc


def _gla_kernel(*args, zero_init):
    q_ref, k_ref, v_ref, g_ref, gate_ref, gn_ref = args[:6]
    o_ref, sf_ref, sb_ref, of_scr, ob_scr = args[-5:]
    C = GLA_CHUNK
    nb, t, _ = q_ref.shape
    n = t // C
    row = lax.broadcasted_iota(jnp.int32, (C, C), 0)
    col = lax.broadcasted_iota(jnp.int32, (C, C), 1)
    half_lo = lax.broadcasted_iota(jnp.int32, (C, 128), 1) < GLA_DK
    refs = (q_ref, k_ref, v_ref, g_ref)
    if zero_init:
        sf_ref[...] = jnp.zeros(sf_ref.shape, F32)
        sb_ref[...] = jnp.zeros(sb_ref.shape, F32)
    else:
        sf_ref[...] = args[6][...]
        sb_ref[...] = args[7][...]

    steps = min(GLA_STEPS_PER_ITER, n)
    assert n % steps == 0

    def body(i, carry):
        units = []
        for u in range(steps):
            c = i * steps + u
            for bi in range(nb):
                units.append((bi, c, True, row >= col, sf_ref, of_scr))
                units.append((bi, n - 1 - c, False, row <= col, sb_ref, ob_scr))
        _gla_chunks(units, *refs, half_lo)
        return carry

    lax.fori_loop(0, n // steps, body, 0)

    rt = GLA_OUT_ROWS
    gn = gn_ref[...]

    def finish(i, carry):
        rows = pl.ds(pl.multiple_of(i * rt, rt), rt)
        for bi in range(nb):
            o = of_scr[bi, rows, :] + ob_scr[bi, rows, :]
            gate = gate_ref[bi, rows, :].astype(F32)
            for hd in range(GLA_HEADS):
                sl = slice(hd * GLA_DV, (hd + 1) * GLA_DV)
                o_ref[bi, rows, sl] = (_rms(o[:, sl], gn) * gate[:, sl]).astype(BF16)
        return carry

    lax.fori_loop(0, t // rt, finish, 0)


GLA_STEPS_PER_ITER = 4
GLA_OUT_ROWS = 256


def _gla(q, k, v, g, gate, gla_g, init_states, nb):
    b, t, _ = q.shape
    seq = lambda w: pl.BlockSpec((nb, t, w), lambda i: (i, 0, 0))
    st = pl.BlockSpec((nb, D_QK, GLA_DV), lambda i: (i, 0, 0))
    zero_init = init_states is None
    return pl.pallas_call(
        functools.partial(_gla_kernel, zero_init=zero_init),
        grid=(b // nb,),
        in_specs=[seq(D_QK), seq(D_QK), seq(D_V), seq(2 * D_QK), seq(D_V),
                  pl.BlockSpec((1, GLA_DV), lambda i: (0, 0))] + ([] if zero_init else [st, st]),
        out_specs=[seq(D_V), st, st],
        out_shape=[jax.ShapeDtypeStruct((b, t, D_V), BF16),
                   jax.ShapeDtypeStruct((b, D_QK, GLA_DV), F32),
                   jax.ShapeDtypeStruct((b, D_QK, GLA_DV), F32)],
        scratch_shapes=[pltpu.VMEM((nb, t, D_V), F32), pltpu.VMEM((nb, t, D_V), F32)],
        compiler_params=pltpu.CompilerParams(
            dimension_semantics=("arbitrary",), vmem_limit_bytes=VMEM_LIMIT),
        name="gla",
    )(q, k, v, g, gate, gla_g, *(() if zero_init else init_states))


ATT_KEY_BLOCK = 256
ATT_Q_TILE = 256
LOG2E = 1.4426950408889634


ATT_TILES_PER_STEP = 4


def _attn_kernel(*args, keys_per_tile, cached):
    qt_ref, k_ref, vt_ref = args[:3]
    kc_ref, vtc_ref = args[3:5] if cached else (None, None)
    o_ref, s0, s1, m0, m1, a0, a1 = args[-7:]
    tq = ATT_Q_TILE
    tiles = qt_ref.shape[1] // tq
    t_new = k_ref.shape[0] // tiles if keys_per_tile else k_ref.shape[0]
    tk = t_new + (kc_ref.shape[0] if cached else 0)
    kb = min(ATT_KEY_BLOCK, tk)
    nkb = tk // kb
    zero = jnp.zeros((HEAD_DIM, tq), BF16)
    ones = jnp.ones((16, kb), BF16)
    items = [(qi, kv) for qi in range(tiles) for kv in range(ATT_KV_HEADS)]
    bufs = (s0, s1)
    mbufs = (m0, m1)
    abufs = (a0, a1)
    separate = nkb > 1

    def q_ext(qi, kv):
        cols = []
        for g in range(GROUP):
            hd = kv * GROUP + g
            qt = qt_ref[hd * HEAD_DIM:(hd + 1) * HEAD_DIM, qi * tq:(qi + 1) * tq]
            cols.append(jnp.concatenate([qt, zero] if kv == 0 else [zero, qt], axis=0))
        return jnp.concatenate(cols, axis=1)

    def key0(qi):
        return qi * tk if keys_per_tile else 0

    def score_piece(n, q_all, i, m):
        r0 = key0(items[n][0]) + i * kb
        keys = k_ref[r0:r0 + kb, :] if i * kb < t_new else kc_ref[r0 - t_new:r0 - t_new + kb, :]
        s = _dot(keys, q_all)
        bufs[n % 2][i * kb:(i + 1) * kb, :] = s
        for r in range(kb // 8):
            tile = s[r * 8:(r + 1) * 8, :]
            m = tile if m is None else jnp.maximum(m, tile)
        return m

    def value_piece(n, m, i, acc):
        qi, kv = items[n]
        c0 = key0(qi) + i * kb
        p = jnp.exp2(bufs[n % 2][i * kb:(i + 1) * kb, :] - m).astype(BF16)
        hd_rows = slice(kv * HEAD_DIM, (kv + 1) * HEAD_DIM)
        vt = vt_ref[hd_rows, c0:c0 + kb] if i * kb < t_new else vtc_ref[hd_rows, c0 - t_new:c0 - t_new + kb]
        vt1 = jnp.concatenate([vt, ones], axis=0)
        d = _dot(vt1, p)
        return d if acc is None else acc + d

    def emit(n, acc):
        qi, kv = items[n]
        o = acc[:HEAD_DIM] / acc[HEAD_DIM:HEAD_DIM + 1]
        for g in range(0, GROUP, 2):
            two = jnp.concatenate([o[:, g * tq:(g + 1) * tq], o[:, (g + 1) * tq:(g + 2) * tq]], axis=0)
            hd = kv * GROUP + g
            o_ref[qi * tq:(qi + 1) * tq, hd * HEAD_DIM:(hd + 2) * HEAD_DIM] = two.T.astype(BF16)

    def pair(n):
        if separate and n >= 2:
            emit(n - 2, abufs[n % 2][...])
        if n > len(items):
            return
        q_all = q_ext(*items[n]) if n < len(items) else None
        m_prev = mbufs[(n - 1) % 2][...] if n > 0 else None
        m_new = acc = None
        for i in range(nkb):
            if n < len(items):
                m_new = score_piece(n, q_all, i, m_new)
            if n > 0:
                acc = value_piece(n - 1, m_prev, i, acc)
        if n > 0 and separate:
            abufs[(n - 1) % 2][...] = acc
        elif n > 0:
            emit(n - 1, acc)
        if n < len(items):
            mbufs[n % 2][...] = jnp.max(m_new, axis=0, keepdims=True)

    for n in range(len(items) + (2 if separate else 1)):
        if separate:
            pl.when(pl.program_id(0) > -(n + 1))(functools.partial(pair, n))
        else:
            pair(n)


def _attention(qt, k, vt, batch, cache=None):
    ntok = qt.shape[1]
    t_new = k.shape[0] // batch
    tq = ATT_Q_TILE
    tiles = ATT_TILES_PER_STEP
    nq = ntok // batch // tq
    keys_per_tile = nq < tiles
    extra_specs, extra, tk = [], (), t_new
    if keys_per_tile:
        assert cache is None
        seqs = tiles // nq
        k_spec = pl.BlockSpec((seqs * t_new, D_KA), lambda t: (t, 0))
        vt_spec = pl.BlockSpec((D_KA, seqs * t_new), lambda t: (0, t))
    else:
        spq = nq // tiles
        k_spec = pl.BlockSpec((t_new, D_KA), lambda t: (t // spq, 0))
        vt_spec = pl.BlockSpec((D_KA, t_new), lambda t: (0, t // spq))
        if cache is not None:
            t_old = cache[0].shape[0] // batch
            assert t_new % ATT_KEY_BLOCK == 0 and t_old % ATT_KEY_BLOCK == 0
            extra_specs = [pl.BlockSpec((t_old, D_KA), lambda t: (t // spq, 0)),
                           pl.BlockSpec((D_KA, t_old), lambda t: (0, t // spq))]
            extra, tk = cache, t_new + t_old
    return pl.pallas_call(
        functools.partial(_attn_kernel, keys_per_tile=keys_per_tile, cached=cache is not None),
        grid=(ntok // (tiles * tq),),
        in_specs=[pl.BlockSpec((D_QA, tiles * tq), lambda t: (0, t)), k_spec, vt_spec] + extra_specs,
        out_specs=pl.BlockSpec((tiles * tq, D_QA), lambda t: (t, 0)),
        out_shape=jax.ShapeDtypeStruct((ntok, D_QA), BF16),
        scratch_shapes=[pltpu.VMEM((tk, GROUP * tq), F32), pltpu.VMEM((tk, GROUP * tq), F32),
                        pltpu.VMEM((1, GROUP * tq), F32), pltpu.VMEM((1, GROUP * tq), F32),
                        pltpu.VMEM((HEAD_DIM + 16, GROUP * tq), F32),
                        pltpu.VMEM((HEAD_DIM + 16, GROUP * tq), F32)],
        compiler_params=pltpu.CompilerParams(
            dimension_semantics=("arbitrary",), vmem_limit_bytes=VMEM_LIMIT),
        name="attention",
    )(qt, k, vt, *extra)


def _post_kernel(og_ref, oa_ref, x_ref, mod_ref, n2_ref, fg_ref, wo_ref, w1_ref, w3_ref, w2_ref,
                 y_ref):
    D = D_MODEL
    mod = mod_ref[0]
    g1, sh2, sc2, g2 = (mod[:, i * D:(i + 1) * D] for i in range(2, 6))
    out = _dot(og_ref[...], wo_ref[0:D_V, :]) + _dot(oa_ref[...], wo_ref[D_V:D_V + D_QA, :])
    x1 = x_ref[...] + g1 * out
    hb = (_rms(x1, n2_ref[...]) * (1.0 + sc2) + sh2).astype(BF16)
    act = (_silu(_dot(hb, w1_ref[...])) * _dot(hb, w3_ref[...])).astype(BF16)
    x2 = x1 + g2 * _dot(act, w2_ref[...])
    y_ref[...] = _rms(x2, fg_ref[...])


def _post_mixer(o_gla, o_att, x, mod3, mod_row0, tokens_per_mod, weights, tm):
    ntok = x.shape[0]
    tiles_per_mod = tokens_per_mod // tm
    whole = pl.BlockSpec(memory_space=pltpu.VMEM)
    row = lambda w: pl.BlockSpec((tm, w), lambda i: (i, 0))
    return pl.pallas_call(
        _post_kernel,
        grid=(ntok // tm,),
        in_specs=[row(D_V), row(D_QA), row(D_MODEL),
                  pl.BlockSpec((1, 1, N_MOD * D_MODEL), lambda i: (mod_row0 + i // tiles_per_mod, 0, 0)),
                  whole, whole, whole, whole, whole, whole],
        out_specs=row(D_MODEL),
        out_shape=jax.ShapeDtypeStruct((ntok, D_MODEL), F32),
        compiler_params=pltpu.CompilerParams(
            dimension_semantics=("arbitrary",), vmem_limit_bytes=VMEM_LIMIT),
        name="post_mixer",
    )(o_gla, o_att, x, mod3, *weights)


def _rope_tables(t):
    rows = t // GRID_W
    row = np.repeat(np.arange(rows, dtype=np.float64), GRID_W)
    col = np.tile(np.arange(GRID_W, dtype=np.float64), rows)
    inv = ROPE_THETA ** (-np.arange(0, ROPE_AXIS_DIM, 2, dtype=np.float64) / ROPE_AXIS_DIM)
    ang_r = row[:, None] * inv
    ang_c = col[:, None] * inv
    cos = np.concatenate([np.cos(ang_r)] * 2 + [np.cos(ang_c)] * 2, axis=-1)
    sin = np.concatenate([-np.sin(ang_r), np.sin(ang_r), -np.sin(ang_c), np.sin(ang_c)], axis=-1)
    tile = lambda a: jnp.asarray(np.tile(a, (1, ATT_KV_HEADS)), F32)
    return tile(cos), tile(sin)


def _block_mean(width, group):
    return jnp.asarray(np.kron(np.eye(width // group), np.ones((group, group))) / group, BF16)


def kernel(x_prompt, x_sample, cache_k, cache_v, state_gla_fwd, state_gla_bwd, c, c_ctx,
           ada_w, ada_b, norm1_g, norm2_g, w_in, w_gk2, b_gk2, gla_norm_g, q_norm_g, k_norm_g,
           w_out, w_ffn1, w_ffn3, w_ffn2, final_g):
    B, T, D = x_prompt.shape
    BL, TL, _ = x_sample.shape
    TP = cache_k.shape[2]
    l = 0

    cond = jnp.zeros((COND_ROWS, D), F32).at[0].set(c_ctx).at[1:1 + BL].set(c)
    mod3 = _adaln(cond, ada_w[l], ada_b[l])

    offs = np.cumsum([0, D_QK, D_QK, D_V, D_V, 2 * GLA_LOWRANK, D_QA, D_KA, D_KA])
    seg = lambda i: w_in[l][:, offs[i]:offs[i + 1]]
    w_in_r = jnp.concatenate([seg(0), seg(1), seg(2), seg(3), seg(5), seg(6), seg(7), seg(4)],
                             axis=1).astype(BF16)
    zero = jnp.zeros((GLA_LOWRANK, D_QK), F32)
    w_gk = jnp.concatenate([jnp.concatenate([w_gk2[l, 0], zero], axis=1),
                            jnp.concatenate([zero, w_gk2[l, 1]], axis=1)], axis=0).astype(BF16)
    pre_w = (norm1_g[l].reshape(1, D), w_in_r, w_gk, b_gk2[l].reshape(1, 2 * D_QK),
             jnp.tile(q_norm_g[l], ATT_HEADS).reshape(1, D_QA),
             jnp.tile(k_norm_g[l], ATT_KV_HEADS).reshape(1, D_KA),
             _block_mean(D_QA, HEAD_DIM), _block_mean(D_KA, HEAD_DIM))
    post_norms = (norm2_g[l].reshape(1, D), final_g.reshape(1, D))
    post_w_f32 = (w_out[l], w_ffn1[l], w_ffn3[l], w_ffn2[l])
    gla_g = gla_norm_g[l].reshape(1, GLA_DV)
    tm_pre = 1024
    tm_post = 512

    xc = x_prompt.reshape(B * T, D)
    q, k, v, gate, g, qa, ka, va, kc, vc, *post_w_bf16 = _pre_mixer(
        xc, mod3, 0, B * T, T, pre_w, None, True, tm_pre, cast=post_w_f32)
    post_w = post_norms + tuple(post_w_bf16)
    r3 = lambda a, b, t: a.reshape(b, t, a.shape[-1])
    o_gla, s_f, s_b = _gla(r3(q, B, T), r3(k, B, T), r3(v, B, T), r3(g, B, T), r3(gate, B, T),
                           gla_g, None, 4)
    o_att = _attention(qa, ka, va, B)
    y_prompt = _post_mixer(o_gla.reshape(B * T, D_V), o_att, xc, mod3, 0, B * T, post_w, tm_post)

    xl = x_sample.reshape(BL * TL, D)
    q, k, v, gate, g, qa, ka, va = _pre_mixer(
        xl, mod3, 1, TL, TL, pre_w, _rope_tables(TL), False, tm_pre)
    o_gla, _, _ = _gla(r3(q, BL, TL), r3(k, BL, TL), r3(v, BL, TL), r3(g, BL, TL), r3(gate, BL, TL),
                       gla_g, (state_gla_fwd[:, l].reshape(BL, D_QK, GLA_DV),
                               state_gla_bwd[:, l].reshape(BL, D_QK, GLA_DV)), BL)

    k_old = cache_k[:, l].reshape(BL * TP, D_KA).astype(BF16)
    vt_old = jnp.transpose(cache_v[:, l], (2, 3, 0, 1)).reshape(D_KA, BL * TP).astype(BF16)
    o_att = _attention(qa, ka, va, BL, (k_old, vt_old))
    y_sample = _post_mixer(o_gla.reshape(BL * TL, D_V), o_att, xl, mod3, 1, TL, post_w, tm_post)

    def cache_out(ct):
        return jnp.transpose(ct.reshape(B, 1, ATT_KV_HEADS, HEAD_DIM, T), (0, 1, 4, 2, 3))

    return (y_prompt.reshape(B, T, D), y_sample.reshape(BL, TL, D), cache_out(kc), cache_out(vc),
            s_f.reshape(B, 1, GLA_HEADS, GLA_DK, GLA_DV), s_b.reshape(B, 1, GLA_HEADS, GLA_DK, GLA_DV))
```

```python
import functools

import numpy as np
import jax
import jax.numpy as jnp
from jax import lax
from jax.experimental import pallas as pl
from jax.experimental.pallas import tpu as pltpu

F32 = jnp.float32
BF16 = jnp.bfloat16

D_MODEL = 1024
GRID_W = 64
GLA_HEADS = 4
GLA_DK = 64
GLA_DV = 128
GLA_LOWRANK = 16
GLA_GATE_NORM = 16.0
GLA_CHUNK = 64
ATT_HEADS = 8
ATT_KV_HEADS = 2
HEAD_DIM = 64
ROPE_AXIS_DIM = HEAD_DIM // 2
ROPE_THETA = 10000.0
D_FF = -(-8 * D_MODEL // (3 * 256)) * 256
N_MOD = 6
EPS = 1e-6

D_QK = GLA_HEADS * GLA_DK
D_V = GLA_HEADS * GLA_DV
D_QA = ATT_HEADS * HEAD_DIM
D_KA = ATT_KV_HEADS * HEAD_DIM
GROUP = ATT_HEADS // ATT_KV_HEADS
O_Q, O_K, O_V, O_OG = 0, D_QK, 2 * D_QK, 2 * D_QK + D_V
O_QA = O_OG + D_V
O_KA = O_QA + D_QA
O_VA = O_KA + D_KA
O_LR = O_VA + D_KA
D_PROJ = O_LR + 2 * GLA_LOWRANK

COND_ROWS = 8
VMEM_LIMIT = 56 * 1024 * 1024


def _dot(a, b):
    return jnp.dot(a, b, preferred_element_type=F32)


def _dot_nt(a, b):
    return lax.dot_general(a, b, (((1,), (1,)), ((), ())), preferred_element_type=F32)


def _dot_tn(a, b):
    return lax.dot_general(a, b, (((0,), (0,)), ((), ())), preferred_element_type=F32)


def _rms(x, g):
    ms = jnp.mean(x * x, axis=-1, keepdims=True)
    return x * lax.rsqrt(ms + EPS) * g


def _silu(x):
    return x * jax.nn.sigmoid(x)


ADALN_K_BLOCK = 256


def _adaln_kernel(cctx_ref, c_ref, w_ref, b_ref, o_ref, acc):
    k = pl.program_id(0)

    @pl.when(k == 0)
    def _():
        acc[...] = jnp.broadcast_to(b_ref[...], acc.shape)

    pad = jnp.zeros((COND_ROWS - 1 - c_ref.shape[0], c_ref.shape[1]), F32)
    cond = jnp.concatenate([cctx_ref[...], c_ref[...], pad], axis=0)
    acc[...] += _dot(_silu(cond).astype(BF16), w_ref[...].astype(BF16))

    @pl.when(k == pl.num_programs(0) - 1)
    def _():
        o_ref[:, 0, :] = acc[...]


def _adaln(c_ctx, c, ada_w, ada_b):
    d, n = ada_w.shape
    tk = ADALN_K_BLOCK
    assert 1 + c.shape[0] <= COND_ROWS
    return pl.pallas_call(
        _adaln_kernel,
        grid=(d // tk,),
        in_specs=[
            pl.BlockSpec((1, tk), lambda k: (0, k)),
            pl.BlockSpec((c.shape[0], tk), lambda k: (0, k)),
            pl.BlockSpec((tk, n), lambda k: (k, 0)),
            pl.BlockSpec((1, n), lambda k: (0, 0)),
        ],
        out_specs=pl.BlockSpec((COND_ROWS, 1, n), lambda k: (0, 0, 0)),
        out_shape=jax.ShapeDtypeStruct((COND_ROWS, 1, n), F32),
        scratch_shapes=[pltpu.VMEM((COND_ROWS, n), F32)],
        compiler_params=pltpu.CompilerParams(
            dimension_semantics=("arbitrary",), vmem_limit_bytes=VMEM_LIMIT),
        name="adaln",
    )(c_ctx.reshape(1, d), c, ada_w, ada_b.reshape(1, n))


def _swap_halves(x, lane_lo):
    n = x.shape[-1]
    up = pltpu.roll(x, n - ROPE_AXIS_DIM // 2, axis=1)
    dn = pltpu.roll(x, ROPE_AXIS_DIM // 2, axis=1)
    return jnp.where(lane_lo, up, dn)


PRE_SUB_ROWS = 256


def _pre_kernel(*refs, rope, emit_cache, n_cast):
    it = iter(refs)
    x_ref, mod_ref, n1_ref, win_ref, wgk_ref, bgk_ref = (next(it) for _ in range(6))
    qg_ref, kg_ref, bdq_ref, bdk_ref = (next(it) for _ in range(4))
    if rope:
        cos_ref, sin_ref = next(it), next(it)
    cast_src = [next(it) for _ in range(n_cast)]
    q_o, k_o, v_o, gate_o, g_o, qa_o, ka_o, va_o = (next(it) for _ in range(8))
    if emit_cache:
        kc_o, vc_o = next(it), next(it)
    for src in cast_src:
        next(it)[...] = src[...].astype(BF16)

    sub = PRE_SUB_ROWS
    n_sub = x_ref.shape[0] // sub
    mod = mod_ref[0]
    sh1 = mod[:, 0:D_MODEL]
    sc1 = mod[:, D_MODEL:2 * D_MODEL]

    def project(j):
        rows = slice(j * sub, (j + 1) * sub)
        h = _rms(x_ref[rows, :], n1_ref[...]) * (1.0 + sc1) + sh1
        return _dot(h.astype(BF16), win_ref[...])

    def finish(j, p_ref):
        rows = slice(j * sub, (j + 1) * sub)
        q_o[rows, :] = p_ref[:, O_Q:O_Q + D_QK].astype(BF16)
        k_o[rows, :] = p_ref[:, O_K:O_K + D_QK].astype(BF16)
        v_o[rows, :] = p_ref[:, O_V:O_V + D_V].astype(BF16)
        gate_o[rows, :] = _silu(p_ref[:, O_OG:O_OG + D_V]).astype(BF16)

        gk = _dot(p_ref[:, O_LR:O_LR + 2 * GLA_LOWRANK].astype(BF16), wgk_ref[...]) + bgk_ref[...]
        g_o[rows, :] = ((jnp.minimum(gk, 0.0) - jnp.log(1.0 + jnp.exp(-jnp.abs(gk))))
                        * (1.0 / GLA_GATE_NORM))

        qa = p_ref[:, O_QA:O_QA + D_QA]
        ka = p_ref[:, O_KA:O_KA + D_KA]
        va = p_ref[:, O_VA:O_VA + D_KA]
        qa = qa * lax.rsqrt(_dot((qa * qa).astype(BF16), bdq_ref[...]) + EPS) * qg_ref[...]
        ka = ka * lax.rsqrt(_dot((ka * ka).astype(BF16), bdk_ref[...]) + EPS) * kg_ref[...]
        if rope:
            lane_q = lax.broadcasted_iota(jnp.int32, qa.shape, 1)
            lane_k = lax.broadcasted_iota(jnp.int32, ka.shape, 1)
            lo_q = (lane_q % ROPE_AXIS_DIM) < ROPE_AXIS_DIM // 2
            lo_k = (lane_k % ROPE_AXIS_DIM) < ROPE_AXIS_DIM // 2
            cos_k = cos_ref[rows, :]
            sin_k = sin_ref[rows, :]
            cos_q = jnp.concatenate([cos_k] * GROUP, axis=1)
            sin_q = jnp.concatenate([sin_k] * GROUP, axis=1)
            qa = qa * cos_q + _swap_halves(qa, lo_q) * sin_q
            ka = ka * cos_k + _swap_halves(ka, lo_k) * sin_k
        vat = va.T
        if emit_cache:
            kat = ka.T
            t = kc_o.shape[2]
            for sq in range(sub // t):
                kc_o[j * (sub // t) + sq] = kat[:, sq * t:(sq + 1) * t]
                vc_o[j * (sub // t) + sq] = vat[:, sq * t:(sq + 1) * t]
        qa_o[:, rows] = (qa * (HEAD_DIM ** -0.5 * LOG2E)).T.astype(BF16)
        ka_o[rows, :] = ka.astype(BF16)
        va_o[:, rows] = vat.astype(BF16)

    p_prev = None
    for n in range(n_sub + 1):
        p_new = project(n) if n < n_sub else None
        if n > 0:
            finish(n - 1, p_prev)
        p_prev = p_new


def _pre_mixer(x, mod3, mod_row0, tokens_per_mod, seq_len, weights, rope_tabs, emit_cache, tm,
               cast=()):
    ntok = x.shape[0]
    n1, w_in_r, w_gk, b_gk, qg, kg, bdq, bdk = weights
    rope = rope_tabs is not None
    tiles_per_mod = tokens_per_mod // tm
    tiles_per_seq = seq_len // tm
    whole = pl.BlockSpec(memory_space=pltpu.VMEM)
    row = lambda w: pl.BlockSpec((tm, w), lambda i: (i, 0))
    in_specs = [
        row(D_MODEL),
        pl.BlockSpec((1, 1, N_MOD * D_MODEL), lambda i: (mod_row0 + i // tiles_per_mod, 0, 0)),
        whole, whole, whole, whole, whole, whole, whole, whole,
    ]
    args = [x, mod3, n1, w_in_r, w_gk, b_gk, qg, kg, bdq, bdk]
    if rope:
        tab = pl.BlockSpec((tm, D_KA), lambda i: (i % tiles_per_seq, 0))
        in_specs += [tab, tab]
        args += list(rope_tabs)
    col = lambda w: pl.BlockSpec((w, tm), lambda i: (0, i))
    out_specs = [row(D_QK), row(D_QK), row(D_V), row(D_V), row(2 * D_QK),
                 col(D_QA), row(D_KA), col(D_KA)]
    out_shape = [
        jax.ShapeDtypeStruct((ntok, D_QK), BF16),
        jax.ShapeDtypeStruct((ntok, D_QK), BF16),
        jax.ShapeDtypeStruct((ntok, D_V), BF16),
        jax.ShapeDtypeStruct((ntok, D_V), BF16),
        jax.ShapeDtypeStruct((ntok, 2 * D_QK), F32),
        jax.ShapeDtypeStruct((D_QA, ntok), BF16),
        jax.ShapeDtypeStruct((ntok, D_KA), BF16),
        jax.ShapeDtypeStruct((D_KA, ntok), BF16),
    ]
    if emit_cache:
        seqs = tm // seq_len
        out_specs += [pl.BlockSpec((seqs, D_KA, seq_len), lambda i: (i, 0, 0))] * 2
        out_shape += [jax.ShapeDtypeStruct((ntok // seq_len, D_KA, seq_len), F32)] * 2
    steps = ntok // tm
    for a in cast:
        slab = pl.BlockSpec((a.shape[0] // steps, a.shape[1]), lambda i: (i, 0))
        in_specs.append(slab)
        args.append(a)
        out_specs.append(slab)
        out_shape.append(jax.ShapeDtypeStruct(a.shape, BF16))
    return pl.pallas_call(
        functools.partial(_pre_kernel, rope=rope, emit_cache=emit_cache, n_cast=len(cast)),
        grid=(ntok // tm,),
        in_specs=in_specs,
        out_specs=out_specs,
        out_shape=out_shape,
        compiler_params=pltpu.CompilerParams(
            dimension_semantics=("arbitrary",), vmem_limit_bytes=VMEM_LIMIT),
        name="pre_mixer_rope" if rope else "pre_mixer",
    )(*args)


def _gla_chunks(units, q_ref, k_ref, v_ref, g_ref, half_lo):
    C = GLA_CHUNK
    n_pair = GLA_HEADS // 2
    rows = [pl.ds(pl.multiple_of(c * C, C), C) for (_, c, _, _, _, _) in units]

    bs = []
    for (bi, _, fwd, tri, _, _), r in zip(units, rows):
        gc = g_ref[bi, r, 0:D_QK] if fwd else g_ref[bi, r, D_QK:2 * D_QK]
        ltri = jnp.where(tri, 1.0, 0.0).astype(BF16)
        g_hi = gc.astype(BF16)
        g_lo = (gc - g_hi.astype(F32)).astype(BF16)
        bs.append(_dot(ltri, g_hi) + _dot(ltri, g_lo))

    qts, kts, kes, decays = [], [], [], []
    for (bi, _, fwd, _, _, _), r, b in zip(units, rows, bs):
        qc = q_ref[bi, r, :].astype(F32) * GLA_DK ** -0.5
        kc = k_ref[bi, r, :].astype(F32)
        btot = b[C - 1:C, :] if fwd else b[0:1, :]
        qts.append((qc * jnp.exp(b)).astype(BF16))
        kts.append((kc * jnp.exp(-b)).astype(BF16))
        kes.append((kc * jnp.exp(btot - b)).astype(BF16))
        decays.append(jnp.transpose(jnp.broadcast_to(jnp.exp(btot), (8, D_QK)))[:, 0:1])

    scores, incs, qms = [], [], []
    for (bi, _, _, tri, _, _), r, qt, kt, ke in zip(units, rows, qts, kts, kes):
        for pair in range(n_pair):
            lanes = slice(pair * 128, (pair + 1) * 128)
            incs.append(_dot_tn(ke[:, lanes], v_ref[bi, r, pair * 2 * GLA_DV:(pair + 1) * 2 * GLA_DV]))
            for hh in range(2):
                qm = jnp.where(half_lo if hh == 0 else ~half_lo, qt[:, lanes], jnp.zeros((C, 128), BF16))
                qms.append(qm)
                scores.append(jnp.where(tri, _dot_nt(qm, kt[:, lanes]), 0.0).astype(BF16))

    for ui, ((bi, _, _, _, s_ref, o_ref), r) in enumerate(zip(units, rows)):
        state = s_ref[bi]
        state_b = state.astype(BF16)
        for hd in range(GLA_HEADS):
            pair = hd // 2
            lhs = jnp.concatenate([qms[ui * GLA_HEADS + hd], scores[ui * GLA_HEADS + hd]], axis=1)
            rhs = jnp.concatenate([state_b[pair * 128:(pair + 1) * 128, :],
                                   v_ref[bi, r, hd * GLA_DV:(hd + 1) * GLA_DV]], axis=0)
            o_ref[bi, r, hd * GLA_DV:(hd + 1) * GLA_DV] = _dot(lhs, rhs)
        inc = jnp.concatenate(
            [incs[ui * n_pair + hd // 2][(hd % 2) * GLA_DK:(hd % 2 + 1) * GLA_DK,
                                         (hd % 2) * GLA_DV:(hd % 2 + 1) * GLA_DV]
             for hd in range(GLA_HEADS)], axis=0)
        s_ref[bi] = decays[ui] * state + inc


def _gla_kernel(*args, zero_init):
    q_ref, k_ref, v_ref, g_ref, gate_ref, gn_ref = args[:6]
    o_ref, sf_ref, sb_ref, of_scr, ob_scr = args[-5:]
    C = GLA_CHUNK
    nb, t, _ = q_ref.shape
    n = t // C
    row = lax.broadcasted_iota(jnp.int32, (C, C), 0)
    col = lax.broadcasted_iota(jnp.int32, (C, C), 1)
    half_lo = lax.broadcasted_iota(jnp.int32, (C, 128), 1) < GLA_DK
    refs = (q_ref, k_ref, v_ref, g_ref)
    if zero_init:
        sf_ref[...] = jnp.zeros(sf_ref.shape, F32)
        sb_ref[...] = jnp.zeros(sb_ref.shape, F32)
    else:
        sf_ref[...] = args[6][...]
        sb_ref[...] = args[7][...]

    steps = min(GLA_STEPS_PER_ITER, n)
    assert n % steps == 0

    def body(i, carry):
        units = []
        for u in range(steps):
            c = i * steps + u
            for bi in range(nb):
                units.append((bi, c, True, row >= col, sf_ref, of_scr))
                units.append((bi, n - 1 - c, False, row <= col, sb_ref, ob_scr))
        _gla_chunks(units, *refs, half_lo)
        return carry

    lax.fori_loop(0, n // steps, body, 0)

    rt = GLA_OUT_ROWS
    gn = gn_ref[...]

    def finish(i, carry):
        rows = pl.ds(pl.multiple_of(i * rt, rt), rt)
        for bi in range(nb):
            o = of_scr[bi, rows, :] + ob_scr[bi, rows, :]
            gate = gate_ref[bi, rows, :].astype(F32)
            for hd in range(GLA_HEADS):
                sl = slice(hd * GLA_DV, (hd + 1) * GLA_DV)
                o_ref[bi, rows, sl] = (_rms(o[:, sl], gn) * gate[:, sl]).astype(BF16)
        return carry

    lax.fori_loop(0, t // rt, finish, 0)


GLA_STEPS_PER_ITER = 4
GLA_OUT_ROWS = 256


def _gla(q, k, v, g, gate, gla_g, init_states, nb):
    b, t, _ = q.shape
    seq = lambda w: pl.BlockSpec((nb, t, w), lambda i: (i, 0, 0))
    st = pl.BlockSpec((nb, D_QK, GLA_DV), lambda i: (i, 0, 0))
    zero_init = init_states is None
    return pl.pallas_call(
        functools.partial(_gla_kernel, zero_init=zero_init),
        grid=(b // nb,),
        in_specs=[seq(D_QK), seq(D_QK), seq(D_V), seq(2 * D_QK), seq(D_V),
                  pl.BlockSpec((1, GLA_DV), lambda i: (0, 0))] + ([] if zero_init else [st, st]),
        out_specs=[seq(D_V), st, st],
        out_shape=[jax.ShapeDtypeStruct((b, t, D_V), BF16),
                   jax.ShapeDtypeStruct((b, D_QK, GLA_DV), F32),
                   jax.ShapeDtypeStruct((b, D_QK, GLA_DV), F32)],
        scratch_shapes=[pltpu.VMEM((nb, t, D_V), F32), pltpu.VMEM((nb, t, D_V), F32)],
        compiler_params=pltpu.CompilerParams(
            dimension_semantics=("arbitrary",), vmem_limit_bytes=VMEM_LIMIT),
        name="gla",
    )(q, k, v, g, gate, gla_g, *(() if zero_init else init_states))


ATT_KEY_BLOCK = 256
ATT_Q_TILE = 256
LOG2E = 1.4426950408889634


ATT_TILES_PER_STEP = 8


def _attn_kernel(*args, keys_per_tile, cached):
    qt_ref, k_ref, vt_ref = args[:3]
    kc_ref, vtc_ref = args[3:5] if cached else (None, None)
    o_ref, s0, s1, m0, m1, a0, a1 = args[-7:]
    tq = ATT_Q_TILE
    tiles = qt_ref.shape[1] // tq
    t_new = k_ref.shape[0] // tiles if keys_per_tile else k_ref.shape[0]
    tk = t_new + (kc_ref.shape[0] if cached else 0)
    kb = min(ATT_KEY_BLOCK, tk)
    nkb = tk // kb
    zero = jnp.zeros((HEAD_DIM, tq), BF16)
    ones = jnp.ones((16, kb), BF16)
    items = [(qi, kv) for qi in range(tiles) for kv in range(ATT_KV_HEADS)]
    bufs = (s0, s1)
    mbufs = (m0, m1)
    abufs = (a0, a1)
    separate = nkb > 1

    def q_ext(qi, kv):
        cols = []
        for g in range(GROUP):
            hd = kv * GROUP + g
            qt = qt_ref[hd * HEAD_DIM:(hd + 1) * HEAD_DIM, qi * tq:(qi + 1) * tq]
            cols.append(jnp.concatenate([qt, zero] if kv == 0 else [zero, qt], axis=0))
        return jnp.concatenate(cols, axis=1)

    def key0(qi):
        return qi * tk if keys_per_tile else 0

    def score_piece(n, q_all, i, m):
        r0 = key0(items[n][0]) + i * kb
        keys = k_ref[r0:r0 + kb, :] if i * kb < t_new else kc_ref[r0 - t_new:r0 - t_new + kb, :]
        s = _dot(keys, q_all)
        bufs[n % 2][i * kb:(i + 1) * kb, :] = s
        for r in range(kb // 8):
            tile = s[r * 8:(r + 1) * 8, :]
            m = tile if m is None else jnp.maximum(m, tile)
        return m

    def value_piece(n, m, i, acc):
        qi, kv = items[n]
        c0 = key0(qi) + i * kb
        p = jnp.exp2(bufs[n % 2][i * kb:(i + 1) * kb, :] - m).astype(BF16)
        hd_rows = slice(kv * HEAD_DIM, (kv + 1) * HEAD_DIM)
        vt = vt_ref[hd_rows, c0:c0 + kb] if i * kb < t_new else vtc_ref[hd_rows, c0 - t_new:c0 - t_new + kb]
        vt1 = jnp.concatenate([vt, ones], axis=0)
        d = _dot(vt1, p)
        return d if acc is None else acc + d

    def emit(n, acc):
        qi, kv = items[n]
        o = acc[:HEAD_DIM] / acc[HEAD_DIM:HEAD_DIM + 1]
        for g in range(0, GROUP, 2):
            two = jnp.concatenate([o[:, g * tq:(g + 1) * tq], o[:, (g + 1) * tq:(g + 2) * tq]], axis=0)
            hd = kv * GROUP + g
            o_ref[qi * tq:(qi + 1) * tq, hd * HEAD_DIM:(hd + 2) * HEAD_DIM] = two.T.astype(BF16)

    def pair(n):
        if separate and n >= 2:
            emit(n - 2, abufs[n % 2][...])
        if n > len(items):
            return
        q_all = q_ext(*items[n]) if n < len(items) else None
        m_prev = mbufs[(n - 1) % 2][...] if n > 0 else None
        m_new = acc = None
        for i in range(nkb):
            if n < len(items):
                m_new = score_piece(n, q_all, i, m_new)
            if n > 0:
                acc = value_piece(n - 1, m_prev, i, acc)
        if n > 0 and separate:
            abufs[(n - 1) % 2][...] = acc
        elif n > 0:
            emit(n - 1, acc)
        if n < len(items):
            mbufs[n % 2][...] = jnp.max(m_new, axis=0, keepdims=True)

    for n in range(len(items) + (2 if separate else 1)):
        if separate:
            pl.when(pl.program_id(0) > -(n + 1))(functools.partial(pair, n))
        else:
            pair(n)


def _attention(qt, k, vt, batch, cache=None):
    ntok = qt.shape[1]
    t_new = k.shape[0] // batch
    tq = ATT_Q_TILE
    tiles = ATT_TILES_PER_STEP
    nq = ntok // batch // tq
    keys_per_tile = nq < tiles
    extra_specs, extra, tk = [], (), t_new
    if keys_per_tile:
        assert cache is None
        seqs = tiles // nq
        k_spec = pl.BlockSpec((seqs * t_new, D_KA), lambda t: (t, 0))
        vt_spec = pl.BlockSpec((D_KA, seqs * t_new), lambda t: (0, t))
    else:
        spq = nq // tiles
        k_spec = pl.BlockSpec((t_new, D_KA), lambda t: (t // spq, 0))
        vt_spec = pl.BlockSpec((D_KA, t_new), lambda t: (0, t // spq))
        if cache is not None:
            t_old = cache[0].shape[0] // batch
            assert t_new % ATT_KEY_BLOCK == 0 and t_old % ATT_KEY_BLOCK == 0
            extra_specs = [pl.BlockSpec((t_old, D_KA), lambda t: (t // spq, 0)),
                           pl.BlockSpec((D_KA, t_old), lambda t: (0, t // spq))]
            extra, tk = cache, t_new + t_old
    return pl.pallas_call(
        functools.partial(_attn_kernel, keys_per_tile=keys_per_tile, cached=cache is not None),
        grid=(ntok // (tiles * tq),),
        in_specs=[pl.BlockSpec((D_QA, tiles * tq), lambda t: (0, t)), k_spec, vt_spec] + extra_specs,
        out_specs=pl.BlockSpec((tiles * tq, D_QA), lambda t: (t, 0)),
        out_shape=jax.ShapeDtypeStruct((ntok, D_QA), BF16),
        scratch_shapes=[pltpu.VMEM((tk, GROUP * tq), F32), pltpu.VMEM((tk, GROUP * tq), F32),
                        pltpu.VMEM((1, GROUP * tq), F32), pltpu.VMEM((1, GROUP * tq), F32),
                        pltpu.VMEM((HEAD_DIM + 16, GROUP * tq), F32),
                        pltpu.VMEM((HEAD_DIM + 16, GROUP * tq), F32)],
        compiler_params=pltpu.CompilerParams(
            dimension_semantics=("arbitrary",), vmem_limit_bytes=VMEM_LIMIT),
        name="attention",
    )(qt, k, vt, *extra)


def _post_kernel(og_ref, oa_ref, x_ref, mod_ref, n2_ref, fg_ref, wo_ref, w1_ref, w3_ref, w2_ref,
                 y_ref):
    D = D_MODEL
    mod = mod_ref[0]
    g1, sh2, sc2, g2 = (mod[:, i * D:(i + 1) * D] for i in range(2, 6))
    out = _dot(og_ref[...], wo_ref[0:D_V, :]) + _dot(oa_ref[...], wo_ref[D_V:D_V + D_QA, :])
    x1 = x_ref[...] + g1 * out
    hb = (_rms(x1, n2_ref[...]) * (1.0 + sc2) + sh2).astype(BF16)
    act = (_silu(_dot(hb, w1_ref[...])) * _dot(hb, w3_ref[...])).astype(BF16)
    x2 = x1 + g2 * _dot(act, w2_ref[...])
    y_ref[...] = _rms(x2, fg_ref[...])


def _post_mixer(o_gla, o_att, x, mod3, mod_row0, tokens_per_mod, weights, tm):
    ntok = x.shape[0]
    tiles_per_mod = tokens_per_mod // tm
    whole = pl.BlockSpec(memory_space=pltpu.VMEM)
    row = lambda w: pl.BlockSpec((tm, w), lambda i: (i, 0))
    return pl.pallas_call(
        _post_kernel,
        grid=(ntok // tm,),
        in_specs=[row(D_V), row(D_QA), row(D_MODEL),
                  pl.BlockSpec((1, 1, N_MOD * D_MODEL), lambda i: (mod_row0 + i // tiles_per_mod, 0, 0)),
                  whole, whole, whole, whole, whole, whole],
        out_specs=row(D_MODEL),
        out_shape=jax.ShapeDtypeStruct((ntok, D_MODEL), F32),
        compiler_params=pltpu.CompilerParams(
            dimension_semantics=("arbitrary",), vmem_limit_bytes=VMEM_LIMIT),
        name="post_mixer",
    )(o_gla, o_att, x, mod3, *weights)


def _rope_tables(t):
    rows = t // GRID_W
    row = np.repeat(np.arange(rows, dtype=np.float64), GRID_W)
    col = np.tile(np.arange(GRID_W, dtype=np.float64), rows)
    inv = ROPE_THETA ** (-np.arange(0, ROPE_AXIS_DIM, 2, dtype=np.float64) / ROPE_AXIS_DIM)
    ang_r = row[:, None] * inv
    ang_c = col[:, None] * inv
    cos = np.concatenate([np.cos(ang_r)] * 2 + [np.cos(ang_c)] * 2, axis=-1)
    sin = np.concatenate([-np.sin(ang_r), np.sin(ang_r), -np.sin(ang_c), np.sin(ang_c)], axis=-1)
    tile = lambda a: jnp.asarray(np.tile(a, (1, ATT_KV_HEADS)), F32)
    return tile(cos), tile(sin)


def _block_mean(width, group):
    return jnp.asarray(np.kron(np.eye(width // group), np.ones((group, group))) / group, BF16)


def kernel(x_prompt, x_sample, cache_k, cache_v, state_gla_fwd, state_gla_bwd, c, c_ctx,
           ada_w, ada_b, norm1_g, norm2_g, w_in, w_gk2, b_gk2, gla_norm_g, q_norm_g, k_norm_g,
           w_out, w_ffn1, w_ffn3, w_ffn2, final_g):
    B, T, D = x_prompt.shape
    BL, TL, _ = x_sample.shape
    TP = cache_k.shape[2]
    l = 0

    mod3 = _adaln(c_ctx, c, ada_w[l], ada_b[l])

    offs = np.cumsum([0, D_QK, D_QK, D_V, D_V, 2 * GLA_LOWRANK, D_QA, D_KA, D_KA])
    seg = lambda i: w_in[l][:, offs[i]:offs[i + 1]]
    w_in_r = jnp.concatenate([seg(0), seg(1), seg(2), seg(3), seg(5), seg(6), seg(7), seg(4)],
                             axis=1).astype(BF16)
    zero = jnp.zeros((GLA_LOWRANK, D_QK), F32)
    w_gk = jnp.concatenate([jnp.concatenate([w_gk2[l, 0], zero], axis=1),
                            jnp.concatenate([zero, w_gk2[l, 1]], axis=1)], axis=0).astype(BF16)
    pre_w = (norm1_g[l].reshape(1, D), w_in_r, w_gk, b_gk2[l].reshape(1, 2 * D_QK),
             jnp.tile(q_norm_g[l], ATT_HEADS).reshape(1, D_QA),
             jnp.tile(k_norm_g[l], ATT_KV_HEADS).reshape(1, D_KA),
             _block_mean(D_QA, HEAD_DIM), _block_mean(D_KA, HEAD_DIM))
    post_norms = (norm2_g[l].reshape(1, D), final_g.reshape(1, D))
    post_w_f32 = (w_out[l], w_ffn1[l], w_ffn3[l], w_ffn2[l])
    gla_g = gla_norm_g[l].reshape(1, GLA_DV)
    tm_pre = 1024
    tm_post = 512

    xc = x_prompt.reshape(B * T, D)
    q, k, v, gate, g, qa, ka, va, kc, vc, *post_w_bf16 = _pre_mixer(
        xc, mod3, 0, B * T, T, pre_w, None, True, tm_pre, cast=post_w_f32)
    post_w = post_norms + tuple(post_w_bf16)
    r3 = lambda a, b, t: a.reshape(b, t, a.shape[-1])
    o_gla, s_f, s_b = _gla(r3(q, B, T), r3(k, B, T), r3(v, B, T), r3(g, B, T), r3(gate, B, T),
                           gla_g, None, 4)
    o_att = _attention(qa, ka, va, B)
    y_prompt = _post_mixer(o_gla.reshape(B * T, D_V), o_att, xc, mod3, 0, B * T, post_w, tm_post)

    xl = x_sample.reshape(BL * TL, D)
    q, k, v, gate, g, qa, ka, va = _pre_mixer(
        xl, mod3, 1, TL, TL, pre_w, _rope_tables(TL), False, tm_pre)
    o_gla, _, _ = _gla(r3(q, BL, TL), r3(k, BL, TL), r3(v, BL, TL), r3(g, BL, TL), r3(gate, BL, TL),
                       gla_g, (state_gla_fwd[:, l].reshape(BL, D_QK, GLA_DV),
                               state_gla_bwd[:, l].reshape(BL, D_QK, GLA_DV)), BL)

    k_old = cache_k[:, l].reshape(BL * TP, D_KA).astype(BF16)
    vt_old = jnp.transpose(cache_v[:, l], (2, 3, 0, 1)).reshape(D_KA, BL * TP).astype(BF16)
    o_att = _attention(qa, ka, va, BL, (k_old, vt_old))
    y_sample = _post_mixer(o_gla.reshape(BL * TL, D_V), o_att, xl, mod3, 1, TL, post_w, tm_post)

    def cache_out(ct):
        return jnp.transpose(ct.reshape(B, 1, ATT_KV_HEADS, HEAD_DIM, T), (0, 1, 4, 2, 3))

    return (y_prompt.reshape(B, T, D), y_sample.reshape(BL, TL, D), cache_out(kc), cache_out(vc),
            s_f.reshape(B, 1, GLA_HEADS, GLA_DK, GLA_DV), s_b.reshape(B, 1, GLA_HEADS, GLA_DK, GLA_DV))
```

```python
import functools

import numpy as np
import jax
import jax.numpy as jnp
from jax import lax
from jax.experimental import pallas as pl
from jax.experimental.pallas import tpu as pltpu

F32 = jnp.float32
BF16 = jnp.bfloat16

D_MODEL = 1024
GRID_W = 64
GLA_HEADS = 4
GLA_DK = 64
GLA_DV = 128
GLA_LOWRANK = 16
GLA_GATE_NORM = 16.0
GLA_CHUNK = 64
ATT_HEADS = 8
ATT_KV_HEADS = 2
HEAD_DIM = 64
ROPE_AXIS_DIM = HEAD_DIM // 2
ROPE_THETA = 10000.0
D_FF = -(-8 * D_MODEL // (3 * 256)) * 256
N_MOD = 6
EPS = 1e-6

D_QK = GLA_HEADS * GLA_DK
D_V = GLA_HEADS * GLA_DV
D_QA = ATT_HEADS * HEAD_DIM
D_KA = ATT_KV_HEADS * HEAD_DIM
GROUP = ATT_HEADS // ATT_KV_HEADS
O_Q, O_K, O_V, O_OG = 0, D_QK, 2 * D_QK, 2 * D_QK + D_V
O_QA = O_OG + D_V
O_KA = O_QA + D_QA
O_VA = O_KA + D_KA
O_LR = O_VA + D_KA
D_PROJ = O_LR + 2 * GLA_LOWRANK

COND_ROWS = 8
VMEM_LIMIT = 56 * 1024 * 1024


def _dot(a, b):
    return jnp.dot(a, b, preferred_element_type=F32)


def _dot_nt(a, b):
    return lax.dot_general(a, b, (((1,), (1,)), ((), ())), preferred_element_type=F32)


def _dot_tn(a, b):
    return lax.dot_general(a, b, (((0,), (0,)), ((), ())), preferred_element_type=F32)


def _rms(x, g):
    ms = jnp.mean(x * x, axis=-1, keepdims=True)
    return x * lax.rsqrt(ms + EPS) * g


def _silu(x):
    return x * jax.nn.sigmoid(x)


ADALN_K_BLOCK = 256


def _adaln_kernel(cctx_ref, c_ref, w_ref, b_ref, o_ref, acc):
    k = pl.program_id(0)

    @pl.when(k == 0)
    def _():
        acc[...] = jnp.broadcast_to(b_ref[...], acc.shape)

    pad = jnp.zeros((COND_ROWS - 1 - c_ref.shape[0], c_ref.shape[1]), F32)
    cond = jnp.concatenate([cctx_ref[...], c_ref[...], pad], axis=0)
    acc[...] += _dot(_silu(cond).astype(BF16), w_ref[...].astype(BF16))

    @pl.when(k == pl.num_programs(0) - 1)
    def _():
        o_ref[:, 0, :] = acc[...]


def _adaln(c_ctx, c, ada_w, ada_b):
    d, n = ada_w.shape
    tk = ADALN_K_BLOCK
    assert 1 + c.shape[0] <= COND_ROWS
    return pl.pallas_call(
        _adaln_kernel,
        grid=(d // tk,),
        in_specs=[
            pl.BlockSpec((1, tk), lambda k: (0, k)),
            pl.BlockSpec((c.shape[0], tk), lambda k: (0, k)),
            pl.BlockSpec((tk, n), lambda k: (k, 0)),
            pl.BlockSpec((1, n), lambda k: (0, 0)),
        ],
        out_specs=pl.BlockSpec((COND_ROWS, 1, n), lambda k: (0, 0, 0)),
        out_shape=jax.ShapeDtypeStruct((COND_ROWS, 1, n), F32),
        scratch_shapes=[pltpu.VMEM((COND_ROWS, n), F32)],
        compiler_params=pltpu.CompilerParams(
            dimension_semantics=("arbitrary",), vmem_limit_bytes=VMEM_LIMIT),
        name="adaln",
    )(c_ctx.reshape(1, d), c, ada_w, ada_b.reshape(1, n))


def _swap_halves(x, lane_lo):
    n = x.shape[-1]
    up = pltpu.roll(x, n - ROPE_AXIS_DIM // 2, axis=1)
    dn = pltpu.roll(x, ROPE_AXIS_DIM // 2, axis=1)
    return jnp.where(lane_lo, up, dn)


PRE_SUB_ROWS = 256


def _pre_kernel(*refs, rope, emit_cache, n_cast):
    it = iter(refs)
    x_ref, mod_ref, n1_ref, win_ref, wgk_ref, bgk_ref = (next(it) for _ in range(6))
    qg_ref, kg_ref, bdq_ref, bdk_ref = (next(it) for _ in range(4))
    if rope:
        cos_ref, sin_ref = next(it), next(it)
    cast_src = [next(it) for _ in range(n_cast)]
    q_o, k_o, v_o, gate_o, g_o, qa_o, ka_o, va_o = (next(it) for _ in range(8))
    if emit_cache:
        kc_o, vc_o = next(it), next(it)
    for src in cast_src:
        next(it)[...] = src[...].astype(BF16)

    sub = PRE_SUB_ROWS
    n_sub = x_ref.shape[0] // sub
    mod = mod_ref[0]
    sh1 = mod[:, 0:D_MODEL]
    sc1 = mod[:, D_MODEL:2 * D_MODEL]

    def project(j):
        rows = slice(j * sub, (j + 1) * sub)
        h = _rms(x_ref[rows, :], n1_ref[...]) * (1.0 + sc1) + sh1
        return _dot(h.astype(BF16), win_ref[...])

    def finish(j, p_ref):
        rows = slice(j * sub, (j + 1) * sub)
        q_o[rows, :] = p_ref[:, O_Q:O_Q + D_QK].astype(BF16)
        k_o[rows, :] = p_ref[:, O_K:O_K + D_QK].astype(BF16)
        v_o[rows, :] = p_ref[:, O_V:O_V + D_V].astype(BF16)
        gate_o[rows, :] = _silu(p_ref[:, O_OG:O_OG + D_V]).astype(BF16)

        gk = _dot(p_ref[:, O_LR:O_LR + 2 * GLA_LOWRANK].astype(BF16), wgk_ref[...]) + bgk_ref[...]
        g_o[rows, :] = ((jnp.minimum(gk, 0.0) - jnp.log(1.0 + jnp.exp(-jnp.abs(gk))))
                        * (1.0 / GLA_GATE_NORM))

        qa = p_ref[:, O_QA:O_QA + D_QA]
        ka = p_ref[:, O_KA:O_KA + D_KA]
        va = p_ref[:, O_VA:O_VA + D_KA]
        qa = qa * lax.rsqrt(_dot((qa * qa).astype(BF16), bdq_ref[...]) + EPS) * qg_ref[...]
        ka = ka * lax.rsqrt(_dot((ka * ka).astype(BF16), bdk_ref[...]) + EPS) * kg_ref[...]
        if rope:
            lane_q = lax.broadcasted_iota(jnp.int32, qa.shape, 1)
            lane_k = lax.broadcasted_iota(jnp.int32, ka.shape, 1)
            lo_q = (lane_q % ROPE_AXIS_DIM) < ROPE_AXIS_DIM // 2
            lo_k = (lane_k % ROPE_AXIS_DIM) < ROPE_AXIS_DIM // 2
            cos_k = cos_ref[rows, :]
            sin_k = sin_ref[rows, :]
            cos_q = jnp.concatenate([cos_k] * GROUP, axis=1)
            sin_q = jnp.concatenate([sin_k] * GROUP, axis=1)
            qa = qa * cos_q + _swap_halves(qa, lo_q) * sin_q
            ka = ka * cos_k + _swap_halves(ka, lo_k) * sin_k
        vat = va.T
        if emit_cache:
            kat = ka.T
            t = kc_o.shape[2]
            for sq in range(sub // t):
                kc_o[j * (sub // t) + sq] = kat[:, sq * t:(sq + 1) * t]
                vc_o[j * (sub // t) + sq] = vat[:, sq * t:(sq + 1) * t]
        qa_o[:, rows] = (qa * (HEAD_DIM ** -0.5 * LOG2E)).T.astype(BF16)
        ka_o[rows, :] = ka.astype(BF16)
        va_o[:, rows] = vat.astype(BF16)

    p_prev = None
    for n in range(n_sub + 1):
        p_new = project(n) if n < n_sub else None
        if n > 0:
            finish(n - 1, p_prev)
        p_prev = p_new


def _pre_mixer(x, mod3, mod_row0, tokens_per_mod, seq_len, weights, rope_tabs, emit_cache, tm,
               cast=()):
    ntok = x.shape[0]
    n1, w_in_r, w_gk, b_gk, qg, kg, bdq, bdk = weights
    rope = rope_tabs is not None
    tiles_per_mod = tokens_per_mod // tm
    tiles_per_seq = seq_len // tm
    whole = pl.BlockSpec(memory_space=pltpu.VMEM)
    row = lambda w: pl.BlockSpec((tm, w), lambda i: (i, 0))
    in_specs = [
        row(D_MODEL),
        pl.BlockSpec((1, 1, N_MOD * D_MODEL), lambda i: (mod_row0 + i // tiles_per_mod, 0, 0)),
        whole, whole, whole, whole, whole, whole, whole, whole,
    ]
    args = [x, mod3, n1, w_in_r, w_gk, b_gk, qg, kg, bdq, bdk]
    if rope:
        tab = pl.BlockSpec((tm, D_KA), lambda i: (i % tiles_per_seq, 0))
        in_specs += [tab, tab]
        args += list(rope_tabs)
    col = lambda w: pl.BlockSpec((w, tm), lambda i: (0, i))
    out_specs = [row(D_QK), row(D_QK), row(D_V), row(D_V), row(2 * D_QK),
                 col(D_QA), row(D_KA), col(D_KA)]
    out_shape = [
        jax.ShapeDtypeStruct((ntok, D_QK), BF16),
        jax.ShapeDtypeStruct((ntok, D_QK), BF16),
        jax.ShapeDtypeStruct((ntok, D_V), BF16),
        jax.ShapeDtypeStruct((ntok, D_V), BF16),
        jax.ShapeDtypeStruct((ntok, 2 * D_QK), F32),
        jax.ShapeDtypeStruct((D_QA, ntok), BF16),
        jax.ShapeDtypeStruct((ntok, D_KA), BF16),
        jax.ShapeDtypeStruct((D_KA, ntok), BF16),
    ]
    if emit_cache:
        seqs = tm // seq_len
        out_specs += [pl.BlockSpec((seqs, D_KA, seq_len), lambda i: (i, 0, 0))] * 2
        out_shape += [jax.ShapeDtypeStruct((ntok // seq_len, D_KA, seq_len), F32)] * 2
    steps = ntok // tm
    for a in cast:
        slab = pl.BlockSpec((a.shape[0] // steps, a.shape[1]), lambda i: (i, 0))
        in_specs.append(slab)
        args.append(a)
        out_specs.append(slab)
        out_shape.append(jax.ShapeDtypeStruct(a.shape, BF16))
    return pl.pallas_call(
        functools.partial(_pre_kernel, rope=rope, emit_cache=emit_cache, n_cast=len(cast)),
        grid=(ntok // tm,),
        in_specs=in_specs,
        out_specs=out_specs,
        out_shape=out_shape,
        compiler_params=pltpu.CompilerParams(
            dimension_semantics=("arbitrary",), vmem_limit_bytes=VMEM_LIMIT),
        name="pre_mixer_rope" if rope else "pre_mixer",
    )(*args)


def _gla_chunks(units, q_ref, k_ref, v_ref, g_ref, half_lo):
    C = GLA_CHUNK
    n_pair = GLA_HEADS // 2
    rows = [pl.ds(pl.multiple_of(c * C, C), C) for (_, c, _, _, _, _) in units]

    bs = []
    for (bi, _, fwd, tri, _, _), r in zip(units, rows):
        gc = g_ref[bi, r, 0:D_QK] if fwd else g_ref[bi, r, D_QK:2 * D_QK]
        ltri = jnp.where(tri, 1.0, 0.0).astype(BF16)
        g_hi = gc.astype(BF16)
        g_lo = (gc - g_hi.astype(F32)).astype(BF16)
        bs.append(_dot(ltri, g_hi) + _dot(ltri, g_lo))

    qts, kts, kes, decays = [], [], [], []
    for (bi, _, fwd, _, _, _), r, b in zip(units, rows, bs):
        qc = q_ref[bi, r, :].astype(F32) * GLA_DK ** -0.5
        kc = k_ref[bi, r, :].astype(F32)
        btot = b[C - 1:C, :] if fwd else b[0:1, :]
        qts.append((qc * jnp.exp(b)).astype(BF16))
        kts.append((kc * jnp.exp(-b)).astype(BF16))
        kes.append((kc * jnp.exp(btot - b)).astype(BF16))
        decays.append(jnp.transpose(jnp.broadcast_to(jnp.exp(btot), (8, D_QK)))[:, 0:1])

    scores, incs, qms = [], [], []
    for (bi, _, _, tri, _, _), r, qt, kt, ke in zip(units, rows, qts, kts, kes):
        for pair in range(n_pair):
            lanes = slice(pair * 128, (pair + 1) * 128)
            incs.append(_dot_tn(ke[:, lanes], v_ref[bi, r, pair * 2 * GLA_DV:(pair + 1) * 2 * GLA_DV]))
            for hh in range(2):
                qm = jnp.where(half_lo if hh == 0 else ~half_lo, qt[:, lanes], jnp.zeros((C, 128), BF16))
                qms.append(qm)
                scores.append(jnp.where(tri, _dot_nt(qm, kt[:, lanes]), 0.0).astype(BF16))

    for ui, ((bi, _, _, _, s_ref, o_ref), r) in enumerate(zip(units, rows)):
        state = s_ref[bi]
        state_b = state.astype(BF16)
        for hd in range(GLA_HEADS):
            pair = hd // 2
            lhs = jnp.concatenate([qms[ui * GLA_HEADS + hd], scores[ui * GLA_HEADS + hd]], axis=1)
            rhs = jnp.concatenate([state_b[pair * 128:(pair + 1) * 128, :],
                                   v_ref[bi, r, hd * GLA_DV:(hd + 1) * GLA_DV]], axis=0)
            o_ref[bi, r, hd * GLA_DV:(hd + 1) * GLA_DV] = _dot(lhs, rhs)
        inc = jnp.concatenate(
            [incs[ui * n_pair + hd // 2][(hd % 2) * GLA_DK:(hd % 2 + 1) * GLA_DK,
                                         (hd % 2) * GLA_DV:(hd % 2 + 1) * GLA_DV]
             for hd in range(GLA_HEADS)], axis=0)
        s_ref[bi] = decays[ui] * state + inc


def _gla_kernel(*args, zero_init):
    q_ref, k_ref, v_ref, g_ref, gate_ref, gn_ref = args[:6]
    o_ref, sf_ref, sb_ref, of_scr, ob_scr = args[-5:]
    C = GLA_CHUNK
    nb, t, _ = q_ref.shape
    n = t // C
    row = lax.broadcasted_iota(jnp.int32, (C, C), 0)
    col = lax.broadcasted_iota(jnp.int32, (C, C), 1)
    half_lo = lax.broadcasted_iota(jnp.int32, (C, 128), 1) < GLA_DK
    refs = (q_ref, k_ref, v_ref, g_ref)
    if zero_init:
        sf_ref[...] = jnp.zeros(sf_ref.shape, F32)
        sb_ref[...] = jnp.zeros(sb_ref.shape, F32)
    else:
        sf_ref[...] = args[6][...]
        sb_ref[...] = args[7][...]

    steps = min(GLA_STEPS_PER_ITER, n)
    assert n % steps == 0

    def body(i, carry):
        units = []
        for u in range(steps):
            c = i * steps + u
            for bi in range(nb):
                units.append((bi, c, True, row >= col, sf_ref, of_scr))
                units.append((bi, n - 1 - c, False, row <= col, sb_ref, ob_scr))
        _gla_chunks(units, *refs, half_lo)
        return carry

    lax.fori_loop(0, n // steps, body, 0)

    rt = GLA_OUT_ROWS
    gn = gn_ref[...]

    def finish(i, carry):
        rows = pl.ds(pl.multiple_of(i * rt, rt), rt)
        for bi in range(nb):
            o = of_scr[bi, rows, :] + ob_scr[bi, rows, :]
            gate = gate_ref[bi, rows, :].astype(F32)
            for hd in range(GLA_HEADS):
                sl = slice(hd * GLA_DV, (hd + 1) * GLA_DV)
                o_ref[bi, rows, sl] = (_rms(o[:, sl], gn) * gate[:, sl]).astype(BF16)
        return carry

    lax.fori_loop(0, t // rt, finish, 0)


GLA_STEPS_PER_ITER = 4
GLA_OUT_ROWS = 256


def _gla(q, k, v, g, gate, gla_g, init_states, nb):
    b, t, _ = q.shape
    seq = lambda w: pl.BlockSpec((nb, t, w), lambda i: (i, 0, 0))
    st = pl.BlockSpec((nb, D_QK, GLA_DV), lambda i: (i, 0, 0))
    zero_init = init_states is None
    return pl.pallas_call(
        functools.partial(_gla_kernel, zero_init=zero_init),
        grid=(b // nb,),
        in_specs=[seq(D_QK), seq(D_QK), seq(D_V), seq(2 * D_QK), seq(D_V),
                  pl.BlockSpec((1, GLA_DV), lambda i: (0, 0))] + ([] if zero_init else [st, st]),
        out_specs=[seq(D_V), st, st],
        out_shape=[jax.ShapeDtypeStruct((b, t, D_V), BF16),
                   jax.ShapeDtypeStruct((b, D_QK, GLA_DV), F32),
                   jax.ShapeDtypeStruct((b, D_QK, GLA_DV), F32)],
        scratch_shapes=[pltpu.VMEM((nb, t, D_V), F32), pltpu.VMEM((nb, t, D_V), F32)],
        compiler_params=pltpu.CompilerParams(
            dimension_semantics=("arbitrary",), vmem_limit_bytes=VMEM_LIMIT),
        name="gla",
    )(q, k, v, g, gate, gla_g, *(() if zero_init else init_states))


ATT_KEY_BLOCK = 256
ATT_Q_TILE = 256
LOG2E = 1.4426950408889634


ATT_TILES_PER_STEP = 4


def _attn_kernel(*args, keys_per_tile, cached):
    qt_ref, k_ref, vt_ref = args[:3]
    kc_ref, vtc_ref = args[3:5] if cached else (None, None)
    o_ref, s0, s1, m0, m1, a0, a1 = args[-7:]
    tq = ATT_Q_TILE
    tiles = qt_ref.shape[1] // tq
    t_new = k_ref.shape[0] // tiles if keys_per_tile else k_ref.shape[0]
    tk = t_new + (kc_ref.shape[0] if cached else 0)
    kb = min(ATT_KEY_BLOCK, tk)
    nkb = tk // kb
    zero = jnp.zeros((HEAD_DIM, tq), BF16)
    ones = jnp.ones((16, kb), BF16)
    items = [(qi, kv) for qi in range(tiles) for kv in range(ATT_KV_HEADS)]
    bufs = (s0, s1)
    mbufs = (m0, m1)
    abufs = (a0, a1)
    separate = nkb > 1

    def q_ext(qi, kv):
        cols = []
        for g in range(GROUP):
            hd = kv * GROUP + g
            qt = qt_ref[hd * HEAD_DIM:(hd + 1) * HEAD_DIM, qi * tq:(qi + 1) * tq]
            cols.append(jnp.concatenate([qt, zero] if kv == 0 else [zero, qt], axis=0))
        return jnp.concatenate(cols, axis=1)

    def key0(qi):
        return qi * tk if keys_per_tile else 0

    def score_piece(n, q_all, i, m):
        r0 = key0(items[n][0]) + i * kb
        keys = k_ref[r0:r0 + kb, :] if i * kb < t_new else kc_ref[r0 - t_new:r0 - t_new + kb, :]
        s = _dot(keys, q_all)
        bufs[n % 2][i * kb:(i + 1) * kb, :] = s
        for r in range(kb // 8):
            tile = s[r * 8:(r + 1) * 8, :]
            m = tile if m is None else jnp.maximum(m, tile)
        return m

    def value_piece(n, m, i, acc):
        qi, kv = items[n]
        c0 = key0(qi) + i * kb
        p = jnp.exp2(bufs[n % 2][i * kb:(i + 1) * kb, :] - m).astype(BF16)
        hd_rows = slice(kv * HEAD_DIM, (kv + 1) * HEAD_DIM)
        vt = vt_ref[hd_rows, c0:c0 + kb] if i * kb < t_new else vtc_ref[hd_rows, c0 - t_new:c0 - t_new + kb]
        vt1 = jnp.concatenate([vt, ones], axis=0)
        d = _dot(vt1, p)
        return d if acc is None else acc + d

    def emit(n, acc):
        qi, kv = items[n]
        o = acc[:HEAD_DIM] / acc[HEAD_DIM:HEAD_DIM + 1]
        for g in range(0, GROUP, 2):
            two = jnp.concatenate([o[:, g * tq:(g + 1) * tq], o[:, (g + 1) * tq:(g + 2) * tq]], axis=0)
            hd = kv * GROUP + g
            o_ref[qi * tq:(qi + 1) * tq, hd * HEAD_DIM:(hd + 2) * HEAD_DIM] = two.T.astype(BF16)

    def pair(n):
        if separate and n >= 2:
            emit(n - 2, abufs[n % 2][...])
        if n > len(items):
            return
        q_all = q_ext(*items[n]) if n < len(items) else None
        m_prev = mbufs[(n - 1) % 2][...] if n > 0 else None
        m_new = acc = None
        for i in range(nkb):
            if n < len(items):
                m_new = score_piece(n, q_all, i, m_new)
            if n > 0:
                acc = value_piece(n - 1, m_prev, i, acc)
        if n > 0 and separate:
            abufs[(n - 1) % 2][...] = acc
        elif n > 0:
            emit(n - 1, acc)
        if n < len(items):
            mbufs[n % 2][...] = jnp.max(m_new, axis=0, keepdims=True)

    for n in range(len(items) + (2 if separate else 1)):
        if separate:
            pl.when(pl.program_id(0) > -(n + 1))(functools.partial(pair, n))
        else:
            pair(n)


def _attention(qt, k, vt, batch, cache=None):
    ntok = qt.shape[1]
    t_new = k.shape[0] // batch
    tq = ATT_Q_TILE
    tiles = ATT_TILES_PER_STEP
    nq = ntok // batch // tq
    keys_per_tile = nq < tiles
    extra_specs, extra, tk = [], (), t_new
    if keys_per_tile:
        assert cache is None
        seqs = tiles // nq
        k_spec = pl.BlockSpec((seqs * t_new, D_KA), lambda t: (t, 0))
        vt_spec = pl.BlockSpec((D_KA, seqs * t_new), lambda t: (0, t))
    else:
        spq = nq // tiles
        k_spec = pl.BlockSpec((t_new, D_KA), lambda t: (t // spq, 0))
        vt_spec = pl.BlockSpec((D_KA, t_new), lambda t: (0, t // spq))
        if cache is not None:
            t_old = cache[0].shape[0] // batch
            assert t_new % ATT_KEY_BLOCK == 0 and t_old % ATT_KEY_BLOCK == 0
            extra_specs = [pl.BlockSpec((t_old, D_KA), lambda t: (t // spq, 0)),
                           pl.BlockSpec((D_KA, t_old), lambda t: (0, t // spq))]
            extra, tk = cache, t_new + t_old
    return pl.pallas_call(
        functools.partial(_attn_kernel, keys_per_tile=keys_per_tile, cached=cache is not None),
        grid=(ntok // (tiles * tq),),
        in_specs=[pl.BlockSpec((D_QA, tiles * tq), lambda t: (0, t)), k_spec, vt_spec] + extra_specs,
        out_specs=pl.BlockSpec((tiles * tq, D_QA), lambda t: (t, 0)),
        out_shape=jax.ShapeDtypeStruct((ntok, D_QA), BF16),
        scratch_shapes=[pltpu.VMEM((tk, GROUP * tq), F32), pltpu.VMEM((tk, GROUP * tq), F32),
                        pltpu.VMEM((1, GROUP * tq), F32), pltpu.VMEM((1, GROUP * tq), F32),
                        pltpu.VMEM((HEAD_DIM + 16, GROUP * tq), F32),
                        pltpu.VMEM((HEAD_DIM + 16, GROUP * tq), F32)],
        compiler_params=pltpu.CompilerParams(
            dimension_semantics=("arbitrary",), vmem_limit_bytes=VMEM_LIMIT),
        name="attention",
    )(qt, k, vt, *extra)


def _post_kernel(og_ref, oa_ref, x_ref, mod_ref, n2_ref, fg_ref, wo_ref, w1_ref, w3_ref, w2_ref,
                 y_ref):
    D = D_MODEL
    mod = mod_ref[0]
    g1, sh2, sc2, g2 = (mod[:, i * D:(i + 1) * D] for i in range(2, 6))
    out = _dot(og_ref[...], wo_ref[0:D_V, :]) + _dot(oa_ref[...], wo_ref[D_V:D_V + D_QA, :])
    x1 = x_ref[...] + g1 * out
    hb = (_rms(x1, n2_ref[...]) * (1.0 + sc2) + sh2).astype(BF16)
    act = (_silu(_dot(hb, w1_ref[...])) * _dot(hb, w3_ref[...])).astype(BF16)
    x2 = x1 + g2 * _dot(act, w2_ref[...])
    y_ref[...] = _rms(x2, fg_ref[...])


def _post_mixer(o_gla, o_att, x, mod3, mod_row0, tokens_per_mod, weights, tm):
    ntok = x.shape[0]
    tiles_per_mod = tokens_per_mod // tm
    whole = pl.BlockSpec(memory_space=pltpu.VMEM)
    row = lambda w: pl.BlockSpec((tm, w), lambda i: (i, 0))
    return pl.pallas_call(
        _post_kernel,
        grid=(ntok // tm,),
        in_specs=[row(D_V), row(D_QA), row(D_MODEL),
                  pl.BlockSpec((1, 1, N_MOD * D_MODEL), lambda i: (mod_row0 + i // tiles_per_mod, 0, 0)),
                  whole, whole, whole, whole, whole, whole],
        out_specs=row(D_MODEL),
        out_shape=jax.ShapeDtypeStruct((ntok, D_MODEL), F32),
        compiler_params=pltpu.CompilerParams(
            dimension_semantics=("arbitrary",), vmem_limit_bytes=VMEM_LIMIT),
        name="post_mixer",
    )(o_gla, o_att, x, mod3, *weights)


def _rope_tables(t):
    rows = t // GRID_W
    row = np.repeat(np.arange(rows, dtype=np.float64), GRID_W)
    col = np.tile(np.arange(GRID_W, dtype=np.float64), rows)
    inv = ROPE_THETA ** (-np.arange(0, ROPE_AXIS_DIM, 2, dtype=np.float64) / ROPE_AXIS_DIM)
    ang_r = row[:, None] * inv
    ang_c = col[:, None] * inv
    cos = np.concatenate([np.cos(ang_r)] * 2 + [np.cos(ang_c)] * 2, axis=-1)
    sin = np.concatenate([-np.sin(ang_r), np.sin(ang_r), -np.sin(ang_c), np.sin(ang_c)], axis=-1)
    tile = lambda a: jnp.asarray(np.tile(a, (1, ATT_KV_HEADS)), F32)
    return tile(cos), tile(sin)


def _block_mean(width, group):
    return jnp.asarray(np.kron(np.eye(width // group), np.ones((group, group))) / group, BF16)


def kernel(x_prompt, x_sample, cache_k, cache_v, state_gla_fwd, state_gla_bwd, c, c_ctx,
           ada_w, ada_b, norm1_g, norm2_g, w_in, w_gk2, b_gk2, gla_norm_g, q_norm_g, k_norm_g,
           w_out, w_ffn1, w_ffn3, w_ffn2, final_g):
    B, T, D = x_prompt.shape
    BL, TL, _ = x_sample.shape
    TP = cache_k.shape[2]
    l = 0

    mod3 = _adaln(c_ctx, c, ada_w[l], ada_b[l])

    offs = np.cumsum([0, D_QK, D_QK, D_V, D_V, 2 * GLA_LOWRANK, D_QA, D_KA, D_KA])
    seg = lambda i: w_in[l][:, offs[i]:offs[i + 1]]
    w_in_r = jnp.concatenate([seg(0), seg(1), seg(2), seg(3), seg(5), seg(6), seg(7), seg(4)],
                             axis=1).astype(BF16)
    zero = jnp.zeros((GLA_LOWRANK, D_QK), F32)
    w_gk = jnp.concatenate([jnp.concatenate([w_gk2[l, 0], zero], axis=1),
                            jnp.concatenate([zero, w_gk2[l, 1]], axis=1)], axis=0).astype(BF16)
    pre_w = (norm1_g[l].reshape(1, D), w_in_r, w_gk, b_gk2[l].reshape(1, 2 * D_QK),
             jnp.tile(q_norm_g[l], ATT_HEADS).reshape(1, D_QA),
             jnp.tile(k_norm_g[l], ATT_KV_HEADS).reshape(1, D_KA),
             _block_mean(D_QA, HEAD_DIM), _block_mean(D_KA, HEAD_DIM))
    post_norms = (norm2_g[l].reshape(1, D), final_g.reshape(1, D))
    post_w_f32 = (w_out[l], w_ffn1[l], w_ffn3[l], w_ffn2[l])
    gla_g = gla_norm_g[l].reshape(1, GLA_DV)
    tm_pre = 1024
    tm_post = 512

    xc = x_prompt.reshape(B * T, D)
    q, k, v, gate, g, qa, ka, va, kc, vc, *post_w_bf16 = _pre_mixer(
        xc, mod3, 0, B * T, T, pre_w, None, True, tm_pre, cast=post_w_f32)
    post_w = post_norms + tuple(post_w_bf16)
    r3 = lambda a, b, t: a.reshape(b, t, a.shape[-1])
    o_gla, s_f, s_b = _gla(r3(q, B, T), r3(k, B, T), r3(v, B, T), r3(g, B, T), r3(gate, B, T),
                           gla_g, None, 4)
    o_att = _attention(qa, ka, va, B)
    y_prompt = _post_mixer(o_gla.reshape(B * T, D_V), o_att, xc, mod3, 0, B * T, post_w, tm_post)

    xl = x_sample.reshape(BL * TL, D)
    q, k, v, gate, g, qa, ka, va = _pre_mixer(
        xl, mod3, 1, TL, TL, pre_w, _rope_tables(TL), False, tm_pre)
    o_gla, _, _ = _gla(r3(q, BL, TL), r3(k, BL, TL), r3(v, BL, TL), r3(g, BL, TL), r3(gate, BL, TL),
                       gla_g, (state_gla_fwd[:, l].reshape(BL, D_QK, GLA_DV),
                               state_gla_bwd[:, l].reshape(BL, D_QK, GLA_DV)), BL)

    k_old = cache_k[:, l].reshape(BL * TP, D_KA).astype(BF16)
    vt_old = jnp.transpose(cache_v[:, l], (2, 3, 0, 1)).reshape(D_KA, BL * TP).astype(BF16)
    o_att = _attention(qa, ka, va, BL, (k_old, vt_old))
    y_sample = _post_mixer(o_gla.reshape(BL * TL, D_V), o_att, xl, mod3, 1, TL, post_w, tm_post)

    def cache_out(ct):
        return jnp.transpose(ct.reshape(B, 1, ATT_KV_HEADS, HEAD_DIM, T), (0, 1, 4, 2, 3))

    return (y_prompt.reshape(B, T, D), y_sample.reshape(BL, TL, D), cache_out(kc), cache_out(vc),
            s_f.reshape(B, 1, GLA_HEADS, GLA_DK, GLA_DV), s_b.reshape(B, 1, GLA_HEADS, GLA_DK, GLA_DV))
```

```python
import functools

import numpy as np
import jax
import jax.numpy as jnp
from jax import lax
from jax.experimental import pallas as pl
from jax.experimental.pallas import tpu as pltpu

F32 = jnp.float32
BF16 = jnp.bfloat16

D_MODEL = 1024
GRID_W = 64
GLA_HEADS = 4
GLA_DK = 64
GLA_DV = 128
GLA_LOWRANK = 16
GLA_GATE_NORM = 16.0
GLA_CHUNK = 64
ATT_HEADS = 8
ATT_KV_HEADS = 2
HEAD_DIM = 64
ROPE_AXIS_DIM = HEAD_DIM // 2
ROPE_THETA = 10000.0
D_FF = -(-8 * D_MODEL // (3 * 256)) * 256
N_MOD = 6
EPS = 1e-6

D_QK = GLA_HEADS * GLA_DK
D_V = GLA_HEADS * GLA_DV
D_QA = ATT_HEADS * HEAD_DIM
D_KA = ATT_KV_HEADS * HEAD_DIM
GROUP = ATT_HEADS // ATT_KV_HEADS
O_Q, O_K, O_V, O_OG = 0, D_QK, 2 * D_QK, 2 * D_QK + D_V
O_QA = O_OG + D_V
O_KA = O_QA + D_QA
O_VA = O_KA + D_KA
O_LR = O_VA + D_KA
D_PROJ = O_LR + 2 * GLA_LOWRANK

COND_ROWS = 8
VMEM_LIMIT = 56 * 1024 * 1024


def _dot(a, b):
    return jnp.dot(a, b, preferred_element_type=F32)


def _dot_nt(a, b):
    return lax.dot_general(a, b, (((1,), (1,)), ((), ())), preferred_element_type=F32)


def _dot_tn(a, b):
    return lax.dot_general(a, b, (((0,), (0,)), ((), ())), preferred_element_type=F32)


def _rms(x, g):
    ms = jnp.mean(x * x, axis=-1, keepdims=True)
    return x * lax.rsqrt(ms + EPS) * g


def _silu(x):
    return x * jax.nn.sigmoid(x)


ADALN_K_BLOCK = 256


def _adaln_kernel(cctx_ref, c_ref, w_ref, b_ref, o_ref, acc):
    k = pl.program_id(0)

    @pl.when(k == 0)
    def _():
        acc[...] = jnp.broadcast_to(b_ref[...], acc.shape)

    pad = jnp.zeros((COND_ROWS - 1 - c_ref.shape[0], c_ref.shape[1]), F32)
    cond = jnp.concatenate([cctx_ref[...], c_ref[...], pad], axis=0)
    acc[...] += _dot(_silu(cond).astype(BF16), w_ref[...].astype(BF16))

    @pl.when(k == pl.num_programs(0) - 1)
    def _():
        o_ref[:, 0, :] = acc[...]


def _adaln(c_ctx, c, ada_w, ada_b):
    d, n = ada_w.shape
    tk = ADALN_K_BLOCK
    assert 1 + c.shape[0] <= COND_ROWS
    return pl.pallas_call(
        _adaln_kernel,
        grid=(d // tk,),
        in_specs=[
            pl.BlockSpec((1, tk), lambda k: (0, k)),
            pl.BlockSpec((c.shape[0], tk), lambda k: (0, k)),
            pl.BlockSpec((tk, n), lambda k: (k, 0)),
            pl.BlockSpec((1, n), lambda k: (0, 0)),
        ],
        out_specs=pl.BlockSpec((COND_ROWS, 1, n), lambda k: (0, 0, 0)),
        out_shape=jax.ShapeDtypeStruct((COND_ROWS, 1, n), F32),
        scratch_shapes=[pltpu.VMEM((COND_ROWS, n), F32)],
        compiler_params=pltpu.CompilerParams(
            dimension_semantics=("arbitrary",), vmem_limit_bytes=VMEM_LIMIT),
        name="adaln",
    )(c_ctx.reshape(1, d), c, ada_w, ada_b.reshape(1, n))


def _swap_halves(x, lane_lo):
    n = x.shape[-1]
    up = pltpu.roll(x, n - ROPE_AXIS_DIM // 2, axis=1)
    dn = pltpu.roll(x, ROPE_AXIS_DIM // 2, axis=1)
    return jnp.where(lane_lo, up, dn)


PRE_SUB_ROWS = 256


def _pre_kernel(*refs, rope, emit_cache, n_cast):
    it = iter(refs)
    x_ref, mod_ref, n1_ref, win_ref, wgk_ref, bgk_ref = (next(it) for _ in range(6))
    qg_ref, kg_ref, bdq_ref, bdk_ref = (next(it) for _ in range(4))
    if rope:
        cos_ref, sin_ref = next(it), next(it)
    cast_src = [next(it) for _ in range(n_cast)]
    q_o, k_o, v_o, gate_o, g_o, qa_o, ka_o, va_o = (next(it) for _ in range(8))
    if emit_cache:
        kc_o, vc_o = next(it), next(it)
    for src in cast_src:
        next(it)[...] = src[...].astype(BF16)

    sub = PRE_SUB_ROWS
    n_sub = x_ref.shape[0] // sub
    mod = mod_ref[0]
    sh1 = mod[:, 0:D_MODEL]
    sc1 = mod[:, D_MODEL:2 * D_MODEL]

    def project(j):
        rows = slice(j * sub, (j + 1) * sub)
        h = _rms(x_ref[rows, :], n1_ref[...]) * (1.0 + sc1) + sh1
        return _dot(h.astype(BF16), win_ref[...])

    def finish(j, p_ref):
        rows = slice(j * sub, (j + 1) * sub)
        q_o[rows, :] = p_ref[:, O_Q:O_Q + D_QK].astype(BF16)
        k_o[rows, :] = p_ref[:, O_K:O_K + D_QK].astype(BF16)
        v_o[rows, :] = p_ref[:, O_V:O_V + D_V].astype(BF16)
        gate_o[rows, :] = _silu(p_ref[:, O_OG:O_OG + D_V]).astype(BF16)

        gk = _dot(p_ref[:, O_LR:O_LR + 2 * GLA_LOWRANK].astype(BF16), wgk_ref[...]) + bgk_ref[...]
        g_o[rows, :] = ((jnp.minimum(gk, 0.0) - jnp.log(1.0 + jnp.exp(-jnp.abs(gk))))
                        * (1.0 / GLA_GATE_NORM))

        qa = p_ref[:, O_QA:O_QA + D_QA]
        ka = p_ref[:, O_KA:O_KA + D_KA]
        va = p_ref[:, O_VA:O_VA + D_KA]
        qa = qa * lax.rsqrt(_dot((qa * qa).astype(BF16), bdq_ref[...]) + EPS) * qg_ref[...]
        ka = ka * lax.rsqrt(_dot((ka * ka).astype(BF16), bdk_ref[...]) + EPS) * kg_ref[...]
        if rope:
            lane_q = lax.broadcasted_iota(jnp.int32, qa.shape, 1)
            lane_k = lax.broadcasted_iota(jnp.int32, ka.shape, 1)
            lo_q = (lane_q % ROPE_AXIS_DIM) < ROPE_AXIS_DIM // 2
            lo_k = (lane_k % ROPE_AXIS_DIM) < ROPE_AXIS_DIM // 2
            cos_k = cos_ref[rows, :]
            sin_k = sin_ref[rows, :]
            cos_q = jnp.concatenate([cos_k] * GROUP, axis=1)
            sin_q = jnp.concatenate([sin_k] * GROUP, axis=1)
            qa = qa * cos_q + _swap_halves(qa, lo_q) * sin_q
            ka = ka * cos_k + _swap_halves(ka, lo_k) * sin_k
        vat = va.T
        if emit_cache:
            kat = ka.T
            t = kc_o.shape[2]
            for sq in range(sub // t):
                kc_o[j * (sub // t) + sq] = kat[:, sq * t:(sq + 1) * t]
                vc_o[j * (sub // t) + sq] = vat[:, sq * t:(sq + 1) * t]
        qa_o[:, rows] = (qa * (HEAD_DIM ** -0.5 * LOG2E)).T.astype(BF16)
        ka_o[rows, :] = ka.astype(BF16)
        va_o[:, rows] = vat.astype(BF16)

    p_prev = None
    for n in range(n_sub + 1):
        p_new = project(n) if n < n_sub else None
        if n > 0:
            finish(n - 1, p_prev)
        p_prev = p_new


def _pre_mixer(x, mod3, mod_row0, tokens_per_mod, seq_len, weights, rope_tabs, emit_cache, tm,
               cast=()):
    ntok = x.shape[0]
    n1, w_in_r, w_gk, b_gk, qg, kg, bdq, bdk = weights
    rope = rope_tabs is not None
    tiles_per_mod = tokens_per_mod // tm
    tiles_per_seq = seq_len // tm
    whole = pl.BlockSpec(memory_space=pltpu.VMEM)
    row = lambda w: pl.BlockSpec((tm, w), lambda i: (i, 0))
    in_specs = [
        row(D_MODEL),
        pl.BlockSpec((1, 1, N_MOD * D_MODEL), lambda i: (mod_row0 + i // tiles_per_mod, 0, 0)),
        whole, whole, whole, whole, whole, whole, whole, whole,
    ]
    args = [x, mod3, n1, w_in_r, w_gk, b_gk, qg, kg, bdq, bdk]
    if rope:
        tab = pl.BlockSpec((tm, D_KA), lambda i: (i % tiles_per_seq, 0))
        in_specs += [tab, tab]
        args += list(rope_tabs)
    col = lambda w: pl.BlockSpec((w, tm), lambda i: (0, i))
    out_specs = [row(D_QK), row(D_QK), row(D_V), row(D_V), row(2 * D_QK),
                 col(D_QA), row(D_KA), col(D_KA)]
    out_shape = [
        jax.ShapeDtypeStruct((ntok, D_QK), BF16),
        jax.ShapeDtypeStruct((ntok, D_QK), BF16),
        jax.ShapeDtypeStruct((ntok, D_V), BF16),
        jax.ShapeDtypeStruct((ntok, D_V), BF16),
        jax.ShapeDtypeStruct((ntok, 2 * D_QK), F32),
        jax.ShapeDtypeStruct((D_QA, ntok), BF16),
        jax.ShapeDtypeStruct((ntok, D_KA), BF16),
        jax.ShapeDtypeStruct((D_KA, ntok), BF16),
    ]
    if emit_cache:
        seqs = tm // seq_len
        out_specs += [pl.BlockSpec((seqs, D_KA, seq_len), lambda i: (i, 0, 0))] * 2
        out_shape += [jax.ShapeDtypeStruct((ntok // seq_len, D_KA, seq_len), F32)] * 2
    steps = ntok // tm
    for a in cast:
        slab = pl.BlockSpec((a.shape[0] // steps, a.shape[1]), lambda i: (i, 0))
        in_specs.append(slab)
        args.append(a)
        out_specs.append(slab)
        out_shape.append(jax.ShapeDtypeStruct(a.shape, BF16))
    return pl.pallas_call(
        functools.partial(_pre_kernel, rope=rope, emit_cache=emit_cache, n_cast=len(cast)),
        grid=(ntok // tm,),
        in_specs=in_specs,
        out_specs=out_specs,
        out_shape=out_shape,
        compiler_params=pltpu.CompilerParams(
            dimension_semantics=("arbitrary",), vmem_limit_bytes=VMEM_LIMIT),
        name="pre_mixer_rope" if rope else "pre_mixer",
    )(*args)


def _gla_chunks(units, q_ref, k_ref, v_ref, g_ref, half_lo):
    C = GLA_CHUNK
    n_pair = GLA_HEADS // 2
    rows = [pl.ds(pl.multiple_of(c * C, C), C) for (_, c, _, _, _, _) in units]

    bs = []
    for (bi, _, fwd, tri, _, _), r in zip(units, rows):
        gc = g_ref[bi, r, 0:D_QK] if fwd else g_ref[bi, r, D_QK:2 * D_QK]
        ltri = jnp.where(tri, 1.0, 0.0).astype(BF16)
        g_hi = gc.astype(BF16)
        g_lo = (gc - g_hi.astype(F32)).astype(BF16)
        bs.append(_dot(ltri, g_hi) + _dot(ltri, g_lo))

    qts, kts, kes, decays = [], [], [], []
    for (bi, _, fwd, _, _, _), r, b in zip(units, rows, bs):
        qc = q_ref[bi, r, :].astype(F32) * GLA_DK ** -0.5
        kc = k_ref[bi, r, :].astype(F32)
        btot = b[C - 1:C, :] if fwd else b[0:1, :]
        qts.append((qc * jnp.exp(b)).astype(BF16))
        kts.append((kc * jnp.exp(-b)).astype(BF16))
        kes.append((kc * jnp.exp(btot - b)).astype(BF16))
        decays.append(jnp.transpose(jnp.broadcast_to(jnp.exp(btot), (8, D_QK)))[:, 0:1])

    scores, incs, qms = [], [], []
    for (bi, _, _, tri, _, _), r, qt, kt, ke in zip(units, rows, qts, kts, kes):
        for pair in range(n_pair):
            lanes = slice(pair * 128, (pair + 1) * 128)
            incs.append(_dot_tn(ke[:, lanes], v_ref[bi, r, pair * 2 * GLA_DV:(pair + 1) * 2 * GLA_DV]))
            for hh in range(2):
                qm = jnp.where(half_lo if hh == 0 else ~half_lo, qt[:, lanes], jnp.zeros((C, 128), BF16))
                qms.append(qm)
                scores.append(jnp.where(tri, _dot_nt(qm, kt[:, lanes]), 0.0).astype(BF16))

    for ui, ((bi, _, _, _, s_ref, o_ref), r) in enumerate(zip(units, rows)):
        state = s_ref[bi]
        state_b = state.astype(BF16)
        for hd in range(GLA_HEADS):
            pair = hd // 2
            lhs = jnp.concatenate([qms[ui * GLA_HEADS + hd], scores[ui * GLA_HEADS + hd]], axis=1)
            rhs = jnp.concatenate([state_b[pair * 128:(pair + 1) * 128, :],
                                   v_ref[bi, r, hd * GLA_DV:(hd + 1) * GLA_DV]], axis=0)
            o_ref[bi, r, hd * GLA_DV:(hd + 1) * GLA_DV] = _dot(lhs, rhs)
        inc = jnp.concatenate(
            [incs[ui * n_pair + hd // 2][(hd % 2) * GLA_DK:(hd % 2 + 1) * GLA_DK,
                                         (hd % 2) * GLA_DV:(hd % 2 + 1) * GLA_DV]
             for hd in range(GLA_HEADS)], axis=0)
        s_ref[bi] = decays[ui] * state + inc


def _gla_kernel(*args, zero_init, n_cast):
    q_ref, k_ref, v_ref, g_ref, gate_ref, gn_ref = args[:6]
    n_in = 6 + (0 if zero_init else 2)
    cast_src = args[n_in:n_in + n_cast]
    o_ref, sf_ref, sb_ref = args[n_in + n_cast:n_in + n_cast + 3]
    cast_dst = args[n_in + n_cast + 3:n_in + 2 * n_cast + 3]
    of_scr, ob_scr = args[-2:]
    for src, dst in zip(cast_src, cast_dst):
        dst[...] = src[...].astype(BF16)
    C = GLA_CHUNK
    nb, t, _ = q_ref.shape
    n = t // C
    row = lax.broadcasted_iota(jnp.int32, (C, C), 0)
    col = lax.broadcasted_iota(jnp.int32, (C, C), 1)
    half_lo = lax.broadcasted_iota(jnp.int32, (C, 128), 1) < GLA_DK
    refs = (q_ref, k_ref, v_ref, g_ref)
    if zero_init:
        sf_ref[...] = jnp.zeros(sf_ref.shape, F32)
        sb_ref[...] = jnp.zeros(sb_ref.shape, F32)
    else:
        sf_ref[...] = args[6][...]
        sb_ref[...] = args[7][...]

    steps = min(GLA_STEPS_PER_ITER, n)
    assert n % steps == 0

    def body(i, carry):
        units = []
        for u in range(steps):
            c = i * steps + u
            for bi in range(nb):
                units.append((bi, c, True, row >= col, sf_ref, of_scr))
                units.append((bi, n - 1 - c, False, row <= col, sb_ref, ob_scr))
        _gla_chunks(units, *refs, half_lo)
        return carry

    lax.fori_loop(0, n // steps, body, 0)

    rt = GLA_OUT_ROWS
    gn = gn_ref[...]

    def finish(i, carry):
        rows = pl.ds(pl.multiple_of(i * rt, rt), rt)
        for bi in range(nb):
            o = of_scr[bi, rows, :] + ob_scr[bi, rows, :]
            gate = gate_ref[bi, rows, :].astype(F32)
            for hd in range(GLA_HEADS):
                sl = slice(hd * GLA_DV, (hd + 1) * GLA_DV)
                o_ref[bi, rows, sl] = (_rms(o[:, sl], gn) * gate[:, sl]).astype(BF16)
        return carry

    lax.fori_loop(0, t // rt, finish, 0)


GLA_STEPS_PER_ITER = 4
GLA_OUT_ROWS = 256


def _gla(q, k, v, g, gate, gla_g, init_states, nb, cast=()):
    b, t, _ = q.shape
    seq = lambda w: pl.BlockSpec((nb, t, w), lambda i: (i, 0, 0))
    st = pl.BlockSpec((nb, D_QK, GLA_DV), lambda i: (i, 0, 0))
    zero_init = init_states is None
    steps = b // nb
    slabs = [pl.BlockSpec((a.shape[0] // steps, a.shape[1]), lambda i: (i, 0)) for a in cast]
    return pl.pallas_call(
        functools.partial(_gla_kernel, zero_init=zero_init, n_cast=len(cast)),
        grid=(b // nb,),
        in_specs=[seq(D_QK), seq(D_QK), seq(D_V), seq(2 * D_QK), seq(D_V),
                  pl.BlockSpec((1, GLA_DV), lambda i: (0, 0))] + ([] if zero_init else [st, st]) + slabs,
        out_specs=[seq(D_V), st, st] + slabs,
        out_shape=[jax.ShapeDtypeStruct((b, t, D_V), BF16),
                   jax.ShapeDtypeStruct((b, D_QK, GLA_DV), F32),
                   jax.ShapeDtypeStruct((b, D_QK, GLA_DV), F32)]
        + [jax.ShapeDtypeStruct(a.shape, BF16) for a in cast],
        scratch_shapes=[pltpu.VMEM((nb, t, D_V), F32), pltpu.VMEM((nb, t, D_V), F32)],
        compiler_params=pltpu.CompilerParams(
            dimension_semantics=("arbitrary",), vmem_limit_bytes=VMEM_LIMIT),
        name="gla",
    )(q, k, v, g, gate, gla_g, *(() if zero_init else init_states), *cast)


ATT_KEY_BLOCK = 256
ATT_Q_TILE = 256
LOG2E = 1.4426950408889634


ATT_TILES_PER_STEP = 4


def _attn_kernel(*args, keys_per_tile, cached):
    qt_ref, k_ref, vt_ref = args[:3]
    kc_ref, vtc_ref = args[3:5] if cached else (None, None)
    o_ref, s0, s1, m0, m1, a0, a1 = args[-7:]
    tq = ATT_Q_TILE
    tiles = qt_ref.shape[1] // tq
    t_new = k_ref.shape[0] // tiles if keys_per_tile else k_ref.shape[0]
    tk = t_new + (kc_ref.shape[0] if cached else 0)
    kb = min(ATT_KEY_BLOCK, tk)
    nkb = tk // kb
    zero = jnp.zeros((HEAD_DIM, tq), BF16)
    ones = jnp.ones((16, kb), BF16)
    items = [(qi, kv) for qi in range(tiles) for kv in range(ATT_KV_HEADS)]
    bufs = (s0, s1)
    mbufs = (m0, m1)
    abufs = (a0, a1)
    separate = nkb > 1

    def q_ext(qi, kv):
        cols = []
        for g in range(GROUP):
            hd = kv * GROUP + g
            qt = qt_ref[hd * HEAD_DIM:(hd + 1) * HEAD_DIM, qi * tq:(qi + 1) * tq]
            cols.append(jnp.concatenate([qt, zero] if kv == 0 else [zero, qt], axis=0))
        return jnp.concatenate(cols, axis=1)

    def key0(qi):
        return qi * tk if keys_per_tile else 0

    def score_piece(n, q_all, i, m):
        r0 = key0(items[n][0]) + i * kb
        keys = k_ref[r0:r0 + kb, :] if i * kb < t_new else kc_ref[r0 - t_new:r0 - t_new + kb, :]
        s = _dot(keys, q_all)
        bufs[n % 2][i * kb:(i + 1) * kb, :] = s
        for r in range(kb // 8):
            tile = s[r * 8:(r + 1) * 8, :]
            m = tile if m is None else jnp.maximum(m, tile)
        return m

    def value_piece(n, m, i, acc):
        qi, kv = items[n]
        c0 = key0(qi) + i * kb
        p = jnp.exp2(bufs[n % 2][i * kb:(i + 1) * kb, :] - m).astype(BF16)
        hd_rows = slice(kv * HEAD_DIM, (kv + 1) * HEAD_DIM)
        vt = vt_ref[hd_rows, c0:c0 + kb] if i * kb < t_new else vtc_ref[hd_rows, c0 - t_new:c0 - t_new + kb]
        vt1 = jnp.concatenate([vt, ones], axis=0)
        d = _dot(vt1, p)
        return d if acc is None else acc + d

    def emit(n, acc):
        qi, kv = items[n]
        o = acc[:HEAD_DIM] / acc[HEAD_DIM:HEAD_DIM + 1]
        for g in range(0, GROUP, 2):
            two = jnp.concatenate([o[:, g * tq:(g + 1) * tq], o[:, (g + 1) * tq:(g + 2) * tq]], axis=0)
            hd = kv * GROUP + g
            o_ref[qi * tq:(qi + 1) * tq, hd * HEAD_DIM:(hd + 2) * HEAD_DIM] = two.T.astype(BF16)

    def pair(n):
        if separate and n >= 2:
            emit(n - 2, abufs[n % 2][...])
        if n > len(items):
            return
        q_all = q_ext(*items[n]) if n < len(items) else None
        m_prev = mbufs[(n - 1) % 2][...] if n > 0 else None
        m_new = acc = None
        for i in range(nkb):
            if n < len(items):
                m_new = score_piece(n, q_all, i, m_new)
            if n > 0:
                acc = value_piece(n - 1, m_prev, i, acc)
        if n > 0 and separate:
            abufs[(n - 1) % 2][...] = acc
        elif n > 0:
            emit(n - 1, acc)
        if n < len(items):
            mbufs[n % 2][...] = jnp.max(m_new, axis=0, keepdims=True)

    for n in range(len(items) + (2 if separate else 1)):
        if separate:
            pl.when(pl.program_id(0) > -(n + 1))(functools.partial(pair, n))
        else:
            pair(n)


def _attention(qt, k, vt, batch, cache=None):
    ntok = qt.shape[1]
    t_new = k.shape[0] // batch
    tq = ATT_Q_TILE
    tiles = ATT_TILES_PER_STEP
    nq = ntok // batch // tq
    keys_per_tile = nq < tiles
    extra_specs, extra, tk = [], (), t_new
    if keys_per_tile:
        assert cache is None
        seqs = tiles // nq
        k_spec = pl.BlockSpec((seqs * t_new, D_KA), lambda t: (t, 0))
        vt_spec = pl.BlockSpec((D_KA, seqs * t_new), lambda t: (0, t))
    else:
        spq = nq // tiles
        k_spec = pl.BlockSpec((t_new, D_KA), lambda t: (t // spq, 0))
        vt_spec = pl.BlockSpec((D_KA, t_new), lambda t: (0, t // spq))
        if cache is not None:
            t_old = cache[0].shape[0] // batch
            assert t_new % ATT_KEY_BLOCK == 0 and t_old % ATT_KEY_BLOCK == 0
            extra_specs = [pl.BlockSpec((t_old, D_KA), lambda t: (t // spq, 0)),
                           pl.BlockSpec((D_KA, t_old), lambda t: (0, t // spq))]
            extra, tk = cache, t_new + t_old
    return pl.pallas_call(
        functools.partial(_attn_kernel, keys_per_tile=keys_per_tile, cached=cache is not None),
        grid=(ntok // (tiles * tq),),
        in_specs=[pl.BlockSpec((D_QA, tiles * tq), lambda t: (0, t)), k_spec, vt_spec] + extra_specs,
        out_specs=pl.BlockSpec((tiles * tq, D_QA), lambda t: (t, 0)),
        out_shape=jax.ShapeDtypeStruct((ntok, D_QA), BF16),
        scratch_shapes=[pltpu.VMEM((tk, GROUP * tq), F32), pltpu.VMEM((tk, GROUP * tq), F32),
                        pltpu.VMEM((1, GROUP * tq), F32), pltpu.VMEM((1, GROUP * tq), F32),
                        pltpu.VMEM((HEAD_DIM + 16, GROUP * tq), F32),
                        pltpu.VMEM((HEAD_DIM + 16, GROUP * tq), F32)],
        compiler_params=pltpu.CompilerParams(
            dimension_semantics=("arbitrary",), vmem_limit_bytes=VMEM_LIMIT),
        name="attention",
    )(qt, k, vt, *extra)


def _post_kernel(og_ref, oa_ref, x_ref, mod_ref, n2_ref, fg_ref, wo_ref, w1_ref, w3_ref, w2_ref,
                 y_ref):
    D = D_MODEL
    mod = mod_ref[0]
    g1, sh2, sc2, g2 = (mod[:, i * D:(i + 1) * D] for i in range(2, 6))
    out = _dot(og_ref[...], wo_ref[0:D_V, :]) + _dot(oa_ref[...], wo_ref[D_V:D_V + D_QA, :])
    x1 = x_ref[...] + g1 * out
    hb = (_rms(x1, n2_ref[...]) * (1.0 + sc2) + sh2).astype(BF16)
    act = (_silu(_dot(hb, w1_ref[...])) * _dot(hb, w3_ref[...])).astype(BF16)
    x2 = x1 + g2 * _dot(act, w2_ref[...])
    y_ref[...] = _rms(x2, fg_ref[...])


def _post_mixer(o_gla, o_att, x, mod3, mod_row0, tokens_per_mod, weights, tm):
    ntok = x.shape[0]
    tiles_per_mod = tokens_per_mod // tm
    whole = pl.BlockSpec(memory_space=pltpu.VMEM)
    row = lambda w: pl.BlockSpec((tm, w), lambda i: (i, 0))
    return pl.pallas_call(
        _post_kernel,
        grid=(ntok // tm,),
        in_specs=[row(D_V), row(D_QA), row(D_MODEL),
                  pl.BlockSpec((1, 1, N_MOD * D_MODEL), lambda i: (mod_row0 + i // tiles_per_mod, 0, 0)),
                  whole, whole, whole, whole, whole, whole],
        out_specs=row(D_MODEL),
        out_shape=jax.ShapeDtypeStruct((ntok, D_MODEL), F32),
        compiler_params=pltpu.CompilerParams(
            dimension_semantics=("arbitrary",), vmem_limit_bytes=VMEM_LIMIT),
        name="post_mixer",
    )(o_gla, o_att, x, mod3, *weights)


def _rope_tables(t):
    rows = t // GRID_W
    row = np.repeat(np.arange(rows, dtype=np.float64), GRID_W)
    col = np.tile(np.arange(GRID_W, dtype=np.float64), rows)
    inv = ROPE_THETA ** (-np.arange(0, ROPE_AXIS_DIM, 2, dtype=np.float64) / ROPE_AXIS_DIM)
    ang_r = row[:, None] * inv
    ang_c = col[:, None] * inv
    cos = np.concatenate([np.cos(ang_r)] * 2 + [np.cos(ang_c)] * 2, axis=-1)
    sin = np.concatenate([-np.sin(ang_r), np.sin(ang_r), -np.sin(ang_c), np.sin(ang_c)], axis=-1)
    tile = lambda a: jnp.asarray(np.tile(a, (1, ATT_KV_HEADS)), F32)
    return tile(cos), tile(sin)


def _block_mean(width, group):
    return jnp.asarray(np.kron(np.eye(width // group), np.ones((group, group))) / group, BF16)


def kernel(x_prompt, x_sample, cache_k, cache_v, state_gla_fwd, state_gla_bwd, c, c_ctx,
           ada_w, ada_b, norm1_g, norm2_g, w_in, w_gk2, b_gk2, gla_norm_g, q_norm_g, k_norm_g,
           w_out, w_ffn1, w_ffn3, w_ffn2, final_g):
    B, T, D = x_prompt.shape
    BL, TL, _ = x_sample.shape
    TP = cache_k.shape[2]
    l = 0

    mod3 = _adaln(c_ctx, c, ada_w[l], ada_b[l])

    offs = np.cumsum([0, D_QK, D_QK, D_V, D_V, 2 * GLA_LOWRANK, D_QA, D_KA, D_KA])
    seg = lambda i: w_in[l][:, offs[i]:offs[i + 1]]
    w_in_r = jnp.concatenate([seg(0), seg(1), seg(2), seg(3), seg(5), seg(6), seg(7), seg(4)],
                             axis=1).astype(BF16)
    zero = jnp.zeros((GLA_LOWRANK, D_QK), F32)
    w_gk = jnp.concatenate([jnp.concatenate([w_gk2[l, 0], zero], axis=1),
                            jnp.concatenate([zero, w_gk2[l, 1]], axis=1)], axis=0).astype(BF16)
    pre_w = (norm1_g[l].reshape(1, D), w_in_r, w_gk, b_gk2[l].reshape(1, 2 * D_QK),
             jnp.tile(q_norm_g[l], ATT_HEADS).reshape(1, D_QA),
             jnp.tile(k_norm_g[l], ATT_KV_HEADS).reshape(1, D_KA),
             _block_mean(D_QA, HEAD_DIM), _block_mean(D_KA, HEAD_DIM))
    post_norms = (norm2_g[l].reshape(1, D), final_g.reshape(1, D))
    post_w_f32 = (w_out[l], w_ffn1[l], w_ffn3[l], w_ffn2[l])
    gla_g = gla_norm_g[l].reshape(1, GLA_DV)
    tm_pre = 1024
    tm_post = 512

    xc = x_prompt.reshape(B * T, D)
    q, k, v, gate, g, qa, ka, va, kc, vc = _pre_mixer(
        xc, mod3, 0, B * T, T, pre_w, None, True, tm_pre)
    r3 = lambda a, b, t: a.reshape(b, t, a.shape[-1])
    o_gla, s_f, s_b, *post_w_bf16 = _gla(
        r3(q, B, T), r3(k, B, T), r3(v, B, T), r3(g, B, T), r3(gate, B, T), gla_g, None, 4,
        cast=post_w_f32)
    post_w = post_norms + tuple(post_w_bf16)
    o_att = _attention(qa, ka, va, B)
    y_prompt = _post_mixer(o_gla.reshape(B * T, D_V), o_att, xc, mod3, 0, B * T, post_w, tm_post)

    xl = x_sample.reshape(BL * TL, D)
    q, k, v, gate, g, qa, ka, va = _pre_mixer(
        xl, mod3, 1, TL, TL, pre_w, _rope_tables(TL), False, tm_pre)
    o_gla, _, _ = _gla(r3(q, BL, TL), r3(k, BL, TL), r3(v, BL, TL), r3(g, BL, TL), r3(gate, BL, TL),
                       gla_g, (state_gla_fwd[:, l].reshape(BL, D_QK, GLA_DV),
                               state_gla_bwd[:, l].reshape(BL, D_QK, GLA_DV)), BL)

    k_old = cache_k[:, l].reshape(BL * TP, D_KA).astype(BF16)
    vt_old = jnp.transpose(cache_v[:, l], (2, 3, 0, 1)).reshape(D_KA, BL * TP).astype(BF16)
    o_att = _attention(qa, ka, va, BL, (k_old, vt_old))
    y_sample = _post_mixer(o_gla.reshape(BL * TL, D_V), o_att, xl, mod3, 1, TL, post_w, tm_post)

    def cache_out(ct):
        return jnp.transpose(ct.reshape(B, 1, ATT_KV_HEADS, HEAD_DIM, T), (0, 1, 4, 2, 3))

    return (y_prompt.reshape(B, T, D), y_sample.reshape(BL, TL, D), cache_out(kc), cache_out(vc),
            s_f.reshape(B, 1, GLA_HEADS, GLA_DK, GLA_DV), s_b.reshape(B, 1, GLA_HEADS, GLA_DK, GLA_DV))
```

```python
import functools

import numpy as np
import jax
import jax.numpy as jnp
from jax import lax
from jax.experimental import pallas as pl
from jax.experimental.pallas import tpu as pltpu

F32 = jnp.float32
BF16 = jnp.bfloat16

D_MODEL = 1024
GRID_W = 64
GLA_HEADS = 4
GLA_DK = 64
GLA_DV = 128
GLA_LOWRANK = 16
GLA_GATE_NORM = 16.0
GLA_CHUNK = 64
ATT_HEADS = 8
ATT_KV_HEADS = 2
HEAD_DIM = 64
ROPE_AXIS_DIM = HEAD_DIM // 2
ROPE_THETA = 10000.0
D_FF = -(-8 * D_MODEL // (3 * 256)) * 256
N_MOD = 6
EPS = 1e-6

D_QK = GLA_HEADS * GLA_DK
D_V = GLA_HEADS * GLA_DV
D_QA = ATT_HEADS * HEAD_DIM
D_KA = ATT_KV_HEADS * HEAD_DIM
GROUP = ATT_HEADS // ATT_KV_HEADS
O_Q, O_K, O_V, O_OG = 0, D_QK, 2 * D_QK, 2 * D_QK + D_V
O_QA = O_OG + D_V
O_KA = O_QA + D_QA
O_VA = O_KA + D_KA
O_LR = O_VA + D_KA
D_PROJ = O_LR + 2 * GLA_LOWRANK

COND_ROWS = 8
VMEM_LIMIT = 56 * 1024 * 1024


def _dot(a, b):
    return jnp.dot(a, b, preferred_element_type=F32)


def _dot_nt(a, b):
    return lax.dot_general(a, b, (((1,), (1,)), ((), ())), preferred_element_type=F32)


def _dot_tn(a, b):
    return lax.dot_general(a, b, (((0,), (0,)), ((), ())), preferred_element_type=F32)


def _rms(x, g):
    ms = jnp.mean(x * x, axis=-1, keepdims=True)
    return x * lax.rsqrt(ms + EPS) * g


def _silu(x):
    return x * jax.nn.sigmoid(x)


ADALN_K_BLOCK = 256


def _adaln_kernel(cctx_ref, c_ref, w_ref, b_ref, wint_ref, o_ref, win_o, acc):
    k = pl.program_id(0)

    lr0, lr1 = O_OG + D_V, O_OG + D_V + 2 * GLA_LOWRANK
    feats = jnp.concatenate([wint_ref[0:lr0, :], wint_ref[lr1:D_PROJ, :], wint_ref[lr0:lr1, :]],
                            axis=0)
    for j in range(0, D_PROJ, 128):
        rows = min(128, D_PROJ - j)
        tile = feats[j:j + rows, :]
        if rows < 128:
            tile = jnp.concatenate([tile, jnp.zeros((128 - rows, tile.shape[1]), F32)], axis=0)
        win_o[:, j:j + rows] = tile.T[:, 0:rows].astype(BF16)

    @pl.when(k == 0)
    def _():
        acc[...] = jnp.broadcast_to(b_ref[...], acc.shape)

    pad = jnp.zeros((COND_ROWS - 1 - c_ref.shape[0], c_ref.shape[1]), F32)
    cond = jnp.concatenate([cctx_ref[...], c_ref[...], pad], axis=0)
    acc[...] += _dot(_silu(cond).astype(BF16), w_ref[...].astype(BF16))

    @pl.when(k == pl.num_programs(0) - 1)
    def _():
        o_ref[:, 0, :] = acc[...]


def _adaln(c_ctx, c, ada_w, ada_b, w_in_t):
    d, n = ada_w.shape
    tk = ADALN_K_BLOCK
    assert 1 + c.shape[0] <= COND_ROWS and w_in_t.shape == (D_PROJ, d)
    return pl.pallas_call(
        _adaln_kernel,
        grid=(d // tk,),
        in_specs=[
            pl.BlockSpec((1, tk), lambda k: (0, k)),
            pl.BlockSpec((c.shape[0], tk), lambda k: (0, k)),
            pl.BlockSpec((tk, n), lambda k: (k, 0)),
            pl.BlockSpec((1, n), lambda k: (0, 0)),
            pl.BlockSpec((D_PROJ, tk), lambda k: (0, k)),
        ],
        out_specs=[pl.BlockSpec((COND_ROWS, 1, n), lambda k: (0, 0, 0)),
                   pl.BlockSpec((tk, D_PROJ), lambda k: (k, 0))],
        out_shape=[jax.ShapeDtypeStruct((COND_ROWS, 1, n), F32),
                   jax.ShapeDtypeStruct((d, D_PROJ), BF16)],
        scratch_shapes=[pltpu.VMEM((COND_ROWS, n), F32)],
        compiler_params=pltpu.CompilerParams(
            dimension_semantics=("arbitrary",), vmem_limit_bytes=VMEM_LIMIT),
        name="adaln",
    )(c_ctx.reshape(1, d), c, ada_w, ada_b.reshape(1, n), w_in_t)


def _swap_halves(x, lane_lo):
    n = x.shape[-1]
    up = pltpu.roll(x, n - ROPE_AXIS_DIM // 2, axis=1)
    dn = pltpu.roll(x, ROPE_AXIS_DIM // 2, axis=1)
    return jnp.where(lane_lo, up, dn)


PRE_SUB_ROWS = 256


def _pre_kernel(*refs, rope, emit_cache, n_cast):
    it = iter(refs)
    x_ref, mod_ref, n1_ref, win_ref, wgk_ref, bgk_ref = (next(it) for _ in range(6))
    qg_ref, kg_ref, bdq_ref, bdk_ref = (next(it) for _ in range(4))
    if rope:
        cos_ref, sin_ref = next(it), next(it)
    cast_src = [next(it) for _ in range(n_cast)]
    q_o, k_o, v_o, gate_o, g_o, qa_o, ka_o, va_o = (next(it) for _ in range(8))
    if emit_cache:
        kc_o, vc_o = next(it), next(it)
    for src in cast_src:
        next(it)[...] = src[...].astype(BF16)

    sub = PRE_SUB_ROWS
    n_sub = x_ref.shape[0] // sub
    mod = mod_ref[0]
    sh1 = mod[:, 0:D_MODEL]
    sc1 = mod[:, D_MODEL:2 * D_MODEL]

    def project(j):
        rows = slice(j * sub, (j + 1) * sub)
        h = _rms(x_ref[rows, :], n1_ref[...]) * (1.0 + sc1) + sh1
        return _dot(h.astype(BF16), win_ref[...])

    def finish(j, p_ref):
        rows = slice(j * sub, (j + 1) * sub)
        q_o[rows, :] = p_ref[:, O_Q:O_Q + D_QK].astype(BF16)
        k_o[rows, :] = p_ref[:, O_K:O_K + D_QK].astype(BF16)
        v_o[rows, :] = p_ref[:, O_V:O_V + D_V].astype(BF16)
        gate_o[rows, :] = _silu(p_ref[:, O_OG:O_OG + D_V]).astype(BF16)

        gk = _dot(p_ref[:, O_LR:O_LR + 2 * GLA_LOWRANK].astype(BF16), wgk_ref[...]) + bgk_ref[...]
        g_o[rows, :] = ((jnp.minimum(gk, 0.0) - jnp.log(1.0 + jnp.exp(-jnp.abs(gk))))
                        * (1.0 / GLA_GATE_NORM))

        qa = p_ref[:, O_QA:O_QA + D_QA]
        ka = p_ref[:, O_KA:O_KA + D_KA]
        va = p_ref[:, O_VA:O_VA + D_KA]
        qa = qa * lax.rsqrt(_dot((qa * qa).astype(BF16), bdq_ref[...]) + EPS) * qg_ref[...]
        ka = ka * lax.rsqrt(_dot((ka * ka).astype(BF16), bdk_ref[...]) + EPS) * kg_ref[...]
        if rope:
            lane_q = lax.broadcasted_iota(jnp.int32, qa.shape, 1)
            lane_k = lax.broadcasted_iota(jnp.int32, ka.shape, 1)
            lo_q = (lane_q % ROPE_AXIS_DIM) < ROPE_AXIS_DIM // 2
            lo_k = (lane_k % ROPE_AXIS_DIM) < ROPE_AXIS_DIM // 2
            cos_k = cos_ref[rows, :]
            sin_k = sin_ref[rows, :]
            cos_q = jnp.concatenate([cos_k] * GROUP, axis=1)
            sin_q = jnp.concatenate([sin_k] * GROUP, axis=1)
            qa = qa * cos_q + _swap_halves(qa, lo_q) * sin_q
            ka = ka * cos_k + _swap_halves(ka, lo_k) * sin_k
        vat = va.T
        if emit_cache:
            kat = ka.T
            t = kc_o.shape[2]
            for sq in range(sub // t):
                kc_o[j * (sub // t) + sq] = kat[:, sq * t:(sq + 1) * t]
                vc_o[j * (sub // t) + sq] = vat[:, sq * t:(sq + 1) * t]
        qa_o[:, rows] = (qa * (HEAD_DIM ** -0.5 * LOG2E)).T.astype(BF16)
        ka_o[rows, :] = ka.astype(BF16)
        va_o[:, rows] = vat.astype(BF16)

    p_prev = None
    for n in range(n_sub + 1):
        p_new = project(n) if n < n_sub else None
        if n > 0:
            finish(n - 1, p_prev)
        p_prev = p_new


def _pre_mixer(x, mod3, mod_row0, tokens_per_mod, seq_len, weights, rope_tabs, emit_cache, tm,
               cast=()):
    ntok = x.shape[0]
    n1, w_in_r, w_gk, b_gk, qg, kg, bdq, bdk = weights
    rope = rope_tabs is not None
    tiles_per_mod = tokens_per_mod // tm
    tiles_per_seq = seq_len // tm
    whole = pl.BlockSpec(memory_space=pltpu.VMEM)
    row = lambda w: pl.BlockSpec((tm, w), lambda i: (i, 0))
    in_specs = [
        row(D_MODEL),
        pl.BlockSpec((1, 1, N_MOD * D_MODEL), lambda i: (mod_row0 + i // tiles_per_mod, 0, 0)),
        whole, whole, whole, whole, whole, whole, whole, whole,
    ]
    args = [x, mod3, n1, w_in_r, w_gk, b_gk, qg, kg, bdq, bdk]
    if rope:
        tab = pl.BlockSpec((tm, D_KA), lambda i: (i % tiles_per_seq, 0))
        in_specs += [tab, tab]
        args += list(rope_tabs)
    col = lambda w: pl.BlockSpec((w, tm), lambda i: (0, i))
    out_specs = [row(D_QK), row(D_QK), row(D_V), row(D_V), row(2 * D_QK),
                 col(D_QA), row(D_KA), col(D_KA)]
    out_shape = [
        jax.ShapeDtypeStruct((ntok, D_QK), BF16),
        jax.ShapeDtypeStruct((ntok, D_QK), BF16),
        jax.ShapeDtypeStruct((ntok, D_V), BF16),
        jax.ShapeDtypeStruct((ntok, D_V), BF16),
        jax.ShapeDtypeStruct((ntok, 2 * D_QK), F32),
        jax.ShapeDtypeStruct((D_QA, ntok), BF16),
        jax.ShapeDtypeStruct((ntok, D_KA), BF16),
        jax.ShapeDtypeStruct((D_KA, ntok), BF16),
    ]
    if emit_cache:
        seqs = tm // seq_len
        out_specs += [pl.BlockSpec((seqs, D_KA, seq_len), lambda i: (i, 0, 0))] * 2
        out_shape += [jax.ShapeDtypeStruct((ntok // seq_len, D_KA, seq_len), F32)] * 2
    steps = ntok // tm
    for a in cast:
        slab = pl.BlockSpec((a.shape[0] // steps, a.shape[1]), lambda i: (i, 0))
        in_specs.append(slab)
        args.append(a)
        out_specs.append(slab)
        out_shape.append(jax.ShapeDtypeStruct(a.shape, BF16))
    return pl.pallas_call(
        functools.partial(_pre_kernel, rope=rope, emit_cache=emit_cache, n_cast=len(cast)),
        grid=(ntok // tm,),
        in_specs=in_specs,
        out_specs=out_specs,
        out_shape=out_shape,
        compiler_params=pltpu.CompilerParams(
            dimension_semantics=("arbitrary",), vmem_limit_bytes=VMEM_LIMIT),
        name="pre_mixer_rope" if rope else "pre_mixer",
    )(*args)


def _gla_chunks(units, q_ref, k_ref, v_ref, g_ref, half_lo):
    C = GLA_CHUNK
    n_pair = GLA_HEADS // 2
    rows = [pl.ds(pl.multiple_of(c * C, C), C) for (_, c, _, _, _, _) in units]

    bs = []
    for (bi, _, fwd, tri, _, _), r in zip(units, rows):
        gc = g_ref[bi, r, 0:D_QK] if fwd else g_ref[bi, r, D_QK:2 * D_QK]
        ltri = jnp.where(tri, 1.0, 0.0).astype(BF16)
        g_hi = gc.astype(BF16)
        g_lo = (gc - g_hi.astype(F32)).astype(BF16)
        bs.append(_dot(ltri, g_hi) + _dot(ltri, g_lo))

    qts, kts, kes, decays = [], [], [], []
    for (bi, _, fwd, _, _, _), r, b in zip(units, rows, bs):
        qc = q_ref[bi, r, :].astype(F32) * GLA_DK ** -0.5
        kc = k_ref[bi, r, :].astype(F32)
        btot = b[C - 1:C, :] if fwd else b[0:1, :]
        qts.append((qc * jnp.exp(b)).astype(BF16))
        kts.append((kc * jnp.exp(-b)).astype(BF16))
        kes.append((kc * jnp.exp(btot - b)).astype(BF16))
        decays.append(jnp.transpose(jnp.broadcast_to(jnp.exp(btot), (8, D_QK)))[:, 0:1])

    scores, incs, qms = [], [], []
    for (bi, _, _, tri, _, _), r, qt, kt, ke in zip(units, rows, qts, kts, kes):
        for pair in range(n_pair):
            lanes = slice(pair * 128, (pair + 1) * 128)
            incs.append(_dot_tn(ke[:, lanes], v_ref[bi, r, pair * 2 * GLA_DV:(pair + 1) * 2 * GLA_DV]))
            for hh in range(2):
                qm = jnp.where(half_lo if hh == 0 else ~half_lo, qt[:, lanes], jnp.zeros((C, 128), BF16))
                qms.append(qm)
                scores.append(jnp.where(tri, _dot_nt(qm, kt[:, lanes]), 0.0).astype(BF16))

    for ui, ((bi, _, _, _, s_ref, o_ref), r) in enumerate(zip(units, rows)):
        state = s_ref[bi]
        state_b = state.astype(BF16)
        for hd in range(GLA_HEADS):
            pair = hd // 2
            lhs = jnp.concatenate([qms[ui * GLA_HEADS + hd], scores[ui * GLA_HEADS + hd]], axis=1)
            rhs = jnp.concatenate([state_b[pair * 128:(pair + 1) * 128, :],
                                   v_ref[bi, r, hd * GLA_DV:(hd + 1) * GLA_DV]], axis=0)
            o_ref[bi, r, hd * GLA_DV:(hd + 1) * GLA_DV] = _dot(lhs, rhs)
        inc = jnp.concatenate(
            [incs[ui * n_pair + hd // 2][(hd % 2) * GLA_DK:(hd % 2 + 1) * GLA_DK,
                                         (hd % 2) * GLA_DV:(hd % 2 + 1) * GLA_DV]
             for hd in range(GLA_HEADS)], axis=0)
        s_ref[bi] = decays[ui] * state + inc


def _gla_kernel(*args, zero_init, n_cast):
    q_ref, k_ref, v_ref, g_ref, gate_ref, gn_ref = args[:6]
    n_in = 6 + (0 if zero_init else 2)
    cast_src = args[n_in:n_in + n_cast]
    o_ref, sf_ref, sb_ref = args[n_in + n_cast:n_in + n_cast + 3]
    cast_dst = args[n_in + n_cast + 3:n_in + 2 * n_cast + 3]
    of_scr, ob_scr = args[-2:]
    for src, dst in zip(cast_src, cast_dst):
        dst[...] = src[...].astype(BF16)
    C = GLA_CHUNK
    nb, t, _ = q_ref.shape
    n = t // C
    row = lax.broadcasted_iota(jnp.int32, (C, C), 0)
    col = lax.broadcasted_iota(jnp.int32, (C, C), 1)
    half_lo = lax.broadcasted_iota(jnp.int32, (C, 128), 1) < GLA_DK
    refs = (q_ref, k_ref, v_ref, g_ref)
    if zero_init:
        sf_ref[...] = jnp.zeros(sf_ref.shape, F32)
        sb_ref[...] = jnp.zeros(sb_ref.shape, F32)
    else:
        sf_ref[...] = args[6][...]
        sb_ref[...] = args[7][...]

    steps = min(GLA_STEPS_PER_ITER, n)
    assert n % steps == 0

    def body(i, carry):
        units = []
        for u in range(steps):
            c = i * steps + u
            for bi in range(nb):
                units.append((bi, c, True, row >= col, sf_ref, of_scr))
                units.append((bi, n - 1 - c, False, row <= col, sb_ref, ob_scr))
        _gla_chunks(units, *refs, half_lo)
        return carry

    lax.fori_loop(0, n // steps, body, 0)

    rt = GLA_OUT_ROWS
    gn = gn_ref[...]

    def finish(i, carry):
        rows = pl.ds(pl.multiple_of(i * rt, rt), rt)
        for bi in range(nb):
            o = of_scr[bi, rows, :] + ob_scr[bi, rows, :]
            gate = gate_ref[bi, rows, :].astype(F32)
            for hd in range(GLA_HEADS):
                sl = slice(hd * GLA_DV, (hd + 1) * GLA_DV)
                o_ref[bi, rows, sl] = (_rms(o[:, sl], gn) * gate[:, sl]).astype(BF16)
        return carry

    lax.fori_loop(0, t // rt, finish, 0)


GLA_STEPS_PER_ITER = 4
GLA_OUT_ROWS = 256


def _gla(q, k, v, g, gate, gla_g, init_states, nb, cast=()):
    b, t, _ = q.shape
    seq = lambda w: pl.BlockSpec((nb, t, w), lambda i: (i, 0, 0))
    st = pl.BlockSpec((nb, D_QK, GLA_DV), lambda i: (i, 0, 0))
    zero_init = init_states is None
    steps = b // nb
    slabs = [pl.BlockSpec((a.shape[0] // steps, a.shape[1]), lambda i: (i, 0)) for a in cast]
    return pl.pallas_call(
        functools.partial(_gla_kernel, zero_init=zero_init, n_cast=len(cast)),
        grid=(b // nb,),
        in_specs=[seq(D_QK), seq(D_QK), seq(D_V), seq(2 * D_QK), seq(D_V),
                  pl.BlockSpec((1, GLA_DV), lambda i: (0, 0))] + ([] if zero_init else [st, st]) + slabs,
        out_specs=[seq(D_V), st, st] + slabs,
        out_shape=[jax.ShapeDtypeStruct((b, t, D_V), BF16),
                   jax.ShapeDtypeStruct((b, D_QK, GLA_DV), F32),
                   jax.ShapeDtypeStruct((b, D_QK, GLA_DV), F32)]
        + [jax.ShapeDtypeStruct(a.shape, BF16) for a in cast],
        scratch_shapes=[pltpu.VMEM((nb, t, D_V), F32), pltpu.VMEM((nb, t, D_V), F32)],
        compiler_params=pltpu.CompilerParams(
            dimension_semantics=("arbitrary",), vmem_limit_bytes=VMEM_LIMIT),
        name="gla",
    )(q, k, v, g, gate, gla_g, *(() if zero_init else init_states), *cast)


ATT_KEY_BLOCK = 256
ATT_Q_TILE = 256
LOG2E = 1.4426950408889634


ATT_TILES_PER_STEP = 4


def _attn_kernel(*args, keys_per_tile, cached):
    qt_ref, k_ref, vt_ref = args[:3]
    kc_ref, vtc_ref = args[3:5] if cached else (None, None)
    o_ref, s0, s1, m0, m1, a0, a1 = args[-7:]
    tq = ATT_Q_TILE
    tiles = qt_ref.shape[1] // tq
    t_new = k_ref.shape[0] // tiles if keys_per_tile else k_ref.shape[0]
    tk = t_new + (kc_ref.shape[0] if cached else 0)
    kb = min(ATT_KEY_BLOCK, tk)
    nkb = tk // kb
    zero = jnp.zeros((HEAD_DIM, tq), BF16)
    ones = jnp.ones((16, kb), BF16)
    items = [(qi, kv) for qi in range(tiles) for kv in range(ATT_KV_HEADS)]
    bufs = (s0, s1)
    mbufs = (m0, m1)
    abufs = (a0, a1)
    separate = nkb > 1

    def q_ext(qi, kv):
        cols = []
        for g in range(GROUP):
            hd = kv * GROUP + g
            qt = qt_ref[hd * HEAD_DIM:(hd + 1) * HEAD_DIM, qi * tq:(qi + 1) * tq]
            cols.append(jnp.concatenate([qt, zero] if kv == 0 else [zero, qt], axis=0))
        return jnp.concatenate(cols, axis=1)

    def key0(qi):
        return qi * tk if keys_per_tile else 0

    def score_piece(n, q_all, i, m):
        r0 = key0(items[n][0]) + i * kb
        keys = k_ref[r0:r0 + kb, :] if i * kb < t_new else kc_ref[r0 - t_new:r0 - t_new + kb, :]
        s = _dot(keys, q_all)
        bufs[n % 2][i * kb:(i + 1) * kb, :] = s
        for r in range(kb // 8):
            tile = s[r * 8:(r + 1) * 8, :]
            m = tile if m is None else jnp.maximum(m, tile)
        return m

    def value_piece(n, m, i, acc):
        qi, kv = items[n]
        c0 = key0(qi) + i * kb
        p = jnp.exp2(bufs[n % 2][i * kb:(i + 1) * kb, :] - m).astype(BF16)
        hd_rows = slice(kv * HEAD_DIM, (kv + 1) * HEAD_DIM)
        vt = vt_ref[hd_rows, c0:c0 + kb] if i * kb < t_new else vtc_ref[hd_rows, c0 - t_new:c0 - t_new + kb]
        vt1 = jnp.concatenate([vt, ones], axis=0)
        d = _dot(vt1, p)
        return d if acc is None else acc + d

    def emit(n, acc):
        qi, kv = items[n]
        o = acc[:HEAD_DIM] / acc[HEAD_DIM:HEAD_DIM + 1]
        for g in range(0, GROUP, 2):
            two = jnp.concatenate([o[:, g * tq:(g + 1) * tq], o[:, (g + 1) * tq:(g + 2) * tq]], axis=0)
            hd = kv * GROUP + g
            o_ref[qi * tq:(qi + 1) * tq, hd * HEAD_DIM:(hd + 2) * HEAD_DIM] = two.T.astype(BF16)

    def pair(n):
        if separate and n >= 2:
            emit(n - 2, abufs[n % 2][...])
        if n > len(items):
            return
        q_all = q_ext(*items[n]) if n < len(items) else None
        m_prev = mbufs[(n - 1) % 2][...] if n > 0 else None
        m_new = acc = None
        for i in range(nkb):
            if n < len(items):
                m_new = score_piece(n, q_all, i, m_new)
            if n > 0:
                acc = value_piece(n - 1, m_prev, i, acc)
        if n > 0 and separate:
            abufs[(n - 1) % 2][...] = acc
        elif n > 0:
            emit(n - 1, acc)
        if n < len(items):
            mbufs[n % 2][...] = jnp.max(m_new, axis=0, keepdims=True)

    for n in range(len(items) + (2 if separate else 1)):
        if separate:
            pl.when(pl.program_id(0) > -(n + 1))(functools.partial(pair, n))
        else:
            pair(n)


def _attention(qt, k, vt, batch, cache=None):
    ntok = qt.shape[1]
    t_new = k.shape[0] // batch
    tq = ATT_Q_TILE
    tiles = ATT_TILES_PER_STEP
    nq = ntok // batch // tq
    keys_per_tile = nq < tiles
    extra_specs, extra, tk = [], (), t_new
    if keys_per_tile:
        assert cache is None
        seqs = tiles // nq
        k_spec = pl.BlockSpec((seqs * t_new, D_KA), lambda t: (t, 0))
        vt_spec = pl.BlockSpec((D_KA, seqs * t_new), lambda t: (0, t))
    else:
        spq = nq // tiles
        k_spec = pl.BlockSpec((t_new, D_KA), lambda t: (t // spq, 0))
        vt_spec = pl.BlockSpec((D_KA, t_new), lambda t: (0, t // spq))
        if cache is not None:
            t_old = cache[0].shape[0] // batch
            assert t_new % ATT_KEY_BLOCK == 0 and t_old % ATT_KEY_BLOCK == 0
            extra_specs = [pl.BlockSpec((t_old, D_KA), lambda t: (t // spq, 0)),
                           pl.BlockSpec((D_KA, t_old), lambda t: (0, t // spq))]
            extra, tk = cache, t_new + t_old
    return pl.pallas_call(
        functools.partial(_attn_kernel, keys_per_tile=keys_per_tile, cached=cache is not None),
        grid=(ntok // (tiles * tq),),
        in_specs=[pl.BlockSpec((D_QA, tiles * tq), lambda t: (0, t)), k_spec, vt_spec] + extra_specs,
        out_specs=pl.BlockSpec((tiles * tq, D_QA), lambda t: (t, 0)),
        out_shape=jax.ShapeDtypeStruct((ntok, D_QA), BF16),
        scratch_shapes=[pltpu.VMEM((tk, GROUP * tq), F32), pltpu.VMEM((tk, GROUP * tq), F32),
                        pltpu.VMEM((1, GROUP * tq), F32), pltpu.VMEM((1, GROUP * tq), F32),
                        pltpu.VMEM((HEAD_DIM + 16, GROUP * tq), F32),
                        pltpu.VMEM((HEAD_DIM + 16, GROUP * tq), F32)],
        compiler_params=pltpu.CompilerParams(
            dimension_semantics=("arbitrary",), vmem_limit_bytes=VMEM_LIMIT),
        name="attention",
    )(qt, k, vt, *extra)


def _post_kernel(og_ref, oa_ref, x_ref, mod_ref, n2_ref, fg_ref, wo_ref, w1_ref, w3_ref, w2_ref,
                 y_ref):
    D = D_MODEL
    mod = mod_ref[0]
    g1, sh2, sc2, g2 = (mod[:, i * D:(i + 1) * D] for i in range(2, 6))
    out = _dot(og_ref[...], wo_ref[0:D_V, :]) + _dot(oa_ref[...], wo_ref[D_V:D_V + D_QA, :])
    x1 = x_ref[...] + g1 * out
    hb = (_rms(x1, n2_ref[...]) * (1.0 + sc2) + sh2).astype(BF16)
    act = (_silu(_dot(hb, w1_ref[...])) * _dot(hb, w3_ref[...])).astype(BF16)
    x2 = x1 + g2 * _dot(act, w2_ref[...])
    y_ref[...] = _rms(x2, fg_ref[...])


def _post_mixer(o_gla, o_att, x, mod3, mod_row0, tokens_per_mod, weights, tm):
    ntok = x.shape[0]
    tiles_per_mod = tokens_per_mod // tm
    whole = pl.BlockSpec(memory_space=pltpu.VMEM)
    row = lambda w: pl.BlockSpec((tm, w), lambda i: (i, 0))
    return pl.pallas_call(
        _post_kernel,
        grid=(ntok // tm,),
        in_specs=[row(D_V), row(D_QA), row(D_MODEL),
                  pl.BlockSpec((1, 1, N_MOD * D_MODEL), lambda i: (mod_row0 + i // tiles_per_mod, 0, 0)),
                  whole, whole, whole, whole, whole, whole],
        out_specs=row(D_MODEL),
        out_shape=jax.ShapeDtypeStruct((ntok, D_MODEL), F32),
        compiler_params=pltpu.CompilerParams(
            dimension_semantics=("arbitrary",), vmem_limit_bytes=VMEM_LIMIT),
        name="post_mixer",
    )(o_gla, o_att, x, mod3, *weights)


def _rope_tables(t):
    rows = t // GRID_W
    row = np.repeat(np.arange(rows, dtype=np.float64), GRID_W)
    col = np.tile(np.arange(GRID_W, dtype=np.float64), rows)
    inv = ROPE_THETA ** (-np.arange(0, ROPE_AXIS_DIM, 2, dtype=np.float64) / ROPE_AXIS_DIM)
    ang_r = row[:, None] * inv
    ang_c = col[:, None] * inv
    cos = np.concatenate([np.cos(ang_r)] * 2 + [np.cos(ang_c)] * 2, axis=-1)
    sin = np.concatenate([-np.sin(ang_r), np.sin(ang_r), -np.sin(ang_c), np.sin(ang_c)], axis=-1)
    tile = lambda a: jnp.asarray(np.tile(a, (1, ATT_KV_HEADS)), F32)
    return tile(cos), tile(sin)


def _block_mean(width, group):
    return jnp.asarray(np.kron(np.eye(width // group), np.ones((group, group))) / group, BF16)


def kernel(x_prompt, x_sample, cache_k, cache_v, state_gla_fwd, state_gla_bwd, c, c_ctx,
           ada_w, ada_b, norm1_g, norm2_g, w_in, w_gk2, b_gk2, gla_norm_g, q_norm_g, k_norm_g,
           w_out, w_ffn1, w_ffn3, w_ffn2, final_g):
    B, T, D = x_prompt.shape
    BL, TL, _ = x_sample.shape
    TP = cache_k.shape[2]
    l = 0

    mod3, w_in_r = _adaln(c_ctx, c, ada_w[l], ada_b[l], jnp.transpose(w_in[l]))
    zero = jnp.zeros((GLA_LOWRANK, D_QK), F32)
    w_gk = jnp.concatenate([jnp.concatenate([w_gk2[l, 0], zero], axis=1),
                            jnp.concatenate([zero, w_gk2[l, 1]], axis=1)], axis=0).astype(BF16)
    pre_w = (norm1_g[l].reshape(1, D), w_in_r, w_gk, b_gk2[l].reshape(1, 2 * D_QK),
             jnp.tile(q_norm_g[l], ATT_HEADS).reshape(1, D_QA),
             jnp.tile(k_norm_g[l], ATT_KV_HEADS).reshape(1, D_KA),
             _block_mean(D_QA, HEAD_DIM), _block_mean(D_KA, HEAD_DIM))
    post_norms = (norm2_g[l].reshape(1, D), final_g.reshape(1, D))
    post_w_f32 = (w_out[l], w_ffn1[l], w_ffn3[l], w_ffn2[l])
    gla_g = gla_norm_g[l].reshape(1, GLA_DV)
    tm_pre = 1024
    tm_post = 512

    xc = x_prompt.reshape(B * T, D)
    q, k, v, gate, g, qa, ka, va, kc, vc = _pre_mixer(
        xc, mod3, 0, B * T, T, pre_w, None, True, tm_pre)
    r3 = lambda a, b, t: a.reshape(b, t, a.shape[-1])
    o_gla, s_f, s_b, *post_w_bf16 = _gla(
        r3(q, B, T), r3(k, B, T), r3(v, B, T), r3(g, B, T), r3(gate, B, T), gla_g, None, 4,
        cast=post_w_f32)
    post_w = post_norms + tuple(post_w_bf16)
    o_att = _attention(qa, ka, va, B)
    y_prompt = _post_mixer(o_gla.reshape(B * T, D_V), o_att, xc, mod3, 0, B * T, post_w, tm_post)

    xl = x_sample.reshape(BL * TL, D)
    q, k, v, gate, g, qa, ka, va = _pre_mixer(
        xl, mod3, 1, TL, TL, pre_w, _rope_tables(TL), False, tm_pre)
    o_gla, _, _ = _gla(r3(q, BL, TL), r3(k, BL, TL), r3(v, BL, TL), r3(g, BL, TL), r3(gate, BL, TL),
                       gla_g, (state_gla_fwd[:, l].reshape(BL, D_QK, GLA_DV),
                               state_gla_bwd[:, l].reshape(BL, D_QK, GLA_DV)), BL)

    k_old = cache_k[:, l].reshape(BL * TP, D_KA).astype(BF16)
    vt_old = jnp.transpose(cache_v[:, l], (2, 3, 0, 1)).reshape(D_KA, BL * TP).astype(BF16)
    o_att = _attention(qa, ka, va, BL, (k_old, vt_old))
    y_sample = _post_mixer(o_gla.reshape(BL * TL, D_V), o_att, xl, mod3, 1, TL, post_w, tm_post)

    def cache_out(ct):
        return jnp.transpose(ct.reshape(B, 1, ATT_KV_HEADS, HEAD_DIM, T), (0, 1, 4, 2, 3))

    return (y_prompt.reshape(B, T, D), y_sample.reshape(BL, TL, D), cache_out(kc), cache_out(vc),
            s_f.reshape(B, 1, GLA_HEADS, GLA_DK, GLA_DV), s_b.reshape(B, 1, GLA_HEADS, GLA_DK, GLA_DV))
```

```python
import functools

import numpy as np
import jax
import jax.numpy as jnp
from jax import lax
from jax.experimental import pallas as pl
from jax.experimental.pallas import tpu as pltpu

F32 = jnp.float32
BF16 = jnp.bfloat16

D_MODEL = 1024
GRID_W = 64
GLA_HEADS = 4
GLA_DK = 64
GLA_DV = 128
GLA_LOWRANK = 16
GLA_GATE_NORM = 16.0
GLA_CHUNK = 64
ATT_HEADS = 8
ATT_KV_HEADS = 2
HEAD_DIM = 64
ROPE_AXIS_DIM = HEAD_DIM // 2
ROPE_THETA = 10000.0
D_FF = -(-8 * D_MODEL // (3 * 256)) * 256
N_MOD = 6
EPS = 1e-6

D_QK = GLA_HEADS * GLA_DK
D_V = GLA_HEADS * GLA_DV
D_QA = ATT_HEADS * HEAD_DIM
D_KA = ATT_KV_HEADS * HEAD_DIM
GROUP = ATT_HEADS // ATT_KV_HEADS
O_Q, O_K, O_V, O_OG = 0, D_QK, 2 * D_QK, 2 * D_QK + D_V
O_QA = O_OG + D_V
O_KA = O_QA + D_QA
O_VA = O_KA + D_KA
O_LR = O_VA + D_KA
D_PROJ = O_LR + 2 * GLA_LOWRANK

COND_ROWS = 8
VMEM_LIMIT = 56 * 1024 * 1024


def _dot(a, b):
    return jnp.dot(a, b, preferred_element_type=F32)


def _dot_nt(a, b):
    return lax.dot_general(a, b, (((1,), (1,)), ((), ())), preferred_element_type=F32)


def _dot_tn(a, b):
    return lax.dot_general(a, b, (((0,), (0,)), ((), ())), preferred_element_type=F32)


def _rms(x, g):
    ms = jnp.mean(x * x, axis=-1, keepdims=True)
    return x * lax.rsqrt(ms + EPS) * g


def _silu(x):
    return x * jax.nn.sigmoid(x)


ADALN_K_BLOCK = 256


def _adaln_kernel(cctx_ref, c_ref, w_ref, b_ref, wint_ref, o_ref, win_o, acc):
    k = pl.program_id(0)

    lr0, lr1 = O_OG + D_V, O_OG + D_V + 2 * GLA_LOWRANK
    feats = jnp.concatenate([wint_ref[0:lr0, :], wint_ref[lr1:D_PROJ, :], wint_ref[lr0:lr1, :]],
                            axis=0)
    for j in range(0, D_PROJ, 128):
        rows = min(128, D_PROJ - j)
        tile = feats[j:j + rows, :]
        if rows < 128:
            tile = jnp.concatenate([tile, jnp.zeros((128 - rows, tile.shape[1]), F32)], axis=0)
        win_o[:, j:j + rows] = tile.T[:, 0:rows].astype(BF16)

    @pl.when(k == 0)
    def _():
        acc[...] = jnp.broadcast_to(b_ref[...], acc.shape)

    pad = jnp.zeros((COND_ROWS - 1 - c_ref.shape[0], c_ref.shape[1]), F32)
    cond = jnp.concatenate([cctx_ref[...], c_ref[...], pad], axis=0)
    acc[...] += _dot(_silu(cond).astype(BF16), w_ref[...].astype(BF16))

    @pl.when(k == pl.num_programs(0) - 1)
    def _():
        o_ref[:, 0, :] = acc[...]


def _adaln(c_ctx, c, ada_w, ada_b, w_in_t):
    d, n = ada_w.shape
    tk = ADALN_K_BLOCK
    assert 1 + c.shape[0] <= COND_ROWS and w_in_t.shape == (D_PROJ, d)
    return pl.pallas_call(
        _adaln_kernel,
        grid=(d // tk,),
        in_specs=[
            pl.BlockSpec((1, tk), lambda k: (0, k)),
            pl.BlockSpec((c.shape[0], tk), lambda k: (0, k)),
            pl.BlockSpec((tk, n), lambda k: (k, 0)),
            pl.BlockSpec((1, n), lambda k: (0, 0)),
            pl.BlockSpec((D_PROJ, tk), lambda k: (0, k)),
        ],
        out_specs=[pl.BlockSpec((COND_ROWS, 1, n), lambda k: (0, 0, 0)),
                   pl.BlockSpec((tk, D_PROJ), lambda k: (k, 0))],
        out_shape=[jax.ShapeDtypeStruct((COND_ROWS, 1, n), F32),
                   jax.ShapeDtypeStruct((d, D_PROJ), BF16)],
        scratch_shapes=[pltpu.VMEM((COND_ROWS, n), F32)],
        compiler_params=pltpu.CompilerParams(
            dimension_semantics=("arbitrary",), vmem_limit_bytes=VMEM_LIMIT),
        name="adaln",
    )(c_ctx.reshape(1, d), c, ada_w, ada_b.reshape(1, n), w_in_t)


def _swap_halves(x, lane_lo):
    n = x.shape[-1]
    up = pltpu.roll(x, n - ROPE_AXIS_DIM // 2, axis=1)
    dn = pltpu.roll(x, ROPE_AXIS_DIM // 2, axis=1)
    return jnp.where(lane_lo, up, dn)


PRE_SUB_ROWS = 256


def _pre_kernel(*refs, rope, emit_cache, n_cast):
    it = iter(refs)
    x_ref, mod_ref, n1_ref, win_ref, wgk_ref, bgk_ref = (next(it) for _ in range(6))
    qg_ref, kg_ref, bdq_ref, bdk_ref = (next(it) for _ in range(4))
    if rope:
        cos_ref, sin_ref = next(it), next(it)
    cast_src = [next(it) for _ in range(n_cast)]
    q_o, k_o, v_o, gate_o, g_o, qa_o, ka_o, va_o = (next(it) for _ in range(8))
    if emit_cache:
        kc_o, vc_o = next(it), next(it)
    for src in cast_src:
        next(it)[...] = src[...].astype(BF16)

    sub = PRE_SUB_ROWS
    n_sub = x_ref.shape[0] // sub
    mod = mod_ref[0]
    sh1 = mod[:, 0:D_MODEL]
    sc1 = mod[:, D_MODEL:2 * D_MODEL]

    def project(j):
        rows = slice(j * sub, (j + 1) * sub)
        h = _rms(x_ref[rows, :], n1_ref[...]) * (1.0 + sc1) + sh1
        return _dot(h.astype(BF16), win_ref[...])

    def finish(j, p_ref):
        rows = slice(j * sub, (j + 1) * sub)
        q_o[rows, :] = p_ref[:, O_Q:O_Q + D_QK].astype(BF16)
        k_o[rows, :] = p_ref[:, O_K:O_K + D_QK].astype(BF16)
        v_o[rows, :] = p_ref[:, O_V:O_V + D_V].astype(BF16)
        gate_o[rows, :] = _silu(p_ref[:, O_OG:O_OG + D_V]).astype(BF16)

        gk = _dot(p_ref[:, O_LR:O_LR + 2 * GLA_LOWRANK].astype(BF16), wgk_ref[...]) + bgk_ref[...]
        g_o[rows, :] = ((jnp.minimum(gk, 0.0) - jnp.log(1.0 + jnp.exp(-jnp.abs(gk))))
                        * (1.0 / GLA_GATE_NORM))

        qa = p_ref[:, O_QA:O_QA + D_QA]
        ka = p_ref[:, O_KA:O_KA + D_KA]
        va = p_ref[:, O_VA:O_VA + D_KA]
        qa = qa * lax.rsqrt(_dot((qa * qa).astype(BF16), bdq_ref[...]) + EPS) * qg_ref[...]
        ka = ka * lax.rsqrt(_dot((ka * ka).astype(BF16), bdk_ref[...]) + EPS) * kg_ref[...]
        if rope:
            lane_q = lax.broadcasted_iota(jnp.int32, qa.shape, 1)
            lane_k = lax.broadcasted_iota(jnp.int32, ka.shape, 1)
            lo_q = (lane_q % ROPE_AXIS_DIM) < ROPE_AXIS_DIM // 2
            lo_k = (lane_k % ROPE_AXIS_DIM) < ROPE_AXIS_DIM // 2
            cos_k = cos_ref[rows, :]
            sin_k = sin_ref[rows, :]
            cos_q = jnp.concatenate([cos_k] * GROUP, axis=1)
            sin_q = jnp.concatenate([sin_k] * GROUP, axis=1)
            qa = qa * cos_q + _swap_halves(qa, lo_q) * sin_q
            ka = ka * cos_k + _swap_halves(ka, lo_k) * sin_k
        vat = va.T
        if emit_cache:
            kat = ka.T
            t = kc_o.shape[2]
            for sq in range(sub // t):
                kc_o[j * (sub // t) + sq] = kat[:, sq * t:(sq + 1) * t]
                vc_o[j * (sub // t) + sq] = vat[:, sq * t:(sq + 1) * t]
        qa_o[:, rows] = (qa * (HEAD_DIM ** -0.5 * LOG2E)).T.astype(BF16)
        ka_o[rows, :] = ka.astype(BF16)
        va_o[:, rows] = vat.astype(BF16)

    p_prev = None
    for n in range(n_sub + 1):
        p_new = project(n) if n < n_sub else None
        if n > 0:
            finish(n - 1, p_prev)
        p_prev = p_new


def _pre_mixer(x, mod3, mod_row0, tokens_per_mod, seq_len, weights, rope_tabs, emit_cache, tm,
               cast=()):
    ntok = x.shape[0]
    n1, w_in_r, w_gk, b_gk, qg, kg, bdq, bdk = weights
    rope = rope_tabs is not None
    tiles_per_mod = tokens_per_mod // tm
    tiles_per_seq = seq_len // tm
    whole = pl.BlockSpec(memory_space=pltpu.VMEM)
    row = lambda w: pl.BlockSpec((tm, w), lambda i: (i, 0))
    in_specs = [
        row(D_MODEL),
        pl.BlockSpec((1, 1, N_MOD * D_MODEL), lambda i: (mod_row0 + i // tiles_per_mod, 0, 0)),
        whole, whole, whole, whole, whole, whole, whole, whole,
    ]
    args = [x, mod3, n1, w_in_r, w_gk, b_gk, qg, kg, bdq, bdk]
    if rope:
        tab = pl.BlockSpec((tm, D_KA), lambda i: (i % tiles_per_seq, 0))
        in_specs += [tab, tab]
        args += list(rope_tabs)
    col = lambda w: pl.BlockSpec((w, tm), lambda i: (0, i))
    out_specs = [row(D_QK), row(D_QK), row(D_V), row(D_V), row(2 * D_QK),
                 col(D_QA), row(D_KA), col(D_KA)]
    out_shape = [
        jax.ShapeDtypeStruct((ntok, D_QK), BF16),
        jax.ShapeDtypeStruct((ntok, D_QK), BF16),
        jax.ShapeDtypeStruct((ntok, D_V), BF16),
        jax.ShapeDtypeStruct((ntok, D_V), BF16),
        jax.ShapeDtypeStruct((ntok, 2 * D_QK), F32),
        jax.ShapeDtypeStruct((D_QA, ntok), BF16),
        jax.ShapeDtypeStruct((ntok, D_KA), BF16),
        jax.ShapeDtypeStruct((D_KA, ntok), BF16),
    ]
    if emit_cache:
        seqs = tm // seq_len
        out_specs += [pl.BlockSpec((seqs, D_KA, seq_len), lambda i: (i, 0, 0))] * 2
        out_shape += [jax.ShapeDtypeStruct((ntok // seq_len, D_KA, seq_len), F32)] * 2
    steps = ntok // tm
    for a in cast:
        slab = pl.BlockSpec((a.shape[0] // steps, a.shape[1]), lambda i: (i, 0))
        in_specs.append(slab)
        args.append(a)
        out_specs.append(slab)
        out_shape.append(jax.ShapeDtypeStruct(a.shape, BF16))
    return pl.pallas_call(
        functools.partial(_pre_kernel, rope=rope, emit_cache=emit_cache, n_cast=len(cast)),
        grid=(ntok // tm,),
        in_specs=in_specs,
        out_specs=out_specs,
        out_shape=out_shape,
        compiler_params=pltpu.CompilerParams(
            dimension_semantics=("arbitrary",), vmem_limit_bytes=VMEM_LIMIT),
        name="pre_mixer_rope" if rope else "pre_mixer",
    )(*args)


def _gla_chunks(units, q_ref, k_ref, v_ref, g_ref, half_lo):
    C = GLA_CHUNK
    n_pair = GLA_HEADS // 2
    rows = [pl.ds(pl.multiple_of(c * C, C), C) for (_, c, _, _, _, _) in units]

    bs = []
    for (bi, _, fwd, tri, _, _), r in zip(units, rows):
        gc = g_ref[bi, r, 0:D_QK] if fwd else g_ref[bi, r, D_QK:2 * D_QK]
        ltri = jnp.where(tri, 1.0, 0.0).astype(BF16)
        g_hi = gc.astype(BF16)
        g_lo = (gc - g_hi.astype(F32)).astype(BF16)
        bs.append(_dot(ltri, g_hi) + _dot(ltri, g_lo))

    qts, kts, kes, decays = [], [], [], []
    for (bi, _, fwd, _, _, _), r, b in zip(units, rows, bs):
        qc = q_ref[bi, r, :].astype(F32) * GLA_DK ** -0.5
        kc = k_ref[bi, r, :].astype(F32)
        btot = b[C - 1:C, :] if fwd else b[0:1, :]
        qts.append((qc * jnp.exp(b)).astype(BF16))
        kts.append((kc * jnp.exp(-b)).astype(BF16))
        kes.append((kc * jnp.exp(btot - b)).astype(BF16))
        decays.append(jnp.transpose(jnp.broadcast_to(jnp.exp(btot), (8, D_QK)))[:, 0:1])

    scores, incs, qms = [], [], []
    for (bi, _, _, tri, _, _), r, qt, kt, ke in zip(units, rows, qts, kts, kes):
        for pair in range(n_pair):
            lanes = slice(pair * 128, (pair + 1) * 128)
            incs.append(_dot_tn(ke[:, lanes], v_ref[bi, r, pair * 2 * GLA_DV:(pair + 1) * 2 * GLA_DV]))
            for hh in range(2):
                qm = jnp.where(half_lo if hh == 0 else ~half_lo, qt[:, lanes], jnp.zeros((C, 128), BF16))
                qms.append(qm)
                scores.append(jnp.where(tri, _dot_nt(qm, kt[:, lanes]), 0.0).astype(BF16))

    for ui, ((bi, _, _, _, s_ref, o_ref), r) in enumerate(zip(units, rows)):
        state = s_ref[bi]
        state_b = state.astype(BF16)
        for hd in range(GLA_HEADS):
            pair = hd // 2
            lhs = jnp.concatenate([qms[ui * GLA_HEADS + hd], scores[ui * GLA_HEADS + hd]], axis=1)
            rhs = jnp.concatenate([state_b[pair * 128:(pair + 1) * 128, :],
                                   v_ref[bi, r, hd * GLA_DV:(hd + 1) * GLA_DV]], axis=0)
            o_ref[bi, r, hd * GLA_DV:(hd + 1) * GLA_DV] = _dot(lhs, rhs)
        inc = jnp.concatenate(
            [incs[ui * n_pair + hd // 2][(hd % 2) * GLA_DK:(hd % 2 + 1) * GLA_DK,
                                         (hd % 2) * GLA_DV:(hd % 2 + 1) * GLA_DV]
             for hd in range(GLA_HEADS)], axis=0)
        s_ref[bi] = decays[ui] * state + inc


def _gla_kernel(*args, zero_init, n_cast):
    q_ref, k_ref, v_ref, g_ref, gate_ref, gn_ref = args[:6]
    n_in = 6 + (0 if zero_init else 2)
    cast_src = args[n_in:n_in + n_cast]
    o_ref, sf_ref, sb_ref = args[n_in + n_cast:n_in + n_cast + 3]
    cast_dst = args[n_in + n_cast + 3:n_in + 2 * n_cast + 3]
    of_scr, ob_scr = args[-2:]
    for src, dst in zip(cast_src, cast_dst):
        dst[...] = src[...].astype(BF16)
    C = GLA_CHUNK
    nb, t, _ = q_ref.shape
    n = t // C
    row = lax.broadcasted_iota(jnp.int32, (C, C), 0)
    col = lax.broadcasted_iota(jnp.int32, (C, C), 1)
    half_lo = lax.broadcasted_iota(jnp.int32, (C, 128), 1) < GLA_DK
    refs = (q_ref, k_ref, v_ref, g_ref)
    if zero_init:
        sf_ref[...] = jnp.zeros(sf_ref.shape, F32)
        sb_ref[...] = jnp.zeros(sb_ref.shape, F32)
    else:
        sf_ref[...] = args[6][...]
        sb_ref[...] = args[7][...]

    steps = min(GLA_STEPS_PER_ITER, n)
    assert n % steps == 0

    def body(i, carry):
        units = []
        for u in range(steps):
            c = i * steps + u
            for bi in range(nb):
                units.append((bi, c, True, row >= col, sf_ref, of_scr))
                units.append((bi, n - 1 - c, False, row <= col, sb_ref, ob_scr))
        _gla_chunks(units, *refs, half_lo)
        return carry

    lax.fori_loop(0, n // steps, body, 0)

    rt = GLA_OUT_ROWS
    gn = gn_ref[...]

    def finish(i, carry):
        rows = pl.ds(pl.multiple_of(i * rt, rt), rt)
        for bi in range(nb):
            o = of_scr[bi, rows, :] + ob_scr[bi, rows, :]
            gate = gate_ref[bi, rows, :].astype(F32)
            for hd in range(GLA_HEADS):
                sl = slice(hd * GLA_DV, (hd + 1) * GLA_DV)
                o_ref[bi, rows, sl] = (_rms(o[:, sl], gn) * gate[:, sl]).astype(BF16)
        return carry

    lax.fori_loop(0, t // rt, finish, 0)


GLA_STEPS_PER_ITER = 4
GLA_OUT_ROWS = 256


def _gla(q, k, v, g, gate, gla_g, init_states, nb, cast=()):
    b, t, _ = q.shape
    seq = lambda w: pl.BlockSpec((nb, t, w), lambda i: (i, 0, 0))
    st = pl.BlockSpec((nb, D_QK, GLA_DV), lambda i: (i, 0, 0))
    zero_init = init_states is None
    steps = b // nb
    slabs = [pl.BlockSpec((a.shape[0] // steps, a.shape[1]), lambda i: (i, 0)) for a in cast]
    return pl.pallas_call(
        functools.partial(_gla_kernel, zero_init=zero_init, n_cast=len(cast)),
        grid=(b // nb,),
        in_specs=[seq(D_QK), seq(D_QK), seq(D_V), seq(2 * D_QK), seq(D_V),
                  pl.BlockSpec((1, GLA_DV), lambda i: (0, 0))] + ([] if zero_init else [st, st]) + slabs,
        out_specs=[seq(D_V), st, st] + slabs,
        out_shape=[jax.ShapeDtypeStruct((b, t, D_V), BF16),
                   jax.ShapeDtypeStruct((b, D_QK, GLA_DV), F32),
                   jax.ShapeDtypeStruct((b, D_QK, GLA_DV), F32)]
        + [jax.ShapeDtypeStruct(a.shape, BF16) for a in cast],
        scratch_shapes=[pltpu.VMEM((nb, t, D_V), F32), pltpu.VMEM((nb, t, D_V), F32)],
        compiler_params=pltpu.CompilerParams(
            dimension_semantics=("arbitrary",), vmem_limit_bytes=VMEM_LIMIT),
        name="gla",
    )(q, k, v, g, gate, gla_g, *(() if zero_init else init_states), *cast)


ATT_KEY_BLOCK = 256
ATT_Q_TILE = 256
LOG2E = 1.4426950408889634


ATT_TILES_PER_STEP = 2


def _attn_kernel(*args, keys_per_tile, cached):
    qt_ref, k_ref, vt_ref = args[:3]
    kc_ref, vtc_ref = args[3:5] if cached else (None, None)
    o_ref, s0, s1, m0, m1, a0, a1 = args[-7:]
    tq = ATT_Q_TILE
    tiles = qt_ref.shape[1] // tq
    t_new = k_ref.shape[0] // tiles if keys_per_tile else k_ref.shape[0]
    tk = t_new + (kc_ref.shape[0] if cached else 0)
    kb = min(ATT_KEY_BLOCK, tk)
    nkb = tk // kb
    zero = jnp.zeros((HEAD_DIM, tq), BF16)
    ones = jnp.ones((16, kb), BF16)
    items = [(qi, kv) for qi in range(tiles) for kv in range(ATT_KV_HEADS)]
    bufs = (s0, s1)
    mbufs = (m0, m1)
    abufs = (a0, a1)
    separate = nkb > 1

    def q_ext(qi, kv):
        cols = []
        for g in range(GROUP):
            hd = kv * GROUP + g
            qt = qt_ref[hd * HEAD_DIM:(hd + 1) * HEAD_DIM, qi * tq:(qi + 1) * tq]
            cols.append(jnp.concatenate([qt, zero] if kv == 0 else [zero, qt], axis=0))
        return jnp.concatenate(cols, axis=1)

    def key0(qi):
        return qi * tk if keys_per_tile else 0

    def score_piece(n, q_all, i, m):
        r0 = key0(items[n][0]) + i * kb
        keys = k_ref[r0:r0 + kb, :] if i * kb < t_new else kc_ref[r0 - t_new:r0 - t_new + kb, :]
        s = _dot(keys, q_all)
        bufs[n % 2][i * kb:(i + 1) * kb, :] = s
        for r in range(kb // 8):
            tile = s[r * 8:(r + 1) * 8, :]
            m = tile if m is None else jnp.maximum(m, tile)
        return m

    def value_piece(n, m, i, acc):
        qi, kv = items[n]
        c0 = key0(qi) + i * kb
        p = jnp.exp2(bufs[n % 2][i * kb:(i + 1) * kb, :] - m).astype(BF16)
        hd_rows = slice(kv * HEAD_DIM, (kv + 1) * HEAD_DIM)
        vt = vt_ref[hd_rows, c0:c0 + kb] if i * kb < t_new else vtc_ref[hd_rows, c0 - t_new:c0 - t_new + kb]
        vt1 = jnp.concatenate([vt, ones], axis=0)
        d = _dot(vt1, p)
        return d if acc is None else acc + d

    def emit(n, acc):
        qi, kv = items[n]
        o = acc[:HEAD_DIM] / acc[HEAD_DIM:HEAD_DIM + 1]
        for g in range(0, GROUP, 2):
            two = jnp.concatenate([o[:, g * tq:(g + 1) * tq], o[:, (g + 1) * tq:(g + 2) * tq]], axis=0)
            hd = kv * GROUP + g
            o_ref[qi * tq:(qi + 1) * tq, hd * HEAD_DIM:(hd + 2) * HEAD_DIM] = two.T.astype(BF16)

    def pair(n):
        if separate and n >= 2:
            emit(n - 2, abufs[n % 2][...])
        if n > len(items):
            return
        q_all = q_ext(*items[n]) if n < len(items) else None
        m_prev = mbufs[(n - 1) % 2][...] if n > 0 else None
        m_new = acc = None
        for i in range(nkb):
            if n < len(items):
                m_new = score_piece(n, q_all, i, m_new)
            if n > 0:
                acc = value_piece(n - 1, m_prev, i, acc)
        if n > 0 and separate:
            abufs[(n - 1) % 2][...] = acc
        elif n > 0:
            emit(n - 1, acc)
        if n < len(items):
            mbufs[n % 2][...] = jnp.max(m_new, axis=0, keepdims=True)

    for n in range(len(items) + (2 if separate else 1)):
        if separate:
            pl.when(pl.program_id(0) > -(n + 1))(functools.partial(pair, n))
        else:
            pair(n)


def _attention(qt, k, vt, batch, cache=None):
    ntok = qt.shape[1]
    t_new = k.shape[0] // batch
    tq = ATT_Q_TILE
    tiles = ATT_TILES_PER_STEP
    nq = ntok // batch // tq
    keys_per_tile = nq < tiles
    extra_specs, extra, tk = [], (), t_new
    if keys_per_tile:
        assert cache is None
        seqs = tiles // nq
        k_spec = pl.BlockSpec((seqs * t_new, D_KA), lambda t: (t, 0))
        vt_spec = pl.BlockSpec((D_KA, seqs * t_new), lambda t: (0, t))
    else:
        spq = nq // tiles
        k_spec = pl.BlockSpec((t_new, D_KA), lambda t: (t // spq, 0))
        vt_spec = pl.BlockSpec((D_KA, t_new), lambda t: (0, t // spq))
        if cache is not None:
            t_old = cache[0].shape[0] // batch
            assert t_new % ATT_KEY_BLOCK == 0 and t_old % ATT_KEY_BLOCK == 0
            extra_specs = [pl.BlockSpec((t_old, D_KA), lambda t: (t // spq, 0)),
                           pl.BlockSpec((D_KA, t_old), lambda t: (0, t // spq))]
            extra, tk = cache, t_new + t_old
    return pl.pallas_call(
        functools.partial(_attn_kernel, keys_per_tile=keys_per_tile, cached=cache is not None),
        grid=(ntok // (tiles * tq),),
        in_specs=[pl.BlockSpec((D_QA, tiles * tq), lambda t: (0, t)), k_spec, vt_spec] + extra_specs,
        out_specs=pl.BlockSpec((tiles * tq, D_QA), lambda t: (t, 0)),
        out_shape=jax.ShapeDtypeStruct((ntok, D_QA), BF16),
        scratch_shapes=[pltpu.VMEM((tk, GROUP * tq), F32), pltpu.VMEM((tk, GROUP * tq), F32),
                        pltpu.VMEM((1, GROUP * tq), F32), pltpu.VMEM((1, GROUP * tq), F32),
                        pltpu.VMEM((HEAD_DIM + 16, GROUP * tq), F32),
                        pltpu.VMEM((HEAD_DIM + 16, GROUP * tq), F32)],
        compiler_params=pltpu.CompilerParams(
            dimension_semantics=("arbitrary",), vmem_limit_bytes=VMEM_LIMIT),
        name="attention",
    )(qt, k, vt, *extra)


def _post_kernel(og_ref, oa_ref, x_ref, mod_ref, n2_ref, fg_ref, wo_ref, w1_ref, w3_ref, w2_ref,
                 y_ref):
    D = D_MODEL
    mod = mod_ref[0]
    g1, sh2, sc2, g2 = (mod[:, i * D:(i + 1) * D] for i in range(2, 6))
    out = _dot(og_ref[...], wo_ref[0:D_V, :]) + _dot(oa_ref[...], wo_ref[D_V:D_V + D_QA, :])
    x1 = x_ref[...] + g1 * out
    hb = (_rms(x1, n2_ref[...]) * (1.0 + sc2) + sh2).astype(BF16)
    act = (_silu(_dot(hb, w1_ref[...])) * _dot(hb, w3_ref[...])).astype(BF16)
    x2 = x1 + g2 * _dot(act, w2_ref[...])
    y_ref[...] = _rms(x2, fg_ref[...])


def _post_mixer(o_gla, o_att, x, mod3, mod_row0, tokens_per_mod, weights, tm):
    ntok = x.shape[0]
    tiles_per_mod = tokens_per_mod // tm
    whole = pl.BlockSpec(memory_space=pltpu.VMEM)
    row = lambda w: pl.BlockSpec((tm, w), lambda i: (i, 0))
    return pl.pallas_call(
        _post_kernel,
        grid=(ntok // tm,),
        in_specs=[row(D_V), row(D_QA), row(D_MODEL),
                  pl.BlockSpec((1, 1, N_MOD * D_MODEL), lambda i: (mod_row0 + i // tiles_per_mod, 0, 0)),
                  whole, whole, whole, whole, whole, whole],
        out_specs=row(D_MODEL),
        out_shape=jax.ShapeDtypeStruct((ntok, D_MODEL), F32),
        compiler_params=pltpu.CompilerParams(
            dimension_semantics=("arbitrary",), vmem_limit_bytes=VMEM_LIMIT),
        name="post_mixer",
    )(o_gla, o_att, x, mod3, *weights)


def _rope_tables(t):
    rows = t // GRID_W
    row = np.repeat(np.arange(rows, dtype=np.float64), GRID_W)
    col = np.tile(np.arange(GRID_W, dtype=np.float64), rows)
    inv = ROPE_THETA ** (-np.arange(0, ROPE_AXIS_DIM, 2, dtype=np.float64) / ROPE_AXIS_DIM)
    ang_r = row[:, None] * inv
    ang_c = col[:, None] * inv
    cos = np.concatenate([np.cos(ang_r)] * 2 + [np.cos(ang_c)] * 2, axis=-1)
    sin = np.concatenate([-np.sin(ang_r), np.sin(ang_r), -np.sin(ang_c), np.sin(ang_c)], axis=-1)
    tile = lambda a: jnp.asarray(np.tile(a, (1, ATT_KV_HEADS)), F32)
    return tile(cos), tile(sin)


def _block_mean(width, group):
    return jnp.asarray(np.kron(np.eye(width // group), np.ones((group, group))) / group, BF16)


def kernel(x_prompt, x_sample, cache_k, cache_v, state_gla_fwd, state_gla_bwd, c, c_ctx,
           ada_w, ada_b, norm1_g, norm2_g, w_in, w_gk2, b_gk2, gla_norm_g, q_norm_g, k_norm_g,
           w_out, w_ffn1, w_ffn3, w_ffn2, final_g):
    B, T, D = x_prompt.shape
    BL, TL, _ = x_sample.shape
    TP = cache_k.shape[2]
    l = 0

    mod3, w_in_r = _adaln(c_ctx, c, ada_w[l], ada_b[l], jnp.transpose(w_in[l]))
    zero = jnp.zeros((GLA_LOWRANK, D_QK), F32)
    w_gk = jnp.concatenate([jnp.concatenate([w_gk2[l, 0], zero], axis=1),
                            jnp.concatenate([zero, w_gk2[l, 1]], axis=1)], axis=0).astype(BF16)
    pre_w = (norm1_g[l].reshape(1, D), w_in_r, w_gk, b_gk2[l].reshape(1, 2 * D_QK),
             jnp.tile(q_norm_g[l], ATT_HEADS).reshape(1, D_QA),
             jnp.tile(k_norm_g[l], ATT_KV_HEADS).reshape(1, D_KA),
             _block_mean(D_QA, HEAD_DIM), _block_mean(D_KA, HEAD_DIM))
    post_norms = (norm2_g[l].reshape(1, D), final_g.reshape(1, D))
    post_w_f32 = (w_out[l], w_ffn1[l], w_ffn3[l], w_ffn2[l])
    gla_g = gla_norm_g[l].reshape(1, GLA_DV)
    tm_pre = 1024
    tm_post = 512

    xc = x_prompt.reshape(B * T, D)
    q, k, v, gate, g, qa, ka, va, kc, vc = _pre_mixer(
        xc, mod3, 0, B * T, T, pre_w, None, True, tm_pre)
    r3 = lambda a, b, t: a.reshape(b, t, a.shape[-1])
    o_gla, s_f, s_b, *post_w_bf16 = _gla(
        r3(q, B, T), r3(k, B, T), r3(v, B, T), r3(g, B, T), r3(gate, B, T), gla_g, None, 4,
        cast=post_w_f32)
    post_w = post_norms + tuple(post_w_bf16)
    o_att = _attention(qa, ka, va, B)
    y_prompt = _post_mixer(o_gla.reshape(B * T, D_V), o_att, xc, mod3, 0, B * T, post_w, tm_post)

    xl = x_sample.reshape(BL * TL, D)
    q, k, v, gate, g, qa, ka, va = _pre_mixer(
        xl, mod3, 1, TL, TL, pre_w, _rope_tables(TL), False, tm_pre)
    o_gla, _, _ = _gla(r3(q, BL, TL), r3(k, BL, TL), r3(v, BL, TL), r3(g, BL, TL), r3(gate, BL, TL),
                       gla_g, (state_gla_fwd[:, l].reshape(BL, D_QK, GLA_DV),
                               state_gla_bwd[:, l].reshape(BL, D_QK, GLA_DV)), BL)

    k_old = cache_k[:, l].reshape(BL * TP, D_KA).astype(BF16)
    vt_old = jnp.transpose(cache_v[:, l], (2, 3, 0, 1)).reshape(D_KA, BL * TP).astype(BF16)
    o_att = _attention(qa, ka, va, BL, (k_old, vt_old))
    y_sample = _post_mixer(o_gla.reshape(BL * TL, D_V), o_att, xl, mod3, 1, TL, post_w, tm_post)

    def cache_out(ct):
        return jnp.transpose(ct.reshape(B, 1, ATT_KV_HEADS, HEAD_DIM, T), (0, 1, 4, 2, 3))

    return (y_prompt.reshape(B, T, D), y_sample.reshape(BL, TL, D), cache_out(kc), cache_out(vc),
            s_f.reshape(B, 1, GLA_HEADS, GLA_DK, GLA_DV), s_b.reshape(B, 1, GLA_HEADS, GLA_DK, GLA_DV))
```

```python
import functools

import numpy as np
import jax
import jax.numpy as jnp
from jax import lax
from jax.experimental import pallas as pl
from jax.experimental.pallas import tpu as pltpu

F32 = jnp.float32
BF16 = jnp.bfloat16

D_MODEL = 1024
GRID_W = 64
GLA_HEADS = 4
GLA_DK = 64
GLA_DV = 128
GLA_LOWRANK = 16
GLA_GATE_NORM = 16.0
GLA_CHUNK = 64
ATT_HEADS = 8
ATT_KV_HEADS = 2
HEAD_DIM = 64
ROPE_AXIS_DIM = HEAD_DIM // 2
ROPE_THETA = 10000.0
D_FF = -(-8 * D_MODEL // (3 * 256)) * 256
N_MOD = 6
EPS = 1e-6

D_QK = GLA_HEADS * GLA_DK
D_V = GLA_HEADS * GLA_DV
D_QA = ATT_HEADS * HEAD_DIM
D_KA = ATT_KV_HEADS * HEAD_DIM
GROUP = ATT_HEADS // ATT_KV_HEADS
O_Q, O_K, O_V, O_OG = 0, D_QK, 2 * D_QK, 2 * D_QK + D_V
O_QA = O_OG + D_V
O_KA = O_QA + D_QA
O_VA = O_KA + D_KA
O_LR = O_VA + D_KA
D_PROJ = O_LR + 2 * GLA_LOWRANK

COND_ROWS = 8
VMEM_LIMIT = 56 * 1024 * 1024


def _dot(a, b):
    return jnp.dot(a, b, preferred_element_type=F32)


def _dot_nt(a, b):
    return lax.dot_general(a, b, (((1,), (1,)), ((), ())), preferred_element_type=F32)


def _dot_tn(a, b):
    return lax.dot_general(a, b, (((0,), (0,)), ((), ())), preferred_element_type=F32)


def _rms(x, g):
    ms = jnp.mean(x * x, axis=-1, keepdims=True)
    return x * lax.rsqrt(ms + EPS) * g


def _silu(x):
    return x * jax.nn.sigmoid(x)


ADALN_K_BLOCK = 256


def _adaln_kernel(cctx_ref, c_ref, w_ref, b_ref, wint_ref, o_ref, win_o, acc):
    k = pl.program_id(0)

    lr0, lr1 = O_OG + D_V, O_OG + D_V + 2 * GLA_LOWRANK
    feats = jnp.concatenate([wint_ref[0:lr0, :], wint_ref[lr1:D_PROJ, :], wint_ref[lr0:lr1, :]],
                            axis=0)
    for j in range(0, D_PROJ, 128):
        rows = min(128, D_PROJ - j)
        tile = feats[j:j + rows, :]
        if rows < 128:
            tile = jnp.concatenate([tile, jnp.zeros((128 - rows, tile.shape[1]), F32)], axis=0)
        win_o[:, j:j + rows] = tile.T[:, 0:rows].astype(BF16)

    @pl.when(k == 0)
    def _():
        acc[...] = jnp.broadcast_to(b_ref[...], acc.shape)

    pad = jnp.zeros((COND_ROWS - 1 - c_ref.shape[0], c_ref.shape[1]), F32)
    cond = jnp.concatenate([cctx_ref[...], c_ref[...], pad], axis=0)
    acc[...] += _dot(_silu(cond).astype(BF16), w_ref[...].astype(BF16))

    @pl.when(k == pl.num_programs(0) - 1)
    def _():
        o_ref[:, 0, :] = acc[...]


def _adaln(c_ctx, c, ada_w, ada_b, w_in_t):
    d, n = ada_w.shape
    tk = ADALN_K_BLOCK
    assert 1 + c.shape[0] <= COND_ROWS and w_in_t.shape == (D_PROJ, d)
    return pl.pallas_call(
        _adaln_kernel,
        grid=(d // tk,),
        in_specs=[
            pl.BlockSpec((1, tk), lambda k: (0, k)),
            pl.BlockSpec((c.shape[0], tk), lambda k: (0, k)),
            pl.BlockSpec((tk, n), lambda k: (k, 0)),
            pl.BlockSpec((1, n), lambda k: (0, 0)),
            pl.BlockSpec((D_PROJ, tk), lambda k: (0, k)),
        ],
        out_specs=[pl.BlockSpec((COND_ROWS, 1, n), lambda k: (0, 0, 0)),
                   pl.BlockSpec((tk, D_PROJ), lambda k: (k, 0))],
        out_shape=[jax.ShapeDtypeStruct((COND_ROWS, 1, n), F32),
                   jax.ShapeDtypeStruct((d, D_PROJ), BF16)],
        scratch_shapes=[pltpu.VMEM((COND_ROWS, n), F32)],
        compiler_params=pltpu.CompilerParams(
            dimension_semantics=("arbitrary",), vmem_limit_bytes=VMEM_LIMIT),
        name="adaln",
    )(c_ctx.reshape(1, d), c, ada_w, ada_b.reshape(1, n), w_in_t)


def _swap_halves(x, lane_lo):
    n = x.shape[-1]
    up = pltpu.roll(x, n - ROPE_AXIS_DIM // 2, axis=1)
    dn = pltpu.roll(x, ROPE_AXIS_DIM // 2, axis=1)
    return jnp.where(lane_lo, up, dn)


PRE_SUB_ROWS = 256


def _pre_kernel(*refs, rope, emit_cache, n_cast):
    it = iter(refs)
    x_ref, mod_ref, n1_ref, win_ref, wgk_ref, bgk_ref = (next(it) for _ in range(6))
    qg_ref, kg_ref, bdq_ref, bdk_ref = (next(it) for _ in range(4))
    if rope:
        cos_ref, sin_ref = next(it), next(it)
    cast_src = [next(it) for _ in range(n_cast)]
    q_o, k_o, v_o, gate_o, g_o, qa_o, ka_o, va_o = (next(it) for _ in range(8))
    if emit_cache:
        kc_o, vc_o = next(it), next(it)
    for src in cast_src:
        next(it)[...] = src[...].astype(BF16)

    sub = PRE_SUB_ROWS
    n_sub = x_ref.shape[0] // sub
    mod = mod_ref[0]
    sh1 = mod[:, 0:D_MODEL]
    sc1 = mod[:, D_MODEL:2 * D_MODEL]

    def project(j):
        rows = slice(j * sub, (j + 1) * sub)
        h = _rms(x_ref[rows, :], n1_ref[...]) * (1.0 + sc1) + sh1
        return _dot(h.astype(BF16), win_ref[...])

    def finish(j, p_ref):
        rows = slice(j * sub, (j + 1) * sub)
        q_o[rows, :] = p_ref[:, O_Q:O_Q + D_QK].astype(BF16)
        k_o[rows, :] = p_ref[:, O_K:O_K + D_QK].astype(BF16)
        v_o[rows, :] = p_ref[:, O_V:O_V + D_V].astype(BF16)
        gate_o[rows, :] = _silu(p_ref[:, O_OG:O_OG + D_V]).astype(BF16)

        gk = _dot(p_ref[:, O_LR:O_LR + 2 * GLA_LOWRANK].astype(BF16), wgk_ref[...]) + bgk_ref[...]
        g_o[rows, :] = ((jnp.minimum(gk, 0.0) - jnp.log(1.0 + jnp.exp(-jnp.abs(gk))))
                        * (1.0 / GLA_GATE_NORM))

        qa = p_ref[:, O_QA:O_QA + D_QA]
        ka = p_ref[:, O_KA:O_KA + D_KA]
        va = p_ref[:, O_VA:O_VA + D_KA]
        qa = qa * lax.rsqrt(_dot((qa * qa).astype(BF16), bdq_ref[...]) + EPS) * qg_ref[...]
        ka = ka * lax.rsqrt(_dot((ka * ka).astype(BF16), bdk_ref[...]) + EPS) * kg_ref[...]
        if rope:
            lane_q = lax.broadcasted_iota(jnp.int32, qa.shape, 1)
            lane_k = lax.broadcasted_iota(jnp.int32, ka.shape, 1)
            lo_q = (lane_q % ROPE_AXIS_DIM) < ROPE_AXIS_DIM // 2
            lo_k = (lane_k % ROPE_AXIS_DIM) < ROPE_AXIS_DIM // 2
            cos_k = cos_ref[rows, :]
            sin_k = sin_ref[rows, :]
            cos_q = jnp.concatenate([cos_k] * GROUP, axis=1)
            sin_q = jnp.concatenate([sin_k] * GROUP, axis=1)
            qa = qa * cos_q + _swap_halves(qa, lo_q) * sin_q
            ka = ka * cos_k + _swap_halves(ka, lo_k) * sin_k
        vat = va.T
        if emit_cache:
            kat = ka.T
            t = kc_o.shape[2]
            for sq in range(sub // t):
                kc_o[j * (sub // t) + sq] = kat[:, sq * t:(sq + 1) * t]
                vc_o[j * (sub // t) + sq] = vat[:, sq * t:(sq + 1) * t]
        qa_o[:, rows] = (qa * (HEAD_DIM ** -0.5 * LOG2E)).T.astype(BF16)
        ka_o[rows, :] = ka.astype(BF16)
        va_o[:, rows] = vat.astype(BF16)

    p_prev = None
    for n in range(n_sub + 1):
        p_new = project(n) if n < n_sub else None
        if n > 0:
            finish(n - 1, p_prev)
        p_prev = p_new


def _pre_mixer(x, mod3, mod_row0, tokens_per_mod, seq_len, weights, rope_tabs, emit_cache, tm,
               cast=()):
    ntok = x.shape[0]
    n1, w_in_r, w_gk, b_gk, qg, kg, bdq, bdk = weights
    rope = rope_tabs is not None
    tiles_per_mod = tokens_per_mod // tm
    tiles_per_seq = seq_len // tm
    whole = pl.BlockSpec(memory_space=pltpu.VMEM)
    row = lambda w: pl.BlockSpec((tm, w), lambda i: (i, 0))
    in_specs = [
        row(D_MODEL),
        pl.BlockSpec((1, 1, N_MOD * D_MODEL), lambda i: (mod_row0 + i // tiles_per_mod, 0, 0)),
        whole, whole, whole, whole, whole, whole, whole, whole,
    ]
    args = [x, mod3, n1, w_in_r, w_gk, b_gk, qg, kg, bdq, bdk]
    if rope:
        tab = pl.BlockSpec((tm, D_KA), lambda i: (i % tiles_per_seq, 0))
        in_specs += [tab, tab]
        args += list(rope_tabs)
    col = lambda w: pl.BlockSpec((w, tm), lambda i: (0, i))
    out_specs = [row(D_QK), row(D_QK), row(D_V), row(D_V), row(2 * D_QK),
                 col(D_QA), row(D_KA), col(D_KA)]
    out_shape = [
        jax.ShapeDtypeStruct((ntok, D_QK), BF16),
        jax.ShapeDtypeStruct((ntok, D_QK), BF16),
        jax.ShapeDtypeStruct((ntok, D_V), BF16),
        jax.ShapeDtypeStruct((ntok, D_V), BF16),
        jax.ShapeDtypeStruct((ntok, 2 * D_QK), F32),
        jax.ShapeDtypeStruct((D_QA, ntok), BF16),
        jax.ShapeDtypeStruct((ntok, D_KA), BF16),
        jax.ShapeDtypeStruct((D_KA, ntok), BF16),
    ]
    if emit_cache:
        seqs = tm // seq_len
        out_specs += [pl.BlockSpec((seqs, D_KA, seq_len), lambda i: (i, 0, 0))] * 2
        out_shape += [jax.ShapeDtypeStruct((ntok // seq_len, D_KA, seq_len), F32)] * 2
    steps = ntok // tm
    for a in cast:
        slab = pl.BlockSpec((a.shape[0] // steps, a.shape[1]), lambda i: (i, 0))
        in_specs.append(slab)
        args.append(a)
        out_specs.append(slab)
        out_shape.append(jax.ShapeDtypeStruct(a.shape, BF16))
    return pl.pallas_call(
        functools.partial(_pre_kernel, rope=rope, emit_cache=emit_cache, n_cast=len(cast)),
        grid=(ntok // tm,),
        in_specs=in_specs,
        out_specs=out_specs,
        out_shape=out_shape,
        compiler_params=pltpu.CompilerParams(
            dimension_semantics=("arbitrary",), vmem_limit_bytes=VMEM_LIMIT),
        name="pre_mixer_rope" if rope else "pre_mixer",
    )(*args)


def _gla_chunks(units, q_ref, k_ref, v_ref, g_ref, half_lo):
    C = GLA_CHUNK
    n_pair = GLA_HEADS // 2
    rows = [pl.ds(pl.multiple_of(c * C, C), C) for (_, c, _, _, _, _) in units]

    bs = []
    for (bi, _, fwd, tri, _, _), r in zip(units, rows):
        gc = g_ref[bi, r, 0:D_QK] if fwd else g_ref[bi, r, D_QK:2 * D_QK]
        ltri = jnp.where(tri, 1.0, 0.0).astype(BF16)
        g_hi = gc.astype(BF16)
        g_lo = (gc - g_hi.astype(F32)).astype(BF16)
        bs.append(_dot(ltri, g_hi) + _dot(ltri, g_lo))

    qts, kts, kes, decays = [], [], [], []
    for (bi, _, fwd, _, _, _), r, b in zip(units, rows, bs):
        qc = q_ref[bi, r, :].astype(F32) * GLA_DK ** -0.5
        kc = k_ref[bi, r, :].astype(F32)
        btot = b[C - 1:C, :] if fwd else b[0:1, :]
        qts.append((qc * jnp.exp(b)).astype(BF16))
        kts.append((kc * jnp.exp(-b)).astype(BF16))
        kes.append((kc * jnp.exp(btot - b)).astype(BF16))
        decays.append(jnp.transpose(jnp.broadcast_to(jnp.exp(btot), (8, D_QK)))[:, 0:1])

    scores, incs, qms = [], [], []
    for (bi, _, _, tri, _, _), r, qt, kt, ke in zip(units, rows, qts, kts, kes):
        for pair in range(n_pair):
            lanes = slice(pair * 128, (pair + 1) * 128)
            incs.append(_dot_tn(ke[:, lanes], v_ref[bi, r, pair * 2 * GLA_DV:(pair + 1) * 2 * GLA_DV]))
            for hh in range(2):
                qm = jnp.where(half_lo if hh == 0 else ~half_lo, qt[:, lanes], jnp.zeros((C, 128), BF16))
                qms.append(qm)
                scores.append(jnp.where(tri, _dot_nt(qm, kt[:, lanes]), 0.0).astype(BF16))

    for ui, ((bi, _, _, _, s_ref, o_ref), r) in enumerate(zip(units, rows)):
        state = s_ref[bi]
        state_b = state.astype(BF16)
        for hd in range(GLA_HEADS):
            pair = hd // 2
            lhs = jnp.concatenate([qms[ui * GLA_HEADS + hd], scores[ui * GLA_HEADS + hd]], axis=1)
            rhs = jnp.concatenate([state_b[pair * 128:(pair + 1) * 128, :],
                                   v_ref[bi, r, hd * GLA_DV:(hd + 1) * GLA_DV]], axis=0)
            o_ref[bi, r, hd * GLA_DV:(hd + 1) * GLA_DV] = _dot(lhs, rhs)
        inc = jnp.concatenate(
            [incs[ui * n_pair + hd // 2][(hd % 2) * GLA_DK:(hd % 2 + 1) * GLA_DK,
                                         (hd % 2) * GLA_DV:(hd % 2 + 1) * GLA_DV]
             for hd in range(GLA_HEADS)], axis=0)
        s_ref[bi] = decays[ui] * state + inc


def _gla_kernel(*args, zero_init, n_cast):
    q_ref, k_ref, v_ref, g_ref, gate_ref, gn_ref = args[:6]
    n_in = 6 + (0 if zero_init else 2)
    cast_src = args[n_in:n_in + n_cast]
    o_ref, sf_ref, sb_ref = args[n_in + n_cast:n_in + n_cast + 3]
    cast_dst = args[n_in + n_cast + 3:n_in + 2 * n_cast + 3]
    of_scr, ob_scr = args[-2:]
    for src, dst in zip(cast_src, cast_dst):
        dst[...] = src[...].astype(BF16)
    C = GLA_CHUNK
    nb, t, _ = q_ref.shape
    n = t // C
    row = lax.broadcasted_iota(jnp.int32, (C, C), 0)
    col = lax.broadcasted_iota(jnp.int32, (C, C), 1)
    half_lo = lax.broadcasted_iota(jnp.int32, (C, 128), 1) < GLA_DK
    refs = (q_ref, k_ref, v_ref, g_ref)
    if zero_init:
        sf_ref[...] = jnp.zeros(sf_ref.shape, F32)
        sb_ref[...] = jnp.zeros(sb_ref.shape, F32)
    else:
        sf_ref[...] = args[6][...]
        sb_ref[...] = args[7][...]

    steps = min(GLA_STEPS_PER_ITER, n)
    assert n % steps == 0

    def body(i, carry):
        units = []
        for u in range(steps):
            c = i * steps + u
            for bi in range(nb):
                units.append((bi, c, True, row >= col, sf_ref, of_scr))
                units.append((bi, n - 1 - c, False, row <= col, sb_ref, ob_scr))
        _gla_chunks(units, *refs, half_lo)
        return carry

    lax.fori_loop(0, n // steps, body, 0)

    rt = GLA_OUT_ROWS
    gn = gn_ref[...]

    def finish(i, carry):
        rows = pl.ds(pl.multiple_of(i * rt, rt), rt)
        for bi in range(nb):
            o = of_scr[bi, rows, :] + ob_scr[bi, rows, :]
            gate = gate_ref[bi, rows, :].astype(F32)
            for hd in range(GLA_HEADS):
                sl = slice(hd * GLA_DV, (hd + 1) * GLA_DV)
                o_ref[bi, rows, sl] = (_rms(o[:, sl], gn) * gate[:, sl]).astype(BF16)
        return carry

    lax.fori_loop(0, t // rt, finish, 0)


GLA_STEPS_PER_ITER = 4
GLA_OUT_ROWS = 256


def _gla(q, k, v, g, gate, gla_g, init_states, nb, cast=()):
    b, t, _ = q.shape
    seq = lambda w: pl.BlockSpec((nb, t, w), lambda i: (i, 0, 0))
    st = pl.BlockSpec((nb, D_QK, GLA_DV), lambda i: (i, 0, 0))
    zero_init = init_states is None
    steps = b // nb
    slabs = [pl.BlockSpec((a.shape[0] // steps, a.shape[1]), lambda i: (i, 0)) for a in cast]
    return pl.pallas_call(
        functools.partial(_gla_kernel, zero_init=zero_init, n_cast=len(cast)),
        grid=(b // nb,),
        in_specs=[seq(D_QK), seq(D_QK), seq(D_V), seq(2 * D_QK), seq(D_V),
                  pl.BlockSpec((1, GLA_DV), lambda i: (0, 0))] + ([] if zero_init else [st, st]) + slabs,
        out_specs=[seq(D_V), st, st] + slabs,
        out_shape=[jax.ShapeDtypeStruct((b, t, D_V), BF16),
                   jax.ShapeDtypeStruct((b, D_QK, GLA_DV), F32),
                   jax.ShapeDtypeStruct((b, D_QK, GLA_DV), F32)]
        + [jax.ShapeDtypeStruct(a.shape, BF16) for a in cast],
        scratch_shapes=[pltpu.VMEM((nb, t, D_V), F32), pltpu.VMEM((nb, t, D_V), F32)],
        compiler_params=pltpu.CompilerParams(
            dimension_semantics=("arbitrary",), vmem_limit_bytes=VMEM_LIMIT),
        name="gla",
    )(q, k, v, g, gate, gla_g, *(() if zero_init else init_states), *cast)


ATT_KEY_BLOCK = 256
ATT_Q_TILE = 256
LOG2E = 1.4426950408889634


ATT_TILES_LONG = 2
ATT_TILES_SHORT = 4


def _attn_kernel(*args, keys_per_tile, cached):
    qt_ref, k_ref, vt_ref = args[:3]
    kc_ref, vtc_ref = args[3:5] if cached else (None, None)
    o_ref, s0, s1, m0, m1, a0, a1 = args[-7:]
    tq = ATT_Q_TILE
    tiles = qt_ref.shape[1] // tq
    t_new = k_ref.shape[0] // tiles if keys_per_tile else k_ref.shape[0]
    tk = t_new + (kc_ref.shape[0] if cached else 0)
    kb = min(ATT_KEY_BLOCK, tk)
    nkb = tk // kb
    zero = jnp.zeros((HEAD_DIM, tq), BF16)
    ones = jnp.ones((16, kb), BF16)
    items = [(qi, kv) for qi in range(tiles) for kv in range(ATT_KV_HEADS)]
    bufs = (s0, s1)
    mbufs = (m0, m1)
    abufs = (a0, a1)
    separate = nkb > 1

    def q_ext(qi, kv):
        cols = []
        for g in range(GROUP):
            hd = kv * GROUP + g
            qt = qt_ref[hd * HEAD_DIM:(hd + 1) * HEAD_DIM, qi * tq:(qi + 1) * tq]
            cols.append(jnp.concatenate([qt, zero] if kv == 0 else [zero, qt], axis=0))
        return jnp.concatenate(cols, axis=1)

    def key0(qi):
        return qi * tk if keys_per_tile else 0

    def score_piece(n, q_all, i, m):
        r0 = key0(items[n][0]) + i * kb
        keys = k_ref[r0:r0 + kb, :] if i * kb < t_new else kc_ref[r0 - t_new:r0 - t_new + kb, :]
        s = _dot(keys, q_all)
        bufs[n % 2][i * kb:(i + 1) * kb, :] = s
        for r in range(kb // 8):
            tile = s[r * 8:(r + 1) * 8, :]
            m = tile if m is None else jnp.maximum(m, tile)
        return m

    def value_piece(n, m, i, acc):
        qi, kv = items[n]
        c0 = key0(qi) + i * kb
        p = jnp.exp2(bufs[n % 2][i * kb:(i + 1) * kb, :] - m).astype(BF16)
        hd_rows = slice(kv * HEAD_DIM, (kv + 1) * HEAD_DIM)
        vt = vt_ref[hd_rows, c0:c0 + kb] if i * kb < t_new else vtc_ref[hd_rows, c0 - t_new:c0 - t_new + kb]
        vt1 = jnp.concatenate([vt, ones], axis=0)
        d = _dot(vt1, p)
        return d if acc is None else acc + d

    def emit(n, acc):
        qi, kv = items[n]
        o = acc[:HEAD_DIM] / acc[HEAD_DIM:HEAD_DIM + 1]
        for g in range(0, GROUP, 2):
            two = jnp.concatenate([o[:, g * tq:(g + 1) * tq], o[:, (g + 1) * tq:(g + 2) * tq]], axis=0)
            hd = kv * GROUP + g
            o_ref[qi * tq:(qi + 1) * tq, hd * HEAD_DIM:(hd + 2) * HEAD_DIM] = two.T.astype(BF16)

    def pair(n):
        if separate and n >= 2:
            emit(n - 2, abufs[n % 2][...])
        if n > len(items):
            return
        q_all = q_ext(*items[n]) if n < len(items) else None
        m_prev = mbufs[(n - 1) % 2][...] if n > 0 else None
        m_new = acc = None
        for i in range(nkb):
            if n < len(items):
                m_new = score_piece(n, q_all, i, m_new)
            if n > 0:
                acc = value_piece(n - 1, m_prev, i, acc)
        if n > 0 and separate:
            abufs[(n - 1) % 2][...] = acc
        elif n > 0:
            emit(n - 1, acc)
        if n < len(items):
            mbufs[n % 2][...] = jnp.max(m_new, axis=0, keepdims=True)

    for n in range(len(items) + (2 if separate else 1)):
        if separate:
            pl.when(pl.program_id(0) > -(n + 1))(functools.partial(pair, n))
        else:
            pair(n)


def _attention(qt, k, vt, batch, cache=None):
    ntok = qt.shape[1]
    t_new = k.shape[0] // batch
    tq = ATT_Q_TILE
    nq = ntok // batch // tq
    tiles = ATT_TILES_LONG if t_new > ATT_KEY_BLOCK else ATT_TILES_SHORT
    keys_per_tile = nq < tiles
    extra_specs, extra, tk = [], (), t_new
    if keys_per_tile:
        assert cache is None
        seqs = tiles // nq
        k_spec = pl.BlockSpec((seqs * t_new, D_KA), lambda t: (t, 0))
        vt_spec = pl.BlockSpec((D_KA, seqs * t_new), lambda t: (0, t))
    else:
        spq = nq // tiles
        k_spec = pl.BlockSpec((t_new, D_KA), lambda t: (t // spq, 0))
        vt_spec = pl.BlockSpec((D_KA, t_new), lambda t: (0, t // spq))
        if cache is not None:
            t_old = cache[0].shape[0] // batch
            assert t_new % ATT_KEY_BLOCK == 0 and t_old % ATT_KEY_BLOCK == 0
            extra_specs = [pl.BlockSpec((t_old, D_KA), lambda t: (t // spq, 0)),
                           pl.BlockSpec((D_KA, t_old), lambda t: (0, t // spq))]
            extra, tk = cache, t_new + t_old
    return pl.pallas_call(
        functools.partial(_attn_kernel, keys_per_tile=keys_per_tile, cached=cache is not None),
        grid=(ntok // (tiles * tq),),
        in_specs=[pl.BlockSpec((D_QA, tiles * tq), lambda t: (0, t)), k_spec, vt_spec] + extra_specs,
        out_specs=pl.BlockSpec((tiles * tq, D_QA), lambda t: (t, 0)),
        out_shape=jax.ShapeDtypeStruct((ntok, D_QA), BF16),
        scratch_shapes=[pltpu.VMEM((tk, GROUP * tq), F32), pltpu.VMEM((tk, GROUP * tq), F32),
                        pltpu.VMEM((1, GROUP * tq), F32), pltpu.VMEM((1, GROUP * tq), F32),
                        pltpu.VMEM((HEAD_DIM + 16, GROUP * tq), F32),
                        pltpu.VMEM((HEAD_DIM + 16, GROUP * tq), F32)],
        compiler_params=pltpu.CompilerParams(
            dimension_semantics=("arbitrary",), vmem_limit_bytes=VMEM_LIMIT),
        name="attention",
    )(qt, k, vt, *extra)


def _post_kernel(og_ref, oa_ref, x_ref, mod_ref, n2_ref, fg_ref, wo_ref, w1_ref, w3_ref, w2_ref,
                 y_ref):
    D = D_MODEL
    mod = mod_ref[0]
    g1, sh2, sc2, g2 = (mod[:, i * D:(i + 1) * D] for i in range(2, 6))
    out = _dot(og_ref[...], wo_ref[0:D_V, :]) + _dot(oa_ref[...], wo_ref[D_V:D_V + D_QA, :])
    x1 = x_ref[...] + g1 * out
    hb = (_rms(x1, n2_ref[...]) * (1.0 + sc2) + sh2).astype(BF16)
    act = (_silu(_dot(hb, w1_ref[...])) * _dot(hb, w3_ref[...])).astype(BF16)
    x2 = x1 + g2 * _dot(act, w2_ref[...])
    y_ref[...] = _rms(x2, fg_ref[...])


def _post_mixer(o_gla, o_att, x, mod3, mod_row0, tokens_per_mod, weights, tm):
    ntok = x.shape[0]
    tiles_per_mod = tokens_per_mod // tm
    whole = pl.BlockSpec(memory_space=pltpu.VMEM)
    row = lambda w: pl.BlockSpec((tm, w), lambda i: (i, 0))
    return pl.pallas_call(
        _post_kernel,
        grid=(ntok // tm,),
        in_specs=[row(D_V), row(D_QA), row(D_MODEL),
                  pl.BlockSpec((1, 1, N_MOD * D_MODEL), lambda i: (mod_row0 + i // tiles_per_mod, 0, 0)),
                  whole, whole, whole, whole, whole, whole],
        out_specs=row(D_MODEL),
        out_shape=jax.ShapeDtypeStruct((ntok, D_MODEL), F32),
        compiler_params=pltpu.CompilerParams(
            dimension_semantics=("arbitrary",), vmem_limit_bytes=VMEM_LIMIT),
        name="post_mixer",
    )(o_gla, o_att, x, mod3, *weights)


def _rope_tables(t):
    rows = t // GRID_W
    row = np.repeat(np.arange(rows, dtype=np.float64), GRID_W)
    col = np.tile(np.arange(GRID_W, dtype=np.float64), rows)
    inv = ROPE_THETA ** (-np.arange(0, ROPE_AXIS_DIM, 2, dtype=np.float64) / ROPE_AXIS_DIM)
    ang_r = row[:, None] * inv
    ang_c = col[:, None] * inv
    cos = np.concatenate([np.cos(ang_r)] * 2 + [np.cos(ang_c)] * 2, axis=-1)
    sin = np.concatenate([-np.sin(ang_r), np.sin(ang_r), -np.sin(ang_c), np.sin(ang_c)], axis=-1)
    tile = lambda a: jnp.asarray(np.tile(a, (1, ATT_KV_HEADS)), F32)
    return tile(cos), tile(sin)


def _block_mean(width, group):
    return jnp.asarray(np.kron(np.eye(width // group), np.ones((group, group))) / group, BF16)


def kernel(x_prompt, x_sample, cache_k, cache_v, state_gla_fwd, state_gla_bwd, c, c_ctx,
           ada_w, ada_b, norm1_g, norm2_g, w_in, w_gk2, b_gk2, gla_norm_g, q_norm_g, k_norm_g,
           w_out, w_ffn1, w_ffn3, w_ffn2, final_g):
    B, T, D = x_prompt.shape
    BL, TL, _ = x_sample.shape
    TP = cache_k.shape[2]
    l = 0

    mod3, w_in_r = _adaln(c_ctx, c, ada_w[l], ada_b[l], jnp.transpose(w_in[l]))
    zero = jnp.zeros((GLA_LOWRANK, D_QK), F32)
    w_gk = jnp.concatenate([jnp.concatenate([w_gk2[l, 0], zero], axis=1),
                            jnp.concatenate([zero, w_gk2[l, 1]], axis=1)], axis=0).astype(BF16)
    pre_w = (norm1_g[l].reshape(1, D), w_in_r, w_gk, b_gk2[l].reshape(1, 2 * D_QK),
             jnp.tile(q_norm_g[l], ATT_HEADS).reshape(1, D_QA),
             jnp.tile(k_norm_g[l], ATT_KV_HEADS).reshape(1, D_KA),
             _block_mean(D_QA, HEAD_DIM), _block_mean(D_KA, HEAD_DIM))
    post_norms = (norm2_g[l].reshape(1, D), final_g.reshape(1, D))
    post_w_f32 = (w_out[l], w_ffn1[l], w_ffn3[l], w_ffn2[l])
    gla_g = gla_norm_g[l].reshape(1, GLA_DV)
    tm_pre = 1024
    tm_post = 512

    xc = x_prompt.reshape(B * T, D)
    q, k, v, gate, g, qa, ka, va, kc, vc = _pre_mixer(
        xc, mod3, 0, B * T, T, pre_w, None, True, tm_pre)
    r3 = lambda a, b, t: a.reshape(b, t, a.shape[-1])
    o_gla, s_f, s_b, *post_w_bf16 = _gla(
        r3(q, B, T), r3(k, B, T), r3(v, B, T), r3(g, B, T), r3(gate, B, T), gla_g, None, 4,
        cast=post_w_f32)
    post_w = post_norms + tuple(post_w_bf16)
    o_att = _attention(qa, ka, va, B)
    y_prompt = _post_mixer(o_gla.reshape(B * T, D_V), o_att, xc, mod3, 0, B * T, post_w, tm_post)

    xl = x_sample.reshape(BL * TL, D)
    q, k, v, gate, g, qa, ka, va = _pre_mixer(
        xl, mod3, 1, TL, TL, pre_w, _rope_tables(TL), False, tm_pre)
    o_gla, _, _ = _gla(r3(q, BL, TL), r3(k, BL, TL), r3(v, BL, TL), r3(g, BL, TL), r3(gate, BL, TL),
                       gla_g, (state_gla_fwd[:, l].reshape(BL, D_QK, GLA_DV),
                               state_gla_bwd[:, l].reshape(BL, D_QK, GLA_DV)), BL)

    k_old = cache_k[:, l].reshape(BL * TP, D_KA).astype(BF16)
    vt_old = jnp.transpose(cache_v[:, l], (2, 3, 0, 1)).reshape(D_KA, BL * TP).astype(BF16)
    o_att = _attention(qa, ka, va, BL, (k_old, vt_old))
    y_sample = _post_mixer(o_gla.reshape(BL * TL, D_V), o_att, xl, mod3, 1, TL, post_w, tm_post)

    def cache_out(ct):
        return jnp.transpose(ct.reshape(B, 1, ATT_KV_HEADS, HEAD_DIM, T), (0, 1, 4, 2, 3))

    return (y_prompt.reshape(B, T, D), y_sample.reshape(BL, TL, D), cache_out(kc), cache_out(vc),
            s_f.reshape(B, 1, GLA_HEADS, GLA_DK, GLA_DV), s_b.reshape(B, 1, GLA_HEADS, GLA_DK, GLA_DV))
```

```python
import functools

import numpy as np
import jax
import jax.numpy as jnp
from jax import lax
from jax.experimental import pallas as pl
from jax.experimental.pallas import tpu as pltpu

F32 = jnp.float32
BF16 = jnp.bfloat16

D_MODEL = 1024
GRID_W = 64
GLA_HEADS = 4
GLA_DK = 64
GLA_DV = 128
GLA_LOWRANK = 16
GLA_GATE_NORM = 16.0
GLA_CHUNK = 64
ATT_HEADS = 8
ATT_KV_HEADS = 2
HEAD_DIM = 64
ROPE_AXIS_DIM = HEAD_DIM // 2
ROPE_THETA = 10000.0
N_MOD = 6
EPS = 1e-6

D_QK = GLA_HEADS * GLA_DK
D_V = GLA_HEADS * GLA_DV
D_QA = ATT_HEADS * HEAD_DIM
D_KA = ATT_KV_HEADS * HEAD_DIM
GROUP = ATT_HEADS // ATT_KV_HEADS
O_Q, O_K, O_V, O_OG = 0, D_QK, 2 * D_QK, 2 * D_QK + D_V
O_QA = O_OG + D_V
O_KA = O_QA + D_QA
O_VA = O_KA + D_KA
O_LR = O_VA + D_KA
D_PROJ = O_LR + 2 * GLA_LOWRANK

COND_ROWS = 8
VMEM_LIMIT = 56 * 1024 * 1024
PRE_TILE_ROWS = 1024
POST_TILE_ROWS = 512
GLA_CONTEXT_SEQS = 4


def _dot(a, b):
    return jnp.dot(a, b, preferred_element_type=F32)


def _dot_nt(a, b):
    return lax.dot_general(a, b, (((1,), (1,)), ((), ())), preferred_element_type=F32)


def _dot_tn(a, b):
    return lax.dot_general(a, b, (((0,), (0,)), ((), ())), preferred_element_type=F32)


def _rms(x, g):
    ms = jnp.mean(x * x, axis=-1, keepdims=True)
    return x * lax.rsqrt(ms + EPS) * g


def _silu(x):
    return x * jax.nn.sigmoid(x)


ADALN_K_BLOCK = 256


def _adaln_kernel(cctx_ref, c_ref, w_ref, b_ref, wint_ref, o_ref, win_o, acc):
    k = pl.program_id(0)

    lr0, lr1 = O_OG + D_V, O_OG + D_V + 2 * GLA_LOWRANK
    feats = jnp.concatenate([wint_ref[0:lr0, :], wint_ref[lr1:D_PROJ, :], wint_ref[lr0:lr1, :]],
                            axis=0)
    for j in range(0, D_PROJ, 128):
        rows = min(128, D_PROJ - j)
        tile = feats[j:j + rows, :]
        if rows < 128:
            tile = jnp.concatenate([tile, jnp.zeros((128 - rows, tile.shape[1]), F32)], axis=0)
        win_o[:, j:j + rows] = tile.T[:, 0:rows].astype(BF16)

    @pl.when(k == 0)
    def _():
        acc[...] = jnp.broadcast_to(b_ref[...], acc.shape)

    pad = jnp.zeros((COND_ROWS - 1 - c_ref.shape[0], c_ref.shape[1]), F32)
    cond = jnp.concatenate([cctx_ref[...], c_ref[...], pad], axis=0)
    acc[...] += _dot(_silu(cond).astype(BF16), w_ref[...].astype(BF16))

    @pl.when(k == pl.num_programs(0) - 1)
    def _():
        o_ref[:, 0, :] = acc[...]


def _adaln(c_ctx, c, ada_w, ada_b, w_in_t):
    d, n = ada_w.shape
    tk = ADALN_K_BLOCK
    assert 1 + c.shape[0] <= COND_ROWS and w_in_t.shape == (D_PROJ, d)
    return pl.pallas_call(
        _adaln_kernel,
        grid=(d // tk,),
        in_specs=[
            pl.BlockSpec((1, tk), lambda k: (0, k)),
            pl.BlockSpec((c.shape[0], tk), lambda k: (0, k)),
            pl.BlockSpec((tk, n), lambda k: (k, 0)),
            pl.BlockSpec((1, n), lambda k: (0, 0)),
            pl.BlockSpec((D_PROJ, tk), lambda k: (0, k)),
        ],
        out_specs=[pl.BlockSpec((COND_ROWS, 1, n), lambda k: (0, 0, 0)),
                   pl.BlockSpec((tk, D_PROJ), lambda k: (k, 0))],
        out_shape=[jax.ShapeDtypeStruct((COND_ROWS, 1, n), F32),
                   jax.ShapeDtypeStruct((d, D_PROJ), BF16)],
        scratch_shapes=[pltpu.VMEM((COND_ROWS, n), F32)],
        compiler_params=pltpu.CompilerParams(
            dimension_semantics=("arbitrary",), vmem_limit_bytes=VMEM_LIMIT),
        name="adaln",
    )(c_ctx.reshape(1, d), c, ada_w, ada_b.reshape(1, n), w_in_t)


def _swap_halves(x, lane_lo):
    n = x.shape[-1]
    up = pltpu.roll(x, n - ROPE_AXIS_DIM // 2, axis=1)
    dn = pltpu.roll(x, ROPE_AXIS_DIM // 2, axis=1)
    return jnp.where(lane_lo, up, dn)


PRE_SUB_ROWS = 256


def _pre_kernel(*refs, rope, emit_cache):
    it = iter(refs)
    x_ref, mod_ref, n1_ref, win_ref, wgk_ref, bgk_ref = (next(it) for _ in range(6))
    qg_ref, kg_ref, bdq_ref, bdk_ref = (next(it) for _ in range(4))
    if rope:
        cos_ref, sin_ref = next(it), next(it)
    q_o, k_o, v_o, gate_o, g_o, qa_o, ka_o, va_o = (next(it) for _ in range(8))
    if emit_cache:
        kc_o, vc_o = next(it), next(it)

    sub = PRE_SUB_ROWS
    n_sub = x_ref.shape[0] // sub
    mod = mod_ref[0]
    sh1 = mod[:, 0:D_MODEL]
    sc1 = mod[:, D_MODEL:2 * D_MODEL]

    def project(j):
        rows = slice(j * sub, (j + 1) * sub)
        h = _rms(x_ref[rows, :], n1_ref[...]) * (1.0 + sc1) + sh1
        return _dot(h.astype(BF16), win_ref[...])

    def finish(j, p_ref):
        rows = slice(j * sub, (j + 1) * sub)
        q_o[rows, :] = p_ref[:, O_Q:O_Q + D_QK].astype(BF16)
        k_o[rows, :] = p_ref[:, O_K:O_K + D_QK].astype(BF16)
        v_o[rows, :] = p_ref[:, O_V:O_V + D_V].astype(BF16)
        gate_o[rows, :] = _silu(p_ref[:, O_OG:O_OG + D_V]).astype(BF16)

        gk = _dot(p_ref[:, O_LR:O_LR + 2 * GLA_LOWRANK].astype(BF16), wgk_ref[...]) + bgk_ref[...]
        g_o[rows, :] = ((jnp.minimum(gk, 0.0) - jnp.log(1.0 + jnp.exp(-jnp.abs(gk))))
                        * (1.0 / GLA_GATE_NORM))

        qa = p_ref[:, O_QA:O_QA + D_QA]
        ka = p_ref[:, O_KA:O_KA + D_KA]
        va = p_ref[:, O_VA:O_VA + D_KA]
        qa = qa * lax.rsqrt(_dot((qa * qa).astype(BF16), bdq_ref[...]) + EPS) * qg_ref[...]
        ka = ka * lax.rsqrt(_dot((ka * ka).astype(BF16), bdk_ref[...]) + EPS) * kg_ref[...]
        if rope:
            lane_q = lax.broadcasted_iota(jnp.int32, qa.shape, 1)
            lane_k = lax.broadcasted_iota(jnp.int32, ka.shape, 1)
            lo_q = (lane_q % ROPE_AXIS_DIM) < ROPE_AXIS_DIM // 2
            lo_k = (lane_k % ROPE_AXIS_DIM) < ROPE_AXIS_DIM // 2
            cos_k = cos_ref[rows, :]
            sin_k = sin_ref[rows, :]
            cos_q = jnp.concatenate([cos_k] * GROUP, axis=1)
            sin_q = jnp.concatenate([sin_k] * GROUP, axis=1)
            qa = qa * cos_q + _swap_halves(qa, lo_q) * sin_q
            ka = ka * cos_k + _swap_halves(ka, lo_k) * sin_k
        vat = va.T
        if emit_cache:
            kat = ka.T
            t = kc_o.shape[2]
            for sq in range(sub // t):
                kc_o[j * (sub // t) + sq] = kat[:, sq * t:(sq + 1) * t]
                vc_o[j * (sub // t) + sq] = vat[:, sq * t:(sq + 1) * t]
        qa_o[:, rows] = (qa * (HEAD_DIM ** -0.5 * LOG2E)).T.astype(BF16)
        ka_o[rows, :] = ka.astype(BF16)
        va_o[:, rows] = vat.astype(BF16)

    p_prev = None
    for n in range(n_sub + 1):
        p_new = project(n) if n < n_sub else None
        if n > 0:
            finish(n - 1, p_prev)
        p_prev = p_new


def _pre_mixer(x, mod3, mod_row0, tokens_per_mod, seq_len, weights, rope_tabs, emit_cache, tm):
    ntok = x.shape[0]
    n1, w_in_r, w_gk, b_gk, qg, kg, bdq, bdk = weights
    rope = rope_tabs is not None
    tiles_per_mod = tokens_per_mod // tm
    tiles_per_seq = seq_len // tm
    whole = pl.BlockSpec(memory_space=pltpu.VMEM)
    row = lambda w: pl.BlockSpec((tm, w), lambda i: (i, 0))
    in_specs = [
        row(D_MODEL),
        pl.BlockSpec((1, 1, N_MOD * D_MODEL), lambda i: (mod_row0 + i // tiles_per_mod, 0, 0)),
        whole, whole, whole, whole, whole, whole, whole, whole,
    ]
    args = [x, mod3, n1, w_in_r, w_gk, b_gk, qg, kg, bdq, bdk]
    if rope:
        tab = pl.BlockSpec((tm, D_KA), lambda i: (i % tiles_per_seq, 0))
        in_specs += [tab, tab]
        args += list(rope_tabs)
    col = lambda w: pl.BlockSpec((w, tm), lambda i: (0, i))
    out_specs = [row(D_QK), row(D_QK), row(D_V), row(D_V), row(2 * D_QK),
                 col(D_QA), row(D_KA), col(D_KA)]
    out_shape = [
        jax.ShapeDtypeStruct((ntok, D_QK), BF16),
        jax.ShapeDtypeStruct((ntok, D_QK), BF16),
        jax.ShapeDtypeStruct((ntok, D_V), BF16),
        jax.ShapeDtypeStruct((ntok, D_V), BF16),
        jax.ShapeDtypeStruct((ntok, 2 * D_QK), F32),
        jax.ShapeDtypeStruct((D_QA, ntok), BF16),
        jax.ShapeDtypeStruct((ntok, D_KA), BF16),
        jax.ShapeDtypeStruct((D_KA, ntok), BF16),
    ]
    if emit_cache:
        seqs = tm // seq_len
        out_specs += [pl.BlockSpec((seqs, D_KA, seq_len), lambda i: (i, 0, 0))] * 2
        out_shape += [jax.ShapeDtypeStruct((ntok // seq_len, D_KA, seq_len), F32)] * 2
    return pl.pallas_call(
        functools.partial(_pre_kernel, rope=rope, emit_cache=emit_cache),
        grid=(ntok // tm,),
        in_specs=in_specs,
        out_specs=out_specs,
        out_shape=out_shape,
        compiler_params=pltpu.CompilerParams(
            dimension_semantics=("arbitrary",), vmem_limit_bytes=VMEM_LIMIT),
        name="pre_mixer_rope" if rope else "pre_mixer",
    )(*args)


def _gla_chunks(units, q_ref, k_ref, v_ref, g_ref, half_lo):
    C = GLA_CHUNK
    n_pair = GLA_HEADS // 2
    rows = [pl.ds(pl.multiple_of(c * C, C), C) for (_, c, _, _, _, _) in units]

    bs = []
    for (bi, _, fwd, tri, _, _), r in zip(units, rows):
        gc = g_ref[bi, r, 0:D_QK] if fwd else g_ref[bi, r, D_QK:2 * D_QK]
        ltri = jnp.where(tri, 1.0, 0.0).astype(BF16)
        g_hi = gc.astype(BF16)
        g_lo = (gc - g_hi.astype(F32)).astype(BF16)
        bs.append(_dot(ltri, g_hi) + _dot(ltri, g_lo))

    qts, kts, kes, decays = [], [], [], []
    for (bi, _, fwd, _, _, _), r, b in zip(units, rows, bs):
        qc = q_ref[bi, r, :].astype(F32) * GLA_DK ** -0.5
        kc = k_ref[bi, r, :].astype(F32)
        btot = b[C - 1:C, :] if fwd else b[0:1, :]
        qts.append((qc * jnp.exp(b)).astype(BF16))
        kts.append((kc * jnp.exp(-b)).astype(BF16))
        kes.append((kc * jnp.exp(btot - b)).astype(BF16))
        decays.append(jnp.transpose(jnp.broadcast_to(jnp.exp(btot), (8, D_QK)))[:, 0:1])

    scores, incs, qms = [], [], []
    for (bi, _, _, tri, _, _), r, qt, kt, ke in zip(units, rows, qts, kts, kes):
        for pair in range(n_pair):
            lanes = slice(pair * 128, (pair + 1) * 128)
            incs.append(_dot_tn(ke[:, lanes], v_ref[bi, r, pair * 2 * GLA_DV:(pair + 1) * 2 * GLA_DV]))
            for hh in range(2):
                qm = jnp.where(half_lo if hh == 0 else ~half_lo, qt[:, lanes], jnp.zeros((C, 128), BF16))
                qms.append(qm)
                scores.append(jnp.where(tri, _dot_nt(qm, kt[:, lanes]), 0.0).astype(BF16))

    for ui, ((bi, _, _, _, s_ref, o_ref), r) in enumerate(zip(units, rows)):
        state = s_ref[bi]
        state_b = state.astype(BF16)
        for hd in range(GLA_HEADS):
            pair = hd // 2
            lhs = jnp.concatenate([qms[ui * GLA_HEADS + hd], scores[ui * GLA_HEADS + hd]], axis=1)
            rhs = jnp.concatenate([state_b[pair * 128:(pair + 1) * 128, :],
                                   v_ref[bi, r, hd * GLA_DV:(hd + 1) * GLA_DV]], axis=0)
            o_ref[bi, r, hd * GLA_DV:(hd + 1) * GLA_DV] = _dot(lhs, rhs)
        inc = jnp.concatenate(
            [incs[ui * n_pair + hd // 2][(hd % 2) * GLA_DK:(hd % 2 + 1) * GLA_DK,
                                         (hd % 2) * GLA_DV:(hd % 2 + 1) * GLA_DV]
             for hd in range(GLA_HEADS)], axis=0)
        s_ref[bi] = decays[ui] * state + inc


def _gla_kernel(*args, zero_init, n_cast):
    q_ref, k_ref, v_ref, g_ref, gate_ref, gn_ref = args[:6]
    n_in = 6 + (0 if zero_init else 2)
    cast_src = args[n_in:n_in + n_cast]
    o_ref, sf_ref, sb_ref = args[n_in + n_cast:n_in + n_cast + 3]
    cast_dst = args[n_in + n_cast + 3:n_in + 2 * n_cast + 3]
    of_scr, ob_scr = args[-2:]
    for src, dst in zip(cast_src, cast_dst):
        dst[...] = src[...].astype(BF16)
    C = GLA_CHUNK
    nb, t, _ = q_ref.shape
    n = t // C
    row = lax.broadcasted_iota(jnp.int32, (C, C), 0)
    col = lax.broadcasted_iota(jnp.int32, (C, C), 1)
    half_lo = lax.broadcasted_iota(jnp.int32, (C, 128), 1) < GLA_DK
    refs = (q_ref, k_ref, v_ref, g_ref)
    if zero_init:
        sf_ref[...] = jnp.zeros(sf_ref.shape, F32)
        sb_ref[...] = jnp.zeros(sb_ref.shape, F32)
    else:
        sf_ref[...] = args[6][...]
        sb_ref[...] = args[7][...]

    steps = min(GLA_STEPS_PER_ITER, n)
    assert n % steps == 0

    def body(i, carry):
        units = []
        for u in range(steps):
            c = i * steps + u
            for bi in range(nb):
                units.append((bi, c, True, row >= col, sf_ref, of_scr))
                units.append((bi, n - 1 - c, False, row <= col, sb_ref, ob_scr))
        _gla_chunks(units, *refs, half_lo)
        return carry

    lax.fori_loop(0, n // steps, body, 0)

    rt = GLA_OUT_ROWS
    gn = gn_ref[...]

    def finish(i, carry):
        rows = pl.ds(pl.multiple_of(i * rt, rt), rt)
        for bi in range(nb):
            o = of_scr[bi, rows, :] + ob_scr[bi, rows, :]
            gate = gate_ref[bi, rows, :].astype(F32)
            for hd in range(GLA_HEADS):
                sl = slice(hd * GLA_DV, (hd + 1) * GLA_DV)
                o_ref[bi, rows, sl] = (_rms(o[:, sl], gn) * gate[:, sl]).astype(BF16)
        return carry

    lax.fori_loop(0, t // rt, finish, 0)


GLA_STEPS_PER_ITER = 4
GLA_OUT_ROWS = 256


def _gla(q, k, v, g, gate, gla_g, init_states, nb, cast=()):
    b, t, _ = q.shape
    seq = lambda w: pl.BlockSpec((nb, t, w), lambda i: (i, 0, 0))
    st = pl.BlockSpec((nb, D_QK, GLA_DV), lambda i: (i, 0, 0))
    zero_init = init_states is None
    steps = b // nb
    slabs = [pl.BlockSpec((a.shape[0] // steps, a.shape[1]), lambda i: (i, 0)) for a in cast]
    return pl.pallas_call(
        functools.partial(_gla_kernel, zero_init=zero_init, n_cast=len(cast)),
        grid=(b // nb,),
        in_specs=[seq(D_QK), seq(D_QK), seq(D_V), seq(2 * D_QK), seq(D_V),
                  pl.BlockSpec((1, GLA_DV), lambda i: (0, 0))] + ([] if zero_init else [st, st]) + slabs,
        out_specs=[seq(D_V), st, st] + slabs,
        out_shape=[jax.ShapeDtypeStruct((b, t, D_V), BF16),
                   jax.ShapeDtypeStruct((b, D_QK, GLA_DV), F32),
                   jax.ShapeDtypeStruct((b, D_QK, GLA_DV), F32)]
        + [jax.ShapeDtypeStruct(a.shape, BF16) for a in cast],
        scratch_shapes=[pltpu.VMEM((nb, t, D_V), F32), pltpu.VMEM((nb, t, D_V), F32)],
        compiler_params=pltpu.CompilerParams(
            dimension_semantics=("arbitrary",), vmem_limit_bytes=VMEM_LIMIT),
        name="gla",
    )(q, k, v, g, gate, gla_g, *(() if zero_init else init_states), *cast)


ATT_KEY_BLOCK = 256
ATT_Q_TILE = 256
LOG2E = 1.4426950408889634


ATT_TILES_LONG = 2
ATT_TILES_SHORT = 4


def _attn_kernel(*args, keys_per_tile, cached):
    qt_ref, k_ref, vt_ref = args[:3]
    kc_ref, vtc_ref = args[3:5] if cached else (None, None)
    o_ref, s0, s1, m0, m1, a0, a1 = args[-7:]
    tq = ATT_Q_TILE
    tiles = qt_ref.shape[1] // tq
    t_new = k_ref.shape[0] // tiles if keys_per_tile else k_ref.shape[0]
    tk = t_new + (kc_ref.shape[0] if cached else 0)
    kb = min(ATT_KEY_BLOCK, tk)
    nkb = tk // kb
    zero = jnp.zeros((HEAD_DIM, tq), BF16)
    ones = jnp.ones((16, kb), BF16)
    items = [(qi, kv) for qi in range(tiles) for kv in range(ATT_KV_HEADS)]
    bufs = (s0, s1)
    mbufs = (m0, m1)
    abufs = (a0, a1)
    separate = nkb > 1

    def q_ext(qi, kv):
        cols = []
        for g in range(GROUP):
            hd = kv * GROUP + g
            qt = qt_ref[hd * HEAD_DIM:(hd + 1) * HEAD_DIM, qi * tq:(qi + 1) * tq]
            cols.append(jnp.concatenate([qt, zero] if kv == 0 else [zero, qt], axis=0))
        return jnp.concatenate(cols, axis=1)

    def key0(qi):
        return qi * tk if keys_per_tile else 0

    def score_piece(n, q_all, i, m):
        r0 = key0(items[n][0]) + i * kb
        keys = k_ref[r0:r0 + kb, :] if i * kb < t_new else kc_ref[r0 - t_new:r0 - t_new + kb, :]
        s = _dot(keys, q_all)
        bufs[n % 2][i * kb:(i + 1) * kb, :] = s
        for r in range(kb // 8):
            tile = s[r * 8:(r + 1) * 8, :]
            m = tile if m is None else jnp.maximum(m, tile)
        return m

    def value_piece(n, m, i, acc):
        qi, kv = items[n]
        c0 = key0(qi) + i * kb
        p = jnp.exp2(bufs[n % 2][i * kb:(i + 1) * kb, :] - m).astype(BF16)
        hd_rows = slice(kv * HEAD_DIM, (kv + 1) * HEAD_DIM)
        vt = vt_ref[hd_rows, c0:c0 + kb] if i * kb < t_new else vtc_ref[hd_rows, c0 - t_new:c0 - t_new + kb]
        vt1 = jnp.concatenate([vt, ones], axis=0)
        d = _dot(vt1, p)
        return d if acc is None else acc + d

    def emit(n, acc):
        qi, kv = items[n]
        o = acc[:HEAD_DIM] / acc[HEAD_DIM:HEAD_DIM + 1]
        for g in range(0, GROUP, 2):
            two = jnp.concatenate([o[:, g * tq:(g + 1) * tq], o[:, (g + 1) * tq:(g + 2) * tq]], axis=0)
            hd = kv * GROUP + g
            o_ref[qi * tq:(qi + 1) * tq, hd * HEAD_DIM:(hd + 2) * HEAD_DIM] = two.T.astype(BF16)

    def pair(n):
        if separate and n >= 2:
            emit(n - 2, abufs[n % 2][...])
        if n > len(items):
            return
        q_all = q_ext(*items[n]) if n < len(items) else None
        m_prev = mbufs[(n - 1) % 2][...] if n > 0 else None
        m_new = acc = None
        for i in range(nkb):
            if n < len(items):
                m_new = score_piece(n, q_all, i, m_new)
            if n > 0:
                acc = value_piece(n - 1, m_prev, i, acc)
        if n > 0 and separate:
            abufs[(n - 1) % 2][...] = acc
        elif n > 0:
            emit(n - 1, acc)
        if n < len(items):
            mbufs[n % 2][...] = jnp.max(m_new, axis=0, keepdims=True)

    for n in range(len(items) + (2 if separate else 1)):
        if separate:
            pl.when(pl.program_id(0) > -(n + 1))(functools.partial(pair, n))
        else:
            pair(n)


def _attention(qt, k, vt, batch, cache=None):
    ntok = qt.shape[1]
    t_new = k.shape[0] // batch
    tq = ATT_Q_TILE
    nq = ntok // batch // tq
    tiles = ATT_TILES_LONG if t_new > ATT_KEY_BLOCK else ATT_TILES_SHORT
    keys_per_tile = nq < tiles
    extra_specs, extra, tk = [], (), t_new
    if keys_per_tile:
        assert cache is None
        seqs = tiles // nq
        k_spec = pl.BlockSpec((seqs * t_new, D_KA), lambda t: (t, 0))
        vt_spec = pl.BlockSpec((D_KA, seqs * t_new), lambda t: (0, t))
    else:
        spq = nq // tiles
        k_spec = pl.BlockSpec((t_new, D_KA), lambda t: (t // spq, 0))
        vt_spec = pl.BlockSpec((D_KA, t_new), lambda t: (0, t // spq))
        if cache is not None:
            t_old = cache[0].shape[0] // batch
            assert t_new % ATT_KEY_BLOCK == 0 and t_old % ATT_KEY_BLOCK == 0
            extra_specs = [pl.BlockSpec((t_old, D_KA), lambda t: (t // spq, 0)),
                           pl.BlockSpec((D_KA, t_old), lambda t: (0, t // spq))]
            extra, tk = cache, t_new + t_old
    return pl.pallas_call(
        functools.partial(_attn_kernel, keys_per_tile=keys_per_tile, cached=cache is not None),
        grid=(ntok // (tiles * tq),),
        in_specs=[pl.BlockSpec((D_QA, tiles * tq), lambda t: (0, t)), k_spec, vt_spec] + extra_specs,
        out_specs=pl.BlockSpec((tiles * tq, D_QA), lambda t: (t, 0)),
        out_shape=jax.ShapeDtypeStruct((ntok, D_QA), BF16),
        scratch_shapes=[pltpu.VMEM((tk, GROUP * tq), F32), pltpu.VMEM((tk, GROUP * tq), F32),
                        pltpu.VMEM((1, GROUP * tq), F32), pltpu.VMEM((1, GROUP * tq), F32),
                        pltpu.VMEM((HEAD_DIM + 16, GROUP * tq), F32),
                        pltpu.VMEM((HEAD_DIM + 16, GROUP * tq), F32)],
        compiler_params=pltpu.CompilerParams(
            dimension_semantics=("arbitrary",), vmem_limit_bytes=VMEM_LIMIT),
        name="attention",
    )(qt, k, vt, *extra)


def _post_kernel(og_ref, oa_ref, x_ref, mod_ref, n2_ref, fg_ref, wo_ref, w1_ref, w3_ref, w2_ref,
                 y_ref):
    D = D_MODEL
    mod = mod_ref[0]
    g1, sh2, sc2, g2 = (mod[:, i * D:(i + 1) * D] for i in range(2, 6))
    out = _dot(og_ref[...], wo_ref[0:D_V, :]) + _dot(oa_ref[...], wo_ref[D_V:D_V + D_QA, :])
    x1 = x_ref[...] + g1 * out
    hb = (_rms(x1, n2_ref[...]) * (1.0 + sc2) + sh2).astype(BF16)
    act = (_silu(_dot(hb, w1_ref[...])) * _dot(hb, w3_ref[...])).astype(BF16)
    x2 = x1 + g2 * _dot(act, w2_ref[...])
    y_ref[...] = _rms(x2, fg_ref[...])


def _post_mixer(o_gla, o_att, x, mod3, mod_row0, tokens_per_mod, weights, tm):
    ntok = x.shape[0]
    tiles_per_mod = tokens_per_mod // tm
    whole = pl.BlockSpec(memory_space=pltpu.VMEM)
    row = lambda w: pl.BlockSpec((tm, w), lambda i: (i, 0))
    return pl.pallas_call(
        _post_kernel,
        grid=(ntok // tm,),
        in_specs=[row(D_V), row(D_QA), row(D_MODEL),
                  pl.BlockSpec((1, 1, N_MOD * D_MODEL), lambda i: (mod_row0 + i // tiles_per_mod, 0, 0)),
                  whole, whole, whole, whole, whole, whole],
        out_specs=row(D_MODEL),
        out_shape=jax.ShapeDtypeStruct((ntok, D_MODEL), F32),
        compiler_params=pltpu.CompilerParams(
            dimension_semantics=("arbitrary",), vmem_limit_bytes=VMEM_LIMIT),
        name="post_mixer",
    )(o_gla, o_att, x, mod3, *weights)


def _rope_tables(t):
    rows = t // GRID_W
    row = np.repeat(np.arange(rows, dtype=np.float64), GRID_W)
    col = np.tile(np.arange(GRID_W, dtype=np.float64), rows)
    inv = ROPE_THETA ** (-np.arange(0, ROPE_AXIS_DIM, 2, dtype=np.float64) / ROPE_AXIS_DIM)
    ang_r = row[:, None] * inv
    ang_c = col[:, None] * inv
    cos = np.concatenate([np.cos(ang_r)] * 2 + [np.cos(ang_c)] * 2, axis=-1)
    sin = np.concatenate([-np.sin(ang_r), np.sin(ang_r), -np.sin(ang_c), np.sin(ang_c)], axis=-1)
    tile = lambda a: jnp.asarray(np.tile(a, (1, ATT_KV_HEADS)), F32)
    return tile(cos), tile(sin)


def _block_mean(width, group):
    return jnp.asarray(np.kron(np.eye(width // group), np.ones((group, group))) / group, BF16)


def kernel(x_prompt, x_sample, cache_k, cache_v, state_gla_fwd, state_gla_bwd, c, c_ctx,
           ada_w, ada_b, norm1_g, norm2_g, w_in, w_gk2, b_gk2, gla_norm_g, q_norm_g, k_norm_g,
           w_out, w_ffn1, w_ffn3, w_ffn2, final_g):
    B, T, D = x_prompt.shape
    BL, TL, _ = x_sample.shape
    TP = cache_k.shape[2]
    l = 0

    mod3, w_in_r = _adaln(c_ctx, c, ada_w[l], ada_b[l], jnp.transpose(w_in[l]))
    zero = jnp.zeros((GLA_LOWRANK, D_QK), F32)
    w_gk = jnp.concatenate([jnp.concatenate([w_gk2[l, 0], zero], axis=1),
                            jnp.concatenate([zero, w_gk2[l, 1]], axis=1)], axis=0).astype(BF16)
    pre_w = (norm1_g[l].reshape(1, D), w_in_r, w_gk, b_gk2[l].reshape(1, 2 * D_QK),
             jnp.tile(q_norm_g[l], ATT_HEADS).reshape(1, D_QA),
             jnp.tile(k_norm_g[l], ATT_KV_HEADS).reshape(1, D_KA),
             _block_mean(D_QA, HEAD_DIM), _block_mean(D_KA, HEAD_DIM))
    post_norms = (norm2_g[l].reshape(1, D), final_g.reshape(1, D))
    post_w_f32 = (w_out[l], w_ffn1[l], w_ffn3[l], w_ffn2[l])
    gla_g = gla_norm_g[l].reshape(1, GLA_DV)

    xc = x_prompt.reshape(B * T, D)
    q, k, v, gate, g, qa, ka, va, kc, vc = _pre_mixer(
        xc, mod3, 0, B * T, T, pre_w, None, True, PRE_TILE_ROWS)
    r3 = lambda a, b, t: a.reshape(b, t, a.shape[-1])
    o_gla, s_f, s_b, *post_w_bf16 = _gla(
        r3(q, B, T), r3(k, B, T), r3(v, B, T), r3(g, B, T), r3(gate, B, T), gla_g, None,
        GLA_CONTEXT_SEQS, cast=post_w_f32)
    post_w = post_norms + tuple(post_w_bf16)
    o_att = _attention(qa, ka, va, B)
    y_prompt = _post_mixer(o_gla.reshape(B * T, D_V), o_att, xc, mod3, 0, B * T, post_w, POST_TILE_ROWS)

    xl = x_sample.reshape(BL * TL, D)
    q, k, v, gate, g, qa, ka, va = _pre_mixer(
        xl, mod3, 1, TL, TL, pre_w, _rope_tables(TL), False, PRE_TILE_ROWS)
    o_gla, _, _ = _gla(r3(q, BL, TL), r3(k, BL, TL), r3(v, BL, TL), r3(g, BL, TL), r3(gate, BL, TL),
                       gla_g, (state_gla_fwd[:, l].reshape(BL, D_QK, GLA_DV),
                               state_gla_bwd[:, l].reshape(BL, D_QK, GLA_DV)), BL)

    k_old = cache_k[:, l].reshape(BL * TP, D_KA).astype(BF16)
    vt_old = jnp.transpose(cache_v[:, l], (2, 3, 0, 1)).reshape(D_KA, BL * TP).astype(BF16)
    o_att = _attention(qa, ka, va, BL, (k_old, vt_old))
    y_sample = _post_mixer(o_gla.reshape(BL * TL, D_V), o_att, xl, mod3, 1, TL, post_w, POST_TILE_ROWS)

    def cache_out(ct):
        return jnp.transpose(ct.reshape(B, 1, ATT_KV_HEADS, HEAD_DIM, T), (0, 1, 4, 2, 3))

    return (y_prompt.reshape(B, T, D), y_sample.reshape(BL, TL, D), cache_out(kc), cache_out(vc),
            s_f.reshape(B, 1, GLA_HEADS, GLA_DK, GLA_DV), s_b.reshape(B, 1, GLA_HEADS, GLA_DK, GLA_DV))
```

```python
import functools

import numpy as np
import jax
import jax.numpy as jnp
from jax import lax
from jax.experimental import pallas as pl
from jax.experimental.pallas import tpu as pltpu

F32 = jnp.float32
BF16 = jnp.bfloat16

D_MODEL = 1024
GRID_W = 64
GLA_HEADS = 4
GLA_DK = 64
GLA_DV = 128
GLA_LOWRANK = 16
GLA_GATE_NORM = 16.0
GLA_CHUNK = 64
ATT_HEADS = 8
ATT_KV_HEADS = 2
HEAD_DIM = 64
ROPE_AXIS_DIM = HEAD_DIM // 2
ROPE_THETA = 10000.0
N_MOD = 6
EPS = 1e-6

D_QK = GLA_HEADS * GLA_DK
D_V = GLA_HEADS * GLA_DV
D_QA = ATT_HEADS * HEAD_DIM
D_KA = ATT_KV_HEADS * HEAD_DIM
GROUP = ATT_HEADS // ATT_KV_HEADS
O_Q, O_K, O_V, O_OG = 0, D_QK, 2 * D_QK, 2 * D_QK + D_V
O_QA = O_OG + D_V
O_KA = O_QA + D_QA
O_VA = O_KA + D_KA
O_LR = O_VA + D_KA
D_PROJ = O_LR + 2 * GLA_LOWRANK

COND_ROWS = 8
VMEM_LIMIT = 56 * 1024 * 1024
PRE_TILE_ROWS = 1024
POST_TILE_ROWS = 512
GLA_CONTEXT_SEQS = 4


def _dot(a, b):
    return jnp.dot(a, b, preferred_element_type=F32)


def _dot_nt(a, b):
    return lax.dot_general(a, b, (((1,), (1,)), ((), ())), preferred_element_type=F32)


def _dot_tn(a, b):
    return lax.dot_general(a, b, (((0,), (0,)), ((), ())), preferred_element_type=F32)


def _rms(x, g):
    ms = jnp.mean(x * x, axis=-1, keepdims=True)
    return x * lax.rsqrt(ms + EPS) * g


def _silu(x):
    return x * jax.nn.sigmoid(x)


ADALN_K_BLOCK = 128


def _adaln_kernel(cctx_ref, c_ref, w_ref, b_ref, wint_ref, o_ref, win_o, acc):
    k = pl.program_id(0)

    lr0, lr1 = O_OG + D_V, O_OG + D_V + 2 * GLA_LOWRANK
    feats = jnp.concatenate([wint_ref[0:lr0, :], wint_ref[lr1:D_PROJ, :], wint_ref[lr0:lr1, :]],
                            axis=0)
    for j in range(0, D_PROJ, 128):
        rows = min(128, D_PROJ - j)
        tile = feats[j:j + rows, :]
        if rows < 128:
            tile = jnp.concatenate([tile, jnp.zeros((128 - rows, tile.shape[1]), F32)], axis=0)
        win_o[:, j:j + rows] = tile.T[:, 0:rows].astype(BF16)

    @pl.when(k == 0)
    def _():
        acc[...] = jnp.broadcast_to(b_ref[...], acc.shape)

    pad = jnp.zeros((COND_ROWS - 1 - c_ref.shape[0], c_ref.shape[1]), F32)
    cond = jnp.concatenate([cctx_ref[...], c_ref[...], pad], axis=0)
    acc[...] += _dot(_silu(cond).astype(BF16), w_ref[...].astype(BF16))

    @pl.when(k == pl.num_programs(0) - 1)
    def _():
        o_ref[:, 0, :] = acc[...]


def _adaln(c_ctx, c, ada_w, ada_b, w_in_t):
    d, n = ada_w.shape
    tk = ADALN_K_BLOCK
    assert 1 + c.shape[0] <= COND_ROWS and w_in_t.shape == (D_PROJ, d)
    return pl.pallas_call(
        _adaln_kernel,
        grid=(d // tk,),
        in_specs=[
            pl.BlockSpec((1, tk), lambda k: (0, k)),
            pl.BlockSpec((c.shape[0], tk), lambda k: (0, k)),
            pl.BlockSpec((tk, n), lambda k: (k, 0)),
            pl.BlockSpec((1, n), lambda k: (0, 0)),
            pl.BlockSpec((D_PROJ, tk), lambda k: (0, k)),
        ],
        out_specs=[pl.BlockSpec((COND_ROWS, 1, n), lambda k: (0, 0, 0)),
                   pl.BlockSpec((tk, D_PROJ), lambda k: (k, 0))],
        out_shape=[jax.ShapeDtypeStruct((COND_ROWS, 1, n), F32),
                   jax.ShapeDtypeStruct((d, D_PROJ), BF16)],
        scratch_shapes=[pltpu.VMEM((COND_ROWS, n), F32)],
        compiler_params=pltpu.CompilerParams(
            dimension_semantics=("arbitrary",), vmem_limit_bytes=VMEM_LIMIT),
        name="adaln",
    )(c_ctx.reshape(1, d), c, ada_w, ada_b.reshape(1, n), w_in_t)


def _swap_halves(x, lane_lo):
    n = x.shape[-1]
    up = pltpu.roll(x, n - ROPE_AXIS_DIM // 2, axis=1)
    dn = pltpu.roll(x, ROPE_AXIS_DIM // 2, axis=1)
    return jnp.where(lane_lo, up, dn)


PRE_SUB_ROWS = 256


def _pre_kernel(*refs, rope, emit_cache):
    it = iter(refs)
    x_ref, mod_ref, n1_ref, win_ref, wgk_ref, bgk_ref = (next(it) for _ in range(6))
    qg_ref, kg_ref, bdq_ref, bdk_ref = (next(it) for _ in range(4))
    if rope:
        cos_ref, sin_ref = next(it), next(it)
    q_o, k_o, v_o, gate_o, g_o, qa_o, ka_o, va_o = (next(it) for _ in range(8))
    if emit_cache:
        kc_o, vc_o = next(it), next(it)

    sub = PRE_SUB_ROWS
    n_sub = x_ref.shape[0] // sub
    mod = mod_ref[0]
    sh1 = mod[:, 0:D_MODEL]
    sc1 = mod[:, D_MODEL:2 * D_MODEL]

    def project(j):
        rows = slice(j * sub, (j + 1) * sub)
        h = _rms(x_ref[rows, :], n1_ref[...]) * (1.0 + sc1) + sh1
        return _dot(h.astype(BF16), win_ref[...])

    def finish(j, p_ref):
        rows = slice(j * sub, (j + 1) * sub)
        q_o[rows, :] = p_ref[:, O_Q:O_Q + D_QK].astype(BF16)
        k_o[rows, :] = p_ref[:, O_K:O_K + D_QK].astype(BF16)
        v_o[rows, :] = p_ref[:, O_V:O_V + D_V].astype(BF16)
        gate_o[rows, :] = _silu(p_ref[:, O_OG:O_OG + D_V]).astype(BF16)

        gk = _dot(p_ref[:, O_LR:O_LR + 2 * GLA_LOWRANK].astype(BF16), wgk_ref[...]) + bgk_ref[...]
        g_o[rows, :] = ((jnp.minimum(gk, 0.0) - jnp.log(1.0 + jnp.exp(-jnp.abs(gk))))
                        * (1.0 / GLA_GATE_NORM))

        qa = p_ref[:, O_QA:O_QA + D_QA]
        ka = p_ref[:, O_KA:O_KA + D_KA]
        va = p_ref[:, O_VA:O_VA + D_KA]
        qa = qa * lax.rsqrt(_dot((qa * qa).astype(BF16), bdq_ref[...]) + EPS) * qg_ref[...]
        ka = ka * lax.rsqrt(_dot((ka * ka).astype(BF16), bdk_ref[...]) + EPS) * kg_ref[...]
        if rope:
            lane_q = lax.broadcasted_iota(jnp.int32, qa.shape, 1)
            lane_k = lax.broadcasted_iota(jnp.int32, ka.shape, 1)
            lo_q = (lane_q % ROPE_AXIS_DIM) < ROPE_AXIS_DIM // 2
            lo_k = (lane_k % ROPE_AXIS_DIM) < ROPE_AXIS_DIM // 2
            cos_k = cos_ref[rows, :]
            sin_k = sin_ref[rows, :]
            cos_q = jnp.concatenate([cos_k] * GROUP, axis=1)
            sin_q = jnp.concatenate([sin_k] * GROUP, axis=1)
            qa = qa * cos_q + _swap_halves(qa, lo_q) * sin_q
            ka = ka * cos_k + _swap_halves(ka, lo_k) * sin_k
        vat = va.T
        if emit_cache:
            kat = ka.T
            t = kc_o.shape[2]
            for sq in range(sub // t):
                kc_o[j * (sub // t) + sq] = kat[:, sq * t:(sq + 1) * t]
                vc_o[j * (sub // t) + sq] = vat[:, sq * t:(sq + 1) * t]
        qa_o[:, rows] = (qa * (HEAD_DIM ** -0.5 * LOG2E)).T.astype(BF16)
        ka_o[rows, :] = ka.astype(BF16)
        va_o[:, rows] = vat.astype(BF16)

    p_prev = None
    for n in range(n_sub + 1):
        p_new = project(n) if n < n_sub else None
        if n > 0:
            finish(n - 1, p_prev)
        p_prev = p_new


def _pre_mixer(x, mod3, mod_row0, tokens_per_mod, seq_len, weights, rope_tabs, emit_cache, tm):
    ntok = x.shape[0]
    n1, w_in_r, w_gk, b_gk, qg, kg, bdq, bdk = weights
    rope = rope_tabs is not None
    tiles_per_mod = tokens_per_mod // tm
    tiles_per_seq = seq_len // tm
    whole = pl.BlockSpec(memory_space=pltpu.VMEM)
    row = lambda w: pl.BlockSpec((tm, w), lambda i: (i, 0))
    in_specs = [
        row(D_MODEL),
        pl.BlockSpec((1, 1, N_MOD * D_MODEL), lambda i: (mod_row0 + i // tiles_per_mod, 0, 0)),
        whole, whole, whole, whole, whole, whole, whole, whole,
    ]
    args = [x, mod3, n1, w_in_r, w_gk, b_gk, qg, kg, bdq, bdk]
    if rope:
        tab = pl.BlockSpec((tm, D_KA), lambda i: (i % tiles_per_seq, 0))
        in_specs += [tab, tab]
        args += list(rope_tabs)
    col = lambda w: pl.BlockSpec((w, tm), lambda i: (0, i))
    out_specs = [row(D_QK), row(D_QK), row(D_V), row(D_V), row(2 * D_QK),
                 col(D_QA), row(D_KA), col(D_KA)]
    out_shape = [
        jax.ShapeDtypeStruct((ntok, D_QK), BF16),
        jax.ShapeDtypeStruct((ntok, D_QK), BF16),
        jax.ShapeDtypeStruct((ntok, D_V), BF16),
        jax.ShapeDtypeStruct((ntok, D_V), BF16),
        jax.ShapeDtypeStruct((ntok, 2 * D_QK), F32),
        jax.ShapeDtypeStruct((D_QA, ntok), BF16),
        jax.ShapeDtypeStruct((ntok, D_KA), BF16),
        jax.ShapeDtypeStruct((D_KA, ntok), BF16),
    ]
    if emit_cache:
        seqs = tm // seq_len
        out_specs += [pl.BlockSpec((seqs, D_KA, seq_len), lambda i: (i, 0, 0))] * 2
        out_shape += [jax.ShapeDtypeStruct((ntok // seq_len, D_KA, seq_len), F32)] * 2
    return pl.pallas_call(
        functools.partial(_pre_kernel, rope=rope, emit_cache=emit_cache),
        grid=(ntok // tm,),
        in_specs=in_specs,
        out_specs=out_specs,
        out_shape=out_shape,
        compiler_params=pltpu.CompilerParams(
            dimension_semantics=("arbitrary",), vmem_limit_bytes=VMEM_LIMIT),
        name="pre_mixer_rope" if rope else "pre_mixer",
    )(*args)


def _gla_chunks(units, q_ref, k_ref, v_ref, g_ref, half_lo):
    C = GLA_CHUNK
    n_pair = GLA_HEADS // 2
    rows = [pl.ds(pl.multiple_of(c * C, C), C) for (_, c, _, _, _, _) in units]

    bs = []
    for (bi, _, fwd, tri, _, _), r in zip(units, rows):
        gc = g_ref[bi, r, 0:D_QK] if fwd else g_ref[bi, r, D_QK:2 * D_QK]
        ltri = jnp.where(tri, 1.0, 0.0).astype(BF16)
        g_hi = gc.astype(BF16)
        g_lo = (gc - g_hi.astype(F32)).astype(BF16)
        bs.append(_dot(ltri, g_hi) + _dot(ltri, g_lo))

    qts, kts, kes, decays = [], [], [], []
    for (bi, _, fwd, _, _, _), r, b in zip(units, rows, bs):
        qc = q_ref[bi, r, :].astype(F32) * GLA_DK ** -0.5
        kc = k_ref[bi, r, :].astype(F32)
        btot = b[C - 1:C, :] if fwd else b[0:1, :]
        qts.append((qc * jnp.exp(b)).astype(BF16))
        kts.append((kc * jnp.exp(-b)).astype(BF16))
        kes.append((kc * jnp.exp(btot - b)).astype(BF16))
        decays.append(jnp.transpose(jnp.broadcast_to(jnp.exp(btot), (8, D_QK)))[:, 0:1])

    scores, incs, qms = [], [], []
    for (bi, _, _, tri, _, _), r, qt, kt, ke in zip(units, rows, qts, kts, kes):
        for pair in range(n_pair):
            lanes = slice(pair * 128, (pair + 1) * 128)
            incs.append(_dot_tn(ke[:, lanes], v_ref[bi, r, pair * 2 * GLA_DV:(pair + 1) * 2 * GLA_DV]))
            for hh in range(2):
                qm = jnp.where(half_lo if hh == 0 else ~half_lo, qt[:, lanes], jnp.zeros((C, 128), BF16))
                qms.append(qm)
                scores.append(jnp.where(tri, _dot_nt(qm, kt[:, lanes]), 0.0).astype(BF16))

    for ui, ((bi, _, _, _, s_ref, o_ref), r) in enumerate(zip(units, rows)):
        state = s_ref[bi]
        state_b = state.astype(BF16)
        for hd in range(GLA_HEADS):
            pair = hd // 2
            lhs = jnp.concatenate([qms[ui * GLA_HEADS + hd], scores[ui * GLA_HEADS + hd]], axis=1)
            rhs = jnp.concatenate([state_b[pair * 128:(pair + 1) * 128, :],
                                   v_ref[bi, r, hd * GLA_DV:(hd + 1) * GLA_DV]], axis=0)
            o_ref[bi, r, hd * GLA_DV:(hd + 1) * GLA_DV] = _dot(lhs, rhs)
        inc = jnp.concatenate(
            [incs[ui * n_pair + hd // 2][(hd % 2) * GLA_DK:(hd % 2 + 1) * GLA_DK,
                                         (hd % 2) * GLA_DV:(hd % 2 + 1) * GLA_DV]
             for hd in range(GLA_HEADS)], axis=0)
        s_ref[bi] = decays[ui] * state + inc


def _gla_kernel(*args, zero_init, n_cast):
    q_ref, k_ref, v_ref, g_ref, gate_ref, gn_ref = args[:6]
    n_in = 6 + (0 if zero_init else 2)
    cast_src = args[n_in:n_in + n_cast]
    o_ref, sf_ref, sb_ref = args[n_in + n_cast:n_in + n_cast + 3]
    cast_dst = args[n_in + n_cast + 3:n_in + 2 * n_cast + 3]
    of_scr, ob_scr = args[-2:]
    for src, dst in zip(cast_src, cast_dst):
        dst[...] = src[...].astype(BF16)
    C = GLA_CHUNK
    nb, t, _ = q_ref.shape
    n = t // C
    row = lax.broadcasted_iota(jnp.int32, (C, C), 0)
    col = lax.broadcasted_iota(jnp.int32, (C, C), 1)
    half_lo = lax.broadcasted_iota(jnp.int32, (C, 128), 1) < GLA_DK
    refs = (q_ref, k_ref, v_ref, g_ref)
    if zero_init:
        sf_ref[...] = jnp.zeros(sf_ref.shape, F32)
        sb_ref[...] = jnp.zeros(sb_ref.shape, F32)
    else:
        sf_ref[...] = args[6][...]
        sb_ref[...] = args[7][...]

    steps = min(GLA_STEPS_PER_ITER, n)
    assert n % steps == 0

    def body(i, carry):
        units = []
        for u in range(steps):
            c = i * steps + u
            for bi in range(nb):
                units.append((bi, c, True, row >= col, sf_ref, of_scr))
                units.append((bi, n - 1 - c, False, row <= col, sb_ref, ob_scr))
        _gla_chunks(units, *refs, half_lo)
        return carry

    lax.fori_loop(0, n // steps, body, 0)

    rt = GLA_OUT_ROWS
    gn = gn_ref[...]

    def finish(i, carry):
        rows = pl.ds(pl.multiple_of(i * rt, rt), rt)
        for bi in range(nb):
            o = of_scr[bi, rows, :] + ob_scr[bi, rows, :]
            gate = gate_ref[bi, rows, :].astype(F32)
            for hd in range(GLA_HEADS):
                sl = slice(hd * GLA_DV, (hd + 1) * GLA_DV)
                o_ref[bi, rows, sl] = (_rms(o[:, sl], gn) * gate[:, sl]).astype(BF16)
        return carry

    lax.fori_loop(0, t // rt, finish, 0)


GLA_STEPS_PER_ITER = 4
GLA_OUT_ROWS = 256


def _gla(q, k, v, g, gate, gla_g, init_states, nb, cast=()):
    b, t, _ = q.shape
    seq = lambda w: pl.BlockSpec((nb, t, w), lambda i: (i, 0, 0))
    st = pl.BlockSpec((nb, D_QK, GLA_DV), lambda i: (i, 0, 0))
    zero_init = init_states is None
    steps = b // nb
    slabs = [pl.BlockSpec((a.shape[0] // steps, a.shape[1]), lambda i: (i, 0)) for a in cast]
    return pl.pallas_call(
        functools.partial(_gla_kernel, zero_init=zero_init, n_cast=len(cast)),
        grid=(b // nb,),
        in_specs=[seq(D_QK), seq(D_QK), seq(D_V), seq(2 * D_QK), seq(D_V),
                  pl.BlockSpec((1, GLA_DV), lambda i: (0, 0))] + ([] if zero_init else [st, st]) + slabs,
        out_specs=[seq(D_V), st, st] + slabs,
        out_shape=[jax.ShapeDtypeStruct((b, t, D_V), BF16),
                   jax.ShapeDtypeStruct((b, D_QK, GLA_DV), F32),
                   jax.ShapeDtypeStruct((b, D_QK, GLA_DV), F32)]
        + [jax.ShapeDtypeStruct(a.shape, BF16) for a in cast],
        scratch_shapes=[pltpu.VMEM((nb, t, D_V), F32), pltpu.VMEM((nb, t, D_V), F32)],
        compiler_params=pltpu.CompilerParams(
            dimension_semantics=("arbitrary",), vmem_limit_bytes=VMEM_LIMIT),
        name="gla",
    )(q, k, v, g, gate, gla_g, *(() if zero_init else init_states), *cast)


ATT_KEY_BLOCK = 256
ATT_Q_TILE = 256
LOG2E = 1.4426950408889634


ATT_TILES_LONG = 2
ATT_TILES_SHORT = 4


def _attn_kernel(*args, keys_per_tile, cached):
    qt_ref, k_ref, vt_ref = args[:3]
    kc_ref, vtc_ref = args[3:5] if cached else (None, None)
    o_ref, s0, s1, m0, m1, a0, a1 = args[-7:]
    tq = ATT_Q_TILE
    tiles = qt_ref.shape[1] // tq
    t_new = k_ref.shape[0] // tiles if keys_per_tile else k_ref.shape[0]
    tk = t_new + (kc_ref.shape[0] if cached else 0)
    kb = min(ATT_KEY_BLOCK, tk)
    nkb = tk // kb
    zero = jnp.zeros((HEAD_DIM, tq), BF16)
    ones = jnp.ones((16, kb), BF16)
    items = [(qi, kv) for qi in range(tiles) for kv in range(ATT_KV_HEADS)]
    bufs = (s0, s1)
    mbufs = (m0, m1)
    abufs = (a0, a1)
    separate = nkb > 1

    def q_ext(qi, kv):
        cols = []
        for g in range(GROUP):
            hd = kv * GROUP + g
            qt = qt_ref[hd * HEAD_DIM:(hd + 1) * HEAD_DIM, qi * tq:(qi + 1) * tq]
            cols.append(jnp.concatenate([qt, zero] if kv == 0 else [zero, qt], axis=0))
        return jnp.concatenate(cols, axis=1)

    def key0(qi):
        return qi * tk if keys_per_tile else 0

    def score_piece(n, q_all, i, m):
        r0 = key0(items[n][0]) + i * kb
        keys = k_ref[r0:r0 + kb, :] if i * kb < t_new else kc_ref[r0 - t_new:r0 - t_new + kb, :]
        s = _dot(keys, q_all)
        bufs[n % 2][i * kb:(i + 1) * kb, :] = s
        for r in range(kb // 8):
            tile = s[r * 8:(r + 1) * 8, :]
            m = tile if m is None else jnp.maximum(m, tile)
        return m

    def value_piece(n, m, i, acc):
        qi, kv = items[n]
        c0 = key0(qi) + i * kb
        p = jnp.exp2(bufs[n % 2][i * kb:(i + 1) * kb, :] - m).astype(BF16)
        hd_rows = slice(kv * HEAD_DIM, (kv + 1) * HEAD_DIM)
        vt = vt_ref[hd_rows, c0:c0 + kb] if i * kb < t_new else vtc_ref[hd_rows, c0 - t_new:c0 - t_new + kb]
        vt1 = jnp.concatenate([vt, ones], axis=0)
        d = _dot(vt1, p)
        return d if acc is None else acc + d

    def emit(n, acc):
        qi, kv = items[n]
        o = acc[:HEAD_DIM] / acc[HEAD_DIM:HEAD_DIM + 1]
        for g in range(0, GROUP, 2):
            two = jnp.concatenate([o[:, g * tq:(g + 1) * tq], o[:, (g + 1) * tq:(g + 2) * tq]], axis=0)
            hd = kv * GROUP + g
            o_ref[qi * tq:(qi + 1) * tq, hd * HEAD_DIM:(hd + 2) * HEAD_DIM] = two.T.astype(BF16)

    def pair(n):
        if separate and n >= 2:
            emit(n - 2, abufs[n % 2][...])
        if n > len(items):
            return
        q_all = q_ext(*items[n]) if n < len(items) else None
        m_prev = mbufs[(n - 1) % 2][...] if n > 0 else None
        m_new = acc = None
        for i in range(nkb):
            if n < len(items):
                m_new = score_piece(n, q_all, i, m_new)
            if n > 0:
                acc = value_piece(n - 1, m_prev, i, acc)
        if n > 0 and separate:
            abufs[(n - 1) % 2][...] = acc
        elif n > 0:
            emit(n - 1, acc)
        if n < len(items):
            mbufs[n % 2][...] = jnp.max(m_new, axis=0, keepdims=True)

    for n in range(len(items) + (2 if separate else 1)):
        if separate:
            pl.when(pl.program_id(0) > -(n + 1))(functools.partial(pair, n))
        else:
            pair(n)


def _attention(qt, k, vt, batch, cache=None):
    ntok = qt.shape[1]
    t_new = k.shape[0] // batch
    tq = ATT_Q_TILE
    nq = ntok // batch // tq
    tiles = ATT_TILES_LONG if t_new > ATT_KEY_BLOCK else ATT_TILES_SHORT
    keys_per_tile = nq < tiles
    extra_specs, extra, tk = [], (), t_new
    if keys_per_tile:
        assert cache is None
        seqs = tiles // nq
        k_spec = pl.BlockSpec((seqs * t_new, D_KA), lambda t: (t, 0))
        vt_spec = pl.BlockSpec((D_KA, seqs * t_new), lambda t: (0, t))
    else:
        spq = nq // tiles
        k_spec = pl.BlockSpec((t_new, D_KA), lambda t: (t // spq, 0))
        vt_spec = pl.BlockSpec((D_KA, t_new), lambda t: (0, t // spq))
        if cache is not None:
            t_old = cache[0].shape[0] // batch
            assert t_new % ATT_KEY_BLOCK == 0 and t_old % ATT_KEY_BLOCK == 0
            extra_specs = [pl.BlockSpec((t_old, D_KA), lambda t: (t // spq, 0)),
                           pl.BlockSpec((D_KA, t_old), lambda t: (0, t // spq))]
            extra, tk = cache, t_new + t_old
    return pl.pallas_call(
        functools.partial(_attn_kernel, keys_per_tile=keys_per_tile, cached=cache is not None),
        grid=(ntok // (tiles * tq),),
        in_specs=[pl.BlockSpec((D_QA, tiles * tq), lambda t: (0, t)), k_spec, vt_spec] + extra_specs,
        out_specs=pl.BlockSpec((tiles * tq, D_QA), lambda t: (t, 0)),
        out_shape=jax.ShapeDtypeStruct((ntok, D_QA), BF16),
        scratch_shapes=[pltpu.VMEM((tk, GROUP * tq), F32), pltpu.VMEM((tk, GROUP * tq), F32),
                        pltpu.VMEM((1, GROUP * tq), F32), pltpu.VMEM((1, GROUP * tq), F32),
                        pltpu.VMEM((HEAD_DIM + 16, GROUP * tq), F32),
                        pltpu.VMEM((HEAD_DIM + 16, GROUP * tq), F32)],
        compiler_params=pltpu.CompilerParams(
            dimension_semantics=("arbitrary",), vmem_limit_bytes=VMEM_LIMIT),
        name="attention",
    )(qt, k, vt, *extra)


def _post_kernel(og_ref, oa_ref, x_ref, mod_ref, n2_ref, fg_ref, wo_ref, w1_ref, w3_ref, w2_ref,
                 y_ref):
    D = D_MODEL
    mod = mod_ref[0]
    g1, sh2, sc2, g2 = (mod[:, i * D:(i + 1) * D] for i in range(2, 6))
    out = _dot(og_ref[...], wo_ref[0:D_V, :]) + _dot(oa_ref[...], wo_ref[D_V:D_V + D_QA, :])
    x1 = x_ref[...] + g1 * out
    hb = (_rms(x1, n2_ref[...]) * (1.0 + sc2) + sh2).astype(BF16)
    act = (_silu(_dot(hb, w1_ref[...])) * _dot(hb, w3_ref[...])).astype(BF16)
    x2 = x1 + g2 * _dot(act, w2_ref[...])
    y_ref[...] = _rms(x2, fg_ref[...])


def _post_mixer(o_gla, o_att, x, mod3, mod_row0, tokens_per_mod, weights, tm):
    ntok = x.shape[0]
    tiles_per_mod = tokens_per_mod // tm
    whole = pl.BlockSpec(memory_space=pltpu.VMEM)
    row = lambda w: pl.BlockSpec((tm, w), lambda i: (i, 0))
    return pl.pallas_call(
        _post_kernel,
        grid=(ntok // tm,),
        in_specs=[row(D_V), row(D_QA), row(D_MODEL),
                  pl.BlockSpec((1, 1, N_MOD * D_MODEL), lambda i: (mod_row0 + i // tiles_per_mod, 0, 0)),
                  whole, whole, whole, whole, whole, whole],
        out_specs=row(D_MODEL),
        out_shape=jax.ShapeDtypeStruct((ntok, D_MODEL), F32),
        compiler_params=pltpu.CompilerParams(
            dimension_semantics=("arbitrary",), vmem_limit_bytes=VMEM_LIMIT),
        name="post_mixer",
    )(o_gla, o_att, x, mod3, *weights)


def _rope_tables(t):
    rows = t // GRID_W
    row = np.repeat(np.arange(rows, dtype=np.float64), GRID_W)
    col = np.tile(np.arange(GRID_W, dtype=np.float64), rows)
    inv = ROPE_THETA ** (-np.arange(0, ROPE_AXIS_DIM, 2, dtype=np.float64) / ROPE_AXIS_DIM)
    ang_r = row[:, None] * inv
    ang_c = col[:, None] * inv
    cos = np.concatenate([np.cos(ang_r)] * 2 + [np.cos(ang_c)] * 2, axis=-1)
    sin = np.concatenate([-np.sin(ang_r), np.sin(ang_r), -np.sin(ang_c), np.sin(ang_c)], axis=-1)
    tile = lambda a: jnp.asarray(np.tile(a, (1, ATT_KV_HEADS)), F32)
    return tile(cos), tile(sin)


def _block_mean(width, group):
    return jnp.asarray(np.kron(np.eye(width // group), np.ones((group, group))) / group, BF16)


def kernel(x_prompt, x_sample, cache_k, cache_v, state_gla_fwd, state_gla_bwd, c, c_ctx,
           ada_w, ada_b, norm1_g, norm2_g, w_in, w_gk2, b_gk2, gla_norm_g, q_norm_g, k_norm_g,
           w_out, w_ffn1, w_ffn3, w_ffn2, final_g):
    B, T, D = x_prompt.shape
    BL, TL, _ = x_sample.shape
    TP = cache_k.shape[2]
    l = 0

    mod3, w_in_r = _adaln(c_ctx, c, ada_w[l], ada_b[l], jnp.transpose(w_in[l]))
    zero = jnp.zeros((GLA_LOWRANK, D_QK), F32)
    w_gk = jnp.concatenate([jnp.concatenate([w_gk2[l, 0], zero], axis=1),
                            jnp.concatenate([zero, w_gk2[l, 1]], axis=1)], axis=0).astype(BF16)
    pre_w = (norm1_g[l].reshape(1, D), w_in_r, w_gk, b_gk2[l].reshape(1, 2 * D_QK),
             jnp.tile(q_norm_g[l], ATT_HEADS).reshape(1, D_QA),
             jnp.tile(k_norm_g[l], ATT_KV_HEADS).reshape(1, D_KA),
             _block_mean(D_QA, HEAD_DIM), _block_mean(D_KA, HEAD_DIM))
    post_norms = (norm2_g[l].reshape(1, D), final_g.reshape(1, D))
    post_w_f32 = (w_out[l], w_ffn1[l], w_ffn3[l], w_ffn2[l])
    gla_g = gla_norm_g[l].reshape(1, GLA_DV)

    xc = x_prompt.reshape(B * T, D)
    q, k, v, gate, g, qa, ka, va, kc, vc = _pre_mixer(
        xc, mod3, 0, B * T, T, pre_w, None, True, PRE_TILE_ROWS)
    r3 = lambda a, b, t: a.reshape(b, t, a.shape[-1])
    o_gla, s_f, s_b, *post_w_bf16 = _gla(
        r3(q, B, T), r3(k, B, T), r3(v, B, T), r3(g, B, T), r3(gate, B, T), gla_g, None,
        GLA_CONTEXT_SEQS, cast=post_w_f32)
    post_w = post_norms + tuple(post_w_bf16)
    o_att = _attention(qa, ka, va, B)
    y_prompt = _post_mixer(o_gla.reshape(B * T, D_V), o_att, xc, mod3, 0, B * T, post_w, POST_TILE_ROWS)

    xl = x_sample.reshape(BL * TL, D)
    q, k, v, gate, g, qa, ka, va = _pre_mixer(
        xl, mod3, 1, TL, TL, pre_w, _rope_tables(TL), False, PRE_TILE_ROWS)
    o_gla, _, _ = _gla(r3(q, BL, TL), r3(k, BL, TL), r3(v, BL, TL), r3(g, BL, TL), r3(gate, BL, TL),
                       gla_g, (state_gla_fwd[:, l].reshape(BL, D_QK, GLA_DV),
                               state_gla_bwd[:, l].reshape(BL, D_QK, GLA_DV)), BL)

    k_old = cache_k[:, l].reshape(BL * TP, D_KA).astype(BF16)
    vt_old = jnp.transpose(cache_v[:, l], (2, 3, 0, 1)).reshape(D_KA, BL * TP).astype(BF16)
    o_att = _attention(qa, ka, va, BL, (k_old, vt_old))
    y_sample = _post_mixer(o_gla.reshape(BL * TL, D_V), o_att, xl, mod3, 1, TL, post_w, POST_TILE_ROWS)

    def cache_out(ct):
        return jnp.transpose(ct.reshape(B, 1, ATT_KV_HEADS, HEAD_DIM, T), (0, 1, 4, 2, 3))

    return (y_prompt.reshape(B, T, D), y_sample.reshape(BL, TL, D), cache_out(kc), cache_out(vc),
            s_f.reshape(B, 1, GLA_HEADS, GLA_DK, GLA_DV), s_b.reshape(B, 1, GLA_HEADS, GLA_DK, GLA_DV))
```

```python
import functools

import numpy as np
import jax
import jax.numpy as jnp
from jax import lax
from jax.experimental import pallas as pl
from jax.experimental.pallas import tpu as pltpu

F32 = jnp.float32
BF16 = jnp.bfloat16

D_MODEL = 1024
GRID_W = 64
GLA_HEADS = 4
GLA_DK = 64
GLA_DV = 128
GLA_LOWRANK = 16
GLA_GATE_NORM = 16.0
GLA_CHUNK = 64
ATT_HEADS = 8
ATT_KV_HEADS = 2
HEAD_DIM = 64
ROPE_AXIS_DIM = HEAD_DIM // 2
ROPE_THETA = 10000.0
N_MOD = 6
EPS = 1e-6

D_QK = GLA_HEADS * GLA_DK
D_V = GLA_HEADS * GLA_DV
D_QA = ATT_HEADS * HEAD_DIM
D_KA = ATT_KV_HEADS * HEAD_DIM
GROUP = ATT_HEADS // ATT_KV_HEADS
O_Q, O_K, O_V, O_OG = 0, D_QK, 2 * D_QK, 2 * D_QK + D_V
O_QA = O_OG + D_V
O_KA = O_QA + D_QA
O_VA = O_KA + D_KA
O_LR = O_VA + D_KA
D_PROJ = O_LR + 2 * GLA_LOWRANK

COND_ROWS = 8
VMEM_LIMIT = 56 * 1024 * 1024
PRE_TILE_ROWS = 1024
POST_TILE_ROWS = 256
GLA_CONTEXT_SEQS = 4


def _dot(a, b):
    return jnp.dot(a, b, preferred_element_type=F32)


def _dot_nt(a, b):
    return lax.dot_general(a, b, (((1,), (1,)), ((), ())), preferred_element_type=F32)


def _dot_tn(a, b):
    return lax.dot_general(a, b, (((0,), (0,)), ((), ())), preferred_element_type=F32)


def _rms(x, g):
    ms = jnp.mean(x * x, axis=-1, keepdims=True)
    return x * lax.rsqrt(ms + EPS) * g


def _silu(x):
    return x * jax.nn.sigmoid(x)


ADALN_K_BLOCK = 256


def _adaln_kernel(cctx_ref, c_ref, w_ref, b_ref, wint_ref, o_ref, win_o, acc):
    k = pl.program_id(0)

    lr0, lr1 = O_OG + D_V, O_OG + D_V + 2 * GLA_LOWRANK
    feats = jnp.concatenate([wint_ref[0:lr0, :], wint_ref[lr1:D_PROJ, :], wint_ref[lr0:lr1, :]],
                            axis=0)
    for j in range(0, D_PROJ, 128):
        rows = min(128, D_PROJ - j)
        tile = feats[j:j + rows, :]
        if rows < 128:
            tile = jnp.concatenate([tile, jnp.zeros((128 - rows, tile.shape[1]), F32)], axis=0)
        win_o[:, j:j + rows] = tile.T[:, 0:rows].astype(BF16)

    @pl.when(k == 0)
    def _():
        acc[...] = jnp.broadcast_to(b_ref[...], acc.shape)

    pad = jnp.zeros((COND_ROWS - 1 - c_ref.shape[0], c_ref.shape[1]), F32)
    cond = jnp.concatenate([cctx_ref[...], c_ref[...], pad], axis=0)
    acc[...] += _dot(_silu(cond).astype(BF16), w_ref[...].astype(BF16))

    @pl.when(k == pl.num_programs(0) - 1)
    def _():
        o_ref[:, 0, :] = acc[...]


def _adaln(c_ctx, c, ada_w, ada_b, w_in_t):
    d, n = ada_w.shape
    tk = ADALN_K_BLOCK
    assert 1 + c.shape[0] <= COND_ROWS and w_in_t.shape == (D_PROJ, d)
    return pl.pallas_call(
        _adaln_kernel,
        grid=(d // tk,),
        in_specs=[
            pl.BlockSpec((1, tk), lambda k: (0, k)),
            pl.BlockSpec((c.shape[0], tk), lambda k: (0, k)),
            pl.BlockSpec((tk, n), lambda k: (k, 0)),
            pl.BlockSpec((1, n), lambda k: (0, 0)),
            pl.BlockSpec((D_PROJ, tk), lambda k: (0, k)),
        ],
        out_specs=[pl.BlockSpec((COND_ROWS, 1, n), lambda k: (0, 0, 0)),
                   pl.BlockSpec((tk, D_PROJ), lambda k: (k, 0))],
        out_shape=[jax.ShapeDtypeStruct((COND_ROWS, 1, n), F32),
                   jax.ShapeDtypeStruct((d, D_PROJ), BF16)],
        scratch_shapes=[pltpu.VMEM((COND_ROWS, n), F32)],
        compiler_params=pltpu.CompilerParams(
            dimension_semantics=("arbitrary",), vmem_limit_bytes=VMEM_LIMIT),
        name="adaln",
    )(c_ctx.reshape(1, d), c, ada_w, ada_b.reshape(1, n), w_in_t)


def _swap_halves(x, lane_lo):
    n = x.shape[-1]
    up = pltpu.roll(x, n - ROPE_AXIS_DIM // 2, axis=1)
    dn = pltpu.roll(x, ROPE_AXIS_DIM // 2, axis=1)
    return jnp.where(lane_lo, up, dn)


PRE_SUB_ROWS = 256


def _pre_kernel(*refs, rope, emit_cache):
    it = iter(refs)
    x_ref, mod_ref, n1_ref, win_ref, wgk_ref, bgk_ref = (next(it) for _ in range(6))
    qg_ref, kg_ref, bdq_ref, bdk_ref = (next(it) for _ in range(4))
    if rope:
        cos_ref, sin_ref = next(it), next(it)
    q_o, k_o, v_o, gate_o, g_o, qa_o, ka_o, va_o = (next(it) for _ in range(8))
    if emit_cache:
        kc_o, vc_o = next(it), next(it)

    sub = PRE_SUB_ROWS
    n_sub = x_ref.shape[0] // sub
    mod = mod_ref[0]
    sh1 = mod[:, 0:D_MODEL]
    sc1 = mod[:, D_MODEL:2 * D_MODEL]

    def project(j):
        rows = slice(j * sub, (j + 1) * sub)
        h = _rms(x_ref[rows, :], n1_ref[...]) * (1.0 + sc1) + sh1
        return _dot(h.astype(BF16), win_ref[...])

    def finish(j, p_ref):
        rows = slice(j * sub, (j + 1) * sub)
        q_o[rows, :] = p_ref[:, O_Q:O_Q + D_QK].astype(BF16)
        k_o[rows, :] = p_ref[:, O_K:O_K + D_QK].astype(BF16)
        v_o[rows, :] = p_ref[:, O_V:O_V + D_V].astype(BF16)
        gate_o[rows, :] = _silu(p_ref[:, O_OG:O_OG + D_V]).astype(BF16)

        gk = _dot(p_ref[:, O_LR:O_LR + 2 * GLA_LOWRANK].astype(BF16), wgk_ref[...]) + bgk_ref[...]
        g_o[rows, :] = ((jnp.minimum(gk, 0.0) - jnp.log(1.0 + jnp.exp(-jnp.abs(gk))))
                        * (1.0 / GLA_GATE_NORM))

        qa = p_ref[:, O_QA:O_QA + D_QA]
        ka = p_ref[:, O_KA:O_KA + D_KA]
        va = p_ref[:, O_VA:O_VA + D_KA]
        qa = qa * lax.rsqrt(_dot((qa * qa).astype(BF16), bdq_ref[...]) + EPS) * qg_ref[...]
        ka = ka * lax.rsqrt(_dot((ka * ka).astype(BF16), bdk_ref[...]) + EPS) * kg_ref[...]
        if rope:
            lane_q = lax.broadcasted_iota(jnp.int32, qa.shape, 1)
            lane_k = lax.broadcasted_iota(jnp.int32, ka.shape, 1)
            lo_q = (lane_q % ROPE_AXIS_DIM) < ROPE_AXIS_DIM // 2
            lo_k = (lane_k % ROPE_AXIS_DIM) < ROPE_AXIS_DIM // 2
            cos_k = cos_ref[rows, :]
            sin_k = sin_ref[rows, :]
            cos_q = jnp.concatenate([cos_k] * GROUP, axis=1)
            sin_q = jnp.concatenate([sin_k] * GROUP, axis=1)
            qa = qa * cos_q + _swap_halves(qa, lo_q) * sin_q
            ka = ka * cos_k + _swap_halves(ka, lo_k) * sin_k
        vat = va.T
        if emit_cache:
            kat = ka.T
            t = kc_o.shape[2]
            for sq in range(sub // t):
                kc_o[j * (sub // t) + sq] = kat[:, sq * t:(sq + 1) * t]
                vc_o[j * (sub // t) + sq] = vat[:, sq * t:(sq + 1) * t]
        qa_o[:, rows] = (qa * (HEAD_DIM ** -0.5 * LOG2E)).T.astype(BF16)
        ka_o[rows, :] = ka.astype(BF16)
        va_o[:, rows] = vat.astype(BF16)

    p_prev = None
    for n in range(n_sub + 1):
        p_new = project(n) if n < n_sub else None
        if n > 0:
            finish(n - 1, p_prev)
        p_prev = p_new


def _pre_mixer(x, mod3, mod_row0, tokens_per_mod, seq_len, weights, rope_tabs, emit_cache, tm):
    ntok = x.shape[0]
    n1, w_in_r, w_gk, b_gk, qg, kg, bdq, bdk = weights
    rope = rope_tabs is not None
    tiles_per_mod = tokens_per_mod // tm
    tiles_per_seq = seq_len // tm
    whole = pl.BlockSpec(memory_space=pltpu.VMEM)
    row = lambda w: pl.BlockSpec((tm, w), lambda i: (i, 0))
    in_specs = [
        row(D_MODEL),
        pl.BlockSpec((1, 1, N_MOD * D_MODEL), lambda i: (mod_row0 + i // tiles_per_mod, 0, 0)),
        whole, whole, whole, whole, whole, whole, whole, whole,
    ]
    args = [x, mod3, n1, w_in_r, w_gk, b_gk, qg, kg, bdq, bdk]
    if rope:
        tab = pl.BlockSpec((tm, D_KA), lambda i: (i % tiles_per_seq, 0))
        in_specs += [tab, tab]
        args += list(rope_tabs)
    col = lambda w: pl.BlockSpec((w, tm), lambda i: (0, i))
    out_specs = [row(D_QK), row(D_QK), row(D_V), row(D_V), row(2 * D_QK),
                 col(D_QA), row(D_KA), col(D_KA)]
    out_shape = [
        jax.ShapeDtypeStruct((ntok, D_QK), BF16),
        jax.ShapeDtypeStruct((ntok, D_QK), BF16),
        jax.ShapeDtypeStruct((ntok, D_V), BF16),
        jax.ShapeDtypeStruct((ntok, D_V), BF16),
        jax.ShapeDtypeStruct((ntok, 2 * D_QK), F32),
        jax.ShapeDtypeStruct((D_QA, ntok), BF16),
        jax.ShapeDtypeStruct((ntok, D_KA), BF16),
        jax.ShapeDtypeStruct((D_KA, ntok), BF16),
    ]
    if emit_cache:
        seqs = tm // seq_len
        out_specs += [pl.BlockSpec((seqs, D_KA, seq_len), lambda i: (i, 0, 0))] * 2
        out_shape += [jax.ShapeDtypeStruct((ntok // seq_len, D_KA, seq_len), F32)] * 2
    return pl.pallas_call(
        functools.partial(_pre_kernel, rope=rope, emit_cache=emit_cache),
        grid=(ntok // tm,),
        in_specs=in_specs,
        out_specs=out_specs,
        out_shape=out_shape,
        compiler_params=pltpu.CompilerParams(
            dimension_semantics=("arbitrary",), vmem_limit_bytes=VMEM_LIMIT),
        name="pre_mixer_rope" if rope else "pre_mixer",
    )(*args)


def _gla_chunks(units, q_ref, k_ref, v_ref, g_ref, half_lo):
    C = GLA_CHUNK
    n_pair = GLA_HEADS // 2
    rows = [pl.ds(pl.multiple_of(c * C, C), C) for (_, c, _, _, _, _) in units]

    bs = []
    for (bi, _, fwd, tri, _, _), r in zip(units, rows):
        gc = g_ref[bi, r, 0:D_QK] if fwd else g_ref[bi, r, D_QK:2 * D_QK]
        ltri = jnp.where(tri, 1.0, 0.0).astype(BF16)
        g_hi = gc.astype(BF16)
        g_lo = (gc - g_hi.astype(F32)).astype(BF16)
        bs.append(_dot(ltri, g_hi) + _dot(ltri, g_lo))

    qts, kts, kes, decays = [], [], [], []
    for (bi, _, fwd, _, _, _), r, b in zip(units, rows, bs):
        qc = q_ref[bi, r, :].astype(F32) * GLA_DK ** -0.5
        kc = k_ref[bi, r, :].astype(F32)
        btot = b[C - 1:C, :] if fwd else b[0:1, :]
        qts.append((qc * jnp.exp(b)).astype(BF16))
        kts.append((kc * jnp.exp(-b)).astype(BF16))
        kes.append((kc * jnp.exp(btot - b)).astype(BF16))
        decays.append(jnp.transpose(jnp.broadcast_to(jnp.exp(btot), (8, D_QK)))[:, 0:1])

    scores, incs, qms = [], [], []
    for (bi, _, _, tri, _, _), r, qt, kt, ke in zip(units, rows, qts, kts, kes):
        for pair in range(n_pair):
            lanes = slice(pair * 128, (pair + 1) * 128)
            incs.append(_dot_tn(ke[:, lanes], v_ref[bi, r, pair * 2 * GLA_DV:(pair + 1) * 2 * GLA_DV]))
            for hh in range(2):
                qm = jnp.where(half_lo if hh == 0 else ~half_lo, qt[:, lanes], jnp.zeros((C, 128), BF16))
                qms.append(qm)
                scores.append(jnp.where(tri, _dot_nt(qm, kt[:, lanes]), 0.0).astype(BF16))

    for ui, ((bi, _, _, _, s_ref, o_ref), r) in enumerate(zip(units, rows)):
        state = s_ref[bi]
        state_b = state.astype(BF16)
        for hd in range(GLA_HEADS):
            pair = hd // 2
            lhs = jnp.concatenate([qms[ui * GLA_HEADS + hd], scores[ui * GLA_HEADS + hd]], axis=1)
            rhs = jnp.concatenate([state_b[pair * 128:(pair + 1) * 128, :],
                                   v_ref[bi, r, hd * GLA_DV:(hd + 1) * GLA_DV]], axis=0)
            o_ref[bi, r, hd * GLA_DV:(hd + 1) * GLA_DV] = _dot(lhs, rhs)
        inc = jnp.concatenate(
            [incs[ui * n_pair + hd // 2][(hd % 2) * GLA_DK:(hd % 2 + 1) * GLA_DK,
                                         (hd % 2) * GLA_DV:(hd % 2 + 1) * GLA_DV]
             for hd in range(GLA_HEADS)], axis=0)
        s_ref[bi] = decays[ui] * state + inc


def _gla_kernel(*args, zero_init, n_cast):
    q_ref, k_ref, v_ref, g_ref, gate_ref, gn_ref = args[:6]
    n_in = 6 + (0 if zero_init else 2)
    cast_src = args[n_in:n_in + n_cast]
    o_ref, sf_ref, sb_ref = args[n_in + n_cast:n_in + n_cast + 3]
    cast_dst = args[n_in + n_cast + 3:n_in + 2 * n_cast + 3]
    of_scr, ob_scr = args[-2:]
    for src, dst in zip(cast_src, cast_dst):
        dst[...] = src[...].astype(BF16)
    C = GLA_CHUNK
    nb, t, _ = q_ref.shape
    n = t // C
    row = lax.broadcasted_iota(jnp.int32, (C, C), 0)
    col = lax.broadcasted_iota(jnp.int32, (C, C), 1)
    half_lo = lax.broadcasted_iota(jnp.int32, (C, 128), 1) < GLA_DK
    refs = (q_ref, k_ref, v_ref, g_ref)
    if zero_init:
        sf_ref[...] = jnp.zeros(sf_ref.shape, F32)
        sb_ref[...] = jnp.zeros(sb_ref.shape, F32)
    else:
        sf_ref[...] = args[6][...]
        sb_ref[...] = args[7][...]

    steps = min(GLA_STEPS_PER_ITER, n)
    assert n % steps == 0

    def body(i, carry):
        units = []
        for u in range(steps):
            c = i * steps + u
            for bi in range(nb):
                units.append((bi, c, True, row >= col, sf_ref, of_scr))
                units.append((bi, n - 1 - c, False, row <= col, sb_ref, ob_scr))
        _gla_chunks(units, *refs, half_lo)
        return carry

    lax.fori_loop(0, n // steps, body, 0)

    rt = GLA_OUT_ROWS
    gn = gn_ref[...]

    def finish(i, carry):
        rows = pl.ds(pl.multiple_of(i * rt, rt), rt)
        for bi in range(nb):
            o = of_scr[bi, rows, :] + ob_scr[bi, rows, :]
            gate = gate_ref[bi, rows, :].astype(F32)
            for hd in range(GLA_HEADS):
                sl = slice(hd * GLA_DV, (hd + 1) * GLA_DV)
                o_ref[bi, rows, sl] = (_rms(o[:, sl], gn) * gate[:, sl]).astype(BF16)
        return carry

    lax.fori_loop(0, t // rt, finish, 0)


GLA_STEPS_PER_ITER = 4
GLA_OUT_ROWS = 256


def _gla(q, k, v, g, gate, gla_g, init_states, nb, cast=()):
    b, t, _ = q.shape
    seq = lambda w: pl.BlockSpec((nb, t, w), lambda i: (i, 0, 0))
    st = pl.BlockSpec((nb, D_QK, GLA_DV), lambda i: (i, 0, 0))
    zero_init = init_states is None
    steps = b // nb
    slabs = [pl.BlockSpec((a.shape[0] // steps, a.shape[1]), lambda i: (i, 0)) for a in cast]
    return pl.pallas_call(
        functools.partial(_gla_kernel, zero_init=zero_init, n_cast=len(cast)),
        grid=(b // nb,),
        in_specs=[seq(D_QK), seq(D_QK), seq(D_V), seq(2 * D_QK), seq(D_V),
                  pl.BlockSpec((1, GLA_DV), lambda i: (0, 0))] + ([] if zero_init else [st, st]) + slabs,
        out_specs=[seq(D_V), st, st] + slabs,
        out_shape=[jax.ShapeDtypeStruct((b, t, D_V), BF16),
                   jax.ShapeDtypeStruct((b, D_QK, GLA_DV), F32),
                   jax.ShapeDtypeStruct((b, D_QK, GLA_DV), F32)]
        + [jax.ShapeDtypeStruct(a.shape, BF16) for a in cast],
        scratch_shapes=[pltpu.VMEM((nb, t, D_V), F32), pltpu.VMEM((nb, t, D_V), F32)],
        compiler_params=pltpu.CompilerParams(
            dimension_semantics=("arbitrary",), vmem_limit_bytes=VMEM_LIMIT),
        name="gla",
    )(q, k, v, g, gate, gla_g, *(() if zero_init else init_states), *cast)


ATT_KEY_BLOCK = 256
ATT_Q_TILE = 256
LOG2E = 1.4426950408889634


ATT_TILES_LONG = 2
ATT_TILES_SHORT = 4


def _attn_kernel(*args, keys_per_tile, cached):
    qt_ref, k_ref, vt_ref = args[:3]
    kc_ref, vtc_ref = args[3:5] if cached else (None, None)
    o_ref, s0, s1, m0, m1, a0, a1 = args[-7:]
    tq = ATT_Q_TILE
    tiles = qt_ref.shape[1] // tq
    t_new = k_ref.shape[0] // tiles if keys_per_tile else k_ref.shape[0]
    tk = t_new + (kc_ref.shape[0] if cached else 0)
    kb = min(ATT_KEY_BLOCK, tk)
    nkb = tk // kb
    zero = jnp.zeros((HEAD_DIM, tq), BF16)
    ones = jnp.ones((16, kb), BF16)
    items = [(qi, kv) for qi in range(tiles) for kv in range(ATT_KV_HEADS)]
    bufs = (s0, s1)
    mbufs = (m0, m1)
    abufs = (a0, a1)
    separate = nkb > 1

    def q_ext(qi, kv):
        cols = []
        for g in range(GROUP):
            hd = kv * GROUP + g
            qt = qt_ref[hd * HEAD_DIM:(hd + 1) * HEAD_DIM, qi * tq:(qi + 1) * tq]
            cols.append(jnp.concatenate([qt, zero] if kv == 0 else [zero, qt], axis=0))
        return jnp.concatenate(cols, axis=1)

    def key0(qi):
        return qi * tk if keys_per_tile else 0

    def score_piece(n, q_all, i, m):
        r0 = key0(items[n][0]) + i * kb
        keys = k_ref[r0:r0 + kb, :] if i * kb < t_new else kc_ref[r0 - t_new:r0 - t_new + kb, :]
        s = _dot(keys, q_all)
        bufs[n % 2][i * kb:(i + 1) * kb, :] = s
        for r in range(kb // 8):
            tile = s[r * 8:(r + 1) * 8, :]
            m = tile if m is None else jnp.maximum(m, tile)
        return m

    def value_piece(n, m, i, acc):
        qi, kv = items[n]
        c0 = key0(qi) + i * kb
        p = jnp.exp2(bufs[n % 2][i * kb:(i + 1) * kb, :] - m).astype(BF16)
        hd_rows = slice(kv * HEAD_DIM, (kv + 1) * HEAD_DIM)
        vt = vt_ref[hd_rows, c0:c0 + kb] if i * kb < t_new else vtc_ref[hd_rows, c0 - t_new:c0 - t_new + kb]
        vt1 = jnp.concatenate([vt, ones], axis=0)
        d = _dot(vt1, p)
        return d if acc is None else acc + d

    def emit(n, acc):
        qi, kv = items[n]
        o = acc[:HEAD_DIM] / acc[HEAD_DIM:HEAD_DIM + 1]
        for g in range(0, GROUP, 2):
            two = jnp.concatenate([o[:, g * tq:(g + 1) * tq], o[:, (g + 1) * tq:(g + 2) * tq]], axis=0)
            hd = kv * GROUP + g
            o_ref[qi * tq:(qi + 1) * tq, hd * HEAD_DIM:(hd + 2) * HEAD_DIM] = two.T.astype(BF16)

    def pair(n):
        if separate and n >= 2:
            emit(n - 2, abufs[n % 2][...])
        if n > len(items):
            return
        q_all = q_ext(*items[n]) if n < len(items) else None
        m_prev = mbufs[(n - 1) % 2][...] if n > 0 else None
        m_new = acc = None
        for i in range(nkb):
            if n < len(items):
                m_new = score_piece(n, q_all, i, m_new)
            if n > 0:
                acc = value_piece(n - 1, m_prev, i, acc)
        if n > 0 and separate:
            abufs[(n - 1) % 2][...] = acc
        elif n > 0:
            emit(n - 1, acc)
        if n < len(items):
            mbufs[n % 2][...] = jnp.max(m_new, axis=0, keepdims=True)

    for n in range(len(items) + (2 if separate else 1)):
        if separate:
            pl.when(pl.program_id(0) > -(n + 1))(functools.partial(pair, n))
        else:
            pair(n)


def _attention(qt, k, vt, batch, cache=None):
    ntok = qt.shape[1]
    t_new = k.shape[0] // batch
    tq = ATT_Q_TILE
    nq = ntok // batch // tq
    tiles = ATT_TILES_LONG if t_new > ATT_KEY_BLOCK else ATT_TILES_SHORT
    keys_per_tile = nq < tiles
    extra_specs, extra, tk = [], (), t_new
    if keys_per_tile:
        assert cache is None
        seqs = tiles // nq
        k_spec = pl.BlockSpec((seqs * t_new, D_KA), lambda t: (t, 0))
        vt_spec = pl.BlockSpec((D_KA, seqs * t_new), lambda t: (0, t))
    else:
        spq = nq // tiles
        k_spec = pl.BlockSpec((t_new, D_KA), lambda t: (t // spq, 0))
        vt_spec = pl.BlockSpec((D_KA, t_new), lambda t: (0, t // spq))
        if cache is not None:
            t_old = cache[0].shape[0] // batch
            assert t_new % ATT_KEY_BLOCK == 0 and t_old % ATT_KEY_BLOCK == 0
            extra_specs = [pl.BlockSpec((t_old, D_KA), lambda t: (t // spq, 0)),
                           pl.BlockSpec((D_KA, t_old), lambda t: (0, t // spq))]
            extra, tk = cache, t_new + t_old
    return pl.pallas_call(
        functools.partial(_attn_kernel, keys_per_tile=keys_per_tile, cached=cache is not None),
        grid=(ntok // (tiles * tq),),
        in_specs=[pl.BlockSpec((D_QA, tiles * tq), lambda t: (0, t)), k_spec, vt_spec] + extra_specs,
        out_specs=pl.BlockSpec((tiles * tq, D_QA), lambda t: (t, 0)),
        out_shape=jax.ShapeDtypeStruct((ntok, D_QA), BF16),
        scratch_shapes=[pltpu.VMEM((tk, GROUP * tq), F32), pltpu.VMEM((tk, GROUP * tq), F32),
                        pltpu.VMEM((1, GROUP * tq), F32), pltpu.VMEM((1, GROUP * tq), F32),
                        pltpu.VMEM((HEAD_DIM + 16, GROUP * tq), F32),
                        pltpu.VMEM((HEAD_DIM + 16, GROUP * tq), F32)],
        compiler_params=pltpu.CompilerParams(
            dimension_semantics=("arbitrary",), vmem_limit_bytes=VMEM_LIMIT),
        name="attention",
    )(qt, k, vt, *extra)


def _post_kernel(og_ref, oa_ref, x_ref, mod_ref, n2_ref, fg_ref, wo_ref, w1_ref, w3_ref, w2_ref,
                 y_ref):
    D = D_MODEL
    mod = mod_ref[0]
    g1, sh2, sc2, g2 = (mod[:, i * D:(i + 1) * D] for i in range(2, 6))
    out = _dot(og_ref[...], wo_ref[0:D_V, :]) + _dot(oa_ref[...], wo_ref[D_V:D_V + D_QA, :])
    x1 = x_ref[...] + g1 * out
    hb = (_rms(x1, n2_ref[...]) * (1.0 + sc2) + sh2).astype(BF16)
    act = (_silu(_dot(hb, w1_ref[...])) * _dot(hb, w3_ref[...])).astype(BF16)
    x2 = x1 + g2 * _dot(act, w2_ref[...])
    y_ref[...] = _rms(x2, fg_ref[...])


def _post_mixer(o_gla, o_att, x, mod3, mod_row0, tokens_per_mod, weights, tm):
    ntok = x.shape[0]
    tiles_per_mod = tokens_per_mod // tm
    whole = pl.BlockSpec(memory_space=pltpu.VMEM)
    row = lambda w: pl.BlockSpec((tm, w), lambda i: (i, 0))
    return pl.pallas_call(
        _post_kernel,
        grid=(ntok // tm,),
        in_specs=[row(D_V), row(D_QA), row(D_MODEL),
                  pl.BlockSpec((1, 1, N_MOD * D_MODEL), lambda i: (mod_row0 + i // tiles_per_mod, 0, 0)),
                  whole, whole, whole, whole, whole, whole],
        out_specs=row(D_MODEL),
        out_shape=jax.ShapeDtypeStruct((ntok, D_MODEL), F32),
        compiler_params=pltpu.CompilerParams(
            dimension_semantics=("arbitrary",), vmem_limit_bytes=VMEM_LIMIT),
        name="post_mixer",
    )(o_gla, o_att, x, mod3, *weights)


def _rope_tables(t):
    rows = t // GRID_W
    row = np.repeat(np.arange(rows, dtype=np.float64), GRID_W)
    col = np.tile(np.arange(GRID_W, dtype=np.float64), rows)
    inv = ROPE_THETA ** (-np.arange(0, ROPE_AXIS_DIM, 2, dtype=np.float64) / ROPE_AXIS_DIM)
    ang_r = row[:, None] * inv
    ang_c = col[:, None] * inv
    cos = np.concatenate([np.cos(ang_r)] * 2 + [np.cos(ang_c)] * 2, axis=-1)
    sin = np.concatenate([-np.sin(ang_r), np.sin(ang_r), -np.sin(ang_c), np.sin(ang_c)], axis=-1)
    tile = lambda a: jnp.asarray(np.tile(a, (1, ATT_KV_HEADS)), F32)
    return tile(cos), tile(sin)


def _block_mean(width, group):
    return jnp.asarray(np.kron(np.eye(width // group), np.ones((group, group))) / group, BF16)


def kernel(x_prompt, x_sample, cache_k, cache_v, state_gla_fwd, state_gla_bwd, c, c_ctx,
           ada_w, ada_b, norm1_g, norm2_g, w_in, w_gk2, b_gk2, gla_norm_g, q_norm_g, k_norm_g,
           w_out, w_ffn1, w_ffn3, w_ffn2, final_g):
    B, T, D = x_prompt.shape
    BL, TL, _ = x_sample.shape
    TP = cache_k.shape[2]
    l = 0

    mod3, w_in_r = _adaln(c_ctx, c, ada_w[l], ada_b[l], jnp.transpose(w_in[l]))
    zero = jnp.zeros((GLA_LOWRANK, D_QK), F32)
    w_gk = jnp.concatenate([jnp.concatenate([w_gk2[l, 0], zero], axis=1),
                            jnp.concatenate([zero, w_gk2[l, 1]], axis=1)], axis=0).astype(BF16)
    pre_w = (norm1_g[l].reshape(1, D), w_in_r, w_gk, b_gk2[l].reshape(1, 2 * D_QK),
             jnp.tile(q_norm_g[l], ATT_HEADS).reshape(1, D_QA),
             jnp.tile(k_norm_g[l], ATT_KV_HEADS).reshape(1, D_KA),
             _block_mean(D_QA, HEAD_DIM), _block_mean(D_KA, HEAD_DIM))
    post_norms = (norm2_g[l].reshape(1, D), final_g.reshape(1, D))
    post_w_f32 = (w_out[l], w_ffn1[l], w_ffn3[l], w_ffn2[l])
    gla_g = gla_norm_g[l].reshape(1, GLA_DV)

    xc = x_prompt.reshape(B * T, D)
    q, k, v, gate, g, qa, ka, va, kc, vc = _pre_mixer(
        xc, mod3, 0, B * T, T, pre_w, None, True, PRE_TILE_ROWS)
    r3 = lambda a, b, t: a.reshape(b, t, a.shape[-1])
    o_gla, s_f, s_b, *post_w_bf16 = _gla(
        r3(q, B, T), r3(k, B, T), r3(v, B, T), r3(g, B, T), r3(gate, B, T), gla_g, None,
        GLA_CONTEXT_SEQS, cast=post_w_f32)
    post_w = post_norms + tuple(post_w_bf16)
    o_att = _attention(qa, ka, va, B)
    y_prompt = _post_mixer(o_gla.reshape(B * T, D_V), o_att, xc, mod3, 0, B * T, post_w, POST_TILE_ROWS)

    xl = x_sample.reshape(BL * TL, D)
    q, k, v, gate, g, qa, ka, va = _pre_mixer(
        xl, mod3, 1, TL, TL, pre_w, _rope_tables(TL), False, PRE_TILE_ROWS)
    o_gla, _, _ = _gla(r3(q, BL, TL), r3(k, BL, TL), r3(v, BL, TL), r3(g, BL, TL), r3(gate, BL, TL),
                       gla_g, (state_gla_fwd[:, l].reshape(BL, D_QK, GLA_DV),
                               state_gla_bwd[:, l].reshape(BL, D_QK, GLA_DV)), BL)

    k_old = cache_k[:, l].reshape(BL * TP, D_KA).astype(BF16)
    vt_old = jnp.transpose(cache_v[:, l], (2, 3, 0, 1)).reshape(D_KA, BL * TP).astype(BF16)
    o_att = _attention(qa, ka, va, BL, (k_old, vt_old))
    y_sample = _post_mixer(o_gla.reshape(BL * TL, D_V), o_att, xl, mod3, 1, TL, post_w, POST_TILE_ROWS)

    def cache_out(ct):
        return jnp.transpose(ct.reshape(B, 1, ATT_KV_HEADS, HEAD_DIM, T), (0, 1, 4, 2, 3))

    return (y_prompt.reshape(B, T, D), y_sample.reshape(BL, TL, D), cache_out(kc), cache_out(vc),
            s_f.reshape(B, 1, GLA_HEADS, GLA_DK, GLA_DV), s_b.reshape(B, 1, GLA_HEADS, GLA_DK, GLA_DV))
```

```python
import functools

import numpy as np
import jax
import jax.numpy as jnp
from jax import lax
from jax.experimental import pallas as pl
from jax.experimental.pallas import tpu as pltpu

F32 = jnp.float32
BF16 = jnp.bfloat16

D_MODEL = 1024
GRID_W = 64
GLA_HEADS = 4
GLA_DK = 64
GLA_DV = 128
GLA_LOWRANK = 16
GLA_GATE_NORM = 16.0
GLA_CHUNK = 64
ATT_HEADS = 8
ATT_KV_HEADS = 2
HEAD_DIM = 64
ROPE_AXIS_DIM = HEAD_DIM // 2
ROPE_THETA = 10000.0
N_MOD = 6
EPS = 1e-6

D_QK = GLA_HEADS * GLA_DK
D_V = GLA_HEADS * GLA_DV
D_QA = ATT_HEADS * HEAD_DIM
D_KA = ATT_KV_HEADS * HEAD_DIM
GROUP = ATT_HEADS // ATT_KV_HEADS
O_Q, O_K, O_V, O_OG = 0, D_QK, 2 * D_QK, 2 * D_QK + D_V
O_QA = O_OG + D_V
O_KA = O_QA + D_QA
O_VA = O_KA + D_KA
O_LR = O_VA + D_KA
D_PROJ = O_LR + 2 * GLA_LOWRANK

COND_ROWS = 8
VMEM_LIMIT = 56 * 1024 * 1024
PRE_TILE_ROWS = 1024
POST_TILE_ROWS = 512
GLA_CONTEXT_SEQS = 4


def _dot(a, b):
    return jnp.dot(a, b, preferred_element_type=F32)


def _dot_nt(a, b):
    return lax.dot_general(a, b, (((1,), (1,)), ((), ())), preferred_element_type=F32)


def _dot_tn(a, b):
    return lax.dot_general(a, b, (((0,), (0,)), ((), ())), preferred_element_type=F32)


def _rms(x, g):
    ms = jnp.mean(x * x, axis=-1, keepdims=True)
    return x * lax.rsqrt(ms + EPS) * g


def _silu(x):
    return x * jax.nn.sigmoid(x)


ADALN_K_BLOCK = 256


def _adaln_kernel(cctx_ref, c_ref, w_ref, b_ref, wint_ref, o_ref, win_o, acc):
    k = pl.program_id(0)

    lr0, lr1 = O_OG + D_V, O_OG + D_V + 2 * GLA_LOWRANK
    feats = jnp.concatenate([wint_ref[0:lr0, :], wint_ref[lr1:D_PROJ, :], wint_ref[lr0:lr1, :]],
                            axis=0)
    for j in range(0, D_PROJ, 128):
        rows = min(128, D_PROJ - j)
        tile = feats[j:j + rows, :]
        if rows < 128:
            tile = jnp.concatenate([tile, jnp.zeros((128 - rows, tile.shape[1]), F32)], axis=0)
        win_o[:, j:j + rows] = tile.T[:, 0:rows].astype(BF16)

    @pl.when(k == 0)
    def _():
        acc[...] = jnp.broadcast_to(b_ref[...], acc.shape)

    pad = jnp.zeros((COND_ROWS - 1 - c_ref.shape[0], c_ref.shape[1]), F32)
    cond = jnp.concatenate([cctx_ref[...], c_ref[...], pad], axis=0)
    acc[...] += _dot(_silu(cond).astype(BF16), w_ref[...].astype(BF16))

    @pl.when(k == pl.num_programs(0) - 1)
    def _():
        o_ref[:, 0, :] = acc[...]


def _adaln(c_ctx, c, ada_w, ada_b, w_in_t):
    d, n = ada_w.shape
    tk = ADALN_K_BLOCK
    assert 1 + c.shape[0] <= COND_ROWS and w_in_t.shape == (D_PROJ, d)
    return pl.pallas_call(
        _adaln_kernel,
        grid=(d // tk,),
        in_specs=[
            pl.BlockSpec((1, tk), lambda k: (0, k)),
            pl.BlockSpec((c.shape[0], tk), lambda k: (0, k)),
            pl.BlockSpec((tk, n), lambda k: (k, 0)),
            pl.BlockSpec((1, n), lambda k: (0, 0)),
            pl.BlockSpec((D_PROJ, tk), lambda k: (0, k)),
        ],
        out_specs=[pl.BlockSpec((COND_ROWS, 1, n), lambda k: (0, 0, 0)),
                   pl.BlockSpec((tk, D_PROJ), lambda k: (k, 0))],
        out_shape=[jax.ShapeDtypeStruct((COND_ROWS, 1, n), F32),
                   jax.ShapeDtypeStruct((d, D_PROJ), BF16)],
        scratch_shapes=[pltpu.VMEM((COND_ROWS, n), F32)],
        compiler_params=pltpu.CompilerParams(
            dimension_semantics=("arbitrary",), vmem_limit_bytes=VMEM_LIMIT),
        name="adaln",
    )(c_ctx.reshape(1, d), c, ada_w, ada_b.reshape(1, n), w_in_t)


def _swap_halves(x, lane_lo):
    n = x.shape[-1]
    up = pltpu.roll(x, n - ROPE_AXIS_DIM // 2, axis=1)
    dn = pltpu.roll(x, ROPE_AXIS_DIM // 2, axis=1)
    return jnp.where(lane_lo, up, dn)


PRE_SUB_ROWS = 512
PRE_SUB_ROWS_ROPE = 256


def _pre_kernel(*refs, rope, emit_cache):
    it = iter(refs)
    x_ref, mod_ref, n1_ref, win_ref, wgk_ref, bgk_ref = (next(it) for _ in range(6))
    qg_ref, kg_ref, bdq_ref, bdk_ref = (next(it) for _ in range(4))
    if rope:
        cos_ref, sin_ref = next(it), next(it)
    q_o, k_o, v_o, gate_o, g_o, qa_o, ka_o, va_o = (next(it) for _ in range(8))
    if emit_cache:
        kc_o, vc_o = next(it), next(it)

    sub = PRE_SUB_ROWS_ROPE if rope else PRE_SUB_ROWS
    n_sub = x_ref.shape[0] // sub
    mod = mod_ref[0]
    sh1 = mod[:, 0:D_MODEL]
    sc1 = mod[:, D_MODEL:2 * D_MODEL]

    def project(j):
        rows = slice(j * sub, (j + 1) * sub)
        h = _rms(x_ref[rows, :], n1_ref[...]) * (1.0 + sc1) + sh1
        return _dot(h.astype(BF16), win_ref[...])

    def finish(j, p_ref):
        rows = slice(j * sub, (j + 1) * sub)
        q_o[rows, :] = p_ref[:, O_Q:O_Q + D_QK].astype(BF16)
        k_o[rows, :] = p_ref[:, O_K:O_K + D_QK].astype(BF16)
        v_o[rows, :] = p_ref[:, O_V:O_V + D_V].astype(BF16)
        gate_o[rows, :] = _silu(p_ref[:, O_OG:O_OG + D_V]).astype(BF16)

        gk = _dot(p_ref[:, O_LR:O_LR + 2 * GLA_LOWRANK].astype(BF16), wgk_ref[...]) + bgk_ref[...]
        g_o[rows, :] = ((jnp.minimum(gk, 0.0) - jnp.log(1.0 + jnp.exp(-jnp.abs(gk))))
                        * (1.0 / GLA_GATE_NORM))

        qa = p_ref[:, O_QA:O_QA + D_QA]
        ka = p_ref[:, O_KA:O_KA + D_KA]
        va = p_ref[:, O_VA:O_VA + D_KA]
        qa = qa * lax.rsqrt(_dot((qa * qa).astype(BF16), bdq_ref[...]) + EPS) * qg_ref[...]
        ka = ka * lax.rsqrt(_dot((ka * ka).astype(BF16), bdk_ref[...]) + EPS) * kg_ref[...]
        if rope:
            lane_q = lax.broadcasted_iota(jnp.int32, qa.shape, 1)
            lane_k = lax.broadcasted_iota(jnp.int32, ka.shape, 1)
            lo_q = (lane_q % ROPE_AXIS_DIM) < ROPE_AXIS_DIM // 2
            lo_k = (lane_k % ROPE_AXIS_DIM) < ROPE_AXIS_DIM // 2
            cos_k = cos_ref[rows, :]
            sin_k = sin_ref[rows, :]
            cos_q = jnp.concatenate([cos_k] * GROUP, axis=1)
            sin_q = jnp.concatenate([sin_k] * GROUP, axis=1)
            qa = qa * cos_q + _swap_halves(qa, lo_q) * sin_q
            ka = ka * cos_k + _swap_halves(ka, lo_k) * sin_k
        vat = va.T
        if emit_cache:
            kat = ka.T
            t = kc_o.shape[2]
            for sq in range(sub // t):
                kc_o[j * (sub // t) + sq] = kat[:, sq * t:(sq + 1) * t]
                vc_o[j * (sub // t) + sq] = vat[:, sq * t:(sq + 1) * t]
        qa_o[:, rows] = (qa * (HEAD_DIM ** -0.5 * LOG2E)).T.astype(BF16)
        ka_o[rows, :] = ka.astype(BF16)
        va_o[:, rows] = vat.astype(BF16)

    p_prev = None
    for n in range(n_sub + 1):
        p_new = project(n) if n < n_sub else None
        if n > 0:
            finish(n - 1, p_prev)
        p_prev = p_new


def _pre_mixer(x, mod3, mod_row0, tokens_per_mod, seq_len, weights, rope_tabs, emit_cache, tm):
    ntok = x.shape[0]
    n1, w_in_r, w_gk, b_gk, qg, kg, bdq, bdk = weights
    rope = rope_tabs is not None
    tiles_per_mod = tokens_per_mod // tm
    tiles_per_seq = seq_len // tm
    whole = pl.BlockSpec(memory_space=pltpu.VMEM)
    row = lambda w: pl.BlockSpec((tm, w), lambda i: (i, 0))
    in_specs = [
        row(D_MODEL),
        pl.BlockSpec((1, 1, N_MOD * D_MODEL), lambda i: (mod_row0 + i // tiles_per_mod, 0, 0)),
        whole, whole, whole, whole, whole, whole, whole, whole,
    ]
    args = [x, mod3, n1, w_in_r, w_gk, b_gk, qg, kg, bdq, bdk]
    if rope:
        tab = pl.BlockSpec((tm, D_KA), lambda i: (i % tiles_per_seq, 0))
        in_specs += [tab, tab]
        args += list(rope_tabs)
    col = lambda w: pl.BlockSpec((w, tm), lambda i: (0, i))
    out_specs = [row(D_QK), row(D_QK), row(D_V), row(D_V), row(2 * D_QK),
                 col(D_QA), row(D_KA), col(D_KA)]
    out_shape = [
        jax.ShapeDtypeStruct((ntok, D_QK), BF16),
        jax.ShapeDtypeStruct((ntok, D_QK), BF16),
        jax.ShapeDtypeStruct((ntok, D_V), BF16),
        jax.ShapeDtypeStruct((ntok, D_V), BF16),
        jax.ShapeDtypeStruct((ntok, 2 * D_QK), F32),
        jax.ShapeDtypeStruct((D_QA, ntok), BF16),
        jax.ShapeDtypeStruct((ntok, D_KA), BF16),
        jax.ShapeDtypeStruct((D_KA, ntok), BF16),
    ]
    if emit_cache:
        seqs = tm // seq_len
        out_specs += [pl.BlockSpec((seqs, D_KA, seq_len), lambda i: (i, 0, 0))] * 2
        out_shape += [jax.ShapeDtypeStruct((ntok // seq_len, D_KA, seq_len), F32)] * 2
    return pl.pallas_call(
        functools.partial(_pre_kernel, rope=rope, emit_cache=emit_cache),
        grid=(ntok // tm,),
        in_specs=in_specs,
        out_specs=out_specs,
        out_shape=out_shape,
        compiler_params=pltpu.CompilerParams(
            dimension_semantics=("arbitrary",), vmem_limit_bytes=VMEM_LIMIT),
        name="pre_mixer_rope" if rope else "pre_mixer",
    )(*args)


def _gla_chunks(units, q_ref, k_ref, v_ref, g_ref, half_lo):
    C = GLA_CHUNK
    n_pair = GLA_HEADS // 2
    rows = [pl.ds(pl.multiple_of(c * C, C), C) for (_, c, _, _, _, _) in units]

    bs = []
    for (bi, _, fwd, tri, _, _), r in zip(units, rows):
        gc = g_ref[bi, r, 0:D_QK] if fwd else g_ref[bi, r, D_QK:2 * D_QK]
        ltri = jnp.where(tri, 1.0, 0.0).astype(BF16)
        g_hi = gc.astype(BF16)
        g_lo = (gc - g_hi.astype(F32)).astype(BF16)
        bs.append(_dot(ltri, g_hi) + _dot(ltri, g_lo))

    qts, kts, kes, decays = [], [], [], []
    for (bi, _, fwd, _, _, _), r, b in zip(units, rows, bs):
        qc = q_ref[bi, r, :].astype(F32) * GLA_DK ** -0.5
        kc = k_ref[bi, r, :].astype(F32)
        btot = b[C - 1:C, :] if fwd else b[0:1, :]
        qts.append((qc * jnp.exp(b)).astype(BF16))
        kts.append((kc * jnp.exp(-b)).astype(BF16))
        kes.append((kc * jnp.exp(btot - b)).astype(BF16))
        decays.append(jnp.transpose(jnp.broadcast_to(jnp.exp(btot), (8, D_QK)))[:, 0:1])

    scores, incs, qms = [], [], []
    for (bi, _, _, tri, _, _), r, qt, kt, ke in zip(units, rows, qts, kts, kes):
        for pair in range(n_pair):
            lanes = slice(pair * 128, (pair + 1) * 128)
            incs.append(_dot_tn(ke[:, lanes], v_ref[bi, r, pair * 2 * GLA_DV:(pair + 1) * 2 * GLA_DV]))
            for hh in range(2):
                qm = jnp.where(half_lo if hh == 0 else ~half_lo, qt[:, lanes], jnp.zeros((C, 128), BF16))
                qms.append(qm)
                scores.append(jnp.where(tri, _dot_nt(qm, kt[:, lanes]), 0.0).astype(BF16))

    for ui, ((bi, _, _, _, s_ref, o_ref), r) in enumerate(zip(units, rows)):
        state = s_ref[bi]
        state_b = state.astype(BF16)
        for hd in range(GLA_HEADS):
            pair = hd // 2
            lhs = jnp.concatenate([qms[ui * GLA_HEADS + hd], scores[ui * GLA_HEADS + hd]], axis=1)
            rhs = jnp.concatenate([state_b[pair * 128:(pair + 1) * 128, :],
                                   v_ref[bi, r, hd * GLA_DV:(hd + 1) * GLA_DV]], axis=0)
            o_ref[bi, r, hd * GLA_DV:(hd + 1) * GLA_DV] = _dot(lhs, rhs)
        inc = jnp.concatenate(
            [incs[ui * n_pair + hd // 2][(hd % 2) * GLA_DK:(hd % 2 + 1) * GLA_DK,
                                         (hd % 2) * GLA_DV:(hd % 2 + 1) * GLA_DV]
             for hd in range(GLA_HEADS)], axis=0)
        s_ref[bi] = decays[ui] * state + inc


def _gla_kernel(*args, zero_init, n_cast):
    q_ref, k_ref, v_ref, g_ref, gate_ref, gn_ref = args[:6]
    n_in = 6 + (0 if zero_init else 2)
    cast_src = args[n_in:n_in + n_cast]
    o_ref, sf_ref, sb_ref = args[n_in + n_cast:n_in + n_cast + 3]
    cast_dst = args[n_in + n_cast + 3:n_in + 2 * n_cast + 3]
    of_scr, ob_scr = args[-2:]
    for src, dst in zip(cast_src, cast_dst):
        dst[...] = src[...].astype(BF16)
    C = GLA_CHUNK
    nb, t, _ = q_ref.shape
    n = t // C
    row = lax.broadcasted_iota(jnp.int32, (C, C), 0)
    col = lax.broadcasted_iota(jnp.int32, (C, C), 1)
    half_lo = lax.broadcasted_iota(jnp.int32, (C, 128), 1) < GLA_DK
    refs = (q_ref, k_ref, v_ref, g_ref)
    if zero_init:
        sf_ref[...] = jnp.zeros(sf_ref.shape, F32)
        sb_ref[...] = jnp.zeros(sb_ref.shape, F32)
    else:
        sf_ref[...] = args[6][...]
        sb_ref[...] = args[7][...]

    steps = min(GLA_STEPS_PER_ITER, n)
    assert n % steps == 0

    def body(i, carry):
        units = []
        for u in range(steps):
            c = i * steps + u
            for bi in range(nb):
                units.append((bi, c, True, row >= col, sf_ref, of_scr))
                units.append((bi, n - 1 - c, False, row <= col, sb_ref, ob_scr))
        _gla_chunks(units, *refs, half_lo)
        return carry

    lax.fori_loop(0, n // steps, body, 0)

    rt = GLA_OUT_ROWS
    gn = gn_ref[...]

    def finish(i, carry):
        rows = pl.ds(pl.multiple_of(i * rt, rt), rt)
        for bi in range(nb):
            o = of_scr[bi, rows, :] + ob_scr[bi, rows, :]
            gate = gate_ref[bi, rows, :].astype(F32)
            for hd in range(GLA_HEADS):
                sl = slice(hd * GLA_DV, (hd + 1) * GLA_DV)
                o_ref[bi, rows, sl] = (_rms(o[:, sl], gn) * gate[:, sl]).astype(BF16)
        return carry

    lax.fori_loop(0, t // rt, finish, 0)


GLA_STEPS_PER_ITER = 4
GLA_OUT_ROWS = 256


def _gla(q, k, v, g, gate, gla_g, init_states, nb, cast=()):
    b, t, _ = q.shape
    seq = lambda w: pl.BlockSpec((nb, t, w), lambda i: (i, 0, 0))
    st = pl.BlockSpec((nb, D_QK, GLA_DV), lambda i: (i, 0, 0))
    zero_init = init_states is None
    steps = b // nb
    slabs = [pl.BlockSpec((a.shape[0] // steps, a.shape[1]), lambda i: (i, 0)) for a in cast]
    return pl.pallas_call(
        functools.partial(_gla_kernel, zero_init=zero_init, n_cast=len(cast)),
        grid=(b // nb,),
        in_specs=[seq(D_QK), seq(D_QK), seq(D_V), seq(2 * D_QK), seq(D_V),
                  pl.BlockSpec((1, GLA_DV), lambda i: (0, 0))] + ([] if zero_init else [st, st]) + slabs,
        out_specs=[seq(D_V), st, st] + slabs,
        out_shape=[jax.ShapeDtypeStruct((b, t, D_V), BF16),
                   jax.ShapeDtypeStruct((b, D_QK, GLA_DV), F32),
                   jax.ShapeDtypeStruct((b, D_QK, GLA_DV), F32)]
        + [jax.ShapeDtypeStruct(a.shape, BF16) for a in cast],
        scratch_shapes=[pltpu.VMEM((nb, t, D_V), F32), pltpu.VMEM((nb, t, D_V), F32)],
        compiler_params=pltpu.CompilerParams(
            dimension_semantics=("arbitrary",), vmem_limit_bytes=VMEM_LIMIT),
        name="gla",
    )(q, k, v, g, gate, gla_g, *(() if zero_init else init_states), *cast)


ATT_KEY_BLOCK = 256
ATT_Q_TILE = 256
LOG2E = 1.4426950408889634


ATT_TILES_LONG = 2
ATT_TILES_SHORT = 8


def _attn_kernel(*args, keys_per_tile, cached):
    qt_ref, k_ref, vt_ref = args[:3]
    kc_ref, vtc_ref = args[3:5] if cached else (None, None)
    o_ref, s0, s1, m0, m1, a0, a1 = args[-7:]
    tq = ATT_Q_TILE
    tiles = qt_ref.shape[1] // tq
    t_new = k_ref.shape[0] // tiles if keys_per_tile else k_ref.shape[0]
    tk = t_new + (kc_ref.shape[0] if cached else 0)
    kb = min(ATT_KEY_BLOCK, tk)
    nkb = tk // kb
    zero = jnp.zeros((HEAD_DIM, tq), BF16)
    ones = jnp.ones((16, kb), BF16)
    items = [(qi, kv) for qi in range(tiles) for kv in range(ATT_KV_HEADS)]
    bufs = (s0, s1)
    mbufs = (m0, m1)
    abufs = (a0, a1)
    separate = nkb > 1

    def q_ext(qi, kv):
        cols = []
        for g in range(GROUP):
            hd = kv * GROUP + g
            qt = qt_ref[hd * HEAD_DIM:(hd + 1) * HEAD_DIM, qi * tq:(qi + 1) * tq]
            cols.append(jnp.concatenate([qt, zero] if kv == 0 else [zero, qt], axis=0))
        return jnp.concatenate(cols, axis=1)

    def key0(qi):
        return qi * tk if keys_per_tile else 0

    def score_piece(n, q_all, i, m):
        r0 = key0(items[n][0]) + i * kb
        keys = k_ref[r0:r0 + kb, :] if i * kb < t_new else kc_ref[r0 - t_new:r0 - t_new + kb, :]
        s = _dot(keys, q_all)
        bufs[n % 2][i * kb:(i + 1) * kb, :] = s
        for r in range(kb // 8):
            tile = s[r * 8:(r + 1) * 8, :]
            m = tile if m is None else jnp.maximum(m, tile)
        return m

    def value_piece(n, m, i, acc):
        qi, kv = items[n]
        c0 = key0(qi) + i * kb
        p = jnp.exp2(bufs[n % 2][i * kb:(i + 1) * kb, :] - m).astype(BF16)
        hd_rows = slice(kv * HEAD_DIM, (kv + 1) * HEAD_DIM)
        vt = vt_ref[hd_rows, c0:c0 + kb] if i * kb < t_new else vtc_ref[hd_rows, c0 - t_new:c0 - t_new + kb]
        vt1 = jnp.concatenate([vt, ones], axis=0)
        d = _dot(vt1, p)
        return d if acc is None else acc + d

    def emit(n, acc):
        qi, kv = items[n]
        o = acc[:HEAD_DIM] / acc[HEAD_DIM:HEAD_DIM + 1]
        for g in range(0, GROUP, 2):
            two = jnp.concatenate([o[:, g * tq:(g + 1) * tq], o[:, (g + 1) * tq:(g + 2) * tq]], axis=0)
            hd = kv * GROUP + g
            o_ref[qi * tq:(qi + 1) * tq, hd * HEAD_DIM:(hd + 2) * HEAD_DIM] = two.T.astype(BF16)

    def pair(n):
        if separate and n >= 2:
            emit(n - 2, abufs[n % 2][...])
        if n > len(items):
            return
        q_all = q_ext(*items[n]) if n < len(items) else None
        m_prev = mbufs[(n - 1) % 2][...] if n > 0 else None
        m_new = acc = None
        for i in range(nkb):
            if n < len(items):
                m_new = score_piece(n, q_all, i, m_new)
            if n > 0:
                acc = value_piece(n - 1, m_prev, i, acc)
        if n > 0 and separate:
            abufs[(n - 1) % 2][...] = acc
        elif n > 0:
            emit(n - 1, acc)
        if n < len(items):
            mbufs[n % 2][...] = jnp.max(m_new, axis=0, keepdims=True)

    for n in range(len(items) + (2 if separate else 1)):
        if separate:
            pl.when(pl.program_id(0) > -(n + 1))(functools.partial(pair, n))
        else:
            pair(n)


def _attention(qt, k, vt, batch, cache=None):
    ntok = qt.shape[1]
    t_new = k.shape[0] // batch
    tq = ATT_Q_TILE
    nq = ntok // batch // tq
    tiles = ATT_TILES_LONG if t_new > ATT_KEY_BLOCK else ATT_TILES_SHORT
    keys_per_tile = nq < tiles
    extra_specs, extra, tk = [], (), t_new
    if keys_per_tile:
        assert cache is None
        seqs = tiles // nq
        k_spec = pl.BlockSpec((seqs * t_new, D_KA), lambda t: (t, 0))
        vt_spec = pl.BlockSpec((D_KA, seqs * t_new), lambda t: (0, t))
    else:
        spq = nq // tiles
        k_spec = pl.BlockSpec((t_new, D_KA), lambda t: (t // spq, 0))
        vt_spec = pl.BlockSpec((D_KA, t_new), lambda t: (0, t // spq))
        if cache is not None:
            t_old = cache[0].shape[0] // batch
            assert t_new % ATT_KEY_BLOCK == 0 and t_old % ATT_KEY_BLOCK == 0
            extra_specs = [pl.BlockSpec((t_old, D_KA), lambda t: (t // spq, 0)),
                           pl.BlockSpec((D_KA, t_old), lambda t: (0, t // spq))]
            extra, tk = cache, t_new + t_old
    return pl.pallas_call(
        functools.partial(_attn_kernel, keys_per_tile=keys_per_tile, cached=cache is not None),
        grid=(ntok // (tiles * tq),),
        in_specs=[pl.BlockSpec((D_QA, tiles * tq), lambda t: (0, t)), k_spec, vt_spec] + extra_specs,
        out_specs=pl.BlockSpec((tiles * tq, D_QA), lambda t: (t, 0)),
        out_shape=jax.ShapeDtypeStruct((ntok, D_QA), BF16),
        scratch_shapes=[pltpu.VMEM((tk, GROUP * tq), F32), pltpu.VMEM((tk, GROUP * tq), F32),
                        pltpu.VMEM((1, GROUP * tq), F32), pltpu.VMEM((1, GROUP * tq), F32),
                        pltpu.VMEM((HEAD_DIM + 16, GROUP * tq), F32),
                        pltpu.VMEM((HEAD_DIM + 16, GROUP * tq), F32)],
        compiler_params=pltpu.CompilerParams(
            dimension_semantics=("arbitrary",), vmem_limit_bytes=VMEM_LIMIT),
        name="attention",
    )(qt, k, vt, *extra)


def _post_kernel(og_ref, oa_ref, x_ref, mod_ref, n2_ref, fg_ref, wo_ref, w1_ref, w3_ref, w2_ref,
                 y_ref):
    D = D_MODEL
    mod = mod_ref[0]
    g1, sh2, sc2, g2 = (mod[:, i * D:(i + 1) * D] for i in range(2, 6))
    out = _dot(og_ref[...], wo_ref[0:D_V, :]) + _dot(oa_ref[...], wo_ref[D_V:D_V + D_QA, :])
    x1 = x_ref[...] + g1 * out
    hb = (_rms(x1, n2_ref[...]) * (1.0 + sc2) + sh2).astype(BF16)
    act = (_silu(_dot(hb, w1_ref[...])) * _dot(hb, w3_ref[...])).astype(BF16)
    x2 = x1 + g2 * _dot(act, w2_ref[...])
    y_ref[...] = _rms(x2, fg_ref[...])


def _post_mixer(o_gla, o_att, x, mod3, mod_row0, tokens_per_mod, weights, tm):
    ntok = x.shape[0]
    tiles_per_mod = tokens_per_mod // tm
    whole = pl.BlockSpec(memory_space=pltpu.VMEM)
    row = lambda w: pl.BlockSpec((tm, w), lambda i: (i, 0))
    return pl.pallas_call(
        _post_kernel,
        grid=(ntok // tm,),
        in_specs=[row(D_V), row(D_QA), row(D_MODEL),
                  pl.BlockSpec((1, 1, N_MOD * D_MODEL), lambda i: (mod_row0 + i // tiles_per_mod, 0, 0)),
                  whole, whole, whole, whole, whole, whole],
        out_specs=row(D_MODEL),
        out_shape=jax.ShapeDtypeStruct((ntok, D_MODEL), F32),
        compiler_params=pltpu.CompilerParams(
            dimension_semantics=("arbitrary",), vmem_limit_bytes=VMEM_LIMIT),
        name="post_mixer",
    )(o_gla, o_att, x, mod3, *weights)


def _rope_tables(t):
    rows = t // GRID_W
    row = np.repeat(np.arange(rows, dtype=np.float64), GRID_W)
    col = np.tile(np.arange(GRID_W, dtype=np.float64), rows)
    inv = ROPE_THETA ** (-np.arange(0, ROPE_AXIS_DIM, 2, dtype=np.float64) / ROPE_AXIS_DIM)
    ang_r = row[:, None] * inv
    ang_c = col[:, None] * inv
    cos = np.concatenate([np.cos(ang_r)] * 2 + [np.cos(ang_c)] * 2, axis=-1)
    sin = np.concatenate([-np.sin(ang_r), np.sin(ang_r), -np.sin(ang_c), np.sin(ang_c)], axis=-1)
    tile = lambda a: jnp.asarray(np.tile(a, (1, ATT_KV_HEADS)), F32)
    return tile(cos), tile(sin)


def _block_mean(width, group):
    return jnp.asarray(np.kron(np.eye(width // group), np.ones((group, group))) / group, BF16)


def kernel(x_prompt, x_sample, cache_k, cache_v, state_gla_fwd, state_gla_bwd, c, c_ctx,
           ada_w, ada_b, norm1_g, norm2_g, w_in, w_gk2, b_gk2, gla_norm_g, q_norm_g, k_norm_g,
           w_out, w_ffn1, w_ffn3, w_ffn2, final_g):
    B, T, D = x_prompt.shape
    BL, TL, _ = x_sample.shape
    TP = cache_k.shape[2]
    l = 0

    mod3, w_in_r = _adaln(c_ctx, c, ada_w[l], ada_b[l], jnp.transpose(w_in[l]))
    zero = jnp.zeros((GLA_LOWRANK, D_QK), F32)
    w_gk = jnp.concatenate([jnp.concatenate([w_gk2[l, 0], zero], axis=1),
                            jnp.concatenate([zero, w_gk2[l, 1]], axis=1)], axis=0).astype(BF16)
    pre_w = (norm1_g[l].reshape(1, D), w_in_r, w_gk, b_gk2[l].reshape(1, 2 * D_QK),
             jnp.tile(q_norm_g[l], ATT_HEADS).reshape(1, D_QA),
             jnp.tile(k_norm_g[l], ATT_KV_HEADS).reshape(1, D_KA),
             _block_mean(D_QA, HEAD_DIM), _block_mean(D_KA, HEAD_DIM))
    post_norms = (norm2_g[l].reshape(1, D), final_g.reshape(1, D))
    post_w_f32 = (w_out[l], w_ffn1[l], w_ffn3[l], w_ffn2[l])
    gla_g = gla_norm_g[l].reshape(1, GLA_DV)

    xc = x_prompt.reshape(B * T, D)
    q, k, v, gate, g, qa, ka, va, kc, vc = _pre_mixer(
        xc, mod3, 0, B * T, T, pre_w, None, True, PRE_TILE_ROWS)
    r3 = lambda a, b, t: a.reshape(b, t, a.shape[-1])
    o_gla, s_f, s_b, *post_w_bf16 = _gla(
        r3(q, B, T), r3(k, B, T), r3(v, B, T), r3(g, B, T), r3(gate, B, T), gla_g, None,
        GLA_CONTEXT_SEQS, cast=post_w_f32)
    post_w = post_norms + tuple(post_w_bf16)
    o_att = _attention(qa, ka, va, B)
    y_prompt = _post_mixer(o_gla.reshape(B * T, D_V), o_att, xc, mod3, 0, B * T, post_w, POST_TILE_ROWS)

    xl = x_sample.reshape(BL * TL, D)
    q, k, v, gate, g, qa, ka, va = _pre_mixer(
        xl, mod3, 1, TL, TL, pre_w, _rope_tables(TL), False, PRE_TILE_ROWS)
    o_gla, _, _ = _gla(r3(q, BL, TL), r3(k, BL, TL), r3(v, BL, TL), r3(g, BL, TL), r3(gate, BL, TL),
                       gla_g, (state_gla_fwd[:, l].reshape(BL, D_QK, GLA_DV),
                               state_gla_bwd[:, l].reshape(BL, D_QK, GLA_DV)), BL)

    k_old = cache_k[:, l].reshape(BL * TP, D_KA).astype(BF16)
    vt_old = jnp.transpose(cache_v[:, l], (2, 3, 0, 1)).reshape(D_KA, BL * TP).astype(BF16)
    o_att = _attention(qa, ka, va, BL, (k_old, vt_old))
    y_sample = _post_mixer(o_gla.reshape(BL * TL, D_V), o_att, xl, mod3, 1, TL, post_w, POST_TILE_ROWS)

    def cache_out(ct):
        return jnp.transpose(ct.reshape(B, 1, ATT_KV_HEADS, HEAD_DIM, T), (0, 1, 4, 2, 3))

    return (y_prompt.reshape(B, T, D), y_sample.reshape(BL, TL, D), cache_out(kc), cache_out(vc),
            s_f.reshape(B, 1, GLA_HEADS, GLA_DK, GLA_DV), s_b.reshape(B, 1, GLA_HEADS, GLA_DK, GLA_DV))
```

```python
import functools

import numpy as np
import jax
import jax.numpy as jnp
from jax import lax
from jax.experimental import pallas as pl
from jax.experimental.pallas import tpu as pltpu

F32 = jnp.float32
BF16 = jnp.bfloat16

D_MODEL = 1024
GRID_W = 64
GLA_HEADS = 4
GLA_DK = 64
GLA_DV = 128
GLA_LOWRANK = 16
GLA_GATE_NORM = 16.0
GLA_CHUNK = 64
ATT_HEADS = 8
ATT_KV_HEADS = 2
HEAD_DIM = 64
ROPE_AXIS_DIM = HEAD_DIM // 2
ROPE_THETA = 10000.0
N_MOD = 6
EPS = 1e-6

D_QK = GLA_HEADS * GLA_DK
D_V = GLA_HEADS * GLA_DV
D_QA = ATT_HEADS * HEAD_DIM
D_KA = ATT_KV_HEADS * HEAD_DIM
GROUP = ATT_HEADS // ATT_KV_HEADS
O_Q, O_K, O_V, O_OG = 0, D_QK, 2 * D_QK, 2 * D_QK + D_V
O_QA = O_OG + D_V
O_KA = O_QA + D_QA
O_VA = O_KA + D_KA
O_LR = O_VA + D_KA
D_PROJ = O_LR + 2 * GLA_LOWRANK

COND_ROWS = 8
VMEM_LIMIT = 56 * 1024 * 1024
PRE_TILE_ROWS = 1024
POST_TILE_ROWS = 512
GLA_CONTEXT_SEQS = 4


def _dot(a, b):
    return jnp.dot(a, b, preferred_element_type=F32)


def _dot_nt(a, b):
    return lax.dot_general(a, b, (((1,), (1,)), ((), ())), preferred_element_type=F32)


def _dot_tn(a, b):
    return lax.dot_general(a, b, (((0,), (0,)), ((), ())), preferred_element_type=F32)


def _rms(x, g):
    ms = jnp.mean(x * x, axis=-1, keepdims=True)
    return x * lax.rsqrt(ms + EPS) * g


def _silu(x):
    return x * jax.nn.sigmoid(x)


ADALN_K_BLOCK = 256


def _adaln_kernel(cctx_ref, c_ref, w_ref, b_ref, wint_ref, o_ref, win_o, acc):
    k = pl.program_id(0)

    lr0, lr1 = O_OG + D_V, O_OG + D_V + 2 * GLA_LOWRANK
    feats = jnp.concatenate([wint_ref[0:lr0, :], wint_ref[lr1:D_PROJ, :], wint_ref[lr0:lr1, :]],
                            axis=0)
    for j in range(0, D_PROJ, 128):
        rows = min(128, D_PROJ - j)
        tile = feats[j:j + rows, :]
        if rows < 128:
            tile = jnp.concatenate([tile, jnp.zeros((128 - rows, tile.shape[1]), F32)], axis=0)
        win_o[:, j:j + rows] = tile.T[:, 0:rows].astype(BF16)

    @pl.when(k == 0)
    def _():
        acc[...] = jnp.broadcast_to(b_ref[...], acc.shape)

    pad = jnp.zeros((COND_ROWS - 1 - c_ref.shape[0], c_ref.shape[1]), F32)
    cond = jnp.concatenate([cctx_ref[...], c_ref[...], pad], axis=0)
    acc[...] += _dot(_silu(cond).astype(BF16), w_ref[...].astype(BF16))

    @pl.when(k == pl.num_programs(0) - 1)
    def _():
        o_ref[:, 0, :] = acc[...]


def _adaln(c_ctx, c, ada_w, ada_b, w_in_t):
    d, n = ada_w.shape
    tk = ADALN_K_BLOCK
    assert 1 + c.shape[0] <= COND_ROWS and w_in_t.shape == (D_PROJ, d)
    return pl.pallas_call(
        _adaln_kernel,
        grid=(d // tk,),
        in_specs=[
            pl.BlockSpec((1, tk), lambda k: (0, k)),
            pl.BlockSpec((c.shape[0], tk), lambda k: (0, k)),
            pl.BlockSpec((tk, n), lambda k: (k, 0)),
            pl.BlockSpec((1, n), lambda k: (0, 0)),
            pl.BlockSpec((D_PROJ, tk), lambda k: (0, k)),
        ],
        out_specs=[pl.BlockSpec((COND_ROWS, 1, n), lambda k: (0, 0, 0)),
                   pl.BlockSpec((tk, D_PROJ), lambda k: (k, 0))],
        out_shape=[jax.ShapeDtypeStruct((COND_ROWS, 1, n), F32),
                   jax.ShapeDtypeStruct((d, D_PROJ), BF16)],
        scratch_shapes=[pltpu.VMEM((COND_ROWS, n), F32)],
        compiler_params=pltpu.CompilerParams(
            dimension_semantics=("arbitrary",), vmem_limit_bytes=VMEM_LIMIT),
        name="adaln",
    )(c_ctx.reshape(1, d), c, ada_w, ada_b.reshape(1, n), w_in_t)


def _swap_halves(x, lane_lo):
    n = x.shape[-1]
    up = pltpu.roll(x, n - ROPE_AXIS_DIM // 2, axis=1)
    dn = pltpu.roll(x, ROPE_AXIS_DIM // 2, axis=1)
    return jnp.where(lane_lo, up, dn)


PRE_SUB_ROWS = 512
PRE_SUB_ROWS_ROPE = 256


def _pre_kernel(*refs, rope, emit_cache):
    it = iter(refs)
    x_ref, mod_ref, n1_ref, win_ref, wgk_ref, bgk_ref = (next(it) for _ in range(6))
    qg_ref, kg_ref, bdq_ref, bdk_ref = (next(it) for _ in range(4))
    if rope:
        cos_ref, sin_ref = next(it), next(it)
    q_o, k_o, v_o, gate_o, g_o, qa_o, ka_o, va_o = (next(it) for _ in range(8))
    if emit_cache:
        kc_o, vc_o = next(it), next(it)

    sub = PRE_SUB_ROWS_ROPE if rope else PRE_SUB_ROWS
    n_sub = x_ref.shape[0] // sub
    mod = mod_ref[0]
    sh1 = mod[:, 0:D_MODEL]
    sc1 = mod[:, D_MODEL:2 * D_MODEL]

    def project(j):
        rows = slice(j * sub, (j + 1) * sub)
        h = _rms(x_ref[rows, :], n1_ref[...]) * (1.0 + sc1) + sh1
        return _dot(h.astype(BF16), win_ref[...])

    def finish(j, p_ref):
        rows = slice(j * sub, (j + 1) * sub)
        q_o[rows, :] = p_ref[:, O_Q:O_Q + D_QK].astype(BF16)
        k_o[rows, :] = p_ref[:, O_K:O_K + D_QK].astype(BF16)
        v_o[rows, :] = p_ref[:, O_V:O_V + D_V].astype(BF16)
        gate_o[rows, :] = _silu(p_ref[:, O_OG:O_OG + D_V]).astype(BF16)

        gk = _dot(p_ref[:, O_LR:O_LR + 2 * GLA_LOWRANK].astype(BF16), wgk_ref[...]) + bgk_ref[...]
        g_o[rows, :] = ((jnp.minimum(gk, 0.0) - jnp.log(1.0 + jnp.exp(-jnp.abs(gk))))
                        * (1.0 / GLA_GATE_NORM))

        qa = p_ref[:, O_QA:O_QA + D_QA]
        ka = p_ref[:, O_KA:O_KA + D_KA]
        va = p_ref[:, O_VA:O_VA + D_KA]
        qa = qa * lax.rsqrt(_dot((qa * qa).astype(BF16), bdq_ref[...]) + EPS) * qg_ref[...]
        ka = ka * lax.rsqrt(_dot((ka * ka).astype(BF16), bdk_ref[...]) + EPS) * kg_ref[...]
        if rope:
            lane_q = lax.broadcasted_iota(jnp.int32, qa.shape, 1)
            lane_k = lax.broadcasted_iota(jnp.int32, ka.shape, 1)
            lo_q = (lane_q % ROPE_AXIS_DIM) < ROPE_AXIS_DIM // 2
            lo_k = (lane_k % ROPE_AXIS_DIM) < ROPE_AXIS_DIM // 2
            cos_k = cos_ref[rows, :]
            sin_k = sin_ref[rows, :]
            cos_q = jnp.concatenate([cos_k] * GROUP, axis=1)
            sin_q = jnp.concatenate([sin_k] * GROUP, axis=1)
            qa = qa * cos_q + _swap_halves(qa, lo_q) * sin_q
            ka = ka * cos_k + _swap_halves(ka, lo_k) * sin_k
        vat = va.T
        if emit_cache:
            kat = ka.T
            t = kc_o.shape[2]
            for sq in range(sub // t):
                kc_o[j * (sub // t) + sq] = kat[:, sq * t:(sq + 1) * t]
                vc_o[j * (sub // t) + sq] = vat[:, sq * t:(sq + 1) * t]
        qa_o[:, rows] = (qa * (HEAD_DIM ** -0.5 * LOG2E)).T.astype(BF16)
        ka_o[rows, :] = ka.astype(BF16)
        va_o[:, rows] = vat.astype(BF16)

    p_prev = None
    for n in range(n_sub + 1):
        p_new = project(n) if n < n_sub else None
        if n > 0:
            finish(n - 1, p_prev)
        p_prev = p_new


def _pre_mixer(x, mod3, mod_row0, tokens_per_mod, seq_len, weights, rope_tabs, emit_cache, tm):
    ntok = x.shape[0]
    n1, w_in_r, w_gk, b_gk, qg, kg, bdq, bdk = weights
    rope = rope_tabs is not None
    tiles_per_mod = tokens_per_mod // tm
    tiles_per_seq = seq_len // tm
    whole = pl.BlockSpec(memory_space=pltpu.VMEM)
    row = lambda w: pl.BlockSpec((tm, w), lambda i: (i, 0))
    in_specs = [
        row(D_MODEL),
        pl.BlockSpec((1, 1, N_MOD * D_MODEL), lambda i: (mod_row0 + i // tiles_per_mod, 0, 0)),
        whole, whole, whole, whole, whole, whole, whole, whole,
    ]
    args = [x, mod3, n1, w_in_r, w_gk, b_gk, qg, kg, bdq, bdk]
    if rope:
        tab = pl.BlockSpec((tm, D_KA), lambda i: (i % tiles_per_seq, 0))
        in_specs += [tab, tab]
        args += list(rope_tabs)
    col = lambda w: pl.BlockSpec((w, tm), lambda i: (0, i))
    out_specs = [row(D_QK), row(D_QK), row(D_V), row(D_V), row(2 * D_QK),
                 col(D_QA), row(D_KA), col(D_KA)]
    out_shape = [
        jax.ShapeDtypeStruct((ntok, D_QK), BF16),
        jax.ShapeDtypeStruct((ntok, D_QK), BF16),
        jax.ShapeDtypeStruct((ntok, D_V), BF16),
        jax.ShapeDtypeStruct((ntok, D_V), BF16),
        jax.ShapeDtypeStruct((ntok, 2 * D_QK), F32),
        jax.ShapeDtypeStruct((D_QA, ntok), BF16),
        jax.ShapeDtypeStruct((ntok, D_KA), BF16),
        jax.ShapeDtypeStruct((D_KA, ntok), BF16),
    ]
    if emit_cache:
        seqs = tm // seq_len
        out_specs += [pl.BlockSpec((seqs, D_KA, seq_len), lambda i: (i, 0, 0))] * 2
        out_shape += [jax.ShapeDtypeStruct((ntok // seq_len, D_KA, seq_len), F32)] * 2
    return pl.pallas_call(
        functools.partial(_pre_kernel, rope=rope, emit_cache=emit_cache),
        grid=(ntok // tm,),
        in_specs=in_specs,
        out_specs=out_specs,
        out_shape=out_shape,
        compiler_params=pltpu.CompilerParams(
            dimension_semantics=("arbitrary",), vmem_limit_bytes=VMEM_LIMIT),
        name="pre_mixer_rope" if rope else "pre_mixer",
    )(*args)


def _gla_chunks(units, q_ref, k_ref, v_ref, g_ref, half_lo):
    C = GLA_CHUNK
    n_pair = GLA_HEADS // 2
    rows = [pl.ds(pl.multiple_of(c * C, C), C) for (_, c, _, _, _, _) in units]

    bs = []
    for (bi, _, fwd, tri, _, _), r in zip(units, rows):
        gc = g_ref[bi, r, 0:D_QK] if fwd else g_ref[bi, r, D_QK:2 * D_QK]
        ltri = jnp.where(tri, 1.0, 0.0).astype(BF16)
        g_hi = gc.astype(BF16)
        g_lo = (gc - g_hi.astype(F32)).astype(BF16)
        bs.append(_dot(ltri, g_hi) + _dot(ltri, g_lo))

    qts, kts, kes, decays = [], [], [], []
    for (bi, _, fwd, _, _, _), r, b in zip(units, rows, bs):
        qc = q_ref[bi, r, :].astype(F32) * GLA_DK ** -0.5
        kc = k_ref[bi, r, :].astype(F32)
        btot = b[C - 1:C, :] if fwd else b[0:1, :]
        qts.append((qc * jnp.exp(b)).astype(BF16))
        kts.append((kc * jnp.exp(-b)).astype(BF16))
        kes.append((kc * jnp.exp(btot - b)).astype(BF16))
        decays.append(jnp.transpose(jnp.broadcast_to(jnp.exp(btot), (8, D_QK)))[:, 0:1])

    scores, incs, qms = [], [], []
    for (bi, _, _, tri, _, _), r, qt, kt, ke in zip(units, rows, qts, kts, kes):
        for pair in range(n_pair):
            lanes = slice(pair * 128, (pair + 1) * 128)
            incs.append(_dot_tn(ke[:, lanes], v_ref[bi, r, pair * 2 * GLA_DV:(pair + 1) * 2 * GLA_DV]))
            for hh in range(2):
                qm = jnp.where(half_lo if hh == 0 else ~half_lo, qt[:, lanes], jnp.zeros((C, 128), BF16))
                qms.append(qm)
                scores.append(jnp.where(tri, _dot_nt(qm, kt[:, lanes]), 0.0).astype(BF16))

    for ui, ((bi, _, _, _, s_ref, o_ref), r) in enumerate(zip(units, rows)):
        state = s_ref[bi]
        state_b = state.astype(BF16)
        for hd in range(GLA_HEADS):
            pair = hd // 2
            lhs = jnp.concatenate([qms[ui * GLA_HEADS + hd], scores[ui * GLA_HEADS + hd]], axis=1)
            rhs = jnp.concatenate([state_b[pair * 128:(pair + 1) * 128, :],
                                   v_ref[bi, r, hd * GLA_DV:(hd + 1) * GLA_DV]], axis=0)
            o_ref[bi, r, hd * GLA_DV:(hd + 1) * GLA_DV] = _dot(lhs, rhs)
        inc = jnp.concatenate(
            [incs[ui * n_pair + hd // 2][(hd % 2) * GLA_DK:(hd % 2 + 1) * GLA_DK,
                                         (hd % 2) * GLA_DV:(hd % 2 + 1) * GLA_DV]
             for hd in range(GLA_HEADS)], axis=0)
        s_ref[bi] = decays[ui] * state + inc


def _gla_kernel(*args, zero_init, n_cast, prefetch):
    q_ref, k_ref, v_ref, g_ref, gate_ref, gn_ref = args[:6]
    n_in = 6 + (0 if zero_init else 2)
    cast_src = args[n_in:n_in + n_cast]
    o_ref, sf_ref, sb_ref = args[n_in + n_cast:n_in + n_cast + 3]
    cast_dst = args[n_in + n_cast + 3:n_in + 2 * n_cast + 3]
    if prefetch:
        of_scr, ob_scr = args[-8:-6]
        hbm = (q_ref, k_ref, v_ref, g_ref, gate_ref)
        vm = args[-6:-1]
        sem = args[-1]
        q_ref, k_ref, v_ref, g_ref, gate_ref = vm
        slab = GLA_STEPS_PER_ITER * GLA_CHUNK
        n_slab = q_ref.shape[1] // slab

        def slab_copy(a, c):
            rows = pl.ds(c * slab, slab)
            return pltpu.make_async_copy(hbm[a].at[:, rows, :], vm[a].at[:, rows, :], sem.at[a, c])

        order = [c for i in range(n_slab // 2) for c in (i, n_slab - 1 - i)]
        for c in order:
            for a in range(4):
                slab_copy(a, c).start()
        for c in order:
            slab_copy(4, c).start()
    else:
        of_scr, ob_scr = args[-2:]
    for src, dst in zip(cast_src, cast_dst):
        dst[...] = src[...].astype(BF16)
    C = GLA_CHUNK
    nb, t, _ = q_ref.shape
    n = t // C
    row = lax.broadcasted_iota(jnp.int32, (C, C), 0)
    col = lax.broadcasted_iota(jnp.int32, (C, C), 1)
    half_lo = lax.broadcasted_iota(jnp.int32, (C, 128), 1) < GLA_DK
    refs = (q_ref, k_ref, v_ref, g_ref)
    if zero_init:
        sf_ref[...] = jnp.zeros(sf_ref.shape, F32)
        sb_ref[...] = jnp.zeros(sb_ref.shape, F32)
    else:
        sf_ref[...] = args[6][...]
        sb_ref[...] = args[7][...]

    steps = min(GLA_STEPS_PER_ITER, n)
    assert n % steps == 0

    def body(i, carry):
        if prefetch:
            @pl.when(i < n_slab // 2)
            def _():
                for a in range(4):
                    slab_copy(a, i).wait()
                    slab_copy(a, n_slab - 1 - i).wait()
        units = []
        for u in range(steps):
            c = i * steps + u
            for bi in range(nb):
                units.append((bi, c, True, row >= col, sf_ref, of_scr))
                units.append((bi, n - 1 - c, False, row <= col, sb_ref, ob_scr))
        _gla_chunks(units, *refs, half_lo)
        return carry

    lax.fori_loop(0, n // steps, body, 0)
    if prefetch:
        assert steps == GLA_STEPS_PER_ITER and n_slab % 2 == 0
        for c in range(n_slab):
            slab_copy(4, c).wait()

    rt = GLA_OUT_ROWS
    gn = gn_ref[...]

    def finish(i, carry):
        rows = pl.ds(pl.multiple_of(i * rt, rt), rt)
        for bi in range(nb):
            o = of_scr[bi, rows, :] + ob_scr[bi, rows, :]
            gate = gate_ref[bi, rows, :].astype(F32)
            for hd in range(GLA_HEADS):
                sl = slice(hd * GLA_DV, (hd + 1) * GLA_DV)
                o_ref[bi, rows, sl] = (_rms(o[:, sl], gn) * gate[:, sl]).astype(BF16)
        return carry

    lax.fori_loop(0, t // rt, finish, 0)


GLA_STEPS_PER_ITER = 4
GLA_OUT_ROWS = 256


def _gla(q, k, v, g, gate, gla_g, init_states, nb, cast=(), prefetch=False):
    b, t, _ = q.shape
    assert not prefetch or b == nb
    widths = (D_QK, D_QK, D_V, 2 * D_QK, D_V)
    dtypes = (BF16, BF16, BF16, F32, BF16)
    if prefetch:
        seq_in = lambda w: pl.BlockSpec(memory_space=pl.ANY)
        extra_scratch = [pltpu.VMEM((nb, t, w), d) for w, d in zip(widths, dtypes)] + [
            pltpu.SemaphoreType.DMA((len(widths), t // (GLA_STEPS_PER_ITER * GLA_CHUNK)))]
    else:
        seq_in = lambda w: pl.BlockSpec((nb, t, w), lambda i: (i, 0, 0))
        extra_scratch = []
    seq = lambda w: pl.BlockSpec((nb, t, w), lambda i: (i, 0, 0))
    st = pl.BlockSpec((nb, D_QK, GLA_DV), lambda i: (i, 0, 0))
    zero_init = init_states is None
    steps = b // nb
    slabs = [pl.BlockSpec((a.shape[0] // steps, a.shape[1]), lambda i: (i, 0)) for a in cast]
    return pl.pallas_call(
        functools.partial(_gla_kernel, zero_init=zero_init, n_cast=len(cast), prefetch=prefetch),
        grid=(b // nb,),
        in_specs=[seq_in(w) for w in widths] + [
                  pl.BlockSpec((1, GLA_DV), lambda i: (0, 0))] + ([] if zero_init else [st, st]) + slabs,
        out_specs=[seq(D_V), st, st] + slabs,
        out_shape=[jax.ShapeDtypeStruct((b, t, D_V), BF16),
                   jax.ShapeDtypeStruct((b, D_QK, GLA_DV), F32),
                   jax.ShapeDtypeStruct((b, D_QK, GLA_DV), F32)]
        + [jax.ShapeDtypeStruct(a.shape, BF16) for a in cast],
        scratch_shapes=[pltpu.VMEM((nb, t, D_V), F32), pltpu.VMEM((nb, t, D_V), F32)] + extra_scratch,
        compiler_params=pltpu.CompilerParams(
            dimension_semantics=("arbitrary",), vmem_limit_bytes=VMEM_LIMIT),
        name="gla",
    )(q, k, v, g, gate, gla_g, *(() if zero_init else init_states), *cast)


ATT_KEY_BLOCK = 256
ATT_Q_TILE = 256
LOG2E = 1.4426950408889634


ATT_TILES_LONG = 2
ATT_TILES_SHORT = 8


def _attn_kernel(*args, keys_per_tile, cached):
    qt_ref, k_ref, vt_ref = args[:3]
    kc_ref, vtc_ref = args[3:5] if cached else (None, None)
    o_ref, s0, s1, m0, m1, a0, a1 = args[-7:]
    tq = ATT_Q_TILE
    tiles = qt_ref.shape[1] // tq
    t_new = k_ref.shape[0] // tiles if keys_per_tile else k_ref.shape[0]
    tk = t_new + (kc_ref.shape[0] if cached else 0)
    kb = min(ATT_KEY_BLOCK, tk)
    nkb = tk // kb
    zero = jnp.zeros((HEAD_DIM, tq), BF16)
    ones = jnp.ones((16, kb), BF16)
    items = [(qi, kv) for qi in range(tiles) for kv in range(ATT_KV_HEADS)]
    bufs = (s0, s1)
    mbufs = (m0, m1)
    abufs = (a0, a1)
    separate = nkb > 1

    def q_ext(qi, kv):
        cols = []
        for g in range(GROUP):
            hd = kv * GROUP + g
            qt = qt_ref[hd * HEAD_DIM:(hd + 1) * HEAD_DIM, qi * tq:(qi + 1) * tq]
            cols.append(jnp.concatenate([qt, zero] if kv == 0 else [zero, qt], axis=0))
        return jnp.concatenate(cols, axis=1)

    def key0(qi):
        return qi * tk if keys_per_tile else 0

    def score_piece(n, q_all, i, m):
        r0 = key0(items[n][0]) + i * kb
        keys = k_ref[r0:r0 + kb, :] if i * kb < t_new else kc_ref[r0 - t_new:r0 - t_new + kb, :]
        s = _dot(keys, q_all)
        bufs[n % 2][i * kb:(i + 1) * kb, :] = s
        for r in range(kb // 8):
            tile = s[r * 8:(r + 1) * 8, :]
            m = tile if m is None else jnp.maximum(m, tile)
        return m

    def value_piece(n, m, i, acc):
        qi, kv = items[n]
        c0 = key0(qi) + i * kb
        p = jnp.exp2(bufs[n % 2][i * kb:(i + 1) * kb, :] - m).astype(BF16)
        hd_rows = slice(kv * HEAD_DIM, (kv + 1) * HEAD_DIM)
        vt = vt_ref[hd_rows, c0:c0 + kb] if i * kb < t_new else vtc_ref[hd_rows, c0 - t_new:c0 - t_new + kb]
        vt1 = jnp.concatenate([vt, ones], axis=0)
        d = _dot(vt1, p)
        return d if acc is None else acc + d

    def emit(n, acc):
        qi, kv = items[n]
        o = acc[:HEAD_DIM] / acc[HEAD_DIM:HEAD_DIM + 1]
        for g in range(0, GROUP, 2):
            two = jnp.concatenate([o[:, g * tq:(g + 1) * tq], o[:, (g + 1) * tq:(g + 2) * tq]], axis=0)
            hd = kv * GROUP + g
            o_ref[qi * tq:(qi + 1) * tq, hd * HEAD_DIM:(hd + 2) * HEAD_DIM] = two.T.astype(BF16)

    def pair(n):
        if separate and n >= 2:
            emit(n - 2, abufs[n % 2][...])
        if n > len(items):
            return
        q_all = q_ext(*items[n]) if n < len(items) else None
        m_prev = mbufs[(n - 1) % 2][...] if n > 0 else None
        m_new = acc = None
        for i in range(nkb):
            if n < len(items):
                m_new = score_piece(n, q_all, i, m_new)
            if n > 0:
                acc = value_piece(n - 1, m_prev, i, acc)
        if n > 0 and separate:
            abufs[(n - 1) % 2][...] = acc
        elif n > 0:
            emit(n - 1, acc)
        if n < len(items):
            mbufs[n % 2][...] = jnp.max(m_new, axis=0, keepdims=True)

    for n in range(len(items) + (2 if separate else 1)):
        if separate:
            pl.when(pl.program_id(0) > -(n + 1))(functools.partial(pair, n))
        else:
            pair(n)


def _attention(qt, k, vt, batch, cache=None):
    ntok = qt.shape[1]
    t_new = k.shape[0] // batch
    tq = ATT_Q_TILE
    nq = ntok // batch // tq
    tiles = ATT_TILES_LONG if t_new > ATT_KEY_BLOCK else ATT_TILES_SHORT
    keys_per_tile = nq < tiles
    extra_specs, extra, tk = [], (), t_new
    if keys_per_tile:
        assert cache is None
        seqs = tiles // nq
        k_spec = pl.BlockSpec((seqs * t_new, D_KA), lambda t: (t, 0))
        vt_spec = pl.BlockSpec((D_KA, seqs * t_new), lambda t: (0, t))
    else:
        spq = nq // tiles
        k_spec = pl.BlockSpec((t_new, D_KA), lambda t: (t // spq, 0))
        vt_spec = pl.BlockSpec((D_KA, t_new), lambda t: (0, t // spq))
        if cache is not None:
            t_old = cache[0].shape[0] // batch
            assert t_new % ATT_KEY_BLOCK == 0 and t_old % ATT_KEY_BLOCK == 0
            extra_specs = [pl.BlockSpec((t_old, D_KA), lambda t: (t // spq, 0)),
                           pl.BlockSpec((D_KA, t_old), lambda t: (0, t // spq))]
            extra, tk = cache, t_new + t_old
    return pl.pallas_call(
        functools.partial(_attn_kernel, keys_per_tile=keys_per_tile, cached=cache is not None),
        grid=(ntok // (tiles * tq),),
        in_specs=[pl.BlockSpec((D_QA, tiles * tq), lambda t: (0, t)), k_spec, vt_spec] + extra_specs,
        out_specs=pl.BlockSpec((tiles * tq, D_QA), lambda t: (t, 0)),
        out_shape=jax.ShapeDtypeStruct((ntok, D_QA), BF16),
        scratch_shapes=[pltpu.VMEM((tk, GROUP * tq), F32), pltpu.VMEM((tk, GROUP * tq), F32),
                        pltpu.VMEM((1, GROUP * tq), F32), pltpu.VMEM((1, GROUP * tq), F32),
                        pltpu.VMEM((HEAD_DIM + 16, GROUP * tq), F32),
                        pltpu.VMEM((HEAD_DIM + 16, GROUP * tq), F32)],
        compiler_params=pltpu.CompilerParams(
            dimension_semantics=("arbitrary",), vmem_limit_bytes=VMEM_LIMIT),
        name="attention",
    )(qt, k, vt, *extra)


def _post_kernel(og_ref, oa_ref, x_ref, mod_ref, n2_ref, fg_ref, wo_ref, w1_ref, w3_ref, w2_ref,
                 y_ref):
    D = D_MODEL
    mod = mod_ref[0]
    g1, sh2, sc2, g2 = (mod[:, i * D:(i + 1) * D] for i in range(2, 6))
    out = _dot(og_ref[...], wo_ref[0:D_V, :]) + _dot(oa_ref[...], wo_ref[D_V:D_V + D_QA, :])
    x1 = x_ref[...] + g1 * out
    hb = (_rms(x1, n2_ref[...]) * (1.0 + sc2) + sh2).astype(BF16)
    act = (_silu(_dot(hb, w1_ref[...])) * _dot(hb, w3_ref[...])).astype(BF16)
    x2 = x1 + g2 * _dot(act, w2_ref[...])
    y_ref[...] = _rms(x2, fg_ref[...])


def _post_mixer(o_gla, o_att, x, mod3, mod_row0, tokens_per_mod, weights, tm):
    ntok = x.shape[0]
    tiles_per_mod = tokens_per_mod // tm
    whole = pl.BlockSpec(memory_space=pltpu.VMEM)
    row = lambda w: pl.BlockSpec((tm, w), lambda i: (i, 0))
    return pl.pallas_call(
        _post_kernel,
        grid=(ntok // tm,),
        in_specs=[row(D_V), row(D_QA), row(D_MODEL),
                  pl.BlockSpec((1, 1, N_MOD * D_MODEL), lambda i: (mod_row0 + i // tiles_per_mod, 0, 0)),
                  whole, whole, whole, whole, whole, whole],
        out_specs=row(D_MODEL),
        out_shape=jax.ShapeDtypeStruct((ntok, D_MODEL), F32),
        compiler_params=pltpu.CompilerParams(
            dimension_semantics=("arbitrary",), vmem_limit_bytes=VMEM_LIMIT),
        name="post_mixer",
    )(o_gla, o_att, x, mod3, *weights)


def _rope_tables(t):
    rows = t // GRID_W
    row = np.repeat(np.arange(rows, dtype=np.float64), GRID_W)
    col = np.tile(np.arange(GRID_W, dtype=np.float64), rows)
    inv = ROPE_THETA ** (-np.arange(0, ROPE_AXIS_DIM, 2, dtype=np.float64) / ROPE_AXIS_DIM)
    ang_r = row[:, None] * inv
    ang_c = col[:, None] * inv
    cos = np.concatenate([np.cos(ang_r)] * 2 + [np.cos(ang_c)] * 2, axis=-1)
    sin = np.concatenate([-np.sin(ang_r), np.sin(ang_r), -np.sin(ang_c), np.sin(ang_c)], axis=-1)
    tile = lambda a: jnp.asarray(np.tile(a, (1, ATT_KV_HEADS)), F32)
    return tile(cos), tile(sin)


def _block_mean(width, group):
    return jnp.asarray(np.kron(np.eye(width // group), np.ones((group, group))) / group, BF16)


def kernel(x_prompt, x_sample, cache_k, cache_v, state_gla_fwd, state_gla_bwd, c, c_ctx,
           ada_w, ada_b, norm1_g, norm2_g, w_in, w_gk2, b_gk2, gla_norm_g, q_norm_g, k_norm_g,
           w_out, w_ffn1, w_ffn3, w_ffn2, final_g):
    B, T, D = x_prompt.shape
    BL, TL, _ = x_sample.shape
    TP = cache_k.shape[2]
    l = 0

    mod3, w_in_r = _adaln(c_ctx, c, ada_w[l], ada_b[l], jnp.transpose(w_in[l]))
    zero = jnp.zeros((GLA_LOWRANK, D_QK), F32)
    w_gk = jnp.concatenate([jnp.concatenate([w_gk2[l, 0], zero], axis=1),
                            jnp.concatenate([zero, w_gk2[l, 1]], axis=1)], axis=0).astype(BF16)
    pre_w = (norm1_g[l].reshape(1, D), w_in_r, w_gk, b_gk2[l].reshape(1, 2 * D_QK),
             jnp.tile(q_norm_g[l], ATT_HEADS).reshape(1, D_QA),
             jnp.tile(k_norm_g[l], ATT_KV_HEADS).reshape(1, D_KA),
             _block_mean(D_QA, HEAD_DIM), _block_mean(D_KA, HEAD_DIM))
    post_norms = (norm2_g[l].reshape(1, D), final_g.reshape(1, D))
    post_w_f32 = (w_out[l], w_ffn1[l], w_ffn3[l], w_ffn2[l])
    gla_g = gla_norm_g[l].reshape(1, GLA_DV)

    xc = x_prompt.reshape(B * T, D)
    q, k, v, gate, g, qa, ka, va, kc, vc = _pre_mixer(
        xc, mod3, 0, B * T, T, pre_w, None, True, PRE_TILE_ROWS)
    r3 = lambda a, b, t: a.reshape(b, t, a.shape[-1])
    o_gla, s_f, s_b, *post_w_bf16 = _gla(
        r3(q, B, T), r3(k, B, T), r3(v, B, T), r3(g, B, T), r3(gate, B, T), gla_g, None,
        GLA_CONTEXT_SEQS, cast=post_w_f32)
    post_w = post_norms + tuple(post_w_bf16)
    o_att = _attention(qa, ka, va, B)
    y_prompt = _post_mixer(o_gla.reshape(B * T, D_V), o_att, xc, mod3, 0, B * T, post_w, POST_TILE_ROWS)

    xl = x_sample.reshape(BL * TL, D)
    q, k, v, gate, g, qa, ka, va = _pre_mixer(
        xl, mod3, 1, TL, TL, pre_w, _rope_tables(TL), False, PRE_TILE_ROWS)
    o_gla, _, _ = _gla(r3(q, BL, TL), r3(k, BL, TL), r3(v, BL, TL), r3(g, BL, TL), r3(gate, BL, TL),
                       gla_g, (state_gla_fwd[:, l].reshape(BL, D_QK, GLA_DV),
                               state_gla_bwd[:, l].reshape(BL, D_QK, GLA_DV)), BL, prefetch=True)

    k_old = cache_k[:, l].reshape(BL * TP, D_KA).astype(BF16)
    vt_old = jnp.transpose(cache_v[:, l], (2, 3, 0, 1)).reshape(D_KA, BL * TP).astype(BF16)
    o_att = _attention(qa, ka, va, BL, (k_old, vt_old))
    y_sample = _post_mixer(o_gla.reshape(BL * TL, D_V), o_att, xl, mod3, 1, TL, post_w, POST_TILE_ROWS)

    def cache_out(ct):
        return jnp.transpose(ct.reshape(B, 1, ATT_KV_HEADS, HEAD_DIM, T), (0, 1, 4, 2, 3))

    return (y_prompt.reshape(B, T, D), y_sample.reshape(BL, TL, D), cache_out(kc), cache_out(vc),
            s_f.reshape(B, 1, GLA_HEADS, GLA_DK, GLA_DV), s_b.reshape(B, 1, GLA_HEADS, GLA_DK, GLA_DV))
```

```python
import functools

import numpy as np
import jax
import jax.numpy as jnp
from jax import lax
from jax.experimental import pallas as pl
from jax.experimental.pallas import tpu as pltpu

F32 = jnp.float32
BF16 = jnp.bfloat16

D_MODEL = 1024
GRID_W = 64
GLA_HEADS = 4
GLA_DK = 64
GLA_DV = 128
GLA_LOWRANK = 16
GLA_GATE_NORM = 16.0
GLA_CHUNK = 64
ATT_HEADS = 8
ATT_KV_HEADS = 2
HEAD_DIM = 64
ROPE_AXIS_DIM = HEAD_DIM // 2
ROPE_THETA = 10000.0
N_MOD = 6
EPS = 1e-6

D_QK = GLA_HEADS * GLA_DK
D_V = GLA_HEADS * GLA_DV
D_QA = ATT_HEADS * HEAD_DIM
D_KA = ATT_KV_HEADS * HEAD_DIM
GROUP = ATT_HEADS // ATT_KV_HEADS
O_Q, O_K, O_V, O_OG = 0, D_QK, 2 * D_QK, 2 * D_QK + D_V
O_QA = O_OG + D_V
O_KA = O_QA + D_QA
O_VA = O_KA + D_KA
O_LR = O_VA + D_KA
D_PROJ = O_LR + 2 * GLA_LOWRANK

COND_ROWS = 8
VMEM_LIMIT = 56 * 1024 * 1024
PRE_TILE_ROWS = 1024
POST_TILE_ROWS = 512
GLA_CONTEXT_SEQS = 4


def _dot(a, b):
    return jnp.dot(a, b, preferred_element_type=F32)


def _dot_nt(a, b):
    return lax.dot_general(a, b, (((1,), (1,)), ((), ())), preferred_element_type=F32)


def _dot_tn(a, b):
    return lax.dot_general(a, b, (((0,), (0,)), ((), ())), preferred_element_type=F32)


def _rms(x, g):
    ms = jnp.mean(x * x, axis=-1, keepdims=True)
    return x * lax.rsqrt(ms + EPS) * g


def _silu(x):
    return x * jax.nn.sigmoid(x)


ADALN_K_BLOCK = 256


def _adaln_kernel(cctx_ref, c_ref, w_ref, b_ref, wint_ref, o_ref, win_o, acc):
    k = pl.program_id(0)

    lr0, lr1 = O_OG + D_V, O_OG + D_V + 2 * GLA_LOWRANK
    feats = jnp.concatenate([wint_ref[0:lr0, :], wint_ref[lr1:D_PROJ, :], wint_ref[lr0:lr1, :]],
                            axis=0)
    for j in range(0, D_PROJ, 128):
        rows = min(128, D_PROJ - j)
        tile = feats[j:j + rows, :]
        if rows < 128:
            tile = jnp.concatenate([tile, jnp.zeros((128 - rows, tile.shape[1]), F32)], axis=0)
        win_o[:, j:j + rows] = tile.T[:, 0:rows].astype(BF16)

    @pl.when(k == 0)
    def _():
        acc[...] = jnp.broadcast_to(b_ref[...], acc.shape)

    pad = jnp.zeros((COND_ROWS - 1 - c_ref.shape[0], c_ref.shape[1]), F32)
    cond = jnp.concatenate([cctx_ref[...], c_ref[...], pad], axis=0)
    acc[...] += _dot(_silu(cond).astype(BF16), w_ref[...].astype(BF16))

    @pl.when(k == pl.num_programs(0) - 1)
    def _():
        o_ref[:, 0, :] = acc[...]


def _adaln(c_ctx, c, ada_w, ada_b, w_in_t):
    d, n = ada_w.shape
    tk = ADALN_K_BLOCK
    assert 1 + c.shape[0] <= COND_ROWS and w_in_t.shape == (D_PROJ, d)
    return pl.pallas_call(
        _adaln_kernel,
        grid=(d // tk,),
        in_specs=[
            pl.BlockSpec((1, tk), lambda k: (0, k)),
            pl.BlockSpec((c.shape[0], tk), lambda k: (0, k)),
            pl.BlockSpec((tk, n), lambda k: (k, 0)),
            pl.BlockSpec((1, n), lambda k: (0, 0)),
            pl.BlockSpec((D_PROJ, tk), lambda k: (0, k)),
        ],
        out_specs=[pl.BlockSpec((COND_ROWS, 1, n), lambda k: (0, 0, 0)),
                   pl.BlockSpec((tk, D_PROJ), lambda k: (k, 0))],
        out_shape=[jax.ShapeDtypeStruct((COND_ROWS, 1, n), F32),
                   jax.ShapeDtypeStruct((d, D_PROJ), BF16)],
        scratch_shapes=[pltpu.VMEM((COND_ROWS, n), F32)],
        compiler_params=pltpu.CompilerParams(
            dimension_semantics=("arbitrary",), vmem_limit_bytes=VMEM_LIMIT),
        name="adaln",
    )(c_ctx.reshape(1, d), c, ada_w, ada_b.reshape(1, n), w_in_t)


def _swap_halves(x, lane_lo):
    n = x.shape[-1]
    up = pltpu.roll(x, n - ROPE_AXIS_DIM // 2, axis=1)
    dn = pltpu.roll(x, ROPE_AXIS_DIM // 2, axis=1)
    return jnp.where(lane_lo, up, dn)


PRE_SUB_ROWS = 512
PRE_SUB_ROWS_ROPE = 256


def _pre_kernel(*refs, rope, emit_cache):
    it = iter(refs)
    x_ref, mod_ref, n1_ref, win_ref, wgk_ref, bgk_ref = (next(it) for _ in range(6))
    qg_ref, kg_ref, bdq_ref, bdk_ref = (next(it) for _ in range(4))
    if rope:
        cos_ref, sin_ref = next(it), next(it)
    q_o, k_o, v_o, gate_o, g_o, qa_o, ka_o, va_o = (next(it) for _ in range(8))
    if emit_cache:
        kc_o, vc_o = next(it), next(it)

    sub = PRE_SUB_ROWS_ROPE if rope else PRE_SUB_ROWS
    n_sub = x_ref.shape[0] // sub
    mod = mod_ref[0]
    sh1 = mod[:, 0:D_MODEL]
    sc1 = mod[:, D_MODEL:2 * D_MODEL]

    def project(j):
        rows = slice(j * sub, (j + 1) * sub)
        h = _rms(x_ref[rows, :], n1_ref[...]) * (1.0 + sc1) + sh1
        return _dot(h.astype(BF16), win_ref[...])

    def finish(j, p_ref):
        rows = slice(j * sub, (j + 1) * sub)
        q_o[rows, :] = p_ref[:, O_Q:O_Q + D_QK].astype(BF16)
        k_o[rows, :] = p_ref[:, O_K:O_K + D_QK].astype(BF16)
        v_o[rows, :] = p_ref[:, O_V:O_V + D_V].astype(BF16)
        gate_o[rows, :] = _silu(p_ref[:, O_OG:O_OG + D_V]).astype(BF16)

        gk = _dot(p_ref[:, O_LR:O_LR + 2 * GLA_LOWRANK].astype(BF16), wgk_ref[...]) + bgk_ref[...]
        g_o[rows, :] = ((jnp.minimum(gk, 0.0) - jnp.log(1.0 + jnp.exp(-jnp.abs(gk))))
                        * (1.0 / GLA_GATE_NORM))

        qa = p_ref[:, O_QA:O_QA + D_QA]
        ka = p_ref[:, O_KA:O_KA + D_KA]
        va = p_ref[:, O_VA:O_VA + D_KA]
        qa = qa * lax.rsqrt(_dot((qa * qa).astype(BF16), bdq_ref[...]) + EPS) * qg_ref[...]
        ka = ka * lax.rsqrt(_dot((ka * ka).astype(BF16), bdk_ref[...]) + EPS) * kg_ref[...]
        if rope:
            lane_q = lax.broadcasted_iota(jnp.int32, qa.shape, 1)
            lane_k = lax.broadcasted_iota(jnp.int32, ka.shape, 1)
            lo_q = (lane_q % ROPE_AXIS_DIM) < ROPE_AXIS_DIM // 2
            lo_k = (lane_k % ROPE_AXIS_DIM) < ROPE_AXIS_DIM // 2
            cos_k = cos_ref[rows, :]
            sin_k = sin_ref[rows, :]
            cos_q = jnp.concatenate([cos_k] * GROUP, axis=1)
            sin_q = jnp.concatenate([sin_k] * GROUP, axis=1)
            qa = qa * cos_q + _swap_halves(qa, lo_q) * sin_q
            ka = ka * cos_k + _swap_halves(ka, lo_k) * sin_k
        vat = va.T
        if emit_cache:
            kat = ka.T
            t = kc_o.shape[2]
            for sq in range(sub // t):
                kc_o[j * (sub // t) + sq] = kat[:, sq * t:(sq + 1) * t]
                vc_o[j * (sub // t) + sq] = vat[:, sq * t:(sq + 1) * t]
        qa_o[:, rows] = (qa * (HEAD_DIM ** -0.5 * LOG2E)).T.astype(BF16)
        ka_o[rows, :] = ka.astype(BF16)
        va_o[:, rows] = vat.astype(BF16)

    p_prev = None
    for n in range(n_sub + 1):
        p_new = project(n) if n < n_sub else None
        if n > 0:
            finish(n - 1, p_prev)
        p_prev = p_new


def _pre_mixer(x, mod3, mod_row0, tokens_per_mod, seq_len, weights, rope_tabs, emit_cache, tm):
    ntok = x.shape[0]
    n1, w_in_r, w_gk, b_gk, qg, kg, bdq, bdk = weights
    rope = rope_tabs is not None
    tiles_per_mod = tokens_per_mod // tm
    tiles_per_seq = seq_len // tm
    whole = pl.BlockSpec(memory_space=pltpu.VMEM)
    row = lambda w: pl.BlockSpec((tm, w), lambda i: (i, 0))
    in_specs = [
        row(D_MODEL),
        pl.BlockSpec((1, 1, N_MOD * D_MODEL), lambda i: (mod_row0 + i // tiles_per_mod, 0, 0)),
        whole, whole, whole, whole, whole, whole, whole, whole,
    ]
    args = [x, mod3, n1, w_in_r, w_gk, b_gk, qg, kg, bdq, bdk]
    if rope:
        tab = pl.BlockSpec((tm, D_KA), lambda i: (i % tiles_per_seq, 0))
        in_specs += [tab, tab]
        args += list(rope_tabs)
    col = lambda w: pl.BlockSpec((w, tm), lambda i: (0, i))
    out_specs = [row(D_QK), row(D_QK), row(D_V), row(D_V), row(2 * D_QK),
                 col(D_QA), row(D_KA), col(D_KA)]
    out_shape = [
        jax.ShapeDtypeStruct((ntok, D_QK), BF16),
        jax.ShapeDtypeStruct((ntok, D_QK), BF16),
        jax.ShapeDtypeStruct((ntok, D_V), BF16),
        jax.ShapeDtypeStruct((ntok, D_V), BF16),
        jax.ShapeDtypeStruct((ntok, 2 * D_QK), F32),
        jax.ShapeDtypeStruct((D_QA, ntok), BF16),
        jax.ShapeDtypeStruct((ntok, D_KA), BF16),
        jax.ShapeDtypeStruct((D_KA, ntok), BF16),
    ]
    if emit_cache:
        seqs = tm // seq_len
        out_specs += [pl.BlockSpec((seqs, D_KA, seq_len), lambda i: (i, 0, 0))] * 2
        out_shape += [jax.ShapeDtypeStruct((ntok // seq_len, D_KA, seq_len), F32)] * 2
    return pl.pallas_call(
        functools.partial(_pre_kernel, rope=rope, emit_cache=emit_cache),
        grid=(ntok // tm,),
        in_specs=in_specs,
        out_specs=out_specs,
        out_shape=out_shape,
        compiler_params=pltpu.CompilerParams(
            dimension_semantics=("arbitrary",), vmem_limit_bytes=VMEM_LIMIT),
        name="pre_mixer_rope" if rope else "pre_mixer",
    )(*args)


def _gla_chunks(units, q_ref, k_ref, v_ref, g_ref, half_lo):
    C = GLA_CHUNK
    n_pair = GLA_HEADS // 2
    rows = [pl.ds(pl.multiple_of(c * C, C), C) for (_, c, _, _, _, _) in units]

    bs = []
    for (bi, _, fwd, tri, _, _), r in zip(units, rows):
        gc = g_ref[bi, r, 0:D_QK] if fwd else g_ref[bi, r, D_QK:2 * D_QK]
        ltri = jnp.where(tri, 1.0, 0.0).astype(BF16)
        g_hi = gc.astype(BF16)
        g_lo = (gc - g_hi.astype(F32)).astype(BF16)
        bs.append(_dot(ltri, g_hi) + _dot(ltri, g_lo))

    qts, kts, kes, decays = [], [], [], []
    for (bi, _, fwd, _, _, _), r, b in zip(units, rows, bs):
        qc = q_ref[bi, r, :].astype(F32) * GLA_DK ** -0.5
        kc = k_ref[bi, r, :].astype(F32)
        btot = b[C - 1:C, :] if fwd else b[0:1, :]
        qts.append((qc * jnp.exp(b)).astype(BF16))
        kts.append((kc * jnp.exp(-b)).astype(BF16))
        kes.append((kc * jnp.exp(btot - b)).astype(BF16))
        decays.append(jnp.transpose(jnp.broadcast_to(jnp.exp(btot), (8, D_QK)))[:, 0:1])

    scores, incs, qms = [], [], []
    for (bi, _, _, tri, _, _), r, qt, kt, ke in zip(units, rows, qts, kts, kes):
        for pair in range(n_pair):
            lanes = slice(pair * 128, (pair + 1) * 128)
            incs.append(_dot_tn(ke[:, lanes], v_ref[bi, r, pair * 2 * GLA_DV:(pair + 1) * 2 * GLA_DV]))
            for hh in range(2):
                qm = jnp.where(half_lo if hh == 0 else ~half_lo, qt[:, lanes], jnp.zeros((C, 128), BF16))
                qms.append(qm)
                scores.append(jnp.where(tri, _dot_nt(qm, kt[:, lanes]), 0.0).astype(BF16))

    for ui, ((bi, _, _, _, s_ref, o_ref), r) in enumerate(zip(units, rows)):
        state = s_ref[bi]
        state_b = state.astype(BF16)
        for hd in range(GLA_HEADS):
            pair = hd // 2
            lhs = jnp.concatenate([qms[ui * GLA_HEADS + hd], scores[ui * GLA_HEADS + hd]], axis=1)
            rhs = jnp.concatenate([state_b[pair * 128:(pair + 1) * 128, :],
                                   v_ref[bi, r, hd * GLA_DV:(hd + 1) * GLA_DV]], axis=0)
            o_ref[bi, r, hd * GLA_DV:(hd + 1) * GLA_DV] = _dot(lhs, rhs)
        inc = jnp.concatenate(
            [incs[ui * n_pair + hd // 2][(hd % 2) * GLA_DK:(hd % 2 + 1) * GLA_DK,
                                         (hd % 2) * GLA_DV:(hd % 2 + 1) * GLA_DV]
             for hd in range(GLA_HEADS)], axis=0)
        s_ref[bi] = decays[ui] * state + inc


def _gla_kernel(*args, zero_init, n_cast, prefetch):
    q_ref, k_ref, v_ref, g_ref, gate_ref, gn_ref = args[:6]
    n_in = 6 + (0 if zero_init else 2)
    cast_src = args[n_in:n_in + n_cast]
    o_ref, sf_ref, sb_ref = args[n_in + n_cast:n_in + n_cast + 3]
    cast_dst = args[n_in + n_cast + 3:n_in + 2 * n_cast + 3]
    if prefetch:
        of_scr, ob_scr = args[-8:-6]
        hbm = (q_ref, k_ref, v_ref, g_ref, gate_ref)
        vm = args[-6:-1]
        sem = args[-1]
        q_ref, k_ref, v_ref, g_ref, gate_ref = vm
        slab = GLA_STEPS_PER_ITER * GLA_CHUNK
        n_slab = q_ref.shape[1] // slab

        def slab_copy(a, c):
            rows = pl.ds(c * slab, slab)
            return pltpu.make_async_copy(hbm[a].at[:, rows, :], vm[a].at[:, rows, :], sem.at[a, c])

        order = [c for i in range(n_slab // 2) for c in (i, n_slab - 1 - i)]
        for c in order:
            for a in range(4):
                slab_copy(a, c).start(priority=a % 2)
        for c in order:
            slab_copy(4, c).start()
    else:
        of_scr, ob_scr = args[-2:]
    for src, dst in zip(cast_src, cast_dst):
        dst[...] = src[...].astype(BF16)
    C = GLA_CHUNK
    nb, t, _ = q_ref.shape
    n = t // C
    row = lax.broadcasted_iota(jnp.int32, (C, C), 0)
    col = lax.broadcasted_iota(jnp.int32, (C, C), 1)
    half_lo = lax.broadcasted_iota(jnp.int32, (C, 128), 1) < GLA_DK
    refs = (q_ref, k_ref, v_ref, g_ref)
    if zero_init:
        sf_ref[...] = jnp.zeros(sf_ref.shape, F32)
        sb_ref[...] = jnp.zeros(sb_ref.shape, F32)
    else:
        sf_ref[...] = args[6][...]
        sb_ref[...] = args[7][...]

    steps = min(GLA_STEPS_PER_ITER, n)
    assert n % steps == 0

    def body(i, carry):
        if prefetch:
            @pl.when(i < n_slab // 2)
            def _():
                for a in range(4):
                    slab_copy(a, i).wait()
                    slab_copy(a, n_slab - 1 - i).wait()
        units = []
        for u in range(steps):
            c = i * steps + u
            for bi in range(nb):
                units.append((bi, c, True, row >= col, sf_ref, of_scr))
                units.append((bi, n - 1 - c, False, row <= col, sb_ref, ob_scr))
        _gla_chunks(units, *refs, half_lo)
        return carry

    lax.fori_loop(0, n // steps, body, 0)
    if prefetch:
        assert steps == GLA_STEPS_PER_ITER and n_slab % 2 == 0
        for c in range(n_slab):
            slab_copy(4, c).wait()

    rt = GLA_OUT_ROWS
    gn = gn_ref[...]

    def finish(i, carry):
        rows = pl.ds(pl.multiple_of(i * rt, rt), rt)
        for bi in range(nb):
            o = of_scr[bi, rows, :] + ob_scr[bi, rows, :]
            gate = gate_ref[bi, rows, :].astype(F32)
            for hd in range(GLA_HEADS):
                sl = slice(hd * GLA_DV, (hd + 1) * GLA_DV)
                o_ref[bi, rows, sl] = (_rms(o[:, sl], gn) * gate[:, sl]).astype(BF16)
        return carry

    lax.fori_loop(0, t // rt, finish, 0)


GLA_STEPS_PER_ITER = 4
GLA_OUT_ROWS = 256


def _gla(q, k, v, g, gate, gla_g, init_states, nb, cast=(), prefetch=False):
    b, t, _ = q.shape
    assert not prefetch or b == nb
    widths = (D_QK, D_QK, D_V, 2 * D_QK, D_V)
    dtypes = (BF16, BF16, BF16, F32, BF16)
    if prefetch:
        seq_in = lambda w: pl.BlockSpec(memory_space=pl.ANY)
        extra_scratch = [pltpu.VMEM((nb, t, w), d) for w, d in zip(widths, dtypes)] + [
            pltpu.SemaphoreType.DMA((len(widths), t // (GLA_STEPS_PER_ITER * GLA_CHUNK)))]
    else:
        seq_in = lambda w: pl.BlockSpec((nb, t, w), lambda i: (i, 0, 0))
        extra_scratch = []
    seq = lambda w: pl.BlockSpec((nb, t, w), lambda i: (i, 0, 0))
    st = pl.BlockSpec((nb, D_QK, GLA_DV), lambda i: (i, 0, 0))
    zero_init = init_states is None
    steps = b // nb
    slabs = [pl.BlockSpec((a.shape[0] // steps, a.shape[1]), lambda i: (i, 0)) for a in cast]
    return pl.pallas_call(
        functools.partial(_gla_kernel, zero_init=zero_init, n_cast=len(cast), prefetch=prefetch),
        grid=(b // nb,),
        in_specs=[seq_in(w) for w in widths] + [
                  pl.BlockSpec((1, GLA_DV), lambda i: (0, 0))] + ([] if zero_init else [st, st]) + slabs,
        out_specs=[seq(D_V), st, st] + slabs,
        out_shape=[jax.ShapeDtypeStruct((b, t, D_V), BF16),
                   jax.ShapeDtypeStruct((b, D_QK, GLA_DV), F32),
                   jax.ShapeDtypeStruct((b, D_QK, GLA_DV), F32)]
        + [jax.ShapeDtypeStruct(a.shape, BF16) for a in cast],
        scratch_shapes=[pltpu.VMEM((nb, t, D_V), F32), pltpu.VMEM((nb, t, D_V), F32)] + extra_scratch,
        compiler_params=pltpu.CompilerParams(
            dimension_semantics=("arbitrary",), vmem_limit_bytes=VMEM_LIMIT),
        name="gla",
    )(q, k, v, g, gate, gla_g, *(() if zero_init else init_states), *cast)


ATT_KEY_BLOCK = 256
ATT_Q_TILE = 256
LOG2E = 1.4426950408889634


ATT_TILES_LONG = 2
ATT_TILES_SHORT = 8


def _attn_kernel(*args, keys_per_tile, cached):
    qt_ref, k_ref, vt_ref = args[:3]
    kc_ref, vtc_ref = args[3:5] if cached else (None, None)
    o_ref, s0, s1, m0, m1, a0, a1 = args[-7:]
    tq = ATT_Q_TILE
    tiles = qt_ref.shape[1] // tq
    t_new = k_ref.shape[0] // tiles if keys_per_tile else k_ref.shape[0]
    tk = t_new + (kc_ref.shape[0] if cached else 0)
    kb = min(ATT_KEY_BLOCK, tk)
    nkb = tk // kb
    zero = jnp.zeros((HEAD_DIM, tq), BF16)
    ones = jnp.ones((16, kb), BF16)
    items = [(qi, kv) for qi in range(tiles) for kv in range(ATT_KV_HEADS)]
    bufs = (s0, s1)
    mbufs = (m0, m1)
    abufs = (a0, a1)
    separate = nkb > 1

    def q_ext(qi, kv):
        cols = []
        for g in range(GROUP):
            hd = kv * GROUP + g
            qt = qt_ref[hd * HEAD_DIM:(hd + 1) * HEAD_DIM, qi * tq:(qi + 1) * tq]
            cols.append(jnp.concatenate([qt, zero] if kv == 0 else [zero, qt], axis=0))
        return jnp.concatenate(cols, axis=1)

    def key0(qi):
        return qi * tk if keys_per_tile else 0

    def score_piece(n, q_all, i, m):
        r0 = key0(items[n][0]) + i * kb
        keys = k_ref[r0:r0 + kb, :] if i * kb < t_new else kc_ref[r0 - t_new:r0 - t_new + kb, :]
        s = _dot(keys, q_all)
        bufs[n % 2][i * kb:(i + 1) * kb, :] = s
        for r in range(kb // 8):
            tile = s[r * 8:(r + 1) * 8, :]
            m = tile if m is None else jnp.maximum(m, tile)
        return m

    def value_piece(n, m, i, acc):
        qi, kv = items[n]
        c0 = key0(qi) + i * kb
        p = jnp.exp2(bufs[n % 2][i * kb:(i + 1) * kb, :] - m).astype(BF16)
        hd_rows = slice(kv * HEAD_DIM, (kv + 1) * HEAD_DIM)
        vt = vt_ref[hd_rows, c0:c0 + kb] if i * kb < t_new else vtc_ref[hd_rows, c0 - t_new:c0 - t_new + kb]
        vt1 = jnp.concatenate([vt, ones], axis=0)
        d = _dot(vt1, p)
        return d if acc is None else acc + d

    def emit(n, acc):
        qi, kv = items[n]
        o = acc[:HEAD_DIM] / acc[HEAD_DIM:HEAD_DIM + 1]
        for g in range(0, GROUP, 2):
            two = jnp.concatenate([o[:, g * tq:(g + 1) * tq], o[:, (g + 1) * tq:(g + 2) * tq]], axis=0)
            hd = kv * GROUP + g
            o_ref[qi * tq:(qi + 1) * tq, hd * HEAD_DIM:(hd + 2) * HEAD_DIM] = two.T.astype(BF16)

    def pair(n):
        if separate and n >= 2:
            emit(n - 2, abufs[n % 2][...])
        if n > len(items):
            return
        q_all = q_ext(*items[n]) if n < len(items) else None
        m_prev = mbufs[(n - 1) % 2][...] if n > 0 else None
        m_new = acc = None
        for i in range(nkb):
            if n < len(items):
                m_new = score_piece(n, q_all, i, m_new)
            if n > 0:
                acc = value_piece(n - 1, m_prev, i, acc)
        if n > 0 and separate:
            abufs[(n - 1) % 2][...] = acc
        elif n > 0:
            emit(n - 1, acc)
        if n < len(items):
            mbufs[n % 2][...] = jnp.max(m_new, axis=0, keepdims=True)

    for n in range(len(items) + (2 if separate else 1)):
        if separate:
            pl.when(pl.program_id(0) > -(n + 1))(functools.partial(pair, n))
        else:
            pair(n)


def _attention(qt, k, vt, batch, cache=None):
    ntok = qt.shape[1]
    t_new = k.shape[0] // batch
    tq = ATT_Q_TILE
    nq = ntok // batch // tq
    tiles = ATT_TILES_LONG if t_new > ATT_KEY_BLOCK else ATT_TILES_SHORT
    keys_per_tile = nq < tiles
    extra_specs, extra, tk = [], (), t_new
    if keys_per_tile:
        assert cache is None
        seqs = tiles // nq
        k_spec = pl.BlockSpec((seqs * t_new, D_KA), lambda t: (t, 0))
        vt_spec = pl.BlockSpec((D_KA, seqs * t_new), lambda t: (0, t))
    else:
        spq = nq // tiles
        k_spec = pl.BlockSpec((t_new, D_KA), lambda t: (t // spq, 0))
        vt_spec = pl.BlockSpec((D_KA, t_new), lambda t: (0, t // spq))
        if cache is not None:
            t_old = cache[0].shape[0] // batch
            assert t_new % ATT_KEY_BLOCK == 0 and t_old % ATT_KEY_BLOCK == 0
            extra_specs = [pl.BlockSpec((t_old, D_KA), lambda t: (t // spq, 0)),
                           pl.BlockSpec((D_KA, t_old), lambda t: (0, t // spq))]
            extra, tk = cache, t_new + t_old
    return pl.pallas_call(
        functools.partial(_attn_kernel, keys_per_tile=keys_per_tile, cached=cache is not None),
        grid=(ntok // (tiles * tq),),
        in_specs=[pl.BlockSpec((D_QA, tiles * tq), lambda t: (0, t)), k_spec, vt_spec] + extra_specs,
        out_specs=pl.BlockSpec((tiles * tq, D_QA), lambda t: (t, 0)),
        out_shape=jax.ShapeDtypeStruct((ntok, D_QA), BF16),
        scratch_shapes=[pltpu.VMEM((tk, GROUP * tq), F32), pltpu.VMEM((tk, GROUP * tq), F32),
                        pltpu.VMEM((1, GROUP * tq), F32), pltpu.VMEM((1, GROUP * tq), F32),
                        pltpu.VMEM((HEAD_DIM + 16, GROUP * tq), F32),
                        pltpu.VMEM((HEAD_DIM + 16, GROUP * tq), F32)],
        compiler_params=pltpu.CompilerParams(
            dimension_semantics=("arbitrary",), vmem_limit_bytes=VMEM_LIMIT),
        name="attention",
    )(qt, k, vt, *extra)


def _post_kernel(og_ref, oa_ref, x_ref, mod_ref, n2_ref, fg_ref, wo_ref, w1_ref, w3_ref, w2_ref,
                 y_ref):
    D = D_MODEL
    mod = mod_ref[0]
    g1, sh2, sc2, g2 = (mod[:, i * D:(i + 1) * D] for i in range(2, 6))
    out = _dot(og_ref[...], wo_ref[0:D_V, :]) + _dot(oa_ref[...], wo_ref[D_V:D_V + D_QA, :])
    x1 = x_ref[...] + g1 * out
    hb = (_rms(x1, n2_ref[...]) * (1.0 + sc2) + sh2).astype(BF16)
    act = (_silu(_dot(hb, w1_ref[...])) * _dot(hb, w3_ref[...])).astype(BF16)
    x2 = x1 + g2 * _dot(act, w2_ref[...])
    y_ref[...] = _rms(x2, fg_ref[...])


def _post_mixer(o_gla, o_att, x, mod3, mod_row0, tokens_per_mod, weights, tm):
    ntok = x.shape[0]
    tiles_per_mod = tokens_per_mod // tm
    whole = pl.BlockSpec(memory_space=pltpu.VMEM)
    row = lambda w: pl.BlockSpec((tm, w), lambda i: (i, 0))
    return pl.pallas_call(
        _post_kernel,
        grid=(ntok // tm,),
        in_specs=[row(D_V), row(D_QA), row(D_MODEL),
                  pl.BlockSpec((1, 1, N_MOD * D_MODEL), lambda i: (mod_row0 + i // tiles_per_mod, 0, 0)),
                  whole, whole, whole, whole, whole, whole],
        out_specs=row(D_MODEL),
        out_shape=jax.ShapeDtypeStruct((ntok, D_MODEL), F32),
        compiler_params=pltpu.CompilerParams(
            dimension_semantics=("arbitrary",), vmem_limit_bytes=VMEM_LIMIT),
        name="post_mixer",
    )(o_gla, o_att, x, mod3, *weights)


def _rope_tables(t):
    rows = t // GRID_W
    row = np.repeat(np.arange(rows, dtype=np.float64), GRID_W)
    col = np.tile(np.arange(GRID_W, dtype=np.float64), rows)
    inv = ROPE_THETA ** (-np.arange(0, ROPE_AXIS_DIM, 2, dtype=np.float64) / ROPE_AXIS_DIM)
    ang_r = row[:, None] * inv
    ang_c = col[:, None] * inv
    cos = np.concatenate([np.cos(ang_r)] * 2 + [np.cos(ang_c)] * 2, axis=-1)
    sin = np.concatenate([-np.sin(ang_r), np.sin(ang_r), -np.sin(ang_c), np.sin(ang_c)], axis=-1)
    tile = lambda a: jnp.asarray(np.tile(a, (1, ATT_KV_HEADS)), F32)
    return tile(cos), tile(sin)


def _block_mean(width, group):
    return jnp.asarray(np.kron(np.eye(width // group), np.ones((group, group))) / group, BF16)


def kernel(x_prompt, x_sample, cache_k, cache_v, state_gla_fwd, state_gla_bwd, c, c_ctx,
           ada_w, ada_b, norm1_g, norm2_g, w_in, w_gk2, b_gk2, gla_norm_g, q_norm_g, k_norm_g,
           w_out, w_ffn1, w_ffn3, w_ffn2, final_g):
    B, T, D = x_prompt.shape
    BL, TL, _ = x_sample.shape
    TP = cache_k.shape[2]
    l = 0

    mod3, w_in_r = _adaln(c_ctx, c, ada_w[l], ada_b[l], jnp.transpose(w_in[l]))
    zero = jnp.zeros((GLA_LOWRANK, D_QK), F32)
    w_gk = jnp.concatenate([jnp.concatenate([w_gk2[l, 0], zero], axis=1),
                            jnp.concatenate([zero, w_gk2[l, 1]], axis=1)], axis=0).astype(BF16)
    pre_w = (norm1_g[l].reshape(1, D), w_in_r, w_gk, b_gk2[l].reshape(1, 2 * D_QK),
             jnp.tile(q_norm_g[l], ATT_HEADS).reshape(1, D_QA),
             jnp.tile(k_norm_g[l], ATT_KV_HEADS).reshape(1, D_KA),
             _block_mean(D_QA, HEAD_DIM), _block_mean(D_KA, HEAD_DIM))
    post_norms = (norm2_g[l].reshape(1, D), final_g.reshape(1, D))
    post_w_f32 = (w_out[l], w_ffn1[l], w_ffn3[l], w_ffn2[l])
    gla_g = gla_norm_g[l].reshape(1, GLA_DV)

    xc = x_prompt.reshape(B * T, D)
    q, k, v, gate, g, qa, ka, va, kc, vc = _pre_mixer(
        xc, mod3, 0, B * T, T, pre_w, None, True, PRE_TILE_ROWS)
    r3 = lambda a, b, t: a.reshape(b, t, a.shape[-1])
    o_gla, s_f, s_b, *post_w_bf16 = _gla(
        r3(q, B, T), r3(k, B, T), r3(v, B, T), r3(g, B, T), r3(gate, B, T), gla_g, None,
        GLA_CONTEXT_SEQS, cast=post_w_f32)
    post_w = post_norms + tuple(post_w_bf16)
    o_att = _attention(qa, ka, va, B)
    y_prompt = _post_mixer(o_gla.reshape(B * T, D_V), o_att, xc, mod3, 0, B * T, post_w, POST_TILE_ROWS)

    xl = x_sample.reshape(BL * TL, D)
    q, k, v, gate, g, qa, ka, va = _pre_mixer(
        xl, mod3, 1, TL, TL, pre_w, _rope_tables(TL), False, PRE_TILE_ROWS)
    o_gla, _, _ = _gla(r3(q, BL, TL), r3(k, BL, TL), r3(v, BL, TL), r3(g, BL, TL), r3(gate, BL, TL),
                       gla_g, (state_gla_fwd[:, l].reshape(BL, D_QK, GLA_DV),
                               state_gla_bwd[:, l].reshape(BL, D_QK, GLA_DV)), BL, prefetch=True)

    k_old = cache_k[:, l].reshape(BL * TP, D_KA).astype(BF16)
    vt_old = jnp.transpose(cache_v[:, l], (2, 3, 0, 1)).reshape(D_KA, BL * TP).astype(BF16)
    o_att = _attention(qa, ka, va, BL, (k_old, vt_old))
    y_sample = _post_mixer(o_gla.reshape(BL * TL, D_V), o_att, xl, mod3, 1, TL, post_w, POST_TILE_ROWS)

    def cache_out(ct):
        return jnp.transpose(ct.reshape(B, 1, ATT_KV_HEADS, HEAD_DIM, T), (0, 1, 4, 2, 3))

    return (y_prompt.reshape(B, T, D), y_sample.reshape(BL, TL, D), cache_out(kc), cache_out(vc),
            s_f.reshape(B, 1, GLA_HEADS, GLA_DK, GLA_DV), s_b.reshape(B, 1, GLA_HEADS, GLA_DK, GLA_DV))
```
